```python
import jax, jax.numpy as jnp
from jax import lax
import numpy as np

D_MODEL = 1024
BATCH = 4
SEQ = 4096
DEPTH = 2

N_GROUPS = 4
GROUP_WIDTH = D_MODEL // N_GROUPS
D_MIX = N_GROUPS * GROUP_WIDTH
N_HEADS = 4
HEAD_DIM = GROUP_WIDTH // N_HEADS
CHUNK = 128
RET_CHUNK = 128
Q_BLOCK = 128
MLA_Q_LORA = D_MODEL // 4
MLA_KV_LORA = D_MODEL // 8
MLA_NOPE = HEAD_DIM
MLA_ROPE = HEAD_DIM // 2
MLA_V = HEAD_DIM
ROPE_BASE = 10000.0
D_FF = 4 * D_MODEL
EPS = 1e-6

IN_SPLIT_SIZES = (
    2 * GROUP_WIDTH,
    GROUP_WIDTH, GROUP_WIDTH, GROUP_WIDTH, GROUP_WIDTH,
    GROUP_WIDTH, GROUP_WIDTH, GROUP_WIDTH, N_HEADS,
    MLA_Q_LORA, MLA_KV_LORA, MLA_ROPE,
)
N_IN = sum(IN_SPLIT_SIZES)
IN_SPLIT_POINTS = tuple(np.cumsum(IN_SPLIT_SIZES)[:-1].tolist())

kernel_name = 'hymba_style_four_mixer_hybrid'


def rms_norm(t, g):
    tf = t.astype(jnp.float32)
    y = tf * lax.rsqrt(jnp.mean(tf * tf, axis=-1, keepdims=True) + EPS)
    return (y * g.astype(jnp.float32)).astype(t.dtype)


def standardize(t):
    mu = jnp.mean(t, axis=-1, keepdims=True)
    var = jnp.mean(jnp.square(t - mu), axis=-1, keepdims=True)
    return (t - mu) * lax.rsqrt(var + EPS)


def to_heads(t, h=N_HEADS):
    b, s, _ = t.shape
    return t.reshape(b, s, h, -1).transpose(0, 2, 1, 3)


def from_heads(t):
    b, h, s, d = t.shape
    return t.transpose(0, 2, 1, 3).reshape(b, s, h * d)


def rotary(t):
    s, d = t.shape[-2], t.shape[-1]
    half = d // 2
    inv_freq = jnp.power(ROPE_BASE, -jnp.arange(half, dtype=jnp.float32) / half)
    ang = jnp.arange(s, dtype=jnp.float32)[:, None] * inv_freq[None, :]
    cos, sin = jnp.cos(ang), jnp.sin(ang)
    t1 = t[..., :half].astype(jnp.float32)
    t2 = t[..., half:].astype(jnp.float32)
    return jnp.concatenate([t1 * cos - t2 * sin, t1 * sin + t2 * cos], axis=-1)


def causal_block_attention(q, k, v, scale, cum_log_f=None):
    b, h, s, dq = q.shape
    dv = v.shape[-1]
    nb = s // Q_BLOCK
    kf = k.astype(jnp.float32)
    vf = v.astype(jnp.float32)
    q_blocks = q.astype(jnp.float32).reshape(b, h, nb, Q_BLOCK, dq).transpose(2, 0, 1, 3, 4)
    key_pos = jnp.arange(s)
    blk_ids = jnp.arange(nb)

    def attend(i, q_blk, cum_blk):
        logits = jnp.einsum('bhqd,bhkd->bhqk', q_blk, kf) * scale
        if cum_blk is not None:
            logits = logits + cum_blk[..., :, None] - cum_log_f[:, :, None, :]
        q_pos = i * Q_BLOCK + jnp.arange(Q_BLOCK)
        mask = key_pos[None, :] <= q_pos[:, None]
        logits = jnp.where(mask, logits, -jnp.inf)
        p = jax.nn.softmax(logits, axis=-1)
        return jnp.einsum('bhqk,bhkd->bhqd', p, vf)

    if cum_log_f is None:
        out = lax.map(lambda a: attend(a[0], a[1], None), (blk_ids, q_blocks))
    else:
        cum_log_f = cum_log_f.astype(jnp.float32)
        cum_blocks = cum_log_f.reshape(b, h, nb, Q_BLOCK).transpose(2, 0, 1, 3)
        out = lax.map(lambda a: attend(a[0], a[1], a[2]), (blk_ids, q_blocks, cum_blocks))
    return out.transpose(1, 2, 0, 3, 4).reshape(b, h, s, dv)


def spatial_gating_chunked(uv, ln_gain, w_s, b_s):
    b, s, _ = uv.shape
    uvf = jax.nn.gelu(uv.astype(jnp.float32))
    u, v = jnp.split(uvf, 2, axis=-1)
    v = standardize(v.reshape(b, s, N_HEADS, HEAD_DIM)) * ln_gain.astype(jnp.float32).reshape(N_HEADS, HEAD_DIM)
    v = v.reshape(b, s // CHUNK, CHUNK, N_HEADS, HEAD_DIM)
    w_causal = jnp.tril(w_s.astype(jnp.float32))
    mixed = jnp.einsum('hts,bnshd->bnthd', w_causal, v) + b_s.astype(jnp.float32).T[None, None, :, :, None]
    return (u * mixed.reshape(b, s, GROUP_WIDTH)).astype(uv.dtype)


def retention_chunkwise(q, k, v, log_gamma):
    b, h, s, dk = q.shape
    dv = v.shape[-1]
    c = RET_CHUNK
    nc = s // c
    qc = q.reshape(b, h, nc, c, dk)
    kc = k.reshape(b, h, nc, c, dk)
    vc = v.reshape(b, h, nc, c, dv)
    j = jnp.arange(c, dtype=jnp.float32)
    lg = log_gamma[:, None]
    rel = j[:, None] - j[None, :]
    intra_decay = jnp.where(rel[None] >= 0, jnp.exp(jnp.maximum(rel, 0.0)[None] * log_gamma[:, None, None]), 0.0)
    scores = jnp.einsum('bhncd,bhnmd->bhncm', qc, kc) * intra_decay[None, :, None]
    intra = jnp.einsum('bhncm,bhnme->bhnce', scores, vc)
    key_w = jnp.exp((c - 1 - j)[None, :] * lg)
    chunk_kv = jnp.einsum('bhncd,bhnce->nbhde', kc * key_w[None, :, None, :, None], vc)
    chunk_decay = jnp.exp(c * log_gamma)[None, :, None, None]

    def step(state, kv):
        return chunk_decay * state + kv, state

    _, prev_states = lax.scan(step, jnp.zeros((b, h, dk, dv), jnp.float32), chunk_kv)
    query_w = jnp.exp((j + 1.0)[None, :] * lg)
    cross = jnp.einsum('bhncd,nbhde->bhnce', qc * query_w[None, :, None, :, None], prev_states)
    return (intra + cross).reshape(b, h, s, dv)


def retention_mixer(q, k, v, g, log_gamma):
    qh = rotary(to_heads(q))
    kh = rotary(to_heads(k)) * (HEAD_DIM ** -0.5)
    vh = to_heads(v).astype(jnp.float32)
    y = standardize(retention_chunkwise(qh, kh, vh, log_gamma))
    return (jax.nn.silu(g.astype(jnp.float32)) * from_heads(y)).astype(q.dtype)


def forgetting_attention(q, k, v, f_logit, b_f):
    log_f = jax.nn.log_sigmoid(f_logit.astype(jnp.float32) + b_f.astype(jnp.float32))
    cum = jnp.cumsum(log_f, axis=1).transpose(0, 2, 1)
    y = causal_block_attention(to_heads(q), to_heads(k), to_heads(v), HEAD_DIM ** -0.5, cum)
    return from_heads(y).astype(q.dtype)


def mla_mixer(c_q, c_kv, k_rope, g_q, w_uq, g_kv, w_ukv):
    b, s, _ = c_q.shape
    q = (rms_norm(c_q, g_q) @ w_uq).reshape(b, s, N_HEADS, MLA_NOPE + MLA_ROPE).transpose(0, 2, 1, 3)
    kv = (rms_norm(c_kv, g_kv) @ w_ukv).reshape(b, s, N_HEADS, MLA_NOPE + MLA_V).transpose(0, 2, 1, 3)
    q = jnp.concatenate([q[..., :MLA_NOPE].astype(jnp.float32), rotary(q[..., MLA_NOPE:])], axis=-1)
    k_r = jnp.broadcast_to(rotary(k_rope[:, None]), (b, N_HEADS, s, MLA_ROPE))
    k = jnp.concatenate([kv[..., :MLA_NOPE].astype(jnp.float32), k_r], axis=-1)
    v = kv[..., MLA_NOPE:]
    y = causal_block_attention(q, k, v, (MLA_NOPE + MLA_ROPE) ** -0.5)
    return from_heads(y).astype(c_q.dtype)


def setup_inputs(seed: int = 0) -> dict:
    key = jax.random.key(seed)
    ks = jax.random.split(key, 18)
    f32 = jnp.float32

    def nrm(k, shape, scale):
        return jax.random.normal(k, shape, f32) * scale

    def gain(k, shape):
        return 1.0 + 0.02 * jax.random.normal(k, shape, f32)

    return {
        'x': nrm(ks[0], (BATCH, SEQ, D_MODEL), 1.0),
        'g_mix_norm': gain(ks[1], (DEPTH, D_MODEL)),
        'w_in': nrm(ks[2], (DEPTH, D_MODEL, N_IN), D_MODEL ** -0.5),
        'b_forget': 2.0 + 0.1 * jax.random.normal(ks[3], (DEPTH, N_HEADS), f32),
        'g_sgu': gain(ks[4], (DEPTH, GROUP_WIDTH)),
        'w_spatial': nrm(ks[5], (DEPTH, N_HEADS, CHUNK, CHUNK), CHUNK ** -0.5),
        'b_spatial': gain(ks[6], (DEPTH, N_HEADS, CHUNK)),
        'g_mla_q': gain(ks[7], (DEPTH, MLA_Q_LORA)),
        'w_uq': nrm(ks[8], (DEPTH, MLA_Q_LORA, N_HEADS * (MLA_NOPE + MLA_ROPE)), MLA_Q_LORA ** -0.5),
        'g_mla_kv': gain(ks[9], (DEPTH, MLA_KV_LORA)),
        'w_ukv': nrm(ks[10], (DEPTH, MLA_KV_LORA, N_HEADS * (MLA_NOPE + MLA_V)), MLA_KV_LORA ** -0.5),
        'g_group_out': gain(ks[11], (DEPTH, D_MIX)),
        'w_out': nrm(ks[12], (DEPTH, D_MIX, D_MODEL), D_MIX ** -0.5),
        'g_ffn_norm': gain(ks[13], (DEPTH, D_MODEL)),
        'w_up': nrm(ks[14], (DEPTH, D_MODEL, D_FF), D_MODEL ** -0.5),
        'w_down': nrm(ks[15], (DEPTH, D_FF, D_MODEL), D_FF ** -0.5),
        'g_final': gain(ks[16], (D_MODEL,)),
    }


def reference(x, g_mix_norm, w_in, b_forget, g_sgu, w_spatial, b_spatial, g_mla_q, w_uq, g_mla_kv, w_ukv,
              g_group_out, w_out, g_ffn_norm, w_up, w_down, g_final):
    b, s, _ = x.shape
    log_gamma = jnp.log1p(-jnp.exp2(-5.0 - jnp.arange(N_HEADS, dtype=jnp.float32)))
    for l in range(DEPTH):
        h = rms_norm(x, g_mix_norm[l])
        z = h @ w_in[l]
        (a_uv, b_q, b_k, b_v, b_g, c_q, c_k, c_v, c_f, d_cq, d_ckv, d_kr) = jnp.split(z, IN_SPLIT_POINTS, axis=-1)
        y_a = spatial_gating_chunked(a_uv, g_sgu[l], w_spatial[l], b_spatial[l])
        y_b = retention_mixer(b_q, b_k, b_v, b_g, log_gamma)
        y_c = forgetting_attention(c_q, c_k, c_v, c_f, b_forget[l])
        y_d = mla_mixer(d_cq, d_ckv, d_kr, g_mla_q[l], w_uq[l], g_mla_kv[l], w_ukv[l])
        y = jnp.stack([y_a, y_b, y_c, y_d], axis=2)
        y = rms_norm(y, g_group_out[l].reshape(N_GROUPS, GROUP_WIDTH)).reshape(b, s, D_MIX)
        x = x + y @ w_out[l]
        h = rms_norm(x, g_ffn_norm[l])
        x = x + jnp.square(jax.nn.relu(h @ w_up[l])) @ w_down[l]
    return rms_norm(x, g_final)
```

```python
import functools
import math

import jax
import jax.numpy as jnp
from jax import lax
from jax.experimental import pallas as pl
from jax.experimental.pallas import tpu as pltpu

F32 = jnp.float32
BF16 = jnp.bfloat16

D_MODEL = 1024
N_HEADS = 4
HEAD_DIM = 64
GROUP_WIDTH = N_HEADS * HEAD_DIM
CHUNK = 128
MLA_Q_LORA = 256
MLA_KV_LORA = 128
MLA_NOPE = 64
MLA_ROPE = 32
ROPE_BASE = 10000.0
D_FF = 4 * D_MODEL
EPS = 1e-6

HEAD_PAD = 128
AUG_ROWS = 8
TOKEN_TILE = 512
ATTN_BLOCK = TOKEN_TILE
FF_CHUNK = 512
VMEM_LIMIT = 56 * 1024 * 1024

ROWS_A = 2 * GROUP_WIDTH
ROWS_B = 4 * GROUP_WIDTH
ROWS_C = 3 * GROUP_WIDTH
ROWS_D = MLA_Q_LORA + MLA_KV_LORA + MLA_ROPE
ROWS_F = 8
ROWS_IN = ROWS_A + ROWS_B + ROWS_C + ROWS_D + ROWS_F

NT = (((1,), (1,)), ((), ()))
TN = (((0,), (0,)), ((), ()))


def _params(*sem):
    return pltpu.CompilerParams(dimension_semantics=sem, vmem_limit_bytes=VMEM_LIMIT)


def _rot_half_rows(t, cos, sin):
    half = t.shape[0] // 2
    t1, t2 = t[:half], t[half:]
    return jnp.concatenate([t1 * cos - t2 * sin, t1 * sin + t2 * cos], axis=0)


def _standardize_rows(t):
    mu = jnp.mean(t, axis=0, keepdims=True)
    var = jnp.mean(jnp.square(t - mu), axis=0, keepdims=True)
    return (t - mu) * lax.rsqrt(var + EPS)


def _rms_rows(t):
    return t * lax.rsqrt(jnp.mean(t * t, axis=0, keepdims=True) + EPS)


def _inproj_body(x_ref, g_ref, w_ref, at_ref, bt_ref, ct_ref, dt_ref, ft_ref):
    x = x_ref[...]
    h = (x * lax.rsqrt(jnp.mean(x * x, axis=-1, keepdims=True) + EPS) * g_ref[...]).astype(BF16)

    def proj(r0, r1):
        return lax.dot_general(w_ref[r0:r1, :], h, NT, preferred_element_type=F32)

    r = 0
    at_ref[...] = proj(r, r + ROWS_A).astype(BF16)
    r += ROWS_A
    half_b = ROWS_B // 2
    bt_ref[0:half_b, :] = proj(r, r + half_b).astype(BF16)
    bt_ref[half_b:ROWS_B, :] = proj(r + half_b, r + ROWS_B).astype(BF16)
    r += ROWS_B
    ct_ref[...] = proj(r, r + ROWS_C).astype(BF16)
    r += ROWS_C
    zdf = proj(r, r + ROWS_D + ROWS_F)
    dt_ref[...] = zdf[0:ROWS_D].astype(BF16)
    ft_ref[...] = zdf[ROWS_D:ROWS_D + ROWS_F]


def _inproj(x, g, w_t):
    t = x.shape[0]
    tm = TOKEN_TILE
    rows = (ROWS_A, ROWS_B, ROWS_C, ROWS_D, ROWS_F)
    dts = (BF16, BF16, BF16, BF16, F32)
    return pl.pallas_call(
        _inproj_body,
        grid=(t // tm,),
        in_specs=[
            pl.BlockSpec((tm, D_MODEL), lambda i: (i, 0)),
            pl.BlockSpec((1, D_MODEL), lambda i: (0, 0)),
            pl.BlockSpec((ROWS_IN, D_MODEL), lambda i: (0, 0)),
        ],
        out_specs=[pl.BlockSpec((r, tm), lambda i: (0, i)) for r in rows],
        out_shape=[jax.ShapeDtypeStruct((r, t), d) for r, d in zip(rows, dts)],
        compiler_params=_params("parallel"),
        name="inproj",
    )(x, g, w_t)


def _sgu_body(at_ref, gain_ref, ws_ref, bs_ref, o_ref):
    tm = at_ref.shape[1]
    nch = tm // CHUNK
    row = lax.broadcasted_iota(jnp.int32, (CHUNK, CHUNK), 0)
    col = lax.broadcasted_iota(jnp.int32, (CHUNK, CHUNK), 1)
    for h in range(N_HEADS):
        r0, r1 = h * HEAD_DIM, (h + 1) * HEAD_DIM
        u = jax.nn.gelu(at_ref[r0:r1, :].astype(F32))
        v = jax.nn.gelu(at_ref[GROUP_WIDTH + r0:GROUP_WIDTH + r1, :].astype(F32))
        v = _standardize_rows(v) * gain_ref[r0:r1, :]
        w = jnp.where(col <= row, ws_ref[h], 0.0).astype(BF16)
        vs = jnp.concatenate([v[:, c * CHUNK:(c + 1) * CHUNK] for c in range(nch)], axis=0).astype(BF16)
        mixed = lax.dot_general(vs, w, NT, preferred_element_type=F32) + bs_ref[h:h + 1, :]
        for c in range(nch):
            o_ref[r0:r1, c * CHUNK:(c + 1) * CHUNK] = (
                u[:, c * CHUNK:(c + 1) * CHUNK] * mixed[c * HEAD_DIM:(c + 1) * HEAD_DIM])


def _sgu(at, gain_col, w_s, b_s):
    t = at.shape[1]
    tm = TOKEN_TILE
    return pl.pallas_call(
        _sgu_body,
        grid=(t // tm,),
        in_specs=[
            pl.BlockSpec((ROWS_A, tm), lambda i: (0, i)),
            pl.BlockSpec((GROUP_WIDTH, 1), lambda i: (0, 0)),
            pl.BlockSpec((N_HEADS, CHUNK, CHUNK), lambda i: (0, 0, 0)),
            pl.BlockSpec((N_HEADS, CHUNK), lambda i: (0, 0)),
        ],
        out_specs=pl.BlockSpec((GROUP_WIDTH, tm), lambda i: (0, i)),
        out_shape=jax.ShapeDtypeStruct((GROUP_WIDTH, t), F32),
        compiler_params=_params("parallel"),
        name="sgu",
    )(at, gain_col, w_s, b_s)


def _ret_body(bt_ref, cos_ref, sin_ref, o_ref, st_ref):
    tm = bt_ref.shape[1]
    nch = tm // CHUNK

    @pl.when(pl.program_id(1) == 0)
    def _():
        st_ref[...] = jnp.zeros_like(st_ref)

    cos, sin = cos_ref[...], sin_ref[...]
    srow = lax.broadcasted_iota(jnp.int32, (CHUNK, CHUNK), 0)
    tcol = lax.broadcasted_iota(jnp.int32, (CHUNK, CHUNK), 1)
    rel = (tcol - srow).astype(F32)
    j = lax.broadcasted_iota(jnp.int32, (1, CHUNK), 1).astype(F32)
    for h in range(N_HEADS):
        log_gamma = math.log1p(-(2.0 ** (-5.0 - h)))
        dec_t = jnp.where(rel >= 0, jnp.exp(jnp.maximum(rel, 0.0) * log_gamma), 0.0)
        query_w = jnp.exp((j + 1.0) * log_gamma)
        key_w = jnp.exp((CHUNK - 1.0 - j) * log_gamma)
        chunk_decay = math.exp(CHUNK * log_gamma)
        r0, r1 = h * HEAD_DIM, (h + 1) * HEAD_DIM
        q = _rot_half_rows(bt_ref[r0:r1, :].astype(F32), cos, sin)
        k = _rot_half_rows(bt_ref[GROUP_WIDTH + r0:GROUP_WIDTH + r1, :].astype(F32), cos, sin) * (HEAD_DIM ** -0.5)
        v = bt_ref[2 * GROUP_WIDTH + r0:2 * GROUP_WIDTH + r1, :].astype(F32)
        g = bt_ref[3 * GROUP_WIDTH + r0:3 * GROUP_WIDTH + r1, :].astype(F32)
        st = st_ref[h]
        ys = []
        for c in range(nch):
            sl = slice(c * CHUNK, (c + 1) * CHUNK)
            qc, kc, vc = q[:, sl], k[:, sl], v[:, sl]
            kcb = kc.astype(BF16)
            a_t = lax.dot_general(kcb, qc.astype(BF16), TN, preferred_element_type=F32)
            p_t = (a_t * dec_t).astype(BF16)
            intra = jnp.dot(vc.astype(BF16), p_t, preferred_element_type=F32)
            cross = jnp.dot(st.astype(BF16), (qc * query_w).astype(BF16), preferred_element_type=F32)
            ys.append(intra + cross)
            st = chunk_decay * st + lax.dot_general((vc * key_w).astype(BF16), kcb, NT,
                                                    preferred_element_type=F32)
        st_ref[h] = st
        y = _standardize_rows(jnp.concatenate(ys, axis=1))
        o_ref[r0:r1, :] = jax.nn.silu(g) * y


def _retention(bt, cos_t, sin_t, batch):
    t = bt.shape[1]
    tm = TOKEN_TILE
    ns = t // batch // tm
    half = HEAD_DIM // 2
    return pl.pallas_call(
        _ret_body,
        grid=(batch, ns),
        in_specs=[
            pl.BlockSpec((ROWS_B, tm), lambda b, s: (0, b * ns + s)),
            pl.BlockSpec((half, tm), lambda b, s: (0, s)),
            pl.BlockSpec((half, tm), lambda b, s: (0, s)),
        ],
        out_specs=pl.BlockSpec((GROUP_WIDTH, tm), lambda b, s: (0, b * ns + s)),
        out_shape=jax.ShapeDtypeStruct((GROUP_WIDTH, t), F32),
        scratch_shapes=[pltpu.VMEM((N_HEADS, HEAD_DIM, HEAD_DIM), F32)],
        compiler_params=_params("parallel", "arbitrary"),
        name="retention",
    )(bt, cos_t, sin_t)


def _select_rows(rows):
    n = rows[0].shape[1]
    ridx = lax.broadcasted_iota(jnp.int32, (AUG_ROWS, n), 0)
    out = jnp.zeros((AUG_ROWS, n), F32)
    for i, r in enumerate(rows):
        out = jnp.where(ridx == i, jnp.broadcast_to(r, (AUG_ROWS, n)), out)
    return out


def _group(feat, extra):
    n = feat.shape[1]
    pad = HEAD_PAD - feat.shape[0] - extra.shape[0]
    return jnp.concatenate([feat, extra, jnp.zeros((pad, n), F32)], axis=0)


def _fox_prep_body(ct_ref, ft_ref, bf_ref, qt_ref, k_ref, vt_ref, carry_ref):
    tm = ct_ref.shape[1]

    @pl.when(pl.program_id(1) == 0)
    def _():
        carry_ref[...] = jnp.zeros_like(carry_ref)

    x = ft_ref[...] + bf_ref[...]
    lf = jnp.minimum(x, 0.0) - jnp.log1p(jnp.exp(-jnp.abs(x)))
    hi = lf.astype(BF16)
    mid = (lf - hi.astype(F32)).astype(BF16)
    lo = (lf - hi.astype(F32) - mid.astype(F32)).astype(BF16)
    srow = lax.broadcasted_iota(jnp.int32, (tm, tm), 0)
    tcol = lax.broadcasted_iota(jnp.int32, (tm, tm), 1)
    upper = jnp.where(srow <= tcol, 1.0, 0.0).astype(BF16)
    parts = jnp.dot(jnp.concatenate([hi, mid, lo], axis=0), upper, preferred_element_type=F32)
    cum = parts[0:8] + parts[8:16] + parts[16:24] + carry_ref[...]
    carry_ref[...] = cum[:, tm - 1:tm]

    c_hi = cum.astype(BF16).astype(F32)
    c_mid = (cum - c_hi).astype(BF16).astype(F32)
    c_lo = (cum - c_hi - c_mid).astype(BF16).astype(F32)
    one = jnp.ones((1, tm), F32)
    ones_row = _select_rows([one])
    for h in range(N_HEADS):
        r0, r1 = h * HEAD_DIM, (h + 1) * HEAD_DIM
        g0, g1 = h * HEAD_PAD, (h + 1) * HEAD_PAD
        q = ct_ref[r0:r1, :].astype(F32) * (HEAD_DIM ** -0.5)
        k = ct_ref[GROUP_WIDTH + r0:GROUP_WIDTH + r1, :].astype(F32)
        v = ct_ref[2 * GROUP_WIDTH + r0:2 * GROUP_WIDTH + r1, :].astype(F32)
        ch, cm, cl = c_hi[h:h + 1], c_mid[h:h + 1], c_lo[h:h + 1]
        q_extra = _select_rows([ch, cm, cl, one, one, one])
        k_extra = _select_rows([one, one, one, -ch, -cm, -cl])
        qt_ref[g0:g1, :] = _group(q, q_extra).astype(BF16)
        k_ref[:, g0:g1] = _group(k, k_extra).T.astype(BF16)
        vt_ref[0, h, 0] = _group(v, ones_row).astype(BF16)


def _fox_prep(ct, ft, bf_col, batch):
    t = ct.shape[1]
    tm = TOKEN_TILE
    ns = t // batch // tm
    return pl.pallas_call(
        _fox_prep_body,
        grid=(batch, ns),
        in_specs=[
            pl.BlockSpec((ROWS_C, tm), lambda b, s: (0, b * ns + s)),
            pl.BlockSpec((ROWS_F, tm), lambda b, s: (0, b * ns + s)),
            pl.BlockSpec((ROWS_F, 1), lambda b, s: (0, 0)),
        ],
        out_specs=[
            pl.BlockSpec((N_HEADS * HEAD_PAD, tm), lambda b, s: (0, b * ns + s)),
            pl.BlockSpec((tm, N_HEADS * HEAD_PAD), lambda b, s: (b * ns + s, 0)),
            pl.BlockSpec((1, N_HEADS, 1, HEAD_PAD, tm), lambda b, s: (b, 0, s, 0, 0)),
        ],
        out_shape=[
            jax.ShapeDtypeStruct((N_HEADS * HEAD_PAD, t), BF16),
            jax.ShapeDtypeStruct((t, N_HEADS * HEAD_PAD), BF16),
            jax.ShapeDtypeStruct((batch, N_HEADS, ns, HEAD_PAD, tm), BF16),
        ],
        scratch_shapes=[pltpu.VMEM((ROWS_F, 1), F32)],
        compiler_params=_params("parallel", "arbitrary"),
        name="fox_prep",
    )(ct, ft, bf_col)


def _mla_prep_body(dt_ref, cos_ref, sin_ref, gq_ref, gkv_ref, wq_ref, wk_ref, wv_ref,
                   qt_ref, k_ref, vt_ref):
    tm = dt_ref.shape[1]
    cos, sin = cos_ref[...], sin_ref[...]
    cq = _rms_rows(dt_ref[0:MLA_Q_LORA, :].astype(F32)) * gq_ref[...]
    ckv = _rms_rows(dt_ref[MLA_Q_LORA:MLA_Q_LORA + MLA_KV_LORA, :].astype(F32)) * gkv_ref[...]
    kr = _rot_half_rows(dt_ref[MLA_Q_LORA + MLA_KV_LORA:ROWS_D, :].astype(F32), cos, sin)
    ckv_b = ckv.astype(BF16)
    q_all = jnp.dot(wq_ref[...], cq.astype(BF16), preferred_element_type=F32)
    q_all = q_all * ((MLA_NOPE + MLA_ROPE) ** -0.5)
    k_all = jnp.dot(wk_ref[...], ckv_b, preferred_element_type=F32)
    v_all = jnp.dot(wv_ref[...], ckv_b, preferred_element_type=F32)
    ones_row = _select_rows([jnp.ones((1, tm), F32)])
    zeros_q = jnp.zeros((HEAD_PAD - MLA_NOPE - MLA_ROPE, tm), F32)
    for h in range(N_HEADS):
        r0, r1 = h * HEAD_DIM, (h + 1) * HEAD_DIM
        g0, g1 = h * HEAD_PAD, (h + 1) * HEAD_PAD
        qg = q_all[g0:g1]
        q_rope = _rot_half_rows(qg[MLA_NOPE:MLA_NOPE + MLA_ROPE], cos, sin)
        qt_ref[g0:g1, :] = jnp.concatenate([qg[0:MLA_NOPE], q_rope, zeros_q], axis=0).astype(BF16)
        k_grp = jnp.concatenate([k_all[r0:r1], kr, zeros_q], axis=0)
        k_ref[:, g0:g1] = k_grp.T.astype(BF16)
        vt_ref[0, h, 0] = _group(v_all[r0:r1], ones_row).astype(BF16)


def _mla_prep(dt, cos_t, sin_t, gq_col, gkv_col, wq_t, wk_t, wv_t, batch):
    t = dt.shape[1]
    tm = TOKEN_TILE
    ns = t // batch // tm
    half = MLA_ROPE // 2
    const = lambda shape: pl.BlockSpec(shape, lambda b, s: (0,) * len(shape))
    return pl.pallas_call(
        _mla_prep_body,
        grid=(batch, ns),
        in_specs=[
            pl.BlockSpec((ROWS_D, tm), lambda b, s: (0, b * ns + s)),
            pl.BlockSpec((half, tm), lambda b, s: (0, s)),
            pl.BlockSpec((half, tm), lambda b, s: (0, s)),
            const((MLA_Q_LORA, 1)),
            const((MLA_KV_LORA, 1)),
            const((N_HEADS * HEAD_PAD, MLA_Q_LORA)),
            const((GROUP_WIDTH, MLA_KV_LORA)),
            const((GROUP_WIDTH, MLA_KV_LORA)),
        ],
        out_specs=[
            pl.BlockSpec((N_HEADS * HEAD_PAD, tm), lambda b, s: (0, b * ns + s)),
            pl.BlockSpec((tm, N_HEADS * HEAD_PAD), lambda b, s: (b * ns + s, 0)),
            pl.BlockSpec((1, N_HEADS, 1, HEAD_PAD, tm), lambda b, s: (b, 0, s, 0, 0)),
        ],
        out_shape=[
            jax.ShapeDtypeStruct((N_HEADS * HEAD_PAD, t), BF16),
            jax.ShapeDtypeStruct((t, N_HEADS * HEAD_PAD), BF16),
            jax.ShapeDtypeStruct((batch, N_HEADS, ns, HEAD_PAD, tm), BF16),
        ],
        compiler_params=_params("parallel", "parallel"),
        name="mla_prep",
    )(dt, cos_t, sin_t, gq_col, gkv_col, wq_t, wk_t, wv_t)


def _attn_body(qt_ref, k_ref, vt_ref, o_ref, acc_ref, m_ref):
    blk = qt_ref.shape[1]
    qi = pl.program_id(2)
    qt = qt_ref[...]
    m_ref[...] = jnp.full(m_ref.shape, -jnp.inf, F32)
    acc_ref[...] = jnp.zeros_like(acc_ref)

    def step(kj, diagonal):
        start = pl.multiple_of(kj * blk, blk)
        s = jnp.dot(k_ref[pl.ds(start, blk), :], qt, preferred_element_type=F32)
        if diagonal:
            key_pos = lax.broadcasted_iota(jnp.int32, (blk, blk), 0)
            qry_pos = lax.broadcasted_iota(jnp.int32, (blk, blk), 1)
            s = jnp.where(key_pos <= qry_pos, s, -jnp.inf)
        m_old = m_ref[...]
        m_new = jnp.maximum(m_old, jnp.max(s, axis=0, keepdims=True))
        p = jnp.exp(s - m_new)
        alpha = jnp.exp(m_old - m_new)
        pv = jnp.dot(vt_ref[0, 0, kj], p.astype(BF16), preferred_element_type=F32)
        acc_ref[...] = acc_ref[...] * alpha + pv[0:HEAD_DIM + AUG_ROWS]
        m_ref[...] = m_new

    def full_step(kj, carry):
        step(kj, False)
        return carry

    lax.fori_loop(0, qi, full_step, 0)
    step(qi, True)
    acc = acc_ref[...]
    o_ref[...] = acc[0:HEAD_DIM] / acc[HEAD_DIM:HEAD_DIM + 1]


def _attention(qt, k, vt):
    batch, _, nb, _, blk = vt.shape
    t = qt.shape[1]
    seq = t // batch
    return pl.pallas_call(
        _attn_body,
        grid=(batch, N_HEADS, nb),
        in_specs=[
            pl.BlockSpec((HEAD_PAD, blk), lambda b, h, i: (h, b * nb + i)),
            pl.BlockSpec((seq, HEAD_PAD), lambda b, h, i: (b, h)),
            pl.BlockSpec((1, 1, nb, HEAD_PAD, blk), lambda b, h, i: (b, h, 0, 0, 0)),
        ],
        out_specs=pl.BlockSpec((HEAD_DIM, blk), lambda b, h, i: (h, b * nb + i)),
        out_shape=jax.ShapeDtypeStruct((GROUP_WIDTH, t), F32),
        scratch_shapes=[pltpu.VMEM((HEAD_DIM + AUG_ROWS, blk), F32), pltpu.VMEM((1, blk), F32)],
        compiler_params=_params("parallel", "parallel", "arbitrary"),
        name="attention",
    )(qt, k, vt)


def _post_body(x_ref, ya_ref, yb_ref, yc_ref, yd_ref, gg_ref, wo_ref, gf_ref, wu_ref, wd_ref,
               gl_ref, o_ref, *, final):
    ys = []
    for g, y_ref in enumerate((ya_ref, yb_ref, yc_ref, yd_ref)):
        ys.append((_rms_rows(y_ref[...]) * gg_ref[g]).astype(BF16))
    y = jnp.concatenate(ys, axis=0)
    x = x_ref[...] + lax.dot_general(y, wo_ref[...], TN, preferred_element_type=F32)
    h = (x * lax.rsqrt(jnp.mean(x * x, axis=-1, keepdims=True) + EPS) * gf_ref[...]).astype(BF16)
    acc = x
    for c in range(D_FF // FF_CHUNK):
        c0, c1 = c * FF_CHUNK, (c + 1) * FF_CHUNK
        a = jnp.maximum(jnp.dot(h, wu_ref[:, c0:c1], preferred_element_type=F32), 0.0)
        acc = acc + jnp.dot((a * a).astype(BF16), wd_ref[c0:c1, :], preferred_element_type=F32)
    if final:
        acc = acc * lax.rsqrt(jnp.mean(acc * acc, axis=-1, keepdims=True) + EPS) * gl_ref[...]
    o_ref[...] = acc


def _post(x, ya, yb, yc, yd, gg_col, w_out, g_ffn, w_up, w_down, g_final, final):
    t = x.shape[0]
    tm = TOKEN_TILE
    const = lambda shape: pl.BlockSpec(shape, lambda i: (0,) * len(shape), pipeline_mode=pl.Buffered(1))
    ytile = pl.BlockSpec((GROUP_WIDTH, tm), lambda i: (0, i))
    return pl.pallas_call(
        functools.partial(_post_body, final=final),
        grid=(t // tm,),
        in_specs=[
            pl.BlockSpec((tm, D_MODEL), lambda i: (i, 0)),
            ytile, ytile, ytile, ytile,
            const((4, GROUP_WIDTH, 1)),
            const((D_MODEL, D_MODEL)),
            const((1, D_MODEL)),
            const((D_MODEL, D_FF)),
            const((D_FF, D_MODEL)),
            const((1, D_MODEL)),
        ],
        out_specs=pl.BlockSpec((tm, D_MODEL), lambda i: (i, 0)),
        out_shape=jax.ShapeDtypeStruct((t, D_MODEL), F32),
        compiler_params=_params("parallel"),
        name="post",
    )(x, ya, yb, yc, yd, gg_col, w_out, g_ffn, w_up, w_down, g_final)


def _rope_tables(seq, half):
    inv_freq = jnp.power(ROPE_BASE, -jnp.arange(half, dtype=F32) / half)
    ang = inv_freq[:, None] * jnp.arange(seq, dtype=F32)[None, :]
    return jnp.cos(ang), jnp.sin(ang)


def _inproj_weight(w_in):
    wt = w_in.T
    n_abc = ROWS_A + ROWS_B + ROWS_C
    f = wt[n_abc:n_abc + N_HEADS]
    d = wt[n_abc + N_HEADS:n_abc + N_HEADS + ROWS_D]
    f = jnp.concatenate([f, jnp.zeros((ROWS_F - N_HEADS, D_MODEL), w_in.dtype)], axis=0)
    return jnp.concatenate([wt[:n_abc], d, f], axis=0).astype(BF16)


def _mla_weights(w_uq, w_ukv):
    wq = w_uq.T.reshape(N_HEADS, MLA_NOPE + MLA_ROPE, MLA_Q_LORA)
    wq = jnp.pad(wq, ((0, 0), (0, HEAD_PAD - MLA_NOPE - MLA_ROPE), (0, 0)))
    wq = wq.reshape(N_HEADS * HEAD_PAD, MLA_Q_LORA).astype(BF16)
    wkv = w_ukv.T.reshape(N_HEADS, 2 * HEAD_DIM, MLA_KV_LORA)
    wk = wkv[:, :HEAD_DIM].reshape(GROUP_WIDTH, MLA_KV_LORA).astype(BF16)
    wv = wkv[:, HEAD_DIM:].reshape(GROUP_WIDTH, MLA_KV_LORA).astype(BF16)
    return wq, wk, wv


def kernel(x, g_mix_norm, w_in, b_forget, g_sgu, w_spatial, b_spatial, g_mla_q, w_uq, g_mla_kv, w_ukv,
           g_group_out, w_out, g_ffn_norm, w_up, w_down, g_final):
    batch, seq, _ = x.shape
    depth = w_in.shape[0]
    assert seq % TOKEN_TILE == 0 and ATTN_BLOCK == TOKEN_TILE
    cos_b, sin_b = _rope_tables(seq, HEAD_DIM // 2)
    cos_d, sin_d = _rope_tables(seq, MLA_ROPE // 2)
    xf = x.reshape(batch * seq, D_MODEL)
    for l in range(depth):
        at, bt, ct, dt, ft = _inproj(xf, g_mix_norm[l][None, :], _inproj_weight(w_in[l]))
        ya = _sgu(at, g_sgu[l][:, None], w_spatial[l], b_spatial[l])
        yb = _retention(bt, cos_b, sin_b, batch)
        bf_col = jnp.pad(b_forget[l], (0, ROWS_F - N_HEADS))[:, None]
        yc = _attention(*_fox_prep(ct, ft, bf_col, batch))
        wq, wk, wv = _mla_weights(w_uq[l], w_ukv[l])
        yd = _attention(*_mla_prep(dt, cos_d, sin_d, g_mla_q[l][:, None], g_mla_kv[l][:, None],
                                   wq, wk, wv, batch))
        xf = _post(xf, ya, yb, yc, yd, g_group_out[l].reshape(4, GROUP_WIDTH, 1),
                   w_out[l].astype(BF16), g_ffn_norm[l][None, :], w_up[l].astype(BF16),
                   w_down[l].astype(BF16), g_final[None, :], final=(l == depth - 1))
    return xf.reshape(batch, seq, D_MODEL)
```

```python
import functools
import math

import jax
import jax.numpy as jnp
from jax import lax
from jax.experimental import pallas as pl
from jax.experimental.pallas import tpu as pltpu

F32 = jnp.float32
BF16 = jnp.bfloat16

D_MODEL = 1024
N_HEADS = 4
HEAD_DIM = 64
GROUP_WIDTH = N_HEADS * HEAD_DIM
CHUNK = 128
MLA_Q_LORA = 256
MLA_KV_LORA = 128
MLA_NOPE = 64
MLA_ROPE = 32
ROPE_BASE = 10000.0
D_FF = 4 * D_MODEL
EPS = 1e-6

HEAD_PAD = 128
AUG_ROWS = 8
PV_ROWS = 80
LOG2E = math.log2(math.e)
TOKEN_TILE = 512
ATTN_BLOCK = TOKEN_TILE
FF_CHUNK = 512
VMEM_LIMIT = 56 * 1024 * 1024

ROWS_A = 2 * GROUP_WIDTH
ROWS_B = 4 * GROUP_WIDTH
ROWS_C = 3 * GROUP_WIDTH
ROWS_D = MLA_Q_LORA + MLA_KV_LORA + MLA_ROPE
ROWS_F = 8
ROWS_IN = ROWS_A + ROWS_B + ROWS_C + ROWS_D + ROWS_F

NT = (((1,), (1,)), ((), ()))
TN = (((0,), (0,)), ((), ()))


def _params(*sem):
    return pltpu.CompilerParams(dimension_semantics=sem, vmem_limit_bytes=VMEM_LIMIT)


def _rot_half_rows(t, cos, sin):
    half = t.shape[0] // 2
    t1, t2 = t[:half], t[half:]
    return jnp.concatenate([t1 * cos - t2 * sin, t1 * sin + t2 * cos], axis=0)


def _standardize_rows(t):
    mu = jnp.mean(t, axis=0, keepdims=True)
    var = jnp.mean(jnp.square(t - mu), axis=0, keepdims=True)
    return (t - mu) * lax.rsqrt(var + EPS)


def _rms_rows(t):
    return t * lax.rsqrt(jnp.mean(t * t, axis=0, keepdims=True) + EPS)


def _inproj_body(x_ref, g_ref, w_ref, at_ref, bt_ref, ct_ref, dt_ref, ft_ref):
    x = x_ref[...]
    h = (x * lax.rsqrt(jnp.mean(x * x, axis=-1, keepdims=True) + EPS) * g_ref[...]).astype(BF16)

    def proj(r0, r1):
        return lax.dot_general(w_ref[r0:r1, :], h, NT, preferred_element_type=F32)

    r = 0
    at_ref[...] = proj(r, r + ROWS_A).astype(BF16)
    r += ROWS_A
    half_b = ROWS_B // 2
    bt_ref[0:half_b, :] = proj(r, r + half_b).astype(BF16)
    bt_ref[half_b:ROWS_B, :] = proj(r + half_b, r + ROWS_B).astype(BF16)
    r += ROWS_B
    ct_ref[...] = proj(r, r + ROWS_C).astype(BF16)
    r += ROWS_C
    zdf = proj(r, r + ROWS_D + ROWS_F)
    dt_ref[...] = zdf[0:ROWS_D].astype(BF16)
    ft_ref[...] = zdf[ROWS_D:ROWS_D + ROWS_F]


def _inproj(x, g, w_t):
    t = x.shape[0]
    tm = TOKEN_TILE
    rows = (ROWS_A, ROWS_B, ROWS_C, ROWS_D, ROWS_F)
    dts = (BF16, BF16, BF16, BF16, F32)
    return pl.pallas_call(
        _inproj_body,
        grid=(t // tm,),
        in_specs=[
            pl.BlockSpec((tm, D_MODEL), lambda i: (i, 0)),
            pl.BlockSpec((1, D_MODEL), lambda i: (0, 0)),
            pl.BlockSpec((ROWS_IN, D_MODEL), lambda i: (0, 0)),
        ],
        out_specs=[pl.BlockSpec((r, tm), lambda i: (0, i)) for r in rows],
        out_shape=[jax.ShapeDtypeStruct((r, t), d) for r, d in zip(rows, dts)],
        compiler_params=_params("parallel"),
        name="inproj",
    )(x, g, w_t)


def _sgu_body(at_ref, gain_ref, ws_ref, bs_ref, o_ref):
    tm = at_ref.shape[1]
    nch = tm // CHUNK
    row = lax.broadcasted_iota(jnp.int32, (CHUNK, CHUNK), 0)
    col = lax.broadcasted_iota(jnp.int32, (CHUNK, CHUNK), 1)
    for h in range(N_HEADS):
        r0, r1 = h * HEAD_DIM, (h + 1) * HEAD_DIM
        u = jax.nn.gelu(at_ref[r0:r1, :].astype(F32))
        v = jax.nn.gelu(at_ref[GROUP_WIDTH + r0:GROUP_WIDTH + r1, :].astype(F32))
        v = _standardize_rows(v) * gain_ref[r0:r1, :]
        w = jnp.where(col <= row, ws_ref[h], 0.0).astype(BF16)
        vs = jnp.concatenate([v[:, c * CHUNK:(c + 1) * CHUNK] for c in range(nch)], axis=0).astype(BF16)
        mixed = lax.dot_general(vs, w, NT, preferred_element_type=F32) + bs_ref[h:h + 1, :]
        for c in range(nch):
            o_ref[r0:r1, c * CHUNK:(c + 1) * CHUNK] = (
                u[:, c * CHUNK:(c + 1) * CHUNK] * mixed[c * HEAD_DIM:(c + 1) * HEAD_DIM])


def _sgu(at, gain_col, w_s, b_s):
    t = at.shape[1]
    tm = TOKEN_TILE
    return pl.pallas_call(
        _sgu_body,
        grid=(t // tm,),
        in_specs=[
            pl.BlockSpec((ROWS_A, tm), lambda i: (0, i)),
            pl.BlockSpec((GROUP_WIDTH, 1), lambda i: (0, 0)),
            pl.BlockSpec((N_HEADS, CHUNK, CHUNK), lambda i: (0, 0, 0)),
            pl.BlockSpec((N_HEADS, CHUNK), lambda i: (0, 0)),
        ],
        out_specs=pl.BlockSpec((GROUP_WIDTH, tm), lambda i: (0, i)),
        out_shape=jax.ShapeDtypeStruct((GROUP_WIDTH, t), F32),
        compiler_params=_params("parallel"),
        name="sgu",
    )(at, gain_col, w_s, b_s)


def _ret_body(bt_ref, cos_ref, sin_ref, o_ref, st_ref):
    tm = bt_ref.shape[1]
    nch = tm // CHUNK

    @pl.when(pl.program_id(1) == 0)
    def _():
        st_ref[...] = jnp.zeros_like(st_ref)

    cos, sin = cos_ref[...], sin_ref[...]
    srow = lax.broadcasted_iota(jnp.int32, (CHUNK, CHUNK), 0)
    tcol = lax.broadcasted_iota(jnp.int32, (CHUNK, CHUNK), 1)
    rel = (tcol - srow).astype(F32)
    j = lax.broadcasted_iota(jnp.int32, (1, CHUNK), 1).astype(F32)
    for h in range(N_HEADS):
        log_gamma = math.log1p(-(2.0 ** (-5.0 - h)))
        dec_t = jnp.where(rel >= 0, jnp.exp(jnp.maximum(rel, 0.0) * log_gamma), 0.0)
        query_w = jnp.exp((j + 1.0) * log_gamma)
        key_w = jnp.exp((CHUNK - 1.0 - j) * log_gamma)
        chunk_decay = math.exp(CHUNK * log_gamma)
        r0, r1 = h * HEAD_DIM, (h + 1) * HEAD_DIM
        q = _rot_half_rows(bt_ref[r0:r1, :].astype(F32), cos, sin)
        k = _rot_half_rows(bt_ref[GROUP_WIDTH + r0:GROUP_WIDTH + r1, :].astype(F32), cos, sin) * (HEAD_DIM ** -0.5)
        v = bt_ref[2 * GROUP_WIDTH + r0:2 * GROUP_WIDTH + r1, :].astype(F32)
        g = bt_ref[3 * GROUP_WIDTH + r0:3 * GROUP_WIDTH + r1, :].astype(F32)
        st = st_ref[h]
        ys = []
        for c in range(nch):
            sl = slice(c * CHUNK, (c + 1) * CHUNK)
            qc, kc, vc = q[:, sl], k[:, sl], v[:, sl]
            kcb = kc.astype(BF16)
            a_t = lax.dot_general(kcb, qc.astype(BF16), TN, preferred_element_type=F32)
            p_t = (a_t * dec_t).astype(BF16)
            intra = jnp.dot(vc.astype(BF16), p_t, preferred_element_type=F32)
            cross = jnp.dot(st.astype(BF16), (qc * query_w).astype(BF16), preferred_element_type=F32)
            ys.append(intra + cross)
            st = chunk_decay * st + lax.dot_general((vc * key_w).astype(BF16), kcb, NT,
                                                    preferred_element_type=F32)
        st_ref[h] = st
        y = _standardize_rows(jnp.concatenate(ys, axis=1))
        o_ref[r0:r1, :] = jax.nn.silu(g) * y


def _retention(bt, cos_t, sin_t, batch):
    t = bt.shape[1]
    tm = TOKEN_TILE
    ns = t // batch // tm
    half = HEAD_DIM // 2
    return pl.pallas_call(
        _ret_body,
        grid=(batch, ns),
        in_specs=[
            pl.BlockSpec((ROWS_B, tm), lambda b, s: (0, b * ns + s)),
            pl.BlockSpec((half, tm), lambda b, s: (0, s)),
            pl.BlockSpec((half, tm), lambda b, s: (0, s)),
        ],
        out_specs=pl.BlockSpec((GROUP_WIDTH, tm), lambda b, s: (0, b * ns + s)),
        out_shape=jax.ShapeDtypeStruct((GROUP_WIDTH, t), F32),
        scratch_shapes=[pltpu.VMEM((N_HEADS, HEAD_DIM, HEAD_DIM), F32)],
        compiler_params=_params("parallel", "arbitrary"),
        name="retention",
    )(bt, cos_t, sin_t)


def _select_rows(rows):
    n = rows[0].shape[1]
    ridx = lax.broadcasted_iota(jnp.int32, (AUG_ROWS, n), 0)
    out = jnp.zeros((AUG_ROWS, n), F32)
    for i, r in enumerate(rows):
        out = jnp.where(ridx == i, jnp.broadcast_to(r, (AUG_ROWS, n)), out)
    return out


def _group(feat, extra):
    n = feat.shape[1]
    pad = HEAD_PAD - feat.shape[0] - extra.shape[0]
    return jnp.concatenate([feat, extra, jnp.zeros((pad, n), F32)], axis=0)


def _fox_prep_body(ct_ref, ft_ref, bf_ref, qt_ref, k_ref, vt_ref, carry_ref):
    tm = ct_ref.shape[1]

    @pl.when(pl.program_id(1) == 0)
    def _():
        carry_ref[...] = jnp.zeros_like(carry_ref)

    x = ft_ref[...] + bf_ref[...]
    lf = jnp.minimum(x, 0.0) - jnp.log1p(jnp.exp(-jnp.abs(x)))
    hi = lf.astype(BF16)
    mid = (lf - hi.astype(F32)).astype(BF16)
    lo = (lf - hi.astype(F32) - mid.astype(F32)).astype(BF16)
    srow = lax.broadcasted_iota(jnp.int32, (tm, tm), 0)
    tcol = lax.broadcasted_iota(jnp.int32, (tm, tm), 1)
    upper = jnp.where(srow <= tcol, 1.0, 0.0).astype(BF16)
    parts = jnp.dot(jnp.concatenate([hi, mid, lo], axis=0), upper, preferred_element_type=F32)
    cum = parts[0:8] + parts[8:16] + parts[16:24] + carry_ref[...]
    carry_ref[...] = cum[:, tm - 1:tm]

    cum2 = cum * LOG2E
    c_hi = cum2.astype(BF16).astype(F32)
    c_mid = (cum2 - c_hi).astype(BF16).astype(F32)
    c_lo = (cum2 - c_hi - c_mid).astype(BF16).astype(F32)
    one = jnp.ones((1, tm), F32)
    ones_row = _select_rows([one])
    for h in range(N_HEADS):
        r0, r1 = h * HEAD_DIM, (h + 1) * HEAD_DIM
        g0, g1 = h * HEAD_PAD, (h + 1) * HEAD_PAD
        q = ct_ref[r0:r1, :].astype(F32) * (HEAD_DIM ** -0.5 * LOG2E)
        k = ct_ref[GROUP_WIDTH + r0:GROUP_WIDTH + r1, :].astype(F32)
        v = ct_ref[2 * GROUP_WIDTH + r0:2 * GROUP_WIDTH + r1, :].astype(F32)
        ch, cm, cl = c_hi[h:h + 1], c_mid[h:h + 1], c_lo[h:h + 1]
        q_extra = _select_rows([ch, cm, cl, one, one, one])
        k_extra = _select_rows([one, one, one, -ch, -cm, -cl])
        qt_ref[g0:g1, :] = _group(q, q_extra).astype(BF16)
        k_ref[:, g0:g1] = _group(k, k_extra).T.astype(BF16)
        vt_ref[0, h, 0] = _group(v, ones_row).astype(BF16)


def _fox_prep(ct, ft, bf_col, batch):
    t = ct.shape[1]
    tm = TOKEN_TILE
    ns = t // batch // tm
    return pl.pallas_call(
        _fox_prep_body,
        grid=(batch, ns),
        in_specs=[
            pl.BlockSpec((ROWS_C, tm), lambda b, s: (0, b * ns + s)),
            pl.BlockSpec((ROWS_F, tm), lambda b, s: (0, b * ns + s)),
            pl.BlockSpec((ROWS_F, 1), lambda b, s: (0, 0)),
        ],
        out_specs=[
            pl.BlockSpec((N_HEADS * HEAD_PAD, tm), lambda b, s: (0, b * ns + s)),
            pl.BlockSpec((tm, N_HEADS * HEAD_PAD), lambda b, s: (b * ns + s, 0)),
            pl.BlockSpec((1, N_HEADS, 1, HEAD_PAD, tm), lambda b, s: (b, 0, s, 0, 0)),
        ],
        out_shape=[
            jax.ShapeDtypeStruct((N_HEADS * HEAD_PAD, t), BF16),
            jax.ShapeDtypeStruct((t, N_HEADS * HEAD_PAD), BF16),
            jax.ShapeDtypeStruct((batch, N_HEADS, ns, HEAD_PAD, tm), BF16),
        ],
        scratch_shapes=[pltpu.VMEM((ROWS_F, 1), F32)],
        compiler_params=_params("parallel", "arbitrary"),
        name="fox_prep",
    )(ct, ft, bf_col)


def _mla_prep_body(dt_ref, cos_ref, sin_ref, gq_ref, gkv_ref, wq_ref, wk_ref, wv_ref,
                   qt_ref, k_ref, vt_ref):
    tm = dt_ref.shape[1]
    cos, sin = cos_ref[...], sin_ref[...]
    cq = _rms_rows(dt_ref[0:MLA_Q_LORA, :].astype(F32)) * gq_ref[...]
    ckv = _rms_rows(dt_ref[MLA_Q_LORA:MLA_Q_LORA + MLA_KV_LORA, :].astype(F32)) * gkv_ref[...]
    kr = _rot_half_rows(dt_ref[MLA_Q_LORA + MLA_KV_LORA:ROWS_D, :].astype(F32), cos, sin)
    ckv_b = ckv.astype(BF16)
    q_all = jnp.dot(wq_ref[...], cq.astype(BF16), preferred_element_type=F32)
    q_all = q_all * ((MLA_NOPE + MLA_ROPE) ** -0.5 * LOG2E)
    k_all = jnp.dot(wk_ref[...], ckv_b, preferred_element_type=F32)
    v_all = jnp.dot(wv_ref[...], ckv_b, preferred_element_type=F32)
    ones_row = _select_rows([jnp.ones((1, tm), F32)])
    zeros_q = jnp.zeros((HEAD_PAD - MLA_NOPE - MLA_ROPE, tm), F32)
    for h in range(N_HEADS):
        r0, r1 = h * HEAD_DIM, (h + 1) * HEAD_DIM
        g0, g1 = h * HEAD_PAD, (h + 1) * HEAD_PAD
        qg = q_all[g0:g1]
        q_rope = _rot_half_rows(qg[MLA_NOPE:MLA_NOPE + MLA_ROPE], cos, sin)
        qt_ref[g0:g1, :] = jnp.concatenate([qg[0:MLA_NOPE], q_rope, zeros_q], axis=0).astype(BF16)
        k_grp = jnp.concatenate([k_all[r0:r1], kr, zeros_q], axis=0)
        k_ref[:, g0:g1] = k_grp.T.astype(BF16)
        vt_ref[0, h, 0] = _group(v_all[r0:r1], ones_row).astype(BF16)


def _mla_prep(dt, cos_t, sin_t, gq_col, gkv_col, wq_t, wk_t, wv_t, batch):
    t = dt.shape[1]
    tm = TOKEN_TILE
    ns = t // batch // tm
    half = MLA_ROPE // 2
    const = lambda shape: pl.BlockSpec(shape, lambda b, s: (0,) * len(shape))
    return pl.pallas_call(
        _mla_prep_body,
        grid=(batch, ns),
        in_specs=[
            pl.BlockSpec((ROWS_D, tm), lambda b, s: (0, b * ns + s)),
            pl.BlockSpec((half, tm), lambda b, s: (0, s)),
            pl.BlockSpec((half, tm), lambda b, s: (0, s)),
            const((MLA_Q_LORA, 1)),
            const((MLA_KV_LORA, 1)),
            const((N_HEADS * HEAD_PAD, MLA_Q_LORA)),
            const((GROUP_WIDTH, MLA_KV_LORA)),
            const((GROUP_WIDTH, MLA_KV_LORA)),
        ],
        out_specs=[
            pl.BlockSpec((N_HEADS * HEAD_PAD, tm), lambda b, s: (0, b * ns + s)),
            pl.BlockSpec((tm, N_HEADS * HEAD_PAD), lambda b, s: (b * ns + s, 0)),
            pl.BlockSpec((1, N_HEADS, 1, HEAD_PAD, tm), lambda b, s: (b, 0, s, 0, 0)),
        ],
        out_shape=[
            jax.ShapeDtypeStruct((N_HEADS * HEAD_PAD, t), BF16),
            jax.ShapeDtypeStruct((t, N_HEADS * HEAD_PAD), BF16),
            jax.ShapeDtypeStruct((batch, N_HEADS, ns, HEAD_PAD, tm), BF16),
        ],
        compiler_params=_params("parallel", "parallel"),
        name="mla_prep",
    )(dt, cos_t, sin_t, gq_col, gkv_col, wq_t, wk_t, wv_t)


def _attn_body(qt_ref, k_ref, vt_ref, o_ref, acc_ref, m_ref, s_ref):
    blk = qt_ref.shape[1]
    qi = pl.program_id(1)
    m_ref[...] = jnp.full(m_ref.shape, -jnp.inf, F32)
    acc_ref[...] = jnp.zeros_like(acc_ref)

    def logits(kj, h):
        start = pl.multiple_of(kj * blk, blk)
        g0, g1 = h * HEAD_PAD, (h + 1) * HEAD_PAD
        return jnp.dot(k_ref[pl.ds(start, blk), g0:g1], qt_ref[g0:g1, :],
                       preferred_element_type=F32)

    def accumulate(s, kj, h):
        m_old = m_ref[h]
        m_new = jnp.maximum(m_old, jnp.max(s, axis=0, keepdims=True))
        p = jnp.exp2(s - m_new)
        alpha = jnp.exp2(m_old - m_new)
        pv = jnp.dot(vt_ref[0, h, kj, 0:PV_ROWS, :], p.astype(BF16), preferred_element_type=F32)
        acc_ref[h] = acc_ref[h] * alpha + pv[0:HEAD_DIM + AUG_ROWS]
        m_ref[h] = m_new

    s_ref[...] = logits(0, 0)

    def full_step(kj, carry):
        s = s_ref[...]
        for h in range(N_HEADS):
            s_next = logits(kj, h + 1) if h + 1 < N_HEADS else logits(kj + 1, 0)
            accumulate(s, kj, h)
            s = s_next
        s_ref[...] = s
        return carry

    lax.fori_loop(0, qi, full_step, 0)

    key_pos = lax.broadcasted_iota(jnp.int32, (blk, blk), 0)
    qry_pos = lax.broadcasted_iota(jnp.int32, (blk, blk), 1)
    visible = key_pos <= qry_pos
    s = s_ref[...]
    for h in range(N_HEADS):
        s_next = logits(qi, h + 1) if h + 1 < N_HEADS else None
        accumulate(jnp.where(visible, s, -jnp.inf), qi, h)
        s = s_next
    for h in range(N_HEADS):
        acc = acc_ref[h]
        o_ref[h * HEAD_DIM:(h + 1) * HEAD_DIM, :] = acc[0:HEAD_DIM] / acc[HEAD_DIM:HEAD_DIM + 1]


def _attention(qt, k, vt):
    batch, _, nb, _, blk = vt.shape
    t = qt.shape[1]
    seq = t // batch
    return pl.pallas_call(
        _attn_body,
        grid=(batch, nb),
        in_specs=[
            pl.BlockSpec((N_HEADS * HEAD_PAD, blk), lambda b, i: (0, b * nb + i)),
            pl.BlockSpec((seq, N_HEADS * HEAD_PAD), lambda b, i: (b, 0)),
            pl.BlockSpec((1, N_HEADS, nb, HEAD_PAD, blk), lambda b, i: (b, 0, 0, 0, 0)),
        ],
        out_specs=pl.BlockSpec((GROUP_WIDTH, blk), lambda b, i: (0, b * nb + i)),
        out_shape=jax.ShapeDtypeStruct((GROUP_WIDTH, t), F32),
        scratch_shapes=[pltpu.VMEM((N_HEADS, HEAD_DIM + AUG_ROWS, blk), F32),
                        pltpu.VMEM((N_HEADS, 1, blk), F32),
                        pltpu.VMEM((blk, blk), F32)],
        compiler_params=_params("parallel", "arbitrary"),
        name="attention",
    )(qt, k, vt)


def _post_body(x_ref, ya_ref, yb_ref, yc_ref, yd_ref, gg_ref, wo_ref, gf_ref, wu_ref, wd_ref,
               gl_ref, o_ref, *, final):
    ys = []
    for g, y_ref in enumerate((ya_ref, yb_ref, yc_ref, yd_ref)):
        ys.append((_rms_rows(y_ref[...]) * gg_ref[g]).astype(BF16))
    y = jnp.concatenate(ys, axis=0)
    x = x_ref[...] + lax.dot_general(y, wo_ref[...], TN, preferred_element_type=F32)
    h = (x * lax.rsqrt(jnp.mean(x * x, axis=-1, keepdims=True) + EPS) * gf_ref[...]).astype(BF16)
    acc = x
    for c in range(D_FF // FF_CHUNK):
        c0, c1 = c * FF_CHUNK, (c + 1) * FF_CHUNK
        a = jnp.maximum(jnp.dot(h, wu_ref[:, c0:c1], preferred_element_type=F32), 0.0)
        acc = acc + jnp.dot((a * a).astype(BF16), wd_ref[c0:c1, :], preferred_element_type=F32)
    if final:
        acc = acc * lax.rsqrt(jnp.mean(acc * acc, axis=-1, keepdims=True) + EPS) * gl_ref[...]
    o_ref[...] = acc


def _post(x, ya, yb, yc, yd, gg_col, w_out, g_ffn, w_up, w_down, g_final, final):
    t = x.shape[0]
    tm = TOKEN_TILE
    const = lambda shape: pl.BlockSpec(shape, lambda i: (0,) * len(shape), pipeline_mode=pl.Buffered(1))
    ytile = pl.BlockSpec((GROUP_WIDTH, tm), lambda i: (0, i))
    return pl.pallas_call(
        functools.partial(_post_body, final=final),
        grid=(t // tm,),
        in_specs=[
            pl.BlockSpec((tm, D_MODEL), lambda i: (i, 0)),
            ytile, ytile, ytile, ytile,
            const((4, GROUP_WIDTH, 1)),
            const((D_MODEL, D_MODEL)),
            const((1, D_MODEL)),
            const((D_MODEL, D_FF)),
            const((D_FF, D_MODEL)),
            const((1, D_MODEL)),
        ],
        out_specs=pl.BlockSpec((tm, D_MODEL), lambda i: (i, 0)),
        out_shape=jax.ShapeDtypeStruct((t, D_MODEL), F32),
        compiler_params=_params("parallel"),
        name="post",
    )(x, ya, yb, yc, yd, gg_col, w_out, g_ffn, w_up, w_down, g_final)


def _rope_tables(seq, half):
    inv_freq = jnp.power(ROPE_BASE, -jnp.arange(half, dtype=F32) / half)
    ang = inv_freq[:, None] * jnp.arange(seq, dtype=F32)[None, :]
    return jnp.cos(ang), jnp.sin(ang)


def _inproj_weight(w_in):
    wt = w_in.T
    n_abc = ROWS_A + ROWS_B + ROWS_C
    f = wt[n_abc:n_abc + N_HEADS]
    d = wt[n_abc + N_HEADS:n_abc + N_HEADS + ROWS_D]
    f = jnp.concatenate([f, jnp.zeros((ROWS_F - N_HEADS, D_MODEL), w_in.dtype)], axis=0)
    return jnp.concatenate([wt[:n_abc], d, f], axis=0).astype(BF16)


def _mla_weights(w_uq, w_ukv):
    wq = w_uq.T.reshape(N_HEADS, MLA_NOPE + MLA_ROPE, MLA_Q_LORA)
    wq = jnp.pad(wq, ((0, 0), (0, HEAD_PAD - MLA_NOPE - MLA_ROPE), (0, 0)))
    wq = wq.reshape(N_HEADS * HEAD_PAD, MLA_Q_LORA).astype(BF16)
    wkv = w_ukv.T.reshape(N_HEADS, 2 * HEAD_DIM, MLA_KV_LORA)
    wk = wkv[:, :HEAD_DIM].reshape(GROUP_WIDTH, MLA_KV_LORA).astype(BF16)
    wv = wkv[:, HEAD_DIM:].reshape(GROUP_WIDTH, MLA_KV_LORA).astype(BF16)
    return wq, wk, wv


def kernel(x, g_mix_norm, w_in, b_forget, g_sgu, w_spatial, b_spatial, g_mla_q, w_uq, g_mla_kv, w_ukv,
           g_group_out, w_out, g_ffn_norm, w_up, w_down, g_final):
    batch, seq, _ = x.shape
    depth = w_in.shape[0]
    assert seq % TOKEN_TILE == 0 and ATTN_BLOCK == TOKEN_TILE
    cos_b, sin_b = _rope_tables(seq, HEAD_DIM // 2)
    cos_d, sin_d = _rope_tables(seq, MLA_ROPE // 2)
    xf = x.reshape(batch * seq, D_MODEL)
    for l in range(depth):
        at, bt, ct, dt, ft = _inproj(xf, g_mix_norm[l][None, :], _inproj_weight(w_in[l]))
        ya = _sgu(at, g_sgu[l][:, None], w_spatial[l], b_spatial[l])
        yb = _retention(bt, cos_b, sin_b, batch)
        bf_col = jnp.pad(b_forget[l], (0, ROWS_F - N_HEADS))[:, None]
        yc = _attention(*_fox_prep(ct, ft, bf_col, batch))
        wq, wk, wv = _mla_weights(w_uq[l], w_ukv[l])
        yd = _attention(*_mla_prep(dt, cos_d, sin_d, g_mla_q[l][:, None], g_mla_kv[l][:, None],
                                   wq, wk, wv, batch))
        xf = _post(xf, ya, yb, yc, yd, g_group_out[l].reshape(4, GROUP_WIDTH, 1),
                   w_out[l].astype(BF16), g_ffn_norm[l][None, :], w_up[l].astype(BF16),
                   w_down[l].astype(BF16), g_final[None, :], final=(l == depth - 1))
    return xf.reshape(batch, seq, D_MODEL)
```

```python
import functools
import math

import jax
import jax.numpy as jnp
import numpy as np
from jax import lax
from jax.experimental import pallas as pl
from jax.experimental.pallas import tpu as pltpu

F32 = jnp.float32
BF16 = jnp.bfloat16

D_MODEL = 1024
N_HEADS = 4
HEAD_DIM = 64
GROUP_WIDTH = N_HEADS * HEAD_DIM
CHUNK = 128
MLA_Q_LORA = 256
MLA_KV_LORA = 128
MLA_NOPE = 64
MLA_ROPE = 32
ROPE_BASE = 10000.0
D_FF = 4 * D_MODEL
EPS = 1e-6

HEAD_PAD = 128
AUG_ROWS = 8
PV_ROWS = 80
LOG2E = math.log2(math.e)
EXP2_UNDERFLOW = 152.0
STAT_QNORM, STAT_KNORM, STAT_CMAX, STAT_CMIN, STAT_DIAG = range(5)
N_STATS = 5
TOKEN_TILE = 512
ATTN_BLOCK = TOKEN_TILE
FF_CHUNK = 512
VMEM_LIMIT = 56 * 1024 * 1024

ROWS_A = 2 * GROUP_WIDTH
ROWS_B = 4 * GROUP_WIDTH
ROWS_C = 3 * GROUP_WIDTH
ROWS_D = MLA_Q_LORA + MLA_KV_LORA + MLA_ROPE
ROWS_F = 8
ROWS_ABC = ROWS_A + ROWS_B + ROWS_C

NT = (((1,), (1,)), ((), ()))
TN = (((0,), (0,)), ((), ()))


def _params(*sem):
    return pltpu.CompilerParams(dimension_semantics=sem, vmem_limit_bytes=VMEM_LIMIT)


def _rot_half_rows(t, cos, sin):
    half = t.shape[0] // 2
    t1, t2 = t[:half], t[half:]
    return jnp.concatenate([t1 * cos - t2 * sin, t1 * sin + t2 * cos], axis=0)


def _standardize_rows(t):
    mu = jnp.mean(t, axis=0, keepdims=True)
    var = jnp.mean(jnp.square(t - mu), axis=0, keepdims=True)
    return (t - mu) * lax.rsqrt(var + EPS)


def _rms_rows(t):
    return t * lax.rsqrt(jnp.mean(t * t, axis=0, keepdims=True) + EPS)


def _inproj_body(x_ref, g_ref, w_ref, wdf_ref, at_ref, bt_ref, ct_ref, dt_ref, ft_ref):
    x = x_ref[...]
    h = (x * lax.rsqrt(jnp.mean(x * x, axis=-1, keepdims=True) + EPS) * g_ref[...]).astype(BF16)

    def proj(w):
        return lax.dot_general(w, h, NT, preferred_element_type=F32)

    r = 0
    at_ref[...] = proj(w_ref[r:r + ROWS_A, :]).astype(BF16)
    r += ROWS_A
    half_b = ROWS_B // 2
    bt_ref[0:half_b, :] = proj(w_ref[r:r + half_b, :]).astype(BF16)
    bt_ref[half_b:ROWS_B, :] = proj(w_ref[r + half_b:r + ROWS_B, :]).astype(BF16)
    r += ROWS_B
    ct_ref[...] = proj(w_ref[r:r + ROWS_C, :]).astype(BF16)
    zdf = proj(wdf_ref[...])
    dt_ref[...] = zdf[0:ROWS_D].astype(BF16)
    ft_ref[...] = zdf[ROWS_D:ROWS_D + ROWS_F]


def _inproj(x, g, w_t, wdf_t):
    t = x.shape[0]
    tm = TOKEN_TILE
    rows = (ROWS_A, ROWS_B, ROWS_C, ROWS_D, ROWS_F)
    dts = (BF16, BF16, BF16, BF16, F32)
    return pl.pallas_call(
        _inproj_body,
        grid=(t // tm,),
        in_specs=[
            pl.BlockSpec((tm, D_MODEL), lambda i: (i, 0)),
            pl.BlockSpec((1, D_MODEL), lambda i: (0, 0)),
            pl.BlockSpec((ROWS_ABC, D_MODEL), lambda i: (0, 0)),
            pl.BlockSpec((ROWS_D + ROWS_F, D_MODEL), lambda i: (0, 0)),
        ],
        out_specs=[pl.BlockSpec((r, tm), lambda i: (0, i)) for r in rows],
        out_shape=[jax.ShapeDtypeStruct((r, t), d) for r, d in zip(rows, dts)],
        compiler_params=_params("parallel"),
        name="inproj",
    )(x, g, w_t, wdf_t)


def _sgu_body(at_ref, gain_ref, ws_ref, bs_ref, o_ref):
    tm = at_ref.shape[1]
    nch = tm // CHUNK
    row = lax.broadcasted_iota(jnp.int32, (CHUNK, CHUNK), 0)
    col = lax.broadcasted_iota(jnp.int32, (CHUNK, CHUNK), 1)
    for h in range(N_HEADS):
        r0, r1 = h * HEAD_DIM, (h + 1) * HEAD_DIM
        u = jax.nn.gelu(at_ref[r0:r1, :].astype(F32))
        v = jax.nn.gelu(at_ref[GROUP_WIDTH + r0:GROUP_WIDTH + r1, :].astype(F32))
        v = _standardize_rows(v) * gain_ref[r0:r1, :]
        w = jnp.where(col <= row, ws_ref[h], 0.0).astype(BF16)
        vs = jnp.concatenate([v[:, c * CHUNK:(c + 1) * CHUNK] for c in range(nch)], axis=0).astype(BF16)
        mixed = lax.dot_general(vs, w, NT, preferred_element_type=F32) + bs_ref[h:h + 1, :]
        for c in range(nch):
            o_ref[r0:r1, c * CHUNK:(c + 1) * CHUNK] = (
                u[:, c * CHUNK:(c + 1) * CHUNK] * mixed[c * HEAD_DIM:(c + 1) * HEAD_DIM])


def _sgu(at, gain_col, w_s, b_s):
    t = at.shape[1]
    tm = TOKEN_TILE
    return pl.pallas_call(
        _sgu_body,
        grid=(t // tm,),
        in_specs=[
            pl.BlockSpec((ROWS_A, tm), lambda i: (0, i)),
            pl.BlockSpec((GROUP_WIDTH, 1), lambda i: (0, 0)),
            pl.BlockSpec((N_HEADS, CHUNK, CHUNK), lambda i: (0, 0, 0)),
            pl.BlockSpec((N_HEADS, CHUNK), lambda i: (0, 0)),
        ],
        out_specs=pl.BlockSpec((GROUP_WIDTH, tm), lambda i: (0, i)),
        out_shape=jax.ShapeDtypeStruct((GROUP_WIDTH, t), F32),
        compiler_params=_params("parallel"),
        name="sgu",
    )(at, gain_col, w_s, b_s)


def _ret_body(bt_ref, cos_ref, sin_ref, o_ref, st_ref):
    tm = bt_ref.shape[1]
    nch = tm // CHUNK

    @pl.when(pl.program_id(1) == 0)
    def _():
        st_ref[...] = jnp.zeros_like(st_ref)

    cos, sin = cos_ref[...], sin_ref[...]
    srow = lax.broadcasted_iota(jnp.int32, (CHUNK, CHUNK), 0)
    tcol = lax.broadcasted_iota(jnp.int32, (CHUNK, CHUNK), 1)
    rel = (tcol - srow).astype(F32)
    j = lax.broadcasted_iota(jnp.int32, (1, CHUNK), 1).astype(F32)
    for h in range(N_HEADS):
        log_gamma = math.log1p(-(2.0 ** (-5.0 - h)))
        dec_t = jnp.where(rel >= 0, jnp.exp(jnp.maximum(rel, 0.0) * log_gamma), 0.0)
        query_w = jnp.exp((j + 1.0) * log_gamma)
        key_w = jnp.exp((CHUNK - 1.0 - j) * log_gamma)
        chunk_decay = math.exp(CHUNK * log_gamma)
        r0, r1 = h * HEAD_DIM, (h + 1) * HEAD_DIM
        q = _rot_half_rows(bt_ref[r0:r1, :].astype(F32), cos, sin)
        k = _rot_half_rows(bt_ref[GROUP_WIDTH + r0:GROUP_WIDTH + r1, :].astype(F32), cos, sin) * (HEAD_DIM ** -0.5)
        v = bt_ref[2 * GROUP_WIDTH + r0:2 * GROUP_WIDTH + r1, :].astype(F32)
        g = bt_ref[3 * GROUP_WIDTH + r0:3 * GROUP_WIDTH + r1, :].astype(F32)
        st = st_ref[h]
        ys = []
        for c in range(nch):
            sl = slice(c * CHUNK, (c + 1) * CHUNK)
            qc, kc, vc = q[:, sl], k[:, sl], v[:, sl]
            kcb = kc.astype(BF16)
            a_t = lax.dot_general(kcb, qc.astype(BF16), TN, preferred_element_type=F32)
            p_t = (a_t * dec_t).astype(BF16)
            intra = jnp.dot(vc.astype(BF16), p_t, preferred_element_type=F32)
            cross = jnp.dot(st.astype(BF16), (qc * query_w).astype(BF16), preferred_element_type=F32)
            ys.append(intra + cross)
            st = chunk_decay * st + lax.dot_general((vc * key_w).astype(BF16), kcb, NT,
                                                    preferred_element_type=F32)
        st_ref[h] = st
        y = _standardize_rows(jnp.concatenate(ys, axis=1))
        o_ref[r0:r1, :] = jax.nn.silu(g) * y


def _retention(bt, cos_t, sin_t, batch):
    t = bt.shape[1]
    tm = TOKEN_TILE
    ns = t // batch // tm
    half = HEAD_DIM // 2
    return pl.pallas_call(
        _ret_body,
        grid=(batch, ns),
        in_specs=[
            pl.BlockSpec((ROWS_B, tm), lambda b, s: (0, b * ns + s)),
            pl.BlockSpec((half, tm), lambda b, s: (0, s)),
            pl.BlockSpec((half, tm), lambda b, s: (0, s)),
        ],
        out_specs=pl.BlockSpec((GROUP_WIDTH, tm), lambda b, s: (0, b * ns + s)),
        out_shape=jax.ShapeDtypeStruct((GROUP_WIDTH, t), F32),
        scratch_shapes=[pltpu.VMEM((N_HEADS, HEAD_DIM, HEAD_DIM), F32)],
        compiler_params=_params("parallel", "arbitrary"),
        name="retention",
    )(bt, cos_t, sin_t)


def _select_rows(rows):
    n = rows[0].shape[1]
    ridx = lax.broadcasted_iota(jnp.int32, (AUG_ROWS, n), 0)
    out = jnp.zeros((AUG_ROWS, n), F32)
    for i, r in enumerate(rows):
        out = jnp.where(ridx == i, jnp.broadcast_to(r, (AUG_ROWS, n)), out)
    return out


def _group(feat, extra):
    n = feat.shape[1]
    pad = HEAD_PAD - feat.shape[0] - extra.shape[0]
    return jnp.concatenate([feat, extra, jnp.zeros((pad, n), F32)], axis=0)


def _fox_prep_body(ct_ref, ft_ref, bf_ref, qt_ref, k_ref, vt_ref, st_ref, carry_ref):
    tm = ct_ref.shape[1]

    @pl.when(pl.program_id(1) == 0)
    def _():
        carry_ref[...] = jnp.zeros_like(carry_ref)

    x = ft_ref[...] + bf_ref[...]
    lf = jnp.minimum(x, 0.0) - jnp.log1p(jnp.exp(-jnp.abs(x)))
    hi = lf.astype(BF16)
    mid = (lf - hi.astype(F32)).astype(BF16)
    lo = (lf - hi.astype(F32) - mid.astype(F32)).astype(BF16)
    srow = lax.broadcasted_iota(jnp.int32, (tm, tm), 0)
    tcol = lax.broadcasted_iota(jnp.int32, (tm, tm), 1)
    upper = jnp.where(srow <= tcol, 1.0, 0.0).astype(BF16)
    parts = jnp.dot(jnp.concatenate([hi, mid, lo], axis=0), upper, preferred_element_type=F32)
    cum = parts[0:8] + parts[8:16] + parts[16:24] + carry_ref[...]
    carry_ref[...] = cum[:, tm - 1:tm]

    cum2 = cum * LOG2E
    c_hi = cum2.astype(BF16).astype(F32)
    c_mid = (cum2 - c_hi).astype(BF16).astype(F32)
    c_lo = (cum2 - c_hi - c_mid).astype(BF16).astype(F32)
    one = jnp.ones((1, tm), F32)
    ones_row = _select_rows([one])
    stat_row = lax.broadcasted_iota(jnp.int32, (8, 128), 0)
    stat_lane = lax.broadcasted_iota(jnp.int32, (8, 128), 1)
    stats = jnp.zeros((8, 128), F32)
    for h in range(N_HEADS):
        r0, r1 = h * HEAD_DIM, (h + 1) * HEAD_DIM
        g0, g1 = h * HEAD_PAD, (h + 1) * HEAD_PAD
        q = ct_ref[r0:r1, :].astype(F32) * (HEAD_DIM ** -0.5 * LOG2E)
        k = ct_ref[GROUP_WIDTH + r0:GROUP_WIDTH + r1, :].astype(F32)
        v = ct_ref[2 * GROUP_WIDTH + r0:2 * GROUP_WIDTH + r1, :].astype(F32)
        ch, cm, cl = c_hi[h:h + 1], c_mid[h:h + 1], c_lo[h:h + 1]
        q_extra = _select_rows([ch, cm, cl, one, one, one])
        k_extra = _select_rows([one, one, one, -ch, -cm, -cl])
        qt_ref[g0:g1, :] = _group(q, q_extra).astype(BF16)
        k_ref[:, g0:g1] = _group(k, k_extra).T.astype(BF16)
        vt_ref[0, h, 0] = _group(v, ones_row).astype(BF16)
        qb, kb = q.astype(BF16).astype(F32), k.astype(BF16).astype(F32)
        c2 = cum2[h:h + 1]
        block_stats = (
            jnp.sqrt(jnp.max(jnp.sum(qb * qb, axis=0, keepdims=True), axis=1, keepdims=True)),
            jnp.sqrt(jnp.max(jnp.sum(kb * kb, axis=0, keepdims=True), axis=1, keepdims=True)),
            jnp.max(c2, axis=1, keepdims=True),
            jnp.min(c2, axis=1, keepdims=True),
            jnp.min(jnp.sum(qb * kb, axis=0, keepdims=True), axis=1, keepdims=True),
        )
        for r, val in enumerate(block_stats):
            stats = jnp.where((stat_row == r) & (stat_lane == h), val, stats)
    st_ref[0, 0] = stats


def _fox_prep(ct, ft, bf_col, batch):
    t = ct.shape[1]
    tm = TOKEN_TILE
    ns = t // batch // tm
    return pl.pallas_call(
        _fox_prep_body,
        grid=(batch, ns),
        in_specs=[
            pl.BlockSpec((ROWS_C, tm), lambda b, s: (0, b * ns + s)),
            pl.BlockSpec((ROWS_F, tm), lambda b, s: (0, b * ns + s)),
            pl.BlockSpec((ROWS_F, 1), lambda b, s: (0, 0)),
        ],
        out_specs=[
            pl.BlockSpec((N_HEADS * HEAD_PAD, tm), lambda b, s: (0, b * ns + s)),
            pl.BlockSpec((tm, N_HEADS * HEAD_PAD), lambda b, s: (b * ns + s, 0)),
            pl.BlockSpec((1, N_HEADS, 1, HEAD_PAD, tm), lambda b, s: (b, 0, s, 0, 0)),
            pl.BlockSpec((1, 1, 8, 128), lambda b, s: (b, s, 0, 0)),
        ],
        out_shape=[
            jax.ShapeDtypeStruct((N_HEADS * HEAD_PAD, t), BF16),
            jax.ShapeDtypeStruct((t, N_HEADS * HEAD_PAD), BF16),
            jax.ShapeDtypeStruct((batch, N_HEADS, ns, HEAD_PAD, tm), BF16),
            jax.ShapeDtypeStruct((batch, ns, 8, 128), F32),
        ],
        scratch_shapes=[pltpu.VMEM((ROWS_F, 1), F32)],
        compiler_params=_params("parallel", "arbitrary"),
        name="fox_prep",
    )(ct, ft, bf_col)


def _mla_prep_body(dt_ref, cos_ref, sin_ref, gq_ref, gkv_ref, wq_ref, wk_ref, wv_ref,
                   qt_ref, k_ref, vt_ref):
    tm = dt_ref.shape[1]
    cos, sin = cos_ref[...], sin_ref[...]
    cq = _rms_rows(dt_ref[0:MLA_Q_LORA, :].astype(F32)) * gq_ref[...]
    ckv = _rms_rows(dt_ref[MLA_Q_LORA:MLA_Q_LORA + MLA_KV_LORA, :].astype(F32)) * gkv_ref[...]
    kr = _rot_half_rows(dt_ref[MLA_Q_LORA + MLA_KV_LORA:ROWS_D, :].astype(F32), cos, sin)
    ckv_b = ckv.astype(BF16)
    q_all = jnp.dot(wq_ref[...], cq.astype(BF16), preferred_element_type=F32)
    q_all = q_all * ((MLA_NOPE + MLA_ROPE) ** -0.5 * LOG2E)
    k_all = jnp.dot(wk_ref[...], ckv_b, preferred_element_type=F32)
    v_all = jnp.dot(wv_ref[...], ckv_b, preferred_element_type=F32)
    ones_row = _select_rows([jnp.ones((1, tm), F32)])
    zeros_q = jnp.zeros((HEAD_PAD - MLA_NOPE - MLA_ROPE, tm), F32)
    for h in range(N_HEADS):
        r0, r1 = h * HEAD_DIM, (h + 1) * HEAD_DIM
        g0, g1 = h * HEAD_PAD, (h + 1) * HEAD_PAD
        qg = q_all[g0:g1]
        q_rope = _rot_half_rows(qg[MLA_NOPE:MLA_NOPE + MLA_ROPE], cos, sin)
        qt_ref[g0:g1, :] = jnp.concatenate([qg[0:MLA_NOPE], q_rope, zeros_q], axis=0).astype(BF16)
        k_grp = jnp.concatenate([k_all[r0:r1], kr, zeros_q], axis=0)
        k_ref[:, g0:g1] = k_grp.T.astype(BF16)
        vt_ref[0, h, 0] = _group(v_all[r0:r1], ones_row).astype(BF16)


def _mla_prep(dt, cos_t, sin_t, gq_col, gkv_col, wq_t, wk_t, wv_t, batch):
    t = dt.shape[1]
    tm = TOKEN_TILE
    ns = t // batch // tm
    half = MLA_ROPE // 2
    const = lambda shape: pl.BlockSpec(shape, lambda b, s: (0,) * len(shape))
    return pl.pallas_call(
        _mla_prep_body,
        grid=(batch, ns),
        in_specs=[
            pl.BlockSpec((ROWS_D, tm), lambda b, s: (0, b * ns + s)),
            pl.BlockSpec((half, tm), lambda b, s: (0, s)),
            pl.BlockSpec((half, tm), lambda b, s: (0, s)),
            const((MLA_Q_LORA, 1)),
            const((MLA_KV_LORA, 1)),
            const((N_HEADS * HEAD_PAD, MLA_Q_LORA)),
            const((GROUP_WIDTH, MLA_KV_LORA)),
            const((GROUP_WIDTH, MLA_KV_LORA)),
        ],
        out_specs=[
            pl.BlockSpec((N_HEADS * HEAD_PAD, tm), lambda b, s: (0, b * ns + s)),
            pl.BlockSpec((tm, N_HEADS * HEAD_PAD), lambda b, s: (b * ns + s, 0)),
            pl.BlockSpec((1, N_HEADS, 1, HEAD_PAD, tm), lambda b, s: (b, 0, s, 0, 0)),
        ],
        out_shape=[
            jax.ShapeDtypeStruct((N_HEADS * HEAD_PAD, t), BF16),
            jax.ShapeDtypeStruct((t, N_HEADS * HEAD_PAD), BF16),
            jax.ShapeDtypeStruct((batch, N_HEADS, ns, HEAD_PAD, tm), BF16),
        ],
        compiler_params=_params("parallel", "parallel"),
        name="mla_prep",
    )(dt, cos_t, sin_t, gq_col, gkv_col, wq_t, wk_t, wv_t)


def _first_needed_block(stats_ref, b, qi, nb):
    def stat(j, r, h):
        return stats_ref[((b * nb + j) * N_STATS + r) * N_HEADS + h]

    skipped = jnp.int32(0)
    leading = jnp.bool_(True)
    for j in range(nb - 1):
        zero = j < qi
        for h in range(N_HEADS):
            bound = (stat(qi, STAT_QNORM, h) * stat(j, STAT_KNORM, h)
                     + stat(qi, STAT_CMAX, h) - stat(j, STAT_CMIN, h))
            zero = jnp.logical_and(zero, bound - stat(qi, STAT_DIAG, h) < -EXP2_UNDERFLOW)
        leading = jnp.logical_and(leading, zero)
        skipped = skipped + leading.astype(jnp.int32)
    return skipped


def _attn_body(*refs, banded):
    if banded:
        stats_ref, qt_ref, k_ref, vt_ref, o_ref, acc_ref, m_ref, s_ref = refs
    else:
        qt_ref, k_ref, vt_ref, o_ref, acc_ref, m_ref, s_ref = refs
    blk = qt_ref.shape[1]
    nb = vt_ref.shape[2]
    qi = pl.program_id(1)
    first = _first_needed_block(stats_ref, pl.program_id(0), qi, nb) if banded else 0
    m_ref[...] = jnp.full(m_ref.shape, -jnp.inf, F32)
    acc_ref[...] = jnp.zeros_like(acc_ref)

    def logits(kj, h):
        start = pl.multiple_of(kj * blk, blk)
        g0, g1 = h * HEAD_PAD, (h + 1) * HEAD_PAD
        return jnp.dot(k_ref[pl.ds(start, blk), g0:g1], qt_ref[g0:g1, :],
                       preferred_element_type=F32)

    def accumulate(s, kj, h):
        m_old = m_ref[h]
        m_new = jnp.maximum(m_old, jnp.max(s, axis=0, keepdims=True))
        p = jnp.exp2(s - m_new)
        alpha = jnp.exp2(m_old - m_new)
        pv = jnp.dot(vt_ref[0, h, kj, 0:PV_ROWS, :], p.astype(BF16), preferred_element_type=F32)
        acc_ref[h] = acc_ref[h] * alpha + pv[0:HEAD_DIM + AUG_ROWS]
        m_ref[h] = m_new

    s_ref[...] = logits(first, 0)

    def full_step(kj, carry):
        s = s_ref[...]
        for h in range(N_HEADS):
            s_next = logits(kj, h + 1) if h + 1 < N_HEADS else logits(kj + 1, 0)
            accumulate(s, kj, h)
            s = s_next
        s_ref[...] = s
        return carry

    lax.fori_loop(first, qi, full_step, 0)

    key_pos = lax.broadcasted_iota(jnp.int32, (blk, blk), 0)
    qry_pos = lax.broadcasted_iota(jnp.int32, (blk, blk), 1)
    visible = key_pos <= qry_pos
    s = s_ref[...]
    for h in range(N_HEADS):
        s_next = logits(qi, h + 1) if h + 1 < N_HEADS else None
        accumulate(jnp.where(visible, s, -jnp.inf), qi, h)
        s = s_next
    for h in range(N_HEADS):
        acc = acc_ref[h]
        o_ref[h * HEAD_DIM:(h + 1) * HEAD_DIM, :] = acc[0:HEAD_DIM] / acc[HEAD_DIM:HEAD_DIM + 1]


def _attention(qt, k, vt, stats=None):
    batch, _, nb, _, blk = vt.shape
    t = qt.shape[1]
    seq = t // batch
    banded = stats is not None
    in_specs = [
        pl.BlockSpec((N_HEADS * HEAD_PAD, blk), lambda b, i: (0, b * nb + i)),
        pl.BlockSpec((seq, N_HEADS * HEAD_PAD), lambda b, i: (b, 0)),
        pl.BlockSpec((1, N_HEADS, nb, HEAD_PAD, blk), lambda b, i: (b, 0, 0, 0, 0)),
    ]
    args = (qt, k, vt)
    if banded:
        in_specs = [pl.BlockSpec(memory_space=pltpu.SMEM)] + in_specs
        args = (stats,) + args
    return pl.pallas_call(
        functools.partial(_attn_body, banded=banded),
        grid=(batch, nb),
        in_specs=in_specs,
        out_specs=pl.BlockSpec((GROUP_WIDTH, blk), lambda b, i: (0, b * nb + i)),
        out_shape=jax.ShapeDtypeStruct((GROUP_WIDTH, t), F32),
        scratch_shapes=[pltpu.VMEM((N_HEADS, HEAD_DIM + AUG_ROWS, blk), F32),
                        pltpu.VMEM((N_HEADS, 1, blk), F32),
                        pltpu.VMEM((blk, blk), F32)],
        compiler_params=_params("parallel", "arbitrary"),
        name="attention",
    )(*args)


def _post_body(x_ref, ya_ref, yb_ref, yc_ref, yd_ref, gg_ref, wo_ref, gf_ref, wu_ref, wd_ref,
               gl_ref, o_ref, *, final):
    ys = []
    for g, y_ref in enumerate((ya_ref, yb_ref, yc_ref, yd_ref)):
        ys.append((_rms_rows(y_ref[...]) * gg_ref[g]).astype(BF16))
    y = jnp.concatenate(ys, axis=0)
    x = x_ref[...] + lax.dot_general(y, wo_ref[...], TN, preferred_element_type=F32)
    h = (x * lax.rsqrt(jnp.mean(x * x, axis=-1, keepdims=True) + EPS) * gf_ref[...]).astype(BF16)
    acc = x
    for c in range(D_FF // FF_CHUNK):
        c0, c1 = c * FF_CHUNK, (c + 1) * FF_CHUNK
        a = jnp.maximum(jnp.dot(h, wu_ref[:, c0:c1], preferred_element_type=F32), 0.0)
        acc = acc + jnp.dot((a * a).astype(BF16), wd_ref[c0:c1, :], preferred_element_type=F32)
    if final:
        acc = acc * lax.rsqrt(jnp.mean(acc * acc, axis=-1, keepdims=True) + EPS) * gl_ref[...]
    o_ref[...] = acc


def _post(x, ya, yb, yc, yd, gg_col, w_out, g_ffn, w_up, w_down, g_final, final):
    t = x.shape[0]
    tm = TOKEN_TILE
    const = lambda shape: pl.BlockSpec(shape, lambda i: (0,) * len(shape), pipeline_mode=pl.Buffered(1))
    ytile = pl.BlockSpec((GROUP_WIDTH, tm), lambda i: (0, i))
    return pl.pallas_call(
        functools.partial(_post_body, final=final),
        grid=(t // tm,),
        in_specs=[
            pl.BlockSpec((tm, D_MODEL), lambda i: (i, 0)),
            ytile, ytile, ytile, ytile,
            const((4, GROUP_WIDTH, 1)),
            const((D_MODEL, D_MODEL)),
            const((1, D_MODEL)),
            const((D_MODEL, D_FF)),
            const((D_FF, D_MODEL)),
            const((1, D_MODEL)),
        ],
        out_specs=pl.BlockSpec((tm, D_MODEL), lambda i: (i, 0)),
        out_shape=jax.ShapeDtypeStruct((t, D_MODEL), F32),
        compiler_params=_params("parallel"),
        name="post",
    )(x, ya, yb, yc, yd, gg_col, w_out, g_ffn, w_up, w_down, g_final)


def _rope_tables(seq, half):
    inv_freq = np.power(ROPE_BASE, -np.arange(half, dtype=np.float64) / half)
    ang = inv_freq[:, None] * np.arange(seq, dtype=np.float64)[None, :]
    return jnp.asarray(np.cos(ang), F32), jnp.asarray(np.sin(ang), F32)


def _inproj_weights(w_in):
    w_abc = w_in[:, :ROWS_ABC].T.astype(BF16)
    f = w_in[:, ROWS_ABC:ROWS_ABC + N_HEADS].T
    d = w_in[:, ROWS_ABC + N_HEADS:ROWS_ABC + N_HEADS + ROWS_D].T
    pad = jnp.zeros((ROWS_F - N_HEADS, D_MODEL), w_in.dtype)
    return w_abc, jnp.concatenate([d, f, pad], axis=0).astype(BF16)


def _mla_weights(w_uq, w_ukv):
    wq = w_uq.T.reshape(N_HEADS, MLA_NOPE + MLA_ROPE, MLA_Q_LORA)
    wq = jnp.pad(wq, ((0, 0), (0, HEAD_PAD - MLA_NOPE - MLA_ROPE), (0, 0)))
    wq = wq.reshape(N_HEADS * HEAD_PAD, MLA_Q_LORA).astype(BF16)
    wkv = w_ukv.T.reshape(N_HEADS, 2 * HEAD_DIM, MLA_KV_LORA)
    wk = wkv[:, :HEAD_DIM].reshape(GROUP_WIDTH, MLA_KV_LORA).astype(BF16)
    wv = wkv[:, HEAD_DIM:].reshape(GROUP_WIDTH, MLA_KV_LORA).astype(BF16)
    return wq, wk, wv


def kernel(x, g_mix_norm, w_in, b_forget, g_sgu, w_spatial, b_spatial, g_mla_q, w_uq, g_mla_kv, w_ukv,
           g_group_out, w_out, g_ffn_norm, w_up, w_down, g_final):
    batch, seq, _ = x.shape
    depth = w_in.shape[0]
    assert seq % TOKEN_TILE == 0 and ATTN_BLOCK == TOKEN_TILE
    cos_b, sin_b = _rope_tables(seq, HEAD_DIM // 2)
    cos_d, sin_d = _rope_tables(seq, MLA_ROPE // 2)
    xf = x.reshape(batch * seq, D_MODEL)
    for l in range(depth):
        at, bt, ct, dt, ft = _inproj(xf, g_mix_norm[l][None, :], *_inproj_weights(w_in[l]))
        ya = _sgu(at, g_sgu[l][:, None], w_spatial[l], b_spatial[l])
        yb = _retention(bt, cos_b, sin_b, batch)
        bf_col = jnp.pad(b_forget[l], (0, ROWS_F - N_HEADS))[:, None]
        qt_c, k_c, vt_c, stats = _fox_prep(ct, ft, bf_col, batch)
        yc = _attention(qt_c, k_c, vt_c, stats[:, :, :N_STATS, :N_HEADS].reshape(-1))
        wq, wk, wv = _mla_weights(w_uq[l], w_ukv[l])
        yd = _attention(*_mla_prep(dt, cos_d, sin_d, g_mla_q[l][:, None], g_mla_kv[l][:, None],
                                   wq, wk, wv, batch))
        xf = _post(xf, ya, yb, yc, yd, g_group_out[l].reshape(4, GROUP_WIDTH, 1),
                   w_out[l].astype(BF16), g_ffn_norm[l][None, :], w_up[l].astype(BF16),
                   w_down[l].astype(BF16), g_final[None, :], final=(l == depth - 1))
    return xf.reshape(batch, seq, D_MODEL)
```

```python
import functools
import math

import jax
import jax.numpy as jnp
import numpy as np
from jax import lax
from jax.experimental import pallas as pl
from jax.experimental.pallas import tpu as pltpu

F32 = jnp.float32
BF16 = jnp.bfloat16

D_MODEL = 1024
N_HEADS = 4
HEAD_DIM = 64
GROUP_WIDTH = N_HEADS * HEAD_DIM
CHUNK = 128
MLA_Q_LORA = 256
MLA_KV_LORA = 128
MLA_NOPE = 64
MLA_ROPE = 32
ROPE_BASE = 10000.0
D_FF = 4 * D_MODEL
EPS = 1e-6

HEAD_PAD = 128
AUG_ROWS = 8
PV_ROWS = 80
LOG2E = math.log2(math.e)
EXP2_UNDERFLOW = 152.0
STAT_QNORM, STAT_KNORM, STAT_CMAX, STAT_CMIN, STAT_DIAG = range(5)
N_STATS = 5
TOKEN_TILE = 512
ATTN_BLOCK = TOKEN_TILE
FF_CHUNK = 512
VMEM_LIMIT = 56 * 1024 * 1024

ROWS_A = 2 * GROUP_WIDTH
ROWS_B = 4 * GROUP_WIDTH
ROWS_C = 3 * GROUP_WIDTH
ROWS_D = MLA_Q_LORA + MLA_KV_LORA + MLA_ROPE
ROWS_F = 8
ROWS_ABC = ROWS_A + ROWS_B + ROWS_C

NT = (((1,), (1,)), ((), ()))
TN = (((0,), (0,)), ((), ()))


def _params(*sem):
    return pltpu.CompilerParams(dimension_semantics=sem, vmem_limit_bytes=VMEM_LIMIT)


def _rot_half_rows(t, cos, sin):
    half = t.shape[0] // 2
    t1, t2 = t[:half], t[half:]
    return jnp.concatenate([t1 * cos - t2 * sin, t1 * sin + t2 * cos], axis=0)


def _standardize_rows(t):
    mu = jnp.mean(t, axis=0, keepdims=True)
    var = jnp.mean(jnp.square(t - mu), axis=0, keepdims=True)
    return (t - mu) * lax.rsqrt(var + EPS)


def _rms_rows(t):
    return t * lax.rsqrt(jnp.mean(t * t, axis=0, keepdims=True) + EPS)


def _sgu(at_ref, gain_ref, ws_ref, bs_ref, o_ref):
    tm = at_ref.shape[1]
    nch = tm // CHUNK
    row = lax.broadcasted_iota(jnp.int32, (CHUNK, CHUNK), 0)
    col = lax.broadcasted_iota(jnp.int32, (CHUNK, CHUNK), 1)
    for h in range(N_HEADS):
        r0, r1 = h * HEAD_DIM, (h + 1) * HEAD_DIM
        u = jax.nn.gelu(at_ref[r0:r1, :].astype(F32))
        v = jax.nn.gelu(at_ref[GROUP_WIDTH + r0:GROUP_WIDTH + r1, :].astype(F32))
        v = _standardize_rows(v) * gain_ref[r0:r1, :]
        w = jnp.where(col <= row, ws_ref[h], 0.0).astype(BF16)
        vs = jnp.concatenate([v[:, c * CHUNK:(c + 1) * CHUNK] for c in range(nch)], axis=0).astype(BF16)
        mixed = lax.dot_general(vs, w, NT, preferred_element_type=F32) + bs_ref[h:h + 1, :]
        for c in range(nch):
            o_ref[r0:r1, c * CHUNK:(c + 1) * CHUNK] = (
                u[:, c * CHUNK:(c + 1) * CHUNK] * mixed[c * HEAD_DIM:(c + 1) * HEAD_DIM])


def _retention(bt_ref, cos_ref, sin_ref, o_ref, st_ref):
    tm = bt_ref.shape[1]
    nch = tm // CHUNK
    cos, sin = cos_ref[...], sin_ref[...]
    srow = lax.broadcasted_iota(jnp.int32, (CHUNK, CHUNK), 0)
    tcol = lax.broadcasted_iota(jnp.int32, (CHUNK, CHUNK), 1)
    rel = (tcol - srow).astype(F32)
    j = lax.broadcasted_iota(jnp.int32, (1, CHUNK), 1).astype(F32)
    for h in range(N_HEADS):
        log_gamma = math.log1p(-(2.0 ** (-5.0 - h)))
        dec_t = jnp.where(rel >= 0, jnp.exp(jnp.maximum(rel, 0.0) * log_gamma), 0.0)
        query_w = jnp.exp((j + 1.0) * log_gamma)
        key_w = jnp.exp((CHUNK - 1.0 - j) * log_gamma)
        chunk_decay = math.exp(CHUNK * log_gamma)
        r0, r1 = h * HEAD_DIM, (h + 1) * HEAD_DIM
        q = _rot_half_rows(bt_ref[r0:r1, :].astype(F32), cos, sin)
        k = _rot_half_rows(bt_ref[GROUP_WIDTH + r0:GROUP_WIDTH + r1, :].astype(F32), cos, sin) * (HEAD_DIM ** -0.5)
        v = bt_ref[2 * GROUP_WIDTH + r0:2 * GROUP_WIDTH + r1, :].astype(F32)
        g = bt_ref[3 * GROUP_WIDTH + r0:3 * GROUP_WIDTH + r1, :].astype(F32)
        st = st_ref[h]
        ys = []
        for c in range(nch):
            sl = slice(c * CHUNK, (c + 1) * CHUNK)
            qc, kc, vc = q[:, sl], k[:, sl], v[:, sl]
            kcb = kc.astype(BF16)
            a_t = lax.dot_general(kcb, qc.astype(BF16), TN, preferred_element_type=F32)
            p_t = (a_t * dec_t).astype(BF16)
            intra = jnp.dot(vc.astype(BF16), p_t, preferred_element_type=F32)
            cross = jnp.dot(st.astype(BF16), (qc * query_w).astype(BF16), preferred_element_type=F32)
            ys.append(intra + cross)
            st = chunk_decay * st + lax.dot_general((vc * key_w).astype(BF16), kcb, NT,
                                                    preferred_element_type=F32)
        st_ref[h] = st
        y = _standardize_rows(jnp.concatenate(ys, axis=1))
        o_ref[r0:r1, :] = jax.nn.silu(g) * y


def _select_rows(rows):
    n = rows[0].shape[1]
    ridx = lax.broadcasted_iota(jnp.int32, (AUG_ROWS, n), 0)
    out = jnp.zeros((AUG_ROWS, n), F32)
    for i, r in enumerate(rows):
        out = jnp.where(ridx == i, jnp.broadcast_to(r, (AUG_ROWS, n)), out)
    return out


def _group(feat, extra):
    n = feat.shape[1]
    pad = HEAD_PAD - feat.shape[0] - extra.shape[0]
    return jnp.concatenate([feat, extra, jnp.zeros((pad, n), F32)], axis=0)


def _fox_prep(ct_ref, ft_ref, bf_ref, qt_ref, k_ref, vt_ref, st_ref, carry_ref):
    tm = ct_ref.shape[1]
    x = ft_ref[...] + bf_ref[...]
    lf = jnp.minimum(x, 0.0) - jnp.log1p(jnp.exp(-jnp.abs(x)))
    hi = lf.astype(BF16)
    mid = (lf - hi.astype(F32)).astype(BF16)
    lo = (lf - hi.astype(F32) - mid.astype(F32)).astype(BF16)
    srow = lax.broadcasted_iota(jnp.int32, (tm, tm), 0)
    tcol = lax.broadcasted_iota(jnp.int32, (tm, tm), 1)
    upper = jnp.where(srow <= tcol, 1.0, 0.0).astype(BF16)
    parts = jnp.dot(jnp.concatenate([hi, mid, lo], axis=0), upper, preferred_element_type=F32)
    cum = parts[0:8] + parts[8:16] + parts[16:24] + carry_ref[...]
    carry_ref[...] = cum[:, tm - 1:tm]

    cum2 = cum * LOG2E
    c_hi = cum2.astype(BF16).astype(F32)
    c_mid = (cum2 - c_hi).astype(BF16).astype(F32)
    c_lo = (cum2 - c_hi - c_mid).astype(BF16).astype(F32)
    one = jnp.ones((1, tm), F32)
    ones_row = _select_rows([one])
    stat_row = lax.broadcasted_iota(jnp.int32, (8, 128), 0)
    stat_lane = lax.broadcasted_iota(jnp.int32, (8, 128), 1)
    stats = jnp.zeros((8, 128), F32)
    for h in range(N_HEADS):
        r0, r1 = h * HEAD_DIM, (h + 1) * HEAD_DIM
        g0, g1 = h * HEAD_PAD, (h + 1) * HEAD_PAD
        q = ct_ref[r0:r1, :].astype(F32) * (HEAD_DIM ** -0.5 * LOG2E)
        k = ct_ref[GROUP_WIDTH + r0:GROUP_WIDTH + r1, :].astype(F32)
        v = ct_ref[2 * GROUP_WIDTH + r0:2 * GROUP_WIDTH + r1, :].astype(F32)
        ch, cm, cl = c_hi[h:h + 1], c_mid[h:h + 1], c_lo[h:h + 1]
        q_extra = _select_rows([ch, cm, cl, one, one, one])
        k_extra = _select_rows([one, one, one, -ch, -cm, -cl])
        qt_ref[g0:g1, :] = _group(q, q_extra).astype(BF16)
        k_ref[:, g0:g1] = _group(k, k_extra).T.astype(BF16)
        vt_ref[0, h, 0] = _group(v, ones_row).astype(BF16)
        qb, kb = q.astype(BF16).astype(F32), k.astype(BF16).astype(F32)
        c2 = cum2[h:h + 1]
        block_stats = (
            jnp.sqrt(jnp.max(jnp.sum(qb * qb, axis=0, keepdims=True), axis=1, keepdims=True)),
            jnp.sqrt(jnp.max(jnp.sum(kb * kb, axis=0, keepdims=True), axis=1, keepdims=True)),
            jnp.max(c2, axis=1, keepdims=True),
            jnp.min(c2, axis=1, keepdims=True),
            jnp.min(jnp.sum(qb * kb, axis=0, keepdims=True), axis=1, keepdims=True),
        )
        for r, val in enumerate(block_stats):
            stats = jnp.where((stat_row == r) & (stat_lane == h), val, stats)
    st_ref[0, 0] = stats


def _mla_prep(dt_ref, cos_ref, sin_ref, gq_ref, gkv_ref, wq_ref, wk_ref, wv_ref, qt_ref, k_ref, vt_ref):
    tm = dt_ref.shape[1]
    cos, sin = cos_ref[...], sin_ref[...]
    cq = _rms_rows(dt_ref[0:MLA_Q_LORA, :].astype(F32)) * gq_ref[...]
    ckv = _rms_rows(dt_ref[MLA_Q_LORA:MLA_Q_LORA + MLA_KV_LORA, :].astype(F32)) * gkv_ref[...]
    kr = _rot_half_rows(dt_ref[MLA_Q_LORA + MLA_KV_LORA:ROWS_D, :].astype(F32), cos, sin)
    ckv_b = ckv.astype(BF16)
    q_all = jnp.dot(wq_ref[...], cq.astype(BF16), preferred_element_type=F32)
    q_all = q_all * ((MLA_NOPE + MLA_ROPE) ** -0.5 * LOG2E)
    k_all = jnp.dot(wk_ref[...], ckv_b, preferred_element_type=F32)
    v_all = jnp.dot(wv_ref[...], ckv_b, preferred_element_type=F32)
    ones_row = _select_rows([jnp.ones((1, tm), F32)])
    zeros_q = jnp.zeros((HEAD_PAD - MLA_NOPE - MLA_ROPE, tm), F32)
    for h in range(N_HEADS):
        r0, r1 = h * HEAD_DIM, (h + 1) * HEAD_DIM
        g0, g1 = h * HEAD_PAD, (h + 1) * HEAD_PAD
        qg = q_all[g0:g1]
        q_rope = _rot_half_rows(qg[MLA_NOPE:MLA_NOPE + MLA_ROPE], cos, sin)
        qt_ref[g0:g1, :] = jnp.concatenate([qg[0:MLA_NOPE], q_rope, zeros_q], axis=0).astype(BF16)
        k_grp = jnp.concatenate([k_all[r0:r1], kr, zeros_q], axis=0)
        k_ref[:, g0:g1] = k_grp.T.astype(BF16)
        vt_ref[0, h, 0] = _group(v_all[r0:r1], ones_row).astype(BF16)


def _front_body(x_ref, g_ref, w_ref, wdf_ref,
                gain_ref, ws_ref, bs_ref,
                cosb_ref, sinb_ref,
                bf_ref,
                cosd_ref, sind_ref, gq_ref, gkv_ref, wq_ref, wk_ref, wv_ref,
                ya_ref, yb_ref,
                qtc_ref, kc_ref, vtc_ref, stats_ref,
                qtd_ref, kd_ref, vtd_ref,
                at_ref, bt_ref, ct_ref, dt_ref, ft_ref, state_ref, carry_ref, *, tiles_per_seq):
    @pl.when(pl.program_id(0) % tiles_per_seq == 0)
    def _():
        state_ref[...] = jnp.zeros_like(state_ref)
        carry_ref[...] = jnp.zeros_like(carry_ref)

    x = x_ref[...]
    h = (x * lax.rsqrt(jnp.mean(x * x, axis=-1, keepdims=True) + EPS) * g_ref[...]).astype(BF16)

    def proj(w):
        return lax.dot_general(w, h, NT, preferred_element_type=F32)

    half_b = ROWS_B // 2
    at_ref[...] = proj(w_ref[0:ROWS_A, :]).astype(BF16)
    bt_ref[0:half_b, :] = proj(w_ref[ROWS_A:ROWS_A + half_b, :]).astype(BF16)
    bt_ref[half_b:ROWS_B, :] = proj(w_ref[ROWS_A + half_b:ROWS_A + ROWS_B, :]).astype(BF16)
    _sgu(at_ref, gain_ref, ws_ref, bs_ref, ya_ref)
    ct_ref[...] = proj(w_ref[ROWS_A + ROWS_B:ROWS_ABC, :]).astype(BF16)
    _retention(bt_ref, cosb_ref, sinb_ref, yb_ref, state_ref)
    zdf = proj(wdf_ref[...])
    dt_ref[...] = zdf[0:ROWS_D].astype(BF16)
    ft_ref[...] = zdf[ROWS_D:ROWS_D + ROWS_F]
    _fox_prep(ct_ref, ft_ref, bf_ref, qtc_ref, kc_ref, vtc_ref, stats_ref, carry_ref)
    _mla_prep(dt_ref, cosd_ref, sind_ref, gq_ref, gkv_ref, wq_ref, wk_ref, wv_ref, qtd_ref, kd_ref, vtd_ref)


def _front(x, g, w_t, wdf_t, gain_col, w_s, b_s, cos_b, sin_b, bf_col,
           cos_d, sin_d, gq_col, gkv_col, wq_t, wk_t, wv_t, batch):
    t = x.shape[0]
    tm = TOKEN_TILE
    ns = t // batch // tm
    const = lambda shape: pl.BlockSpec(shape, lambda i: (0,) * len(shape), pipeline_mode=pl.Buffered(1))
    rows_t = lambda rows: pl.BlockSpec((rows, tm), lambda i: (0, i))
    table = lambda rows: pl.BlockSpec((rows, tm), lambda i: (0, i % ns))
    k_tile = pl.BlockSpec((tm, N_HEADS * HEAD_PAD), lambda i: (i, 0))
    vt_tile = pl.BlockSpec((1, N_HEADS, 1, HEAD_PAD, tm), lambda i: (i // ns, 0, i % ns, 0, 0))
    qkv_shapes = [
        jax.ShapeDtypeStruct((N_HEADS * HEAD_PAD, t), BF16),
        jax.ShapeDtypeStruct((t, N_HEADS * HEAD_PAD), BF16),
        jax.ShapeDtypeStruct((batch, N_HEADS, ns, HEAD_PAD, tm), BF16),
    ]
    return pl.pallas_call(
        functools.partial(_front_body, tiles_per_seq=ns),
        grid=(t // tm,),
        in_specs=[
            pl.BlockSpec((tm, D_MODEL), lambda i: (i, 0)),
            const((1, D_MODEL)),
            const((ROWS_ABC, D_MODEL)),
            const((ROWS_D + ROWS_F, D_MODEL)),
            const((GROUP_WIDTH, 1)),
            const((N_HEADS, CHUNK, CHUNK)),
            const((N_HEADS, CHUNK)),
            table(HEAD_DIM // 2), table(HEAD_DIM // 2),
            const((ROWS_F, 1)),
            table(MLA_ROPE // 2), table(MLA_ROPE // 2),
            const((MLA_Q_LORA, 1)),
            const((MLA_KV_LORA, 1)),
            const((N_HEADS * HEAD_PAD, MLA_Q_LORA)),
            const((GROUP_WIDTH, MLA_KV_LORA)),
            const((GROUP_WIDTH, MLA_KV_LORA)),
        ],
        out_specs=[
            rows_t(GROUP_WIDTH), rows_t(GROUP_WIDTH),
            rows_t(N_HEADS * HEAD_PAD), k_tile, vt_tile,
            pl.BlockSpec((1, 1, 8, 128), lambda i: (i // ns, i % ns, 0, 0)),
            rows_t(N_HEADS * HEAD_PAD), k_tile, vt_tile,
        ],
        out_shape=[
            jax.ShapeDtypeStruct((GROUP_WIDTH, t), F32),
            jax.ShapeDtypeStruct((GROUP_WIDTH, t), F32),
            *qkv_shapes,
            jax.ShapeDtypeStruct((batch, ns, 8, 128), F32),
            *qkv_shapes,
        ],
        scratch_shapes=[
            pltpu.VMEM((ROWS_A, tm), BF16),
            pltpu.VMEM((ROWS_B, tm), BF16),
            pltpu.VMEM((ROWS_C, tm), BF16),
            pltpu.VMEM((ROWS_D, tm), BF16),
            pltpu.VMEM((ROWS_F, tm), F32),
            pltpu.VMEM((N_HEADS, HEAD_DIM, HEAD_DIM), F32),
            pltpu.VMEM((ROWS_F, 1), F32),
        ],
        compiler_params=_params("arbitrary"),
        name="front",
    )(x, g, w_t, wdf_t, gain_col, w_s, b_s, cos_b, sin_b, bf_col,
      cos_d, sin_d, gq_col, gkv_col, wq_t, wk_t, wv_t)


def _first_needed_block(stats_ref, b, qi, nb):
    def stat(j, r, h):
        return stats_ref[((b * nb + j) * N_STATS + r) * N_HEADS + h]

    skipped = jnp.int32(0)
    leading = jnp.bool_(True)
    for j in range(nb - 1):
        zero = j < qi
        for h in range(N_HEADS):
            bound = (stat(qi, STAT_QNORM, h) * stat(j, STAT_KNORM, h)
                     + stat(qi, STAT_CMAX, h) - stat(j, STAT_CMIN, h))
            zero = jnp.logical_and(zero, bound - stat(qi, STAT_DIAG, h) < -EXP2_UNDERFLOW)
        leading = jnp.logical_and(leading, zero)
        skipped = skipped + leading.astype(jnp.int32)
    return skipped


def _attn_body(*refs, banded):
    if banded:
        stats_ref, qt_ref, k_ref, vt_ref, o_ref, acc_ref, m_ref, s_ref = refs
    else:
        qt_ref, k_ref, vt_ref, o_ref, acc_ref, m_ref, s_ref = refs
    blk = qt_ref.shape[1]
    nb = vt_ref.shape[2]
    qi = pl.program_id(1)
    first = _first_needed_block(stats_ref, pl.program_id(0), qi, nb) if banded else 0
    m_ref[...] = jnp.full(m_ref.shape, -jnp.inf, F32)
    acc_ref[...] = jnp.zeros_like(acc_ref)

    def logits(kj, h):
        start = pl.multiple_of(kj * blk, blk)
        g0, g1 = h * HEAD_PAD, (h + 1) * HEAD_PAD
        return jnp.dot(k_ref[pl.ds(start, blk), g0:g1], qt_ref[g0:g1, :],
                       preferred_element_type=F32)

    def accumulate(s, kj, h):
        m_old = m_ref[h]
        m_new = jnp.maximum(m_old, jnp.max(s, axis=0, keepdims=True))
        p = jnp.exp2(s - m_new)
        alpha = jnp.exp2(m_old - m_new)
        pv = jnp.dot(vt_ref[0, h, kj, 0:PV_ROWS, :], p.astype(BF16), preferred_element_type=F32)
        acc_ref[h] = acc_ref[h] * alpha + pv[0:HEAD_DIM + AUG_ROWS]
        m_ref[h] = m_new

    s_ref[...] = logits(first, 0)

    def full_step(kj, carry):
        s = s_ref[...]
        for h in range(N_HEADS):
            s_next = logits(kj, h + 1) if h + 1 < N_HEADS else logits(kj + 1, 0)
            accumulate(s, kj, h)
            s = s_next
        s_ref[...] = s
        return carry

    lax.fori_loop(first, qi, full_step, 0)

    key_pos = lax.broadcasted_iota(jnp.int32, (blk, blk), 0)
    qry_pos = lax.broadcasted_iota(jnp.int32, (blk, blk), 1)
    visible = key_pos <= qry_pos
    s = s_ref[...]
    for h in range(N_HEADS):
        s_next = logits(qi, h + 1) if h + 1 < N_HEADS else None
        accumulate(jnp.where(visible, s, -jnp.inf), qi, h)
        s = s_next
    for h in range(N_HEADS):
        acc = acc_ref[h]
        o_ref[h * HEAD_DIM:(h + 1) * HEAD_DIM, :] = acc[0:HEAD_DIM] / acc[HEAD_DIM:HEAD_DIM + 1]


def _attention(qt, k, vt, stats=None):
    batch, _, nb, _, blk = vt.shape
    t = qt.shape[1]
    seq = t // batch
    banded = stats is not None
    in_specs = [
        pl.BlockSpec((N_HEADS * HEAD_PAD, blk), lambda b, i: (0, b * nb + i)),
        pl.BlockSpec((seq, N_HEADS * HEAD_PAD), lambda b, i: (b, 0)),
        pl.BlockSpec((1, N_HEADS, nb, HEAD_PAD, blk), lambda b, i: (b, 0, 0, 0, 0)),
    ]
    args = (qt, k, vt)
    if banded:
        in_specs = [pl.BlockSpec(memory_space=pltpu.SMEM)] + in_specs
        args = (stats,) + args
    return pl.pallas_call(
        functools.partial(_attn_body, banded=banded),
        grid=(batch, nb),
        in_specs=in_specs,
        out_specs=pl.BlockSpec((GROUP_WIDTH, blk), lambda b, i: (0, b * nb + i)),
        out_shape=jax.ShapeDtypeStruct((GROUP_WIDTH, t), F32),
        scratch_shapes=[pltpu.VMEM((N_HEADS, HEAD_DIM + AUG_ROWS, blk), F32),
                        pltpu.VMEM((N_HEADS, 1, blk), F32),
                        pltpu.VMEM((blk, blk), F32)],
        compiler_params=_params("parallel", "arbitrary"),
        name="attention",
    )(*args)


def _post_body(x_ref, ya_ref, yb_ref, yc_ref, yd_ref, gg_ref, wo_ref, gf_ref, wu_ref, wd_ref,
               gl_ref, o_ref, *, final):
    ys = []
    for g, y_ref in enumerate((ya_ref, yb_ref, yc_ref, yd_ref)):
        ys.append((_rms_rows(y_ref[...]) * gg_ref[g]).astype(BF16))
    y = jnp.concatenate(ys, axis=0)
    x = x_ref[...] + lax.dot_general(y, wo_ref[...], TN, preferred_element_type=F32)
    h = (x * lax.rsqrt(jnp.mean(x * x, axis=-1, keepdims=True) + EPS) * gf_ref[...]).astype(BF16)
    acc = x
    for c in range(D_FF // FF_CHUNK):
        c0, c1 = c * FF_CHUNK, (c + 1) * FF_CHUNK
        a = jnp.maximum(jnp.dot(h, wu_ref[:, c0:c1], preferred_element_type=F32), 0.0)
        acc = acc + jnp.dot((a * a).astype(BF16), wd_ref[c0:c1, :], preferred_element_type=F32)
    if final:
        acc = acc * lax.rsqrt(jnp.mean(acc * acc, axis=-1, keepdims=True) + EPS) * gl_ref[...]
    o_ref[...] = acc


def _post(x, ya, yb, yc, yd, gg_col, w_out, g_ffn, w_up, w_down, g_final, final):
    t = x.shape[0]
    tm = TOKEN_TILE
    const = lambda shape: pl.BlockSpec(shape, lambda i: (0,) * len(shape), pipeline_mode=pl.Buffered(1))
    ytile = pl.BlockSpec((GROUP_WIDTH, tm), lambda i: (0, i))
    return pl.pallas_call(
        functools.partial(_post_body, final=final),
        grid=(t // tm,),
        in_specs=[
            pl.BlockSpec((tm, D_MODEL), lambda i: (i, 0)),
            ytile, ytile, ytile, ytile,
            const((4, GROUP_WIDTH, 1)),
            const((D_MODEL, D_MODEL)),
            const((1, D_MODEL)),
            const((D_MODEL, D_FF)),
            const((D_FF, D_MODEL)),
            const((1, D_MODEL)),
        ],
        out_specs=pl.BlockSpec((tm, D_MODEL), lambda i: (i, 0)),
        out_shape=jax.ShapeDtypeStruct((t, D_MODEL), F32),
        compiler_params=_params("parallel"),
        name="post",
    )(x, ya, yb, yc, yd, gg_col, w_out, g_ffn, w_up, w_down, g_final)


def _rope_tables(seq, half):
    inv_freq = np.power(ROPE_BASE, -np.arange(half, dtype=np.float64) / half)
    ang = inv_freq[:, None] * np.arange(seq, dtype=np.float64)[None, :]
    return jnp.asarray(np.cos(ang), F32), jnp.asarray(np.sin(ang), F32)


def _inproj_weights(w_in):
    w_abc = w_in[:, :ROWS_ABC].T.astype(BF16)
    f = w_in[:, ROWS_ABC:ROWS_ABC + N_HEADS].T
    d = w_in[:, ROWS_ABC + N_HEADS:ROWS_ABC + N_HEADS + ROWS_D].T
    pad = jnp.zeros((ROWS_F - N_HEADS, D_MODEL), w_in.dtype)
    return w_abc, jnp.concatenate([d, f, pad], axis=0).astype(BF16)


def _mla_weights(w_uq, w_ukv):
    wq = w_uq.T.reshape(N_HEADS, MLA_NOPE + MLA_ROPE, MLA_Q_LORA)
    wq = jnp.pad(wq, ((0, 0), (0, HEAD_PAD - MLA_NOPE - MLA_ROPE), (0, 0)))
    wq = wq.reshape(N_HEADS * HEAD_PAD, MLA_Q_LORA).astype(BF16)
    wkv = w_ukv.T.reshape(N_HEADS, 2 * HEAD_DIM, MLA_KV_LORA)
    wk = wkv[:, :HEAD_DIM].reshape(GROUP_WIDTH, MLA_KV_LORA).astype(BF16)
    wv = wkv[:, HEAD_DIM:].reshape(GROUP_WIDTH, MLA_KV_LORA).astype(BF16)
    return wq, wk, wv


def kernel(x, g_mix_norm, w_in, b_forget, g_sgu, w_spatial, b_spatial, g_mla_q, w_uq, g_mla_kv, w_ukv,
           g_group_out, w_out, g_ffn_norm, w_up, w_down, g_final):
    batch, seq, _ = x.shape
    depth = w_in.shape[0]
    assert seq % TOKEN_TILE == 0 and ATTN_BLOCK == TOKEN_TILE
    cos_b, sin_b = _rope_tables(seq, HEAD_DIM // 2)
    cos_d, sin_d = _rope_tables(seq, MLA_ROPE // 2)
    xf = x.reshape(batch * seq, D_MODEL)
    for l in range(depth):
        bf_col = jnp.pad(b_forget[l], (0, ROWS_F - N_HEADS))[:, None]
        ya, yb, qt_c, k_c, vt_c, stats, qt_d, k_d, vt_d = _front(
            xf, g_mix_norm[l][None, :], *_inproj_weights(w_in[l]),
            g_sgu[l][:, None], w_spatial[l], b_spatial[l], cos_b, sin_b, bf_col,
            cos_d, sin_d, g_mla_q[l][:, None], g_mla_kv[l][:, None], *_mla_weights(w_uq[l], w_ukv[l]),
            batch)
        yc = _attention(qt_c, k_c, vt_c, stats[:, :, :N_STATS, :N_HEADS].reshape(-1))
        yd = _attention(qt_d, k_d, vt_d)
        xf = _post(xf, ya, yb, yc, yd, g_group_out[l].reshape(4, GROUP_WIDTH, 1),
                   w_out[l].astype(BF16), g_ffn_norm[l][None, :], w_up[l].astype(BF16),
                   w_down[l].astype(BF16), g_final[None, :], final=(l == depth - 1))
    return xf.reshape(batch, seq, D_MODEL)
```

```python
import functools
import math

import jax
import jax.numpy as jnp
import numpy as np
from jax import lax
from jax.experimental import pallas as pl
from jax.experimental.pallas import tpu as pltpu

F32 = jnp.float32
BF16 = jnp.bfloat16

D_MODEL = 1024
N_HEADS = 4
HEAD_DIM = 64
GROUP_WIDTH = N_HEADS * HEAD_DIM
CHUNK = 128
MLA_Q_LORA = 256
MLA_KV_LORA = 128
MLA_NOPE = 64
MLA_ROPE = 32
ROPE_BASE = 10000.0
D_FF = 4 * D_MODEL
EPS = 1e-6

HEAD_PAD = 128
AUG_ROWS = 8
PV_ROWS = 80
LOG2E = math.log2(math.e)
EXP2_UNDERFLOW = 152.0
STAT_QNORM, STAT_KNORM, STAT_CMAX, STAT_CMIN, STAT_DIAG, STAT_GAP = range(6)
N_STATS = 6
GAP_LIMIT = 64.0
TOKEN_TILE = 512
ATTN_BLOCK = TOKEN_TILE
FF_CHUNK = 512
VMEM_LIMIT = 56 * 1024 * 1024

ROWS_A = 2 * GROUP_WIDTH
ROWS_B = 4 * GROUP_WIDTH
ROWS_C = 3 * GROUP_WIDTH
ROWS_D = MLA_Q_LORA + MLA_KV_LORA + MLA_ROPE
ROWS_F = 8
ROWS_ABC = ROWS_A + ROWS_B + ROWS_C

NT = (((1,), (1,)), ((), ()))
TN = (((0,), (0,)), ((), ()))


def _params(*sem):
    return pltpu.CompilerParams(dimension_semantics=sem, vmem_limit_bytes=VMEM_LIMIT)


def _rot_half_rows(t, cos, sin):
    half = t.shape[0] // 2
    t1, t2 = t[:half], t[half:]
    return jnp.concatenate([t1 * cos - t2 * sin, t1 * sin + t2 * cos], axis=0)


def _standardize_rows(t):
    mu = jnp.mean(t, axis=0, keepdims=True)
    var = jnp.mean(jnp.square(t - mu), axis=0, keepdims=True)
    return (t - mu) * lax.rsqrt(var + EPS)


def _rms_rows(t):
    return t * lax.rsqrt(jnp.mean(t * t, axis=0, keepdims=True) + EPS)


def _sgu(at_ref, gain_ref, ws_ref, bs_ref, o_ref):
    tm = at_ref.shape[1]
    nch = tm // CHUNK
    row = lax.broadcasted_iota(jnp.int32, (CHUNK, CHUNK), 0)
    col = lax.broadcasted_iota(jnp.int32, (CHUNK, CHUNK), 1)
    for h in range(N_HEADS):
        r0, r1 = h * HEAD_DIM, (h + 1) * HEAD_DIM
        u = jax.nn.gelu(at_ref[r0:r1, :].astype(F32))
        v = jax.nn.gelu(at_ref[GROUP_WIDTH + r0:GROUP_WIDTH + r1, :].astype(F32))
        v = _standardize_rows(v) * gain_ref[r0:r1, :]
        w = jnp.where(col <= row, ws_ref[h], 0.0).astype(BF16)
        vs = jnp.concatenate([v[:, c * CHUNK:(c + 1) * CHUNK] for c in range(nch)], axis=0).astype(BF16)
        mixed = lax.dot_general(vs, w, NT, preferred_element_type=F32) + bs_ref[h:h + 1, :]
        for c in range(nch):
            o_ref[r0:r1, c * CHUNK:(c + 1) * CHUNK] = (
                u[:, c * CHUNK:(c + 1) * CHUNK] * mixed[c * HEAD_DIM:(c + 1) * HEAD_DIM])


def _retention(bt_ref, cos_ref, sin_ref, o_ref, st_ref):
    tm = bt_ref.shape[1]
    nch = tm // CHUNK
    cos, sin = cos_ref[...], sin_ref[...]
    srow = lax.broadcasted_iota(jnp.int32, (CHUNK, CHUNK), 0)
    tcol = lax.broadcasted_iota(jnp.int32, (CHUNK, CHUNK), 1)
    rel = (tcol - srow).astype(F32)
    j = lax.broadcasted_iota(jnp.int32, (1, CHUNK), 1).astype(F32)
    for h in range(N_HEADS):
        log_gamma = math.log1p(-(2.0 ** (-5.0 - h)))
        dec_t = jnp.where(rel >= 0, jnp.exp(jnp.maximum(rel, 0.0) * log_gamma), 0.0)
        query_w = jnp.exp((j + 1.0) * log_gamma)
        key_w = jnp.exp((CHUNK - 1.0 - j) * log_gamma)
        chunk_decay = math.exp(CHUNK * log_gamma)
        r0, r1 = h * HEAD_DIM, (h + 1) * HEAD_DIM
        q = _rot_half_rows(bt_ref[r0:r1, :].astype(F32), cos, sin)
        k = _rot_half_rows(bt_ref[GROUP_WIDTH + r0:GROUP_WIDTH + r1, :].astype(F32), cos, sin) * (HEAD_DIM ** -0.5)
        v = bt_ref[2 * GROUP_WIDTH + r0:2 * GROUP_WIDTH + r1, :].astype(F32)
        g = bt_ref[3 * GROUP_WIDTH + r0:3 * GROUP_WIDTH + r1, :].astype(F32)
        st = st_ref[h]
        ys = []
        for c in range(nch):
            sl = slice(c * CHUNK, (c + 1) * CHUNK)
            qc, kc, vc = q[:, sl], k[:, sl], v[:, sl]
            kcb = kc.astype(BF16)
            a_t = lax.dot_general(kcb, qc.astype(BF16), TN, preferred_element_type=F32)
            p_t = (a_t * dec_t).astype(BF16)
            intra = jnp.dot(vc.astype(BF16), p_t, preferred_element_type=F32)
            cross = jnp.dot(st.astype(BF16), (qc * query_w).astype(BF16), preferred_element_type=F32)
            ys.append(intra + cross)
            st = chunk_decay * st + lax.dot_general((vc * key_w).astype(BF16), kcb, NT,
                                                    preferred_element_type=F32)
        st_ref[h] = st
        y = _standardize_rows(jnp.concatenate(ys, axis=1))
        o_ref[r0:r1, :] = jax.nn.silu(g) * y


def _select_rows(rows):
    n = rows[0].shape[1]
    ridx = lax.broadcasted_iota(jnp.int32, (AUG_ROWS, n), 0)
    out = jnp.zeros((AUG_ROWS, n), F32)
    for i, r in enumerate(rows):
        out = jnp.where(ridx == i, jnp.broadcast_to(r, (AUG_ROWS, n)), out)
    return out


def _group(feat, extra):
    n = feat.shape[1]
    pad = HEAD_PAD - feat.shape[0] - extra.shape[0]
    return jnp.concatenate([feat, extra, jnp.zeros((pad, n), F32)], axis=0)


def _split3(x):
    hi = x.astype(BF16).astype(F32)
    mid = (x - hi).astype(BF16).astype(F32)
    lo = (x - hi - mid).astype(BF16).astype(F32)
    return hi, mid, lo


def _logit_bound(qb, kb, kmax_ref, row):
    qnorm = jnp.sqrt(jnp.sum(qb * qb, axis=0, keepdims=True))
    knorm = jnp.sqrt(jnp.max(jnp.sum(kb * kb, axis=0, keepdims=True), axis=1, keepdims=True))
    kmax = jnp.maximum(kmax_ref[row:row + 1, :], knorm)
    kmax_ref[row:row + 1, :] = kmax
    return qnorm * kmax, jnp.sum(qb * kb, axis=0, keepdims=True), qnorm, knorm


def _write_stats(st_ref, per_head):
    stat_row = lax.broadcasted_iota(jnp.int32, (8, 128), 0)
    stat_lane = lax.broadcasted_iota(jnp.int32, (8, 128), 1)
    stats = jnp.zeros((8, 128), F32)
    for h, vals in enumerate(per_head):
        for r, val in enumerate(vals):
            stats = jnp.where((stat_row == r) & (stat_lane == h), val, stats)
    st_ref[0, 0] = stats


def _fox_prep(ct_ref, ft_ref, bf_ref, qt_ref, k_ref, vt_ref, st_ref, carry_ref, kmax_ref):
    tm = ct_ref.shape[1]
    x = ft_ref[...] + bf_ref[...]
    lf = jnp.minimum(x, 0.0) - jnp.log1p(jnp.exp(-jnp.abs(x)))
    hi = lf.astype(BF16)
    mid = (lf - hi.astype(F32)).astype(BF16)
    lo = (lf - hi.astype(F32) - mid.astype(F32)).astype(BF16)
    srow = lax.broadcasted_iota(jnp.int32, (tm, tm), 0)
    tcol = lax.broadcasted_iota(jnp.int32, (tm, tm), 1)
    upper = jnp.where(srow <= tcol, 1.0, 0.0).astype(BF16)
    parts = jnp.dot(jnp.concatenate([hi, mid, lo], axis=0), upper, preferred_element_type=F32)
    cum = parts[0:8] + parts[8:16] + parts[16:24] + carry_ref[...]
    carry_ref[...] = cum[:, tm - 1:tm]

    cum2 = cum * LOG2E
    one = jnp.ones((1, tm), F32)
    ones_row = _select_rows([one])
    stats = []
    for h in range(N_HEADS):
        r0, r1 = h * HEAD_DIM, (h + 1) * HEAD_DIM
        g0, g1 = h * HEAD_PAD, (h + 1) * HEAD_PAD
        q = ct_ref[r0:r1, :].astype(F32) * (HEAD_DIM ** -0.5 * LOG2E)
        k = ct_ref[GROUP_WIDTH + r0:GROUP_WIDTH + r1, :].astype(F32)
        v = ct_ref[2 * GROUP_WIDTH + r0:2 * GROUP_WIDTH + r1, :].astype(F32)
        qb, kb = q.astype(BF16).astype(F32), k.astype(BF16).astype(F32)
        bound, diag, qnorm, knorm = _logit_bound(qb, kb, kmax_ref, h)
        c2 = cum2[h:h + 1]
        q_extra = _select_rows([*_split3(c2 - bound), one, one, one])
        k_extra = _select_rows([one, one, one, *(-part for part in _split3(c2))])
        qt_ref[g0:g1, :] = _group(q, q_extra).astype(BF16)
        k_ref[:, g0:g1] = _group(k, k_extra).T.astype(BF16)
        vt_ref[0, h, 0] = _group(v, ones_row).astype(BF16)
        stats.append((
            jnp.max(qnorm, axis=1, keepdims=True),
            knorm,
            jnp.max(c2, axis=1, keepdims=True),
            jnp.min(c2, axis=1, keepdims=True),
            jnp.min(diag, axis=1, keepdims=True),
            jnp.max(bound - diag, axis=1, keepdims=True),
        ))
    _write_stats(st_ref, stats)


def _mla_prep(dt_ref, cos_ref, sin_ref, gq_ref, gkv_ref, wq_ref, wk_ref, wv_ref, qt_ref, k_ref, vt_ref,
              st_ref, kmax_ref):
    tm = dt_ref.shape[1]
    cos, sin = cos_ref[...], sin_ref[...]
    cq = _rms_rows(dt_ref[0:MLA_Q_LORA, :].astype(F32)) * gq_ref[...]
    ckv = _rms_rows(dt_ref[MLA_Q_LORA:MLA_Q_LORA + MLA_KV_LORA, :].astype(F32)) * gkv_ref[...]
    kr = _rot_half_rows(dt_ref[MLA_Q_LORA + MLA_KV_LORA:ROWS_D, :].astype(F32), cos, sin)
    ckv_b = ckv.astype(BF16)
    q_all = jnp.dot(wq_ref[...], cq.astype(BF16), preferred_element_type=F32)
    q_all = q_all * ((MLA_NOPE + MLA_ROPE) ** -0.5 * LOG2E)
    k_all = jnp.dot(wk_ref[...], ckv_b, preferred_element_type=F32)
    v_all = jnp.dot(wv_ref[...], ckv_b, preferred_element_type=F32)
    one = jnp.ones((1, tm), F32)
    ones_row = _select_rows([one])
    k_extra = _select_rows([one, one, one])
    zero = jnp.zeros((1, 1), F32)
    stats = []
    for h in range(N_HEADS):
        r0, r1 = h * HEAD_DIM, (h + 1) * HEAD_DIM
        g0, g1 = h * HEAD_PAD, (h + 1) * HEAD_PAD
        qg = q_all[g0:g1]
        q_rope = _rot_half_rows(qg[MLA_NOPE:MLA_NOPE + MLA_ROPE], cos, sin)
        q = jnp.concatenate([qg[0:MLA_NOPE], q_rope], axis=0)
        k = jnp.concatenate([k_all[r0:r1], kr], axis=0)
        qb, kb = q.astype(BF16).astype(F32), k.astype(BF16).astype(F32)
        bound, diag, _, _ = _logit_bound(qb, kb, kmax_ref, N_HEADS + h)
        q_extra = _select_rows([*_split3(-bound)])
        qt_ref[g0:g1, :] = _group(q, q_extra).astype(BF16)
        k_ref[:, g0:g1] = _group(k, k_extra).T.astype(BF16)
        vt_ref[0, h, 0] = _group(v_all[r0:r1], ones_row).astype(BF16)
        stats.append((zero,) * STAT_GAP + (jnp.max(bound - diag, axis=1, keepdims=True),))
    _write_stats(st_ref, stats)


def _front_body(x_ref, g_ref, w_ref, wdf_ref,
                gain_ref, ws_ref, bs_ref,
                cosb_ref, sinb_ref,
                bf_ref,
                cosd_ref, sind_ref, gq_ref, gkv_ref, wq_ref, wk_ref, wv_ref,
                ya_ref, yb_ref,
                qtc_ref, kc_ref, vtc_ref, stats_ref,
                qtd_ref, kd_ref, vtd_ref, statsd_ref,
                at_ref, bt_ref, ct_ref, dt_ref, ft_ref, state_ref, carry_ref, kmax_ref, *, tiles_per_seq):
    @pl.when(pl.program_id(0) % tiles_per_seq == 0)
    def _():
        state_ref[...] = jnp.zeros_like(state_ref)
        carry_ref[...] = jnp.zeros_like(carry_ref)
        kmax_ref[...] = jnp.zeros_like(kmax_ref)

    x = x_ref[...]
    h = (x * lax.rsqrt(jnp.mean(x * x, axis=-1, keepdims=True) + EPS) * g_ref[...]).astype(BF16)

    def proj(w):
        return lax.dot_general(w, h, NT, preferred_element_type=F32)

    half_b = ROWS_B // 2
    at_ref[...] = proj(w_ref[0:ROWS_A, :]).astype(BF16)
    bt_ref[0:half_b, :] = proj(w_ref[ROWS_A:ROWS_A + half_b, :]).astype(BF16)
    bt_ref[half_b:ROWS_B, :] = proj(w_ref[ROWS_A + half_b:ROWS_A + ROWS_B, :]).astype(BF16)
    _sgu(at_ref, gain_ref, ws_ref, bs_ref, ya_ref)
    ct_ref[...] = proj(w_ref[ROWS_A + ROWS_B:ROWS_ABC, :]).astype(BF16)
    _retention(bt_ref, cosb_ref, sinb_ref, yb_ref, state_ref)
    zdf = proj(wdf_ref[...])
    dt_ref[...] = zdf[0:ROWS_D].astype(BF16)
    ft_ref[...] = zdf[ROWS_D:ROWS_D + ROWS_F]
    _fox_prep(ct_ref, ft_ref, bf_ref, qtc_ref, kc_ref, vtc_ref, stats_ref, carry_ref, kmax_ref)
    _mla_prep(dt_ref, cosd_ref, sind_ref, gq_ref, gkv_ref, wq_ref, wk_ref, wv_ref, qtd_ref, kd_ref, vtd_ref,
              statsd_ref, kmax_ref)


def _front(x, g, w_t, wdf_t, gain_col, w_s, b_s, cos_b, sin_b, bf_col,
           cos_d, sin_d, gq_col, gkv_col, wq_t, wk_t, wv_t, batch):
    t = x.shape[0]
    tm = TOKEN_TILE
    ns = t // batch // tm
    const = lambda shape: pl.BlockSpec(shape, lambda i: (0,) * len(shape), pipeline_mode=pl.Buffered(1))
    rows_t = lambda rows: pl.BlockSpec((rows, tm), lambda i: (0, i))
    table = lambda rows: pl.BlockSpec((rows, tm), lambda i: (0, i % ns))
    k_tile = pl.BlockSpec((tm, N_HEADS * HEAD_PAD), lambda i: (i, 0))
    vt_tile = pl.BlockSpec((1, N_HEADS, 1, HEAD_PAD, tm), lambda i: (i // ns, 0, i % ns, 0, 0))
    stats_tile = pl.BlockSpec((1, 1, 8, 128), lambda i: (i // ns, i % ns, 0, 0))
    qkv_shapes = [
        jax.ShapeDtypeStruct((N_HEADS * HEAD_PAD, t), BF16),
        jax.ShapeDtypeStruct((t, N_HEADS * HEAD_PAD), BF16),
        jax.ShapeDtypeStruct((batch, N_HEADS, ns, HEAD_PAD, tm), BF16),
        jax.ShapeDtypeStruct((batch, ns, 8, 128), F32),
    ]
    return pl.pallas_call(
        functools.partial(_front_body, tiles_per_seq=ns),
        grid=(t // tm,),
        in_specs=[
            pl.BlockSpec((tm, D_MODEL), lambda i: (i, 0)),
            const((1, D_MODEL)),
            const((ROWS_ABC, D_MODEL)),
            const((ROWS_D + ROWS_F, D_MODEL)),
            const((GROUP_WIDTH, 1)),
            const((N_HEADS, CHUNK, CHUNK)),
            const((N_HEADS, CHUNK)),
            table(HEAD_DIM // 2), table(HEAD_DIM // 2),
            const((ROWS_F, 1)),
            table(MLA_ROPE // 2), table(MLA_ROPE // 2),
            const((MLA_Q_LORA, 1)),
            const((MLA_KV_LORA, 1)),
            const((N_HEADS * HEAD_PAD, MLA_Q_LORA)),
            const((GROUP_WIDTH, MLA_KV_LORA)),
            const((GROUP_WIDTH, MLA_KV_LORA)),
        ],
        out_specs=[
            rows_t(GROUP_WIDTH), rows_t(GROUP_WIDTH),
            rows_t(N_HEADS * HEAD_PAD), k_tile, vt_tile, stats_tile,
            rows_t(N_HEADS * HEAD_PAD), k_tile, vt_tile, stats_tile,
        ],
        out_shape=[
            jax.ShapeDtypeStruct((GROUP_WIDTH, t), F32),
            jax.ShapeDtypeStruct((GROUP_WIDTH, t), F32),
            *qkv_shapes,
            *qkv_shapes,
        ],
        scratch_shapes=[
            pltpu.VMEM((ROWS_A, tm), BF16),
            pltpu.VMEM((ROWS_B, tm), BF16),
            pltpu.VMEM((ROWS_C, tm), BF16),
            pltpu.VMEM((ROWS_D, tm), BF16),
            pltpu.VMEM((ROWS_F, tm), F32),
            pltpu.VMEM((N_HEADS, HEAD_DIM, HEAD_DIM), F32),
            pltpu.VMEM((ROWS_F, 1), F32),
            pltpu.VMEM((2 * N_HEADS, 1), F32),
        ],
        compiler_params=_params("arbitrary"),
        name="front",
    )(x, g, w_t, wdf_t, gain_col, w_s, b_s, cos_b, sin_b, bf_col,
      cos_d, sin_d, gq_col, gkv_col, wq_t, wk_t, wv_t)


def _first_needed_block(stats_ref, b, qi, nb):
    def stat(j, r, h):
        return stats_ref[((b * nb + j) * N_STATS + r) * N_HEADS + h]

    skipped = jnp.int32(0)
    leading = jnp.bool_(True)
    for j in range(nb - 1):
        zero = j < qi
        for h in range(N_HEADS):
            bound = (stat(qi, STAT_QNORM, h) * stat(j, STAT_KNORM, h)
                     + stat(qi, STAT_CMAX, h) - stat(j, STAT_CMIN, h))
            zero = jnp.logical_and(zero, bound - stat(qi, STAT_DIAG, h) < -EXP2_UNDERFLOW)
        leading = jnp.logical_and(leading, zero)
        skipped = skipped + leading.astype(jnp.int32)
    return skipped


def _attn_body(*refs, banded, online):
    refs = list(refs)
    stats_ref = refs.pop(0) if banded else None
    qt_ref, k_ref, vt_ref, o_ref, acc_ref, s_ref = refs[:6]
    m_ref = refs[6] if online else None
    blk = qt_ref.shape[1]
    nb = vt_ref.shape[2]
    qi = pl.program_id(1)
    first = _first_needed_block(stats_ref, pl.program_id(0), qi, nb) if banded else 0
    acc_ref[...] = jnp.zeros_like(acc_ref)
    if online:
        m_ref[...] = jnp.full(m_ref.shape, -jnp.inf, F32)

    def logits(kj, h):
        start = pl.multiple_of(kj * blk, blk)
        g0, g1 = h * HEAD_PAD, (h + 1) * HEAD_PAD
        return jnp.dot(k_ref[pl.ds(start, blk), g0:g1], qt_ref[g0:g1, :],
                       preferred_element_type=F32)

    def accumulate(s, kj, h):
        acc = acc_ref[h]
        if online:
            m_old = m_ref[h]
            m_new = jnp.maximum(m_old, jnp.max(s, axis=0, keepdims=True))
            m_ref[h] = m_new
            s = s - m_new
            acc = acc * jnp.exp2(m_old - m_new)
        pv = jnp.dot(vt_ref[0, h, kj, 0:PV_ROWS, :], jnp.exp2(s).astype(BF16), preferred_element_type=F32)
        acc_ref[h] = acc + pv[0:HEAD_DIM + AUG_ROWS]

    s_ref[...] = logits(first, 0)

    def full_step(kj, carry):
        s = s_ref[...]
        for h in range(N_HEADS):
            s_next = logits(kj, h + 1) if h + 1 < N_HEADS else logits(kj + 1, 0)
            accumulate(s, kj, h)
            s = s_next
        s_ref[...] = s
        return carry

    lax.fori_loop(first, qi, full_step, 0)

    key_pos = lax.broadcasted_iota(jnp.int32, (blk, blk), 0)
    qry_pos = lax.broadcasted_iota(jnp.int32, (blk, blk), 1)
    visible = key_pos <= qry_pos
    s = s_ref[...]
    for h in range(N_HEADS):
        s_next = logits(qi, h + 1) if h + 1 < N_HEADS else None
        accumulate(jnp.where(visible, s, -jnp.inf), qi, h)
        s = s_next
    for h in range(N_HEADS):
        acc = acc_ref[h]
        o_ref[h * HEAD_DIM:(h + 1) * HEAD_DIM, :] = acc[0:HEAD_DIM] / acc[HEAD_DIM:HEAD_DIM + 1]


def _attention_call(qt, k, vt, skip_stats, online):
    batch, _, nb, _, blk = vt.shape
    t = qt.shape[1]
    seq = t // batch
    banded = skip_stats is not None
    in_specs = [
        pl.BlockSpec((N_HEADS * HEAD_PAD, blk), lambda b, i: (0, b * nb + i)),
        pl.BlockSpec((seq, N_HEADS * HEAD_PAD), lambda b, i: (b, 0)),
        pl.BlockSpec((1, N_HEADS, nb, HEAD_PAD, blk), lambda b, i: (b, 0, 0, 0, 0)),
    ]
    args = (qt, k, vt)
    if banded:
        in_specs = [pl.BlockSpec(memory_space=pltpu.SMEM)] + in_specs
        args = (skip_stats,) + args
    scratch = [pltpu.VMEM((N_HEADS, HEAD_DIM + AUG_ROWS, blk), F32), pltpu.VMEM((blk, blk), F32)]
    if online:
        scratch.append(pltpu.VMEM((N_HEADS, 1, blk), F32))
    return pl.pallas_call(
        functools.partial(_attn_body, banded=banded, online=online),
        grid=(batch, nb),
        in_specs=in_specs,
        out_specs=pl.BlockSpec((GROUP_WIDTH, blk), lambda b, i: (0, b * nb + i)),
        out_shape=jax.ShapeDtypeStruct((GROUP_WIDTH, t), F32),
        scratch_shapes=scratch,
        compiler_params=_params("parallel", "arbitrary"),
        name="attention_online" if online else "attention",
    )(*args)


def _attention(qt, k, vt, stats, banded):
    stats = stats[:, :, :N_STATS, :N_HEADS]
    skip_stats = stats.reshape(-1) if banded else None
    bound_is_tight = jnp.max(stats[:, :, STAT_GAP, :]) <= GAP_LIMIT
    return lax.cond(bound_is_tight,
                    lambda: _attention_call(qt, k, vt, skip_stats, online=False),
                    lambda: _attention_call(qt, k, vt, skip_stats, online=True))


def _post_body(x_ref, ya_ref, yb_ref, yc_ref, yd_ref, gg_ref, wo_ref, gf_ref, wu_ref, wd_ref,
               gl_ref, o_ref, *, final):
    ys = []
    for g, y_ref in enumerate((ya_ref, yb_ref, yc_ref, yd_ref)):
        ys.append((_rms_rows(y_ref[...]) * gg_ref[g]).astype(BF16))
    y = jnp.concatenate(ys, axis=0)
    x = x_ref[...] + lax.dot_general(y, wo_ref[...], TN, preferred_element_type=F32)
    h = (x * lax.rsqrt(jnp.mean(x * x, axis=-1, keepdims=True) + EPS) * gf_ref[...]).astype(BF16)
    acc = x
    for c in range(D_FF // FF_CHUNK):
        c0, c1 = c * FF_CHUNK, (c + 1) * FF_CHUNK
        a = jnp.maximum(jnp.dot(h, wu_ref[:, c0:c1], preferred_element_type=F32), 0.0)
        acc = acc + jnp.dot((a * a).astype(BF16), wd_ref[c0:c1, :], preferred_element_type=F32)
    if final:
        acc = acc * lax.rsqrt(jnp.mean(acc * acc, axis=-1, keepdims=True) + EPS) * gl_ref[...]
    o_ref[...] = acc


def _post(x, ya, yb, yc, yd, gg_col, w_out, g_ffn, w_up, w_down, g_final, final):
    t = x.shape[0]
    tm = TOKEN_TILE
    const = lambda shape: pl.BlockSpec(shape, lambda i: (0,) * len(shape), pipeline_mode=pl.Buffered(1))
    ytile = pl.BlockSpec((GROUP_WIDTH, tm), lambda i: (0, i))
    return pl.pallas_call(
        functools.partial(_post_body, final=final),
        grid=(t // tm,),
        in_specs=[
            pl.BlockSpec((tm, D_MODEL), lambda i: (i, 0)),
            ytile, ytile, ytile, ytile,
            const((4, GROUP_WIDTH, 1)),
            const((D_MODEL, D_MODEL)),
            const((1, D_MODEL)),
            const((D_MODEL, D_FF)),
            const((D_FF, D_MODEL)),
            const((1, D_MODEL)),
        ],
        out_specs=pl.BlockSpec((tm, D_MODEL), lambda i: (i, 0)),
        out_shape=jax.ShapeDtypeStruct((t, D_MODEL), F32),
        compiler_params=_params("parallel"),
        name="post",
    )(x, ya, yb, yc, yd, gg_col, w_out, g_ffn, w_up, w_down, g_final)


def _rope_tables(seq, half):
    inv_freq = np.power(ROPE_BASE, -np.arange(half, dtype=np.float64) / half)
    ang = inv_freq[:, None] * np.arange(seq, dtype=np.float64)[None, :]
    return jnp.asarray(np.cos(ang), F32), jnp.asarray(np.sin(ang), F32)


def _inproj_weights(w_in):
    w_abc = w_in[:, :ROWS_ABC].T.astype(BF16)
    f = w_in[:, ROWS_ABC:ROWS_ABC + N_HEADS].T
    d = w_in[:, ROWS_ABC + N_HEADS:ROWS_ABC + N_HEADS + ROWS_D].T
    pad = jnp.zeros((ROWS_F - N_HEADS, D_MODEL), w_in.dtype)
    return w_abc, jnp.concatenate([d, f, pad], axis=0).astype(BF16)


def _mla_weights(w_uq, w_ukv):
    wq = w_uq.T.reshape(N_HEADS, MLA_NOPE + MLA_ROPE, MLA_Q_LORA)
    wq = jnp.pad(wq, ((0, 0), (0, HEAD_PAD - MLA_NOPE - MLA_ROPE), (0, 0)))
    wq = wq.reshape(N_HEADS * HEAD_PAD, MLA_Q_LORA).astype(BF16)
    wkv = w_ukv.T.reshape(N_HEADS, 2 * HEAD_DIM, MLA_KV_LORA)
    wk = wkv[:, :HEAD_DIM].reshape(GROUP_WIDTH, MLA_KV_LORA).astype(BF16)
    wv = wkv[:, HEAD_DIM:].reshape(GROUP_WIDTH, MLA_KV_LORA).astype(BF16)
    return wq, wk, wv


def kernel(x, g_mix_norm, w_in, b_forget, g_sgu, w_spatial, b_spatial, g_mla_q, w_uq, g_mla_kv, w_ukv,
           g_group_out, w_out, g_ffn_norm, w_up, w_down, g_final):
    batch, seq, _ = x.shape
    depth = w_in.shape[0]
    assert seq % TOKEN_TILE == 0 and ATTN_BLOCK == TOKEN_TILE
    cos_b, sin_b = _rope_tables(seq, HEAD_DIM // 2)
    cos_d, sin_d = _rope_tables(seq, MLA_ROPE // 2)
    xf = x.reshape(batch * seq, D_MODEL)
    for l in range(depth):
        bf_col = jnp.pad(b_forget[l], (0, ROWS_F - N_HEADS))[:, None]
        ya, yb, qt_c, k_c, vt_c, stats_c, qt_d, k_d, vt_d, stats_d = _front(
            xf, g_mix_norm[l][None, :], *_inproj_weights(w_in[l]),
            g_sgu[l][:, None], w_spatial[l], b_spatial[l], cos_b, sin_b, bf_col,
            cos_d, sin_d, g_mla_q[l][:, None], g_mla_kv[l][:, None], *_mla_weights(w_uq[l], w_ukv[l]),
            batch)
        yc = _attention(qt_c, k_c, vt_c, stats_c, banded=True)
        yd = _attention(qt_d, k_d, vt_d, stats_d, banded=False)
        xf = _post(xf, ya, yb, yc, yd, g_group_out[l].reshape(4, GROUP_WIDTH, 1),
                   w_out[l].astype(BF16), g_ffn_norm[l][None, :], w_up[l].astype(BF16),
                   w_down[l].astype(BF16), g_final[None, :], final=(l == depth - 1))
    return xf.reshape(batch, seq, D_MODEL)
```

```python
import functools
import math

import jax
import jax.numpy as jnp
import numpy as np
from jax import lax
from jax.experimental import pallas as pl
from jax.experimental.pallas import tpu as pltpu

F32 = jnp.float32
BF16 = jnp.bfloat16

D_MODEL = 1024
N_HEADS = 4
HEAD_DIM = 64
GROUP_WIDTH = N_HEADS * HEAD_DIM
CHUNK = 128
MLA_Q_LORA = 256
MLA_KV_LORA = 128
MLA_NOPE = 64
MLA_ROPE = 32
ROPE_BASE = 10000.0
D_FF = 4 * D_MODEL
EPS = 1e-6

HEAD_PAD = 128
AUG_ROWS = 8
PV_ROWS = 80
LOG2E = math.log2(math.e)
EXP2_UNDERFLOW = 152.0
STAT_QNORM, STAT_KNORM, STAT_CMAX, STAT_CMIN, STAT_DIAG, STAT_GAP = range(6)
N_STATS = 6
GAP_LIMIT = 64.0
TOKEN_TILE = 512
ATTN_BLOCK = TOKEN_TILE
FF_CHUNK = 512
VMEM_LIMIT = 56 * 1024 * 1024

ROWS_A = 2 * GROUP_WIDTH
ROWS_B = 4 * GROUP_WIDTH
ROWS_C = 3 * GROUP_WIDTH
ROWS_D = MLA_Q_LORA + MLA_KV_LORA + MLA_ROPE
ROWS_F = 8
ROWS_ABC = ROWS_A + ROWS_B + ROWS_C

NT = (((1,), (1,)), ((), ()))
TN = (((0,), (0,)), ((), ()))


def _params(*sem):
    return pltpu.CompilerParams(dimension_semantics=sem, vmem_limit_bytes=VMEM_LIMIT)


def _rot_half_rows(t, cos, sin):
    half = t.shape[0] // 2
    t1, t2 = t[:half], t[half:]
    return jnp.concatenate([t1 * cos - t2 * sin, t1 * sin + t2 * cos], axis=0)


def _standardize_rows(t):
    mu = jnp.mean(t, axis=0, keepdims=True)
    var = jnp.mean(jnp.square(t - mu), axis=0, keepdims=True)
    return (t - mu) * lax.rsqrt(var + EPS)


def _rms_rows(t):
    return t * lax.rsqrt(jnp.mean(t * t, axis=0, keepdims=True) + EPS)


def _sgu(at_ref, gain_ref, ws_ref, bs_ref, o_ref):
    tm = at_ref.shape[1]
    nch = tm // CHUNK
    row = lax.broadcasted_iota(jnp.int32, (CHUNK, CHUNK), 0)
    col = lax.broadcasted_iota(jnp.int32, (CHUNK, CHUNK), 1)
    for h in range(N_HEADS):
        r0, r1 = h * HEAD_DIM, (h + 1) * HEAD_DIM
        u = jax.nn.gelu(at_ref[r0:r1, :].astype(F32))
        v = jax.nn.gelu(at_ref[GROUP_WIDTH + r0:GROUP_WIDTH + r1, :].astype(F32))
        v = _standardize_rows(v) * gain_ref[r0:r1, :]
        w = jnp.where(col <= row, ws_ref[h], 0.0).astype(BF16)
        vs = jnp.concatenate([v[:, c * CHUNK:(c + 1) * CHUNK] for c in range(nch)], axis=0).astype(BF16)
        mixed = lax.dot_general(vs, w, NT, preferred_element_type=F32) + bs_ref[h:h + 1, :]
        for c in range(nch):
            o_ref[r0:r1, c * CHUNK:(c + 1) * CHUNK] = (
                u[:, c * CHUNK:(c + 1) * CHUNK] * mixed[c * HEAD_DIM:(c + 1) * HEAD_DIM])


def _retention(bt_ref, cos_ref, sin_ref, o_ref, st_ref):
    tm = bt_ref.shape[1]
    nch = tm // CHUNK
    cos, sin = cos_ref[...], sin_ref[...]
    srow = lax.broadcasted_iota(jnp.int32, (CHUNK, CHUNK), 0)
    tcol = lax.broadcasted_iota(jnp.int32, (CHUNK, CHUNK), 1)
    rel = (tcol - srow).astype(F32)
    j = lax.broadcasted_iota(jnp.int32, (1, CHUNK), 1).astype(F32)
    for h in range(N_HEADS):
        log_gamma = math.log1p(-(2.0 ** (-5.0 - h)))
        dec_t = jnp.where(rel >= 0, jnp.exp(jnp.maximum(rel, 0.0) * log_gamma), 0.0)
        query_w = jnp.exp((j + 1.0) * log_gamma)
        key_w = jnp.exp((CHUNK - 1.0 - j) * log_gamma)
        chunk_decay = math.exp(CHUNK * log_gamma)
        r0, r1 = h * HEAD_DIM, (h + 1) * HEAD_DIM
        q = _rot_half_rows(bt_ref[r0:r1, :].astype(F32), cos, sin)
        k = _rot_half_rows(bt_ref[GROUP_WIDTH + r0:GROUP_WIDTH + r1, :].astype(F32), cos, sin) * (HEAD_DIM ** -0.5)
        v = bt_ref[2 * GROUP_WIDTH + r0:2 * GROUP_WIDTH + r1, :].astype(F32)
        g = bt_ref[3 * GROUP_WIDTH + r0:3 * GROUP_WIDTH + r1, :].astype(F32)
        st = st_ref[h]
        ys = []
        for c in range(nch):
            sl = slice(c * CHUNK, (c + 1) * CHUNK)
            qc, kc, vc = q[:, sl], k[:, sl], v[:, sl]
            kcb = kc.astype(BF16)
            a_t = lax.dot_general(kcb, qc.astype(BF16), TN, preferred_element_type=F32)
            p_t = (a_t * dec_t).astype(BF16)
            intra = jnp.dot(vc.astype(BF16), p_t, preferred_element_type=F32)
            cross = jnp.dot(st.astype(BF16), (qc * query_w).astype(BF16), preferred_element_type=F32)
            ys.append(intra + cross)
            st = chunk_decay * st + lax.dot_general((vc * key_w).astype(BF16), kcb, NT,
                                                    preferred_element_type=F32)
        st_ref[h] = st
        y = _standardize_rows(jnp.concatenate(ys, axis=1))
        o_ref[r0:r1, :] = jax.nn.silu(g) * y


def _select_rows(rows):
    n = rows[0].shape[1]
    ridx = lax.broadcasted_iota(jnp.int32, (AUG_ROWS, n), 0)
    out = jnp.zeros((AUG_ROWS, n), F32)
    for i, r in enumerate(rows):
        out = jnp.where(ridx == i, jnp.broadcast_to(r, (AUG_ROWS, n)), out)
    return out


def _group(feat, extra):
    n = feat.shape[1]
    pad = HEAD_PAD - feat.shape[0] - extra.shape[0]
    return jnp.concatenate([feat, extra, jnp.zeros((pad, n), F32)], axis=0)


def _split3(x):
    hi = x.astype(BF16).astype(F32)
    mid = (x - hi).astype(BF16).astype(F32)
    lo = (x - hi - mid).astype(BF16).astype(F32)
    return hi, mid, lo


def _logit_bound(qb, kb, kmax_ref, row):
    qnorm = jnp.sqrt(jnp.sum(qb * qb, axis=0, keepdims=True))
    knorm = jnp.sqrt(jnp.max(jnp.sum(kb * kb, axis=0, keepdims=True), axis=1, keepdims=True))
    kmax = jnp.maximum(kmax_ref[row:row + 1, :], knorm)
    kmax_ref[row:row + 1, :] = kmax
    return qnorm * kmax, jnp.sum(qb * kb, axis=0, keepdims=True), qnorm, knorm


def _write_stats(st_ref, per_head):
    stat_row = lax.broadcasted_iota(jnp.int32, (8, 128), 0)
    stat_lane = lax.broadcasted_iota(jnp.int32, (8, 128), 1)
    stats = jnp.zeros((8, 128), F32)
    for h, vals in enumerate(per_head):
        for r, val in enumerate(vals):
            stats = jnp.where((stat_row == r) & (stat_lane == h), val, stats)
    st_ref[0, 0] = stats


def _fox_prep(ct_ref, ft_ref, bf_ref, qt_ref, k_ref, vt_ref, st_ref, carry_ref, kmax_ref):
    tm = ct_ref.shape[1]
    x = ft_ref[...] + bf_ref[...]
    lf = jnp.minimum(x, 0.0) - jnp.log1p(jnp.exp(-jnp.abs(x)))
    hi = lf.astype(BF16)
    mid = (lf - hi.astype(F32)).astype(BF16)
    lo = (lf - hi.astype(F32) - mid.astype(F32)).astype(BF16)
    srow = lax.broadcasted_iota(jnp.int32, (tm, tm), 0)
    tcol = lax.broadcasted_iota(jnp.int32, (tm, tm), 1)
    upper = jnp.where(srow <= tcol, 1.0, 0.0).astype(BF16)
    parts = jnp.dot(jnp.concatenate([hi, mid, lo], axis=0), upper, preferred_element_type=F32)
    cum = parts[0:8] + parts[8:16] + parts[16:24] + carry_ref[...]
    carry_ref[...] = cum[:, tm - 1:tm]

    cum2 = cum * LOG2E
    one = jnp.ones((1, tm), F32)
    ones_row = _select_rows([one])
    stats = []
    for h in range(N_HEADS):
        r0, r1 = h * HEAD_DIM, (h + 1) * HEAD_DIM
        g0, g1 = h * HEAD_PAD, (h + 1) * HEAD_PAD
        q = ct_ref[r0:r1, :].astype(F32) * (HEAD_DIM ** -0.5 * LOG2E)
        k = ct_ref[GROUP_WIDTH + r0:GROUP_WIDTH + r1, :].astype(F32)
        v = ct_ref[2 * GROUP_WIDTH + r0:2 * GROUP_WIDTH + r1, :].astype(F32)
        qb, kb = q.astype(BF16).astype(F32), k.astype(BF16).astype(F32)
        bound, diag, qnorm, knorm = _logit_bound(qb, kb, kmax_ref, h)
        c2 = cum2[h:h + 1]
        q_extra = _select_rows([*_split3(c2 - bound), one, one, one])
        k_extra = _select_rows([one, one, one, *(-part for part in _split3(c2))])
        qt_ref[g0:g1, :] = _group(q, q_extra).astype(BF16)
        k_ref[:, g0:g1] = _group(k, k_extra).T.astype(BF16)
        vt_ref[0, h, 0] = _group(v, ones_row).astype(BF16)
        stats.append((
            jnp.max(qnorm, axis=1, keepdims=True),
            knorm,
            jnp.max(c2, axis=1, keepdims=True),
            jnp.min(c2, axis=1, keepdims=True),
            jnp.min(diag, axis=1, keepdims=True),
            jnp.max(bound - diag, axis=1, keepdims=True),
        ))
    _write_stats(st_ref, stats)


def _mla_prep(dt_ref, cos_ref, sin_ref, gq_ref, gkv_ref, wq_ref, wk_ref, wv_ref, qt_ref, k_ref, vt_ref,
              st_ref, kmax_ref):
    tm = dt_ref.shape[1]
    cos, sin = cos_ref[...], sin_ref[...]
    cq = _rms_rows(dt_ref[0:MLA_Q_LORA, :].astype(F32)) * gq_ref[...]
    ckv = _rms_rows(dt_ref[MLA_Q_LORA:MLA_Q_LORA + MLA_KV_LORA, :].astype(F32)) * gkv_ref[...]
    kr = _rot_half_rows(dt_ref[MLA_Q_LORA + MLA_KV_LORA:ROWS_D, :].astype(F32), cos, sin)
    ckv_b = ckv.astype(BF16)
    q_all = jnp.dot(wq_ref[...], cq.astype(BF16), preferred_element_type=F32)
    q_all = q_all * ((MLA_NOPE + MLA_ROPE) ** -0.5 * LOG2E)
    k_all = jnp.dot(wk_ref[...], ckv_b, preferred_element_type=F32)
    v_all = jnp.dot(wv_ref[...], ckv_b, preferred_element_type=F32)
    one = jnp.ones((1, tm), F32)
    ones_row = _select_rows([one])
    k_extra = _select_rows([one, one, one])
    zero = jnp.zeros((1, 1), F32)
    stats = []
    for h in range(N_HEADS):
        r0, r1 = h * HEAD_DIM, (h + 1) * HEAD_DIM
        g0, g1 = h * HEAD_PAD, (h + 1) * HEAD_PAD
        qg = q_all[g0:g1]
        q_rope = _rot_half_rows(qg[MLA_NOPE:MLA_NOPE + MLA_ROPE], cos, sin)
        q = jnp.concatenate([qg[0:MLA_NOPE], q_rope], axis=0)
        k = jnp.concatenate([k_all[r0:r1], kr], axis=0)
        qb, kb = q.astype(BF16).astype(F32), k.astype(BF16).astype(F32)
        bound, diag, _, _ = _logit_bound(qb, kb, kmax_ref, N_HEADS + h)
        q_extra = _select_rows([*_split3(-bound)])
        qt_ref[g0:g1, :] = _group(q, q_extra).astype(BF16)
        k_ref[:, g0:g1] = _group(k, k_extra).T.astype(BF16)
        vt_ref[0, h, 0] = _group(v_all[r0:r1], ones_row).astype(BF16)
        stats.append((zero,) * STAT_GAP + (jnp.max(bound - diag, axis=1, keepdims=True),))
    _write_stats(st_ref, stats)


def _front_body(x_ref, g_ref, w_ref, wdf_ref,
                gain_ref, ws_ref, bs_ref,
                cosb_ref, sinb_ref,
                bf_ref,
                cosd_ref, sind_ref, gq_ref, gkv_ref, wq_ref, wk_ref, wv_ref,
                ya_ref, yb_ref,
                qtc_ref, kc_ref, vtc_ref, stats_ref,
                qtd_ref, kd_ref, vtd_ref, statsd_ref,
                at_ref, bt_ref, ct_ref, dt_ref, ft_ref, state_ref, carry_ref, kmax_ref, *, tiles_per_seq):
    @pl.when(pl.program_id(0) % tiles_per_seq == 0)
    def _():
        state_ref[...] = jnp.zeros_like(state_ref)
        carry_ref[...] = jnp.zeros_like(carry_ref)
        kmax_ref[...] = jnp.zeros_like(kmax_ref)

    x = x_ref[...]
    h = (x * lax.rsqrt(jnp.mean(x * x, axis=-1, keepdims=True) + EPS) * g_ref[...]).astype(BF16)

    def proj(w):
        return lax.dot_general(w, h, NT, preferred_element_type=F32)

    half_b = ROWS_B // 2
    at_ref[...] = proj(w_ref[0:ROWS_A, :]).astype(BF16)
    bt_ref[0:half_b, :] = proj(w_ref[ROWS_A:ROWS_A + half_b, :]).astype(BF16)
    bt_ref[half_b:ROWS_B, :] = proj(w_ref[ROWS_A + half_b:ROWS_A + ROWS_B, :]).astype(BF16)
    _sgu(at_ref, gain_ref, ws_ref, bs_ref, ya_ref)
    ct_ref[...] = proj(w_ref[ROWS_A + ROWS_B:ROWS_ABC, :]).astype(BF16)
    _retention(bt_ref, cosb_ref, sinb_ref, yb_ref, state_ref)
    zdf = proj(wdf_ref[...])
    dt_ref[...] = zdf[0:ROWS_D].astype(BF16)
    ft_ref[...] = zdf[ROWS_D:ROWS_D + ROWS_F]
    _fox_prep(ct_ref, ft_ref, bf_ref, qtc_ref, kc_ref, vtc_ref, stats_ref, carry_ref, kmax_ref)
    _mla_prep(dt_ref, cosd_ref, sind_ref, gq_ref, gkv_ref, wq_ref, wk_ref, wv_ref, qtd_ref, kd_ref, vtd_ref,
              statsd_ref, kmax_ref)


def _front(x, g, w_t, wdf_t, gain_col, w_s, b_s, cos_b, sin_b, bf_col,
           cos_d, sin_d, gq_col, gkv_col, wq_t, wk_t, wv_t, batch, layer):
    t = x.shape[0]
    tm = TOKEN_TILE
    ns = t // batch // tm
    const = lambda shape: pl.BlockSpec(shape, lambda i: (0,) * len(shape), pipeline_mode=pl.Buffered(1))
    of_layer = lambda shape: pl.BlockSpec((None,) + shape, lambda i: (layer,) + (0,) * len(shape),
                                          pipeline_mode=pl.Buffered(1))
    rows_t = lambda rows: pl.BlockSpec((rows, tm), lambda i: (0, i))
    table = lambda rows: pl.BlockSpec((rows, tm), lambda i: (0, i % ns))
    k_tile = pl.BlockSpec((tm, N_HEADS * HEAD_PAD), lambda i: (i, 0))
    vt_tile = pl.BlockSpec((1, N_HEADS, 1, HEAD_PAD, tm), lambda i: (i // ns, 0, i % ns, 0, 0))
    stats_tile = pl.BlockSpec((1, 1, 8, 128), lambda i: (i // ns, i % ns, 0, 0))
    qkv_shapes = [
        jax.ShapeDtypeStruct((N_HEADS * HEAD_PAD, t), BF16),
        jax.ShapeDtypeStruct((t, N_HEADS * HEAD_PAD), BF16),
        jax.ShapeDtypeStruct((batch, N_HEADS, ns, HEAD_PAD, tm), BF16),
        jax.ShapeDtypeStruct((batch, ns, 8, 128), F32),
    ]
    return pl.pallas_call(
        functools.partial(_front_body, tiles_per_seq=ns),
        grid=(t // tm,),
        in_specs=[
            pl.BlockSpec((tm, D_MODEL), lambda i: (i, 0)),
            const((1, D_MODEL)),
            of_layer((ROWS_ABC, D_MODEL)),
            of_layer((ROWS_D + ROWS_F, D_MODEL)),
            const((GROUP_WIDTH, 1)),
            const((N_HEADS, CHUNK, CHUNK)),
            const((N_HEADS, CHUNK)),
            table(HEAD_DIM // 2), table(HEAD_DIM // 2),
            const((ROWS_F, 1)),
            table(MLA_ROPE // 2), table(MLA_ROPE // 2),
            const((MLA_Q_LORA, 1)),
            const((MLA_KV_LORA, 1)),
            const((N_HEADS * HEAD_PAD, MLA_Q_LORA)),
            const((GROUP_WIDTH, MLA_KV_LORA)),
            const((GROUP_WIDTH, MLA_KV_LORA)),
        ],
        out_specs=[
            rows_t(GROUP_WIDTH), rows_t(GROUP_WIDTH),
            rows_t(N_HEADS * HEAD_PAD), k_tile, vt_tile, stats_tile,
            rows_t(N_HEADS * HEAD_PAD), k_tile, vt_tile, stats_tile,
        ],
        out_shape=[
            jax.ShapeDtypeStruct((GROUP_WIDTH, t), F32),
            jax.ShapeDtypeStruct((GROUP_WIDTH, t), F32),
            *qkv_shapes,
            *qkv_shapes,
        ],
        scratch_shapes=[
            pltpu.VMEM((ROWS_A, tm), BF16),
            pltpu.VMEM((ROWS_B, tm), BF16),
            pltpu.VMEM((ROWS_C, tm), BF16),
            pltpu.VMEM((ROWS_D, tm), BF16),
            pltpu.VMEM((ROWS_F, tm), F32),
            pltpu.VMEM((N_HEADS, HEAD_DIM, HEAD_DIM), F32),
            pltpu.VMEM((ROWS_F, 1), F32),
            pltpu.VMEM((2 * N_HEADS, 1), F32),
        ],
        compiler_params=_params("arbitrary"),
        name="front",
    )(x, g, w_t, wdf_t, gain_col, w_s, b_s, cos_b, sin_b, bf_col,
      cos_d, sin_d, gq_col, gkv_col, wq_t, wk_t, wv_t)


def _first_needed_block(stats_ref, b, qi, nb):
    def stat(j, r, h):
        return stats_ref[((b * nb + j) * N_STATS + r) * N_HEADS + h]

    skipped = jnp.int32(0)
    leading = jnp.bool_(True)
    for j in range(nb - 1):
        zero = j < qi
        for h in range(N_HEADS):
            bound = (stat(qi, STAT_QNORM, h) * stat(j, STAT_KNORM, h)
                     + stat(qi, STAT_CMAX, h) - stat(j, STAT_CMIN, h))
            zero = jnp.logical_and(zero, bound - stat(qi, STAT_DIAG, h) < -EXP2_UNDERFLOW)
        leading = jnp.logical_and(leading, zero)
        skipped = skipped + leading.astype(jnp.int32)
    return skipped


def _attn_body(*refs, banded, online):
    refs = list(refs)
    stats_ref = refs.pop(0) if banded else None
    qt_ref, k_ref, vt_ref, o_ref, acc_ref, s_ref = refs[:6]
    m_ref = refs[6] if online else None
    blk = qt_ref.shape[1]
    nb = vt_ref.shape[2]
    qi = pl.program_id(1)
    first = _first_needed_block(stats_ref, pl.program_id(0), qi, nb) if banded else 0
    acc_ref[...] = jnp.zeros_like(acc_ref)
    if online:
        m_ref[...] = jnp.full(m_ref.shape, -jnp.inf, F32)

    def logits(kj, h):
        start = pl.multiple_of(kj * blk, blk)
        g0, g1 = h * HEAD_PAD, (h + 1) * HEAD_PAD
        return jnp.dot(k_ref[pl.ds(start, blk), g0:g1], qt_ref[g0:g1, :],
                       preferred_element_type=F32)

    def accumulate(s, kj, h):
        acc = acc_ref[h]
        if online:
            m_old = m_ref[h]
            m_new = jnp.maximum(m_old, jnp.max(s, axis=0, keepdims=True))
            m_ref[h] = m_new
            s = s - m_new
            acc = acc * jnp.exp2(m_old - m_new)
        pv = jnp.dot(vt_ref[0, h, kj, 0:PV_ROWS, :], jnp.exp2(s).astype(BF16), preferred_element_type=F32)
        acc_ref[h] = acc + pv[0:HEAD_DIM + AUG_ROWS]

    s_ref[...] = logits(first, 0)

    def full_step(kj, carry):
        s = s_ref[...]
        for h in range(N_HEADS):
            s_next = logits(kj, h + 1) if h + 1 < N_HEADS else logits(kj + 1, 0)
            accumulate(s, kj, h)
            s = s_next
        s_ref[...] = s
        return carry

    lax.fori_loop(first, qi, full_step, 0)

    if online:
        key_pos = lax.broadcasted_iota(jnp.int32, (blk, blk), 0)
        qry_pos = lax.broadcasted_iota(jnp.int32, (blk, blk), 1)
        visible = key_pos <= qry_pos
        s = s_ref[...]
        for h in range(N_HEADS):
            s_next = logits(qi, h + 1) if h + 1 < N_HEADS else None
            accumulate(jnp.where(visible, s, -jnp.inf), qi, h)
            s = s_next
    else:
        half = blk // 2
        start = pl.multiple_of(qi * blk, blk)
        key_pos = lax.broadcasted_iota(jnp.int32, (half, blk), 0)
        qry_pos = lax.broadcasted_iota(jnp.int32, (half, blk), 1)
        visible_a = key_pos <= qry_pos
        visible_b = visible_a[:, 0:half]

        def quadrant_logits(h):
            g0, g1 = h * HEAD_PAD, (h + 1) * HEAD_PAD
            s_a = jnp.dot(k_ref[pl.ds(start, half), g0:g1], qt_ref[g0:g1, :], preferred_element_type=F32)
            s_b = jnp.dot(k_ref[pl.ds(start + half, half), g0:g1], qt_ref[g0:g1, half:blk],
                          preferred_element_type=F32)
            return s_a, s_b

        s_full = s_ref[...]
        s_a, s_b = s_full[0:half], s_full[half:blk, half:blk]
        for h in range(N_HEADS):
            s_next = quadrant_logits(h + 1) if h + 1 < N_HEADS else None
            p_a = jnp.exp2(jnp.where(visible_a, s_a, -jnp.inf)).astype(BF16)
            p_b = jnp.exp2(jnp.where(visible_b, s_b, -jnp.inf)).astype(BF16)
            pv_a = jnp.dot(vt_ref[0, h, qi, 0:PV_ROWS, 0:half], p_a, preferred_element_type=F32)
            pv_b = jnp.dot(vt_ref[0, h, qi, 0:PV_ROWS, half:blk], p_b, preferred_element_type=F32)
            acc_ref[h] = acc_ref[h] + pv_a[0:HEAD_DIM + AUG_ROWS]
            acc_ref[h, :, half:blk] = acc_ref[h, :, half:blk] + pv_b[0:HEAD_DIM + AUG_ROWS]
            if s_next is not None:
                s_a, s_b = s_next
    for h in range(N_HEADS):
        acc = acc_ref[h]
        o_ref[h * HEAD_DIM:(h + 1) * HEAD_DIM, :] = acc[0:HEAD_DIM] / acc[HEAD_DIM:HEAD_DIM + 1]


def _attention_call(qt, k, vt, skip_stats, online):
    batch, _, nb, _, blk = vt.shape
    t = qt.shape[1]
    seq = t // batch
    banded = skip_stats is not None
    in_specs = [
        pl.BlockSpec((N_HEADS * HEAD_PAD, blk), lambda b, i: (0, b * nb + i)),
        pl.BlockSpec((seq, N_HEADS * HEAD_PAD), lambda b, i: (b, 0)),
        pl.BlockSpec((1, N_HEADS, nb, HEAD_PAD, blk), lambda b, i: (b, 0, 0, 0, 0)),
    ]
    args = (qt, k, vt)
    if banded:
        in_specs = [pl.BlockSpec(memory_space=pltpu.SMEM)] + in_specs
        args = (skip_stats,) + args
    scratch = [pltpu.VMEM((N_HEADS, HEAD_DIM + AUG_ROWS, blk), F32), pltpu.VMEM((blk, blk), F32)]
    if online:
        scratch.append(pltpu.VMEM((N_HEADS, 1, blk), F32))
    return pl.pallas_call(
        functools.partial(_attn_body, banded=banded, online=online),
        grid=(batch, nb),
        in_specs=in_specs,
        out_specs=pl.BlockSpec((GROUP_WIDTH, blk), lambda b, i: (0, b * nb + i)),
        out_shape=jax.ShapeDtypeStruct((GROUP_WIDTH, t), F32),
        scratch_shapes=scratch,
        compiler_params=_params("parallel", "arbitrary"),
        name="attention_online" if online else "attention",
    )(*args)


def _attention(qt, k, vt, stats, banded):
    stats = stats[:, :, :N_STATS, :N_HEADS]
    skip_stats = stats.reshape(-1) if banded else None
    bound_is_tight = jnp.max(stats[:, :, STAT_GAP, :]) <= GAP_LIMIT
    return lax.cond(bound_is_tight,
                    lambda: _attention_call(qt, k, vt, skip_stats, online=False),
                    lambda: _attention_call(qt, k, vt, skip_stats, online=True))


def _post_body(x_ref, ya_ref, yb_ref, yc_ref, yd_ref, gg_ref, wo_ref, gf_ref, wu_ref, wd_ref,
               gl_ref, o_ref, *, final):
    ys = []
    for g, y_ref in enumerate((ya_ref, yb_ref, yc_ref, yd_ref)):
        ys.append((_rms_rows(y_ref[...]) * gg_ref[g]).astype(BF16))
    y = jnp.concatenate(ys, axis=0)
    x = x_ref[...] + lax.dot_general(y, wo_ref[...], TN, preferred_element_type=F32)
    h = (x * lax.rsqrt(jnp.mean(x * x, axis=-1, keepdims=True) + EPS) * gf_ref[...]).astype(BF16)
    acc = x
    for c in range(D_FF // FF_CHUNK):
        c0, c1 = c * FF_CHUNK, (c + 1) * FF_CHUNK
        a = jnp.maximum(jnp.dot(h, wu_ref[:, c0:c1], preferred_element_type=F32), 0.0)
        acc = acc + jnp.dot((a * a).astype(BF16), wd_ref[c0:c1, :], preferred_element_type=F32)
    if final:
        acc = acc * lax.rsqrt(jnp.mean(acc * acc, axis=-1, keepdims=True) + EPS) * gl_ref[...]
    o_ref[...] = acc


def _post(x, ya, yb, yc, yd, gg_col, w_out, g_ffn, w_up, w_down, g_final, layer, final):
    t = x.shape[0]
    tm = TOKEN_TILE
    const = lambda shape: pl.BlockSpec(shape, lambda i: (0,) * len(shape), pipeline_mode=pl.Buffered(1))
    of_layer = lambda shape: pl.BlockSpec((None,) + shape, lambda i: (layer,) + (0,) * len(shape),
                                          pipeline_mode=pl.Buffered(1))
    ytile = pl.BlockSpec((GROUP_WIDTH, tm), lambda i: (0, i))
    return pl.pallas_call(
        functools.partial(_post_body, final=final),
        grid=(t // tm,),
        in_specs=[
            pl.BlockSpec((tm, D_MODEL), lambda i: (i, 0)),
            ytile, ytile, ytile, ytile,
            const((4, GROUP_WIDTH, 1)),
            of_layer((D_MODEL, D_MODEL)),
            const((1, D_MODEL)),
            of_layer((D_MODEL, D_FF)),
            of_layer((D_FF, D_MODEL)),
            const((1, D_MODEL)),
        ],
        out_specs=pl.BlockSpec((tm, D_MODEL), lambda i: (i, 0)),
        out_shape=jax.ShapeDtypeStruct((t, D_MODEL), F32),
        compiler_params=_params("parallel"),
        name="post",
    )(x, ya, yb, yc, yd, gg_col, w_out, g_ffn, w_up, w_down, g_final)


def _rope_tables(seq, half):
    inv_freq = np.power(ROPE_BASE, -np.arange(half, dtype=np.float64) / half)
    ang = inv_freq[:, None] * np.arange(seq, dtype=np.float64)[None, :]
    return jnp.asarray(np.cos(ang), F32), jnp.asarray(np.sin(ang), F32)


def _inproj_weights(w_in):
    wt = jnp.swapaxes(w_in, 1, 2)
    w_abc = wt[:, :ROWS_ABC].astype(BF16)
    f = wt[:, ROWS_ABC:ROWS_ABC + N_HEADS]
    d = wt[:, ROWS_ABC + N_HEADS:ROWS_ABC + N_HEADS + ROWS_D]
    pad = jnp.zeros((w_in.shape[0], ROWS_F - N_HEADS, D_MODEL), w_in.dtype)
    return w_abc, jnp.concatenate([d, f, pad], axis=1).astype(BF16)


def _mla_weights(w_uq, w_ukv):
    wq = w_uq.T.reshape(N_HEADS, MLA_NOPE + MLA_ROPE, MLA_Q_LORA)
    wq = jnp.pad(wq, ((0, 0), (0, HEAD_PAD - MLA_NOPE - MLA_ROPE), (0, 0)))
    wq = wq.reshape(N_HEADS * HEAD_PAD, MLA_Q_LORA).astype(BF16)
    wkv = w_ukv.T.reshape(N_HEADS, 2 * HEAD_DIM, MLA_KV_LORA)
    wk = wkv[:, :HEAD_DIM].reshape(GROUP_WIDTH, MLA_KV_LORA).astype(BF16)
    wv = wkv[:, HEAD_DIM:].reshape(GROUP_WIDTH, MLA_KV_LORA).astype(BF16)
    return wq, wk, wv


def kernel(x, g_mix_norm, w_in, b_forget, g_sgu, w_spatial, b_spatial, g_mla_q, w_uq, g_mla_kv, w_ukv,
           g_group_out, w_out, g_ffn_norm, w_up, w_down, g_final):
    batch, seq, _ = x.shape
    depth = w_in.shape[0]
    assert seq % TOKEN_TILE == 0 and ATTN_BLOCK == TOKEN_TILE
    cos_b, sin_b = _rope_tables(seq, HEAD_DIM // 2)
    cos_d, sin_d = _rope_tables(seq, MLA_ROPE // 2)
    w_abc, w_df = _inproj_weights(w_in)
    w_out_b, w_up_b, w_down_b = w_out.astype(BF16), w_up.astype(BF16), w_down.astype(BF16)
    xf = x.reshape(batch * seq, D_MODEL)
    for l in range(depth):
        bf_col = jnp.pad(b_forget[l], (0, ROWS_F - N_HEADS))[:, None]
        ya, yb, qt_c, k_c, vt_c, stats_c, qt_d, k_d, vt_d, stats_d = _front(
            xf, g_mix_norm[l][None, :], w_abc, w_df,
            g_sgu[l][:, None], w_spatial[l], b_spatial[l], cos_b, sin_b, bf_col,
            cos_d, sin_d, g_mla_q[l][:, None], g_mla_kv[l][:, None], *_mla_weights(w_uq[l], w_ukv[l]),
            batch, l)
        yc = _attention(qt_c, k_c, vt_c, stats_c, banded=True)
        yd = _attention(qt_d, k_d, vt_d, stats_d, banded=False)
        xf = _post(xf, ya, yb, yc, yd, g_group_out[l].reshape(4, GROUP_WIDTH, 1),
                   w_out_b, g_ffn_norm[l][None, :], w_up_b, w_down_b, g_final[None, :],
                   layer=l, final=(l == depth - 1))
    return xf.reshape(batch, seq, D_MODEL)
```

```python
import functools
import math

import jax
import jax.numpy as jnp
import numpy as np
from jax import lax
from jax.experimental import pallas as pl
from jax.experimental.pallas import tpu as pltpu

F32 = jnp.float32
BF16 = jnp.bfloat16

D_MODEL = 1024
N_HEADS = 4
HEAD_DIM = 64
GROUP_WIDTH = N_HEADS * HEAD_DIM
CHUNK = 128
MLA_Q_LORA = 256
MLA_KV_LORA = 128
MLA_NOPE = 64
MLA_ROPE = 32
ROPE_BASE = 10000.0
D_FF = 4 * D_MODEL
EPS = 1e-6

HEAD_PAD = 128
AUG_ROWS = 8
PV_ROWS = 80
LOG2E = math.log2(math.e)
EXP2_UNDERFLOW = 152.0
STAT_QNORM, STAT_KNORM, STAT_CMAX, STAT_CMIN, STAT_DIAG, STAT_GAP = range(6)
N_STATS = 6
GAP_LIMIT = 64.0
TOKEN_TILE = 512
ATTN_BLOCK = TOKEN_TILE
FF_CHUNK = 512
VMEM_LIMIT = 56 * 1024 * 1024

ROWS_A = 2 * GROUP_WIDTH
ROWS_B = 4 * GROUP_WIDTH
ROWS_C = 3 * GROUP_WIDTH
ROWS_D = MLA_Q_LORA + MLA_KV_LORA + MLA_ROPE
ROWS_F = 8
ROWS_ABC = ROWS_A + ROWS_B + ROWS_C

NT = (((1,), (1,)), ((), ()))
TN = (((0,), (0,)), ((), ()))


def _params(*sem):
    return pltpu.CompilerParams(dimension_semantics=sem, vmem_limit_bytes=VMEM_LIMIT)


def _rot_half_rows(t, cos, sin):
    half = t.shape[0] // 2
    t1, t2 = t[:half], t[half:]
    return jnp.concatenate([t1 * cos - t2 * sin, t1 * sin + t2 * cos], axis=0)


def _standardize_rows(t):
    mu = jnp.mean(t, axis=0, keepdims=True)
    var = jnp.mean(jnp.square(t - mu), axis=0, keepdims=True)
    return (t - mu) * lax.rsqrt(var + EPS)


def _rms_rows(t):
    return t * lax.rsqrt(jnp.mean(t * t, axis=0, keepdims=True) + EPS)


def _sgu(at_ref, gain_ref, ws_ref, bs_ref, o_ref):
    tm = at_ref.shape[1]
    nch = tm // CHUNK
    row = lax.broadcasted_iota(jnp.int32, (CHUNK, CHUNK), 0)
    col = lax.broadcasted_iota(jnp.int32, (CHUNK, CHUNK), 1)
    for h in range(N_HEADS):
        r0, r1 = h * HEAD_DIM, (h + 1) * HEAD_DIM
        u = jax.nn.gelu(at_ref[r0:r1, :].astype(F32))
        v = jax.nn.gelu(at_ref[GROUP_WIDTH + r0:GROUP_WIDTH + r1, :].astype(F32))
        v = _standardize_rows(v) * gain_ref[r0:r1, :]
        w = jnp.where(col <= row, ws_ref[h], 0.0).astype(BF16)
        vs = jnp.concatenate([v[:, c * CHUNK:(c + 1) * CHUNK] for c in range(nch)], axis=0).astype(BF16)
        mixed = lax.dot_general(vs, w, NT, preferred_element_type=F32) + bs_ref[h:h + 1, :]
        for c in range(nch):
            o_ref[r0:r1, c * CHUNK:(c + 1) * CHUNK] = (
                u[:, c * CHUNK:(c + 1) * CHUNK] * mixed[c * HEAD_DIM:(c + 1) * HEAD_DIM])


def _retention(bt_ref, cos_ref, sin_ref, o_ref, st_ref):
    tm = bt_ref.shape[1]
    nch = tm // CHUNK
    cos, sin = cos_ref[...], sin_ref[...]
    srow = lax.broadcasted_iota(jnp.int32, (CHUNK, CHUNK), 0)
    tcol = lax.broadcasted_iota(jnp.int32, (CHUNK, CHUNK), 1)
    rel = (tcol - srow).astype(F32)
    j = lax.broadcasted_iota(jnp.int32, (1, CHUNK), 1).astype(F32)
    for h in range(N_HEADS):
        log_gamma = math.log1p(-(2.0 ** (-5.0 - h)))
        dec_t = jnp.where(rel >= 0, jnp.exp(jnp.maximum(rel, 0.0) * log_gamma), 0.0)
        query_w = jnp.exp((j + 1.0) * log_gamma)
        key_w = jnp.exp((CHUNK - 1.0 - j) * log_gamma)
        chunk_decay = math.exp(CHUNK * log_gamma)
        r0, r1 = h * HEAD_DIM, (h + 1) * HEAD_DIM
        q = _rot_half_rows(bt_ref[r0:r1, :].astype(F32), cos, sin)
        k = _rot_half_rows(bt_ref[GROUP_WIDTH + r0:GROUP_WIDTH + r1, :].astype(F32), cos, sin) * (HEAD_DIM ** -0.5)
        v = bt_ref[2 * GROUP_WIDTH + r0:2 * GROUP_WIDTH + r1, :].astype(F32)
        g = bt_ref[3 * GROUP_WIDTH + r0:3 * GROUP_WIDTH + r1, :].astype(F32)
        st = st_ref[h]
        ys = []
        for c in range(nch):
            sl = slice(c * CHUNK, (c + 1) * CHUNK)
            qc, kc, vc = q[:, sl], k[:, sl], v[:, sl]
            kcb = kc.astype(BF16)
            a_t = lax.dot_general(kcb, qc.astype(BF16), TN, preferred_element_type=F32)
            p_t = (a_t * dec_t).astype(BF16)
            intra = jnp.dot(vc.astype(BF16), p_t, preferred_element_type=F32)
            cross = jnp.dot(st.astype(BF16), (qc * query_w).astype(BF16), preferred_element_type=F32)
            ys.append(intra + cross)
            st = chunk_decay * st + lax.dot_general((vc * key_w).astype(BF16), kcb, NT,
                                                    preferred_element_type=F32)
        st_ref[h] = st
        y = _standardize_rows(jnp.concatenate(ys, axis=1))
        o_ref[r0:r1, :] = jax.nn.silu(g) * y


def _select_rows(rows):
    n = rows[0].shape[1]
    ridx = lax.broadcasted_iota(jnp.int32, (AUG_ROWS, n), 0)
    out = jnp.zeros((AUG_ROWS, n), F32)
    for i, r in enumerate(rows):
        out = jnp.where(ridx == i, jnp.broadcast_to(r, (AUG_ROWS, n)), out)
    return out


def _group(feat, extra):
    n = feat.shape[1]
    pad = HEAD_PAD - feat.shape[0] - extra.shape[0]
    return jnp.concatenate([feat, extra, jnp.zeros((pad, n), F32)], axis=0)


def _split3(x):
    hi = x.astype(BF16).astype(F32)
    mid = (x - hi).astype(BF16).astype(F32)
    lo = (x - hi - mid).astype(BF16).astype(F32)
    return hi, mid, lo


def _logit_bound(qb, kb, kmax_ref, row):
    qnorm = jnp.sqrt(jnp.sum(qb * qb, axis=0, keepdims=True))
    knorm = jnp.sqrt(jnp.max(jnp.sum(kb * kb, axis=0, keepdims=True), axis=1, keepdims=True))
    kmax = jnp.maximum(kmax_ref[row:row + 1, :], knorm)
    kmax_ref[row:row + 1, :] = kmax
    return qnorm * kmax, jnp.sum(qb * kb, axis=0, keepdims=True), qnorm, knorm


def _write_stats(st_ref, per_head):
    stat_row = lax.broadcasted_iota(jnp.int32, (8, 128), 0)
    stat_lane = lax.broadcasted_iota(jnp.int32, (8, 128), 1)
    stats = jnp.zeros((8, 128), F32)
    for h, vals in enumerate(per_head):
        for r, val in enumerate(vals):
            stats = jnp.where((stat_row == r) & (stat_lane == h), val, stats)
    st_ref[0, 0] = stats


def _fox_prep(ct_ref, ft_ref, bf_ref, qt_ref, k_ref, vt_ref, st_ref, carry_ref, kmax_ref):
    tm = ct_ref.shape[1]
    x = ft_ref[...] + bf_ref[...]
    lf = jnp.minimum(x, 0.0) - jnp.log1p(jnp.exp(-jnp.abs(x)))
    hi = lf.astype(BF16)
    mid = (lf - hi.astype(F32)).astype(BF16)
    lo = (lf - hi.astype(F32) - mid.astype(F32)).astype(BF16)
    srow = lax.broadcasted_iota(jnp.int32, (tm, tm), 0)
    tcol = lax.broadcasted_iota(jnp.int32, (tm, tm), 1)
    upper = jnp.where(srow <= tcol, 1.0, 0.0).astype(BF16)
    parts = jnp.dot(jnp.concatenate([hi, mid, lo], axis=0), upper, preferred_element_type=F32)
    cum = parts[0:8] + parts[8:16] + parts[16:24] + carry_ref[...]
    carry_ref[...] = cum[:, tm - 1:tm]

    cum2 = cum * LOG2E
    one = jnp.ones((1, tm), F32)
    ones_row = _select_rows([one])
    stats = []
    for h in range(N_HEADS):
        r0, r1 = h * HEAD_DIM, (h + 1) * HEAD_DIM
        g0, g1 = h * HEAD_PAD, (h + 1) * HEAD_PAD
        q = ct_ref[r0:r1, :].astype(F32) * (HEAD_DIM ** -0.5 * LOG2E)
        k = ct_ref[GROUP_WIDTH + r0:GROUP_WIDTH + r1, :].astype(F32)
        v = ct_ref[2 * GROUP_WIDTH + r0:2 * GROUP_WIDTH + r1, :].astype(F32)
        qb, kb = q.astype(BF16).astype(F32), k.astype(BF16).astype(F32)
        bound, diag, qnorm, knorm = _logit_bound(qb, kb, kmax_ref, h)
        c2 = cum2[h:h + 1]
        q_extra = _select_rows([*_split3(c2 - bound), one, one, one])
        k_extra = _select_rows([one, one, one, *(-part for part in _split3(c2))])
        qt_ref[g0:g1, :] = _group(q, q_extra).astype(BF16)
        k_ref[:, g0:g1] = _group(k, k_extra).T.astype(BF16)
        vt_ref[0, h, 0] = _group(v, ones_row).astype(BF16)
        stats.append((
            jnp.max(qnorm, axis=1, keepdims=True),
            knorm,
            jnp.max(c2, axis=1, keepdims=True),
            jnp.min(c2, axis=1, keepdims=True),
            jnp.min(diag, axis=1, keepdims=True),
            jnp.max(bound - diag, axis=1, keepdims=True),
        ))
    _write_stats(st_ref, stats)


def _mla_prep(dt_ref, cos_ref, sin_ref, gq_ref, gkv_ref, wq_ref, wk_ref, wv_ref, qt_ref, k_ref, vt_ref,
              st_ref, kmax_ref):
    tm = dt_ref.shape[1]
    cos, sin = cos_ref[...], sin_ref[...]
    cq = _rms_rows(dt_ref[0:MLA_Q_LORA, :].astype(F32)) * gq_ref[...]
    ckv = _rms_rows(dt_ref[MLA_Q_LORA:MLA_Q_LORA + MLA_KV_LORA, :].astype(F32)) * gkv_ref[...]
    kr = _rot_half_rows(dt_ref[MLA_Q_LORA + MLA_KV_LORA:ROWS_D, :].astype(F32), cos, sin)
    ckv_b = ckv.astype(BF16)
    q_all = jnp.dot(wq_ref[...], cq.astype(BF16), preferred_element_type=F32)
    q_all = q_all * ((MLA_NOPE + MLA_ROPE) ** -0.5 * LOG2E)
    k_all = jnp.dot(wk_ref[...], ckv_b, preferred_element_type=F32)
    v_all = jnp.dot(wv_ref[...], ckv_b, preferred_element_type=F32)
    one = jnp.ones((1, tm), F32)
    ones_row = _select_rows([one])
    k_extra = _select_rows([one, one, one])
    zero = jnp.zeros((1, 1), F32)
    stats = []
    for h in range(N_HEADS):
        r0, r1 = h * HEAD_DIM, (h + 1) * HEAD_DIM
        g0, g1 = h * HEAD_PAD, (h + 1) * HEAD_PAD
        qg = q_all[g0:g1]
        q_rope = _rot_half_rows(qg[MLA_NOPE:MLA_NOPE + MLA_ROPE], cos, sin)
        q = jnp.concatenate([qg[0:MLA_NOPE], q_rope], axis=0)
        k = jnp.concatenate([k_all[r0:r1], kr], axis=0)
        qb, kb = q.astype(BF16).astype(F32), k.astype(BF16).astype(F32)
        bound, diag, _, _ = _logit_bound(qb, kb, kmax_ref, N_HEADS + h)
        q_extra = _select_rows([*_split3(-bound)])
        qt_ref[g0:g1, :] = _group(q, q_extra).astype(BF16)
        k_ref[:, g0:g1] = _group(k, k_extra).T.astype(BF16)
        vt_ref[0, h, 0] = _group(v_all[r0:r1], ones_row).astype(BF16)
        stats.append((zero,) * STAT_GAP + (jnp.max(bound - diag, axis=1, keepdims=True),))
    _write_stats(st_ref, stats)


def _front_body(x_ref, g_ref, w_ref, wdf_ref,
                gain_ref, ws_ref, bs_ref,
                cosb_ref, sinb_ref,
                bf_ref,
                cosd_ref, sind_ref, gq_ref, gkv_ref, wq_ref, wk_ref, wv_ref,
                ya_ref, yb_ref,
                qtc_ref, kc_ref, vtc_ref, stats_ref,
                qtd_ref, kd_ref, vtd_ref, statsd_ref,
                at_ref, bt_ref, ct_ref, dt_ref, ft_ref, state_ref, carry_ref, kmax_ref, *, tiles_per_seq):
    @pl.when(pl.program_id(0) % tiles_per_seq == 0)
    def _():
        state_ref[...] = jnp.zeros_like(state_ref)
        carry_ref[...] = jnp.zeros_like(carry_ref)
        kmax_ref[...] = jnp.zeros_like(kmax_ref)

    x = x_ref[...]
    h = (x * lax.rsqrt(jnp.mean(x * x, axis=-1, keepdims=True) + EPS) * g_ref[...]).astype(BF16)

    def proj(w):
        return lax.dot_general(w, h, NT, preferred_element_type=F32)

    half_b = ROWS_B // 2
    zdf = proj(wdf_ref[...])
    dt_ref[...] = zdf[0:ROWS_D].astype(BF16)
    ft_ref[...] = zdf[ROWS_D:ROWS_D + ROWS_F]
    at_ref[...] = proj(w_ref[0:ROWS_A, :]).astype(BF16)
    _mla_prep(dt_ref, cosd_ref, sind_ref, gq_ref, gkv_ref, wq_ref, wk_ref, wv_ref, qtd_ref, kd_ref, vtd_ref,
              statsd_ref, kmax_ref)
    ct_ref[...] = proj(w_ref[ROWS_A + ROWS_B:ROWS_ABC, :]).astype(BF16)
    _sgu(at_ref, gain_ref, ws_ref, bs_ref, ya_ref)
    bt_ref[0:half_b, :] = proj(w_ref[ROWS_A:ROWS_A + half_b, :]).astype(BF16)
    _fox_prep(ct_ref, ft_ref, bf_ref, qtc_ref, kc_ref, vtc_ref, stats_ref, carry_ref, kmax_ref)
    bt_ref[half_b:ROWS_B, :] = proj(w_ref[ROWS_A + half_b:ROWS_A + ROWS_B, :]).astype(BF16)
    _retention(bt_ref, cosb_ref, sinb_ref, yb_ref, state_ref)


def _front(x, g, w_t, wdf_t, gain_col, w_s, b_s, cos_b, sin_b, bf_col,
           cos_d, sin_d, gq_col, gkv_col, wq_t, wk_t, wv_t, batch, layer):
    t = x.shape[0]
    tm = TOKEN_TILE
    ns = t // batch // tm
    const = lambda shape: pl.BlockSpec(shape, lambda i: (0,) * len(shape), pipeline_mode=pl.Buffered(1))
    of_layer = lambda shape: pl.BlockSpec((None,) + shape, lambda i: (layer,) + (0,) * len(shape),
                                          pipeline_mode=pl.Buffered(1))
    rows_t = lambda rows: pl.BlockSpec((rows, tm), lambda i: (0, i))
    table = lambda rows: pl.BlockSpec((rows, tm), lambda i: (0, i % ns))
    k_tile = pl.BlockSpec((tm, N_HEADS * HEAD_PAD), lambda i: (i, 0))
    vt_tile = pl.BlockSpec((1, N_HEADS, 1, HEAD_PAD, tm), lambda i: (i // ns, 0, i % ns, 0, 0))
    stats_tile = pl.BlockSpec((1, 1, 8, 128), lambda i: (i // ns, i % ns, 0, 0))
    qkv_shapes = [
        jax.ShapeDtypeStruct((N_HEADS * HEAD_PAD, t), BF16),
        jax.ShapeDtypeStruct((t, N_HEADS * HEAD_PAD), BF16),
        jax.ShapeDtypeStruct((batch, N_HEADS, ns, HEAD_PAD, tm), BF16),
        jax.ShapeDtypeStruct((batch, ns, 8, 128), F32),
    ]
    return pl.pallas_call(
        functools.partial(_front_body, tiles_per_seq=ns),
        grid=(t // tm,),
        in_specs=[
            pl.BlockSpec((tm, D_MODEL), lambda i: (i, 0)),
            const((1, D_MODEL)),
            of_layer((ROWS_ABC, D_MODEL)),
            of_layer((ROWS_D + ROWS_F, D_MODEL)),
            const((GROUP_WIDTH, 1)),
            const((N_HEADS, CHUNK, CHUNK)),
            const((N_HEADS, CHUNK)),
            table(HEAD_DIM // 2), table(HEAD_DIM // 2),
            const((ROWS_F, 1)),
            table(MLA_ROPE // 2), table(MLA_ROPE // 2),
            const((MLA_Q_LORA, 1)),
            const((MLA_KV_LORA, 1)),
            const((N_HEADS * HEAD_PAD, MLA_Q_LORA)),
            const((GROUP_WIDTH, MLA_KV_LORA)),
            const((GROUP_WIDTH, MLA_KV_LORA)),
        ],
        out_specs=[
            rows_t(GROUP_WIDTH), rows_t(GROUP_WIDTH),
            rows_t(N_HEADS * HEAD_PAD), k_tile, vt_tile, stats_tile,
            rows_t(N_HEADS * HEAD_PAD), k_tile, vt_tile, stats_tile,
        ],
        out_shape=[
            jax.ShapeDtypeStruct((GROUP_WIDTH, t), F32),
            jax.ShapeDtypeStruct((GROUP_WIDTH, t), F32),
            *qkv_shapes,
            *qkv_shapes,
        ],
        scratch_shapes=[
            pltpu.VMEM((ROWS_A, tm), BF16),
            pltpu.VMEM((ROWS_B, tm), BF16),
            pltpu.VMEM((ROWS_C, tm), BF16),
            pltpu.VMEM((ROWS_D, tm), BF16),
            pltpu.VMEM((ROWS_F, tm), F32),
            pltpu.VMEM((N_HEADS, HEAD_DIM, HEAD_DIM), F32),
            pltpu.VMEM((ROWS_F, 1), F32),
            pltpu.VMEM((2 * N_HEADS, 1), F32),
        ],
        compiler_params=_params("arbitrary"),
        name="front",
    )(x, g, w_t, wdf_t, gain_col, w_s, b_s, cos_b, sin_b, bf_col,
      cos_d, sin_d, gq_col, gkv_col, wq_t, wk_t, wv_t)


def _first_needed_block(stats_ref, b, qi, nb):
    def stat(j, r, h):
        return stats_ref[((b * nb + j) * N_STATS + r) * N_HEADS + h]

    skipped = jnp.int32(0)
    leading = jnp.bool_(True)
    for j in range(nb - 1):
        zero = j < qi
        for h in range(N_HEADS):
            bound = (stat(qi, STAT_QNORM, h) * stat(j, STAT_KNORM, h)
                     + stat(qi, STAT_CMAX, h) - stat(j, STAT_CMIN, h))
            zero = jnp.logical_and(zero, bound - stat(qi, STAT_DIAG, h) < -EXP2_UNDERFLOW)
        leading = jnp.logical_and(leading, zero)
        skipped = skipped + leading.astype(jnp.int32)
    return skipped


def _attn_body(*refs, banded, online):
    refs = list(refs)
    stats_ref = refs.pop(0) if banded else None
    qt_ref, k_ref, vt_ref, o_ref, acc_ref, s_ref = refs[:6]
    m_ref = refs[6] if online else None
    blk = qt_ref.shape[1]
    nb = vt_ref.shape[2]
    qi = pl.program_id(1)
    first = _first_needed_block(stats_ref, pl.program_id(0), qi, nb) if banded else 0
    acc_ref[...] = jnp.zeros_like(acc_ref)
    if online:
        m_ref[...] = jnp.full(m_ref.shape, -jnp.inf, F32)

    def logits(kj, h):
        start = pl.multiple_of(kj * blk, blk)
        g0, g1 = h * HEAD_PAD, (h + 1) * HEAD_PAD
        return jnp.dot(k_ref[pl.ds(start, blk), g0:g1], qt_ref[g0:g1, :],
                       preferred_element_type=F32)

    def accumulate(s, kj, h):
        acc = acc_ref[h]
        if online:
            m_old = m_ref[h]
            m_new = jnp.maximum(m_old, jnp.max(s, axis=0, keepdims=True))
            m_ref[h] = m_new
            s = s - m_new
            acc = acc * jnp.exp2(m_old - m_new)
        pv = jnp.dot(vt_ref[0, h, kj, 0:PV_ROWS, :], jnp.exp2(s).astype(BF16), preferred_element_type=F32)
        acc_ref[h] = acc + pv[0:HEAD_DIM + AUG_ROWS]

    s_ref[...] = logits(first, 0)

    def full_step(kj, carry):
        s = s_ref[...]
        for h in range(N_HEADS):
            s_next = logits(kj, h + 1) if h + 1 < N_HEADS else logits(kj + 1, 0)
            accumulate(s, kj, h)
            s = s_next
        s_ref[...] = s
        return carry

    lax.fori_loop(first, qi, full_step, 0)

    if online:
        key_pos = lax.broadcasted_iota(jnp.int32, (blk, blk), 0)
        qry_pos = lax.broadcasted_iota(jnp.int32, (blk, blk), 1)
        visible = key_pos <= qry_pos
        s = s_ref[...]
        for h in range(N_HEADS):
            s_next = logits(qi, h + 1) if h + 1 < N_HEADS else None
            accumulate(jnp.where(visible, s, -jnp.inf), qi, h)
            s = s_next
    else:
        half = blk // 2
        start = pl.multiple_of(qi * blk, blk)
        key_pos = lax.broadcasted_iota(jnp.int32, (half, blk), 0)
        qry_pos = lax.broadcasted_iota(jnp.int32, (half, blk), 1)
        visible_a = key_pos <= qry_pos
        visible_b = visible_a[:, 0:half]

        def quadrant_logits(h):
            g0, g1 = h * HEAD_PAD, (h + 1) * HEAD_PAD
            s_a = jnp.dot(k_ref[pl.ds(start, half), g0:g1], qt_ref[g0:g1, :], preferred_element_type=F32)
            s_b = jnp.dot(k_ref[pl.ds(start + half, half), g0:g1], qt_ref[g0:g1, half:blk],
                          preferred_element_type=F32)
            return s_a, s_b

        s_full = s_ref[...]
        s_a, s_b = s_full[0:half], s_full[half:blk, half:blk]
        for h in range(N_HEADS):
            s_next = quadrant_logits(h + 1) if h + 1 < N_HEADS else None
            p_a = jnp.exp2(jnp.where(visible_a, s_a, -jnp.inf)).astype(BF16)
            p_b = jnp.exp2(jnp.where(visible_b, s_b, -jnp.inf)).astype(BF16)
            pv_a = jnp.dot(vt_ref[0, h, qi, 0:PV_ROWS, 0:half], p_a, preferred_element_type=F32)
            pv_b = jnp.dot(vt_ref[0, h, qi, 0:PV_ROWS, half:blk], p_b, preferred_element_type=F32)
            acc_ref[h] = acc_ref[h] + pv_a[0:HEAD_DIM + AUG_ROWS]
            acc_ref[h, :, half:blk] = acc_ref[h, :, half:blk] + pv_b[0:HEAD_DIM + AUG_ROWS]
            if s_next is not None:
                s_a, s_b = s_next
    for h in range(N_HEADS):
        acc = acc_ref[h]
        o_ref[h * HEAD_DIM:(h + 1) * HEAD_DIM, :] = acc[0:HEAD_DIM] / acc[HEAD_DIM:HEAD_DIM + 1]


def _attention_call(qt, k, vt, skip_stats, online):
    batch, _, nb, _, blk = vt.shape
    t = qt.shape[1]
    seq = t // batch
    banded = skip_stats is not None
    in_specs = [
        pl.BlockSpec((N_HEADS * HEAD_PAD, blk), lambda b, i: (0, b * nb + i)),
        pl.BlockSpec((seq, N_HEADS * HEAD_PAD), lambda b, i: (b, 0)),
        pl.BlockSpec((1, N_HEADS, nb, HEAD_PAD, blk), lambda b, i: (b, 0, 0, 0, 0)),
    ]
    args = (qt, k, vt)
    if banded:
        in_specs = [pl.BlockSpec(memory_space=pltpu.SMEM)] + in_specs
        args = (skip_stats,) + args
    scratch = [pltpu.VMEM((N_HEADS, HEAD_DIM + AUG_ROWS, blk), F32), pltpu.VMEM((blk, blk), F32)]
    if online:
        scratch.append(pltpu.VMEM((N_HEADS, 1, blk), F32))
    return pl.pallas_call(
        functools.partial(_attn_body, banded=banded, online=online),
        grid=(batch, nb),
        in_specs=in_specs,
        out_specs=pl.BlockSpec((GROUP_WIDTH, blk), lambda b, i: (0, b * nb + i)),
        out_shape=jax.ShapeDtypeStruct((GROUP_WIDTH, t), F32),
        scratch_shapes=scratch,
        compiler_params=_params("parallel", "arbitrary"),
        name="attention_online" if online else "attention",
    )(*args)


def _attention(qt, k, vt, stats, banded):
    stats = stats[:, :, :N_STATS, :N_HEADS]
    skip_stats = stats.reshape(-1) if banded else None
    bound_is_tight = jnp.max(stats[:, :, STAT_GAP, :]) <= GAP_LIMIT
    return lax.cond(bound_is_tight,
                    lambda: _attention_call(qt, k, vt, skip_stats, online=False),
                    lambda: _attention_call(qt, k, vt, skip_stats, online=True))


def _post_body(x_ref, ya_ref, yb_ref, yc_ref, yd_ref, gg_ref, wo_ref, gf_ref, wu_ref, wd_ref,
               gl_ref, o_ref, *, final):
    ys = []
    for g, y_ref in enumerate((ya_ref, yb_ref, yc_ref, yd_ref)):
        ys.append((_rms_rows(y_ref[...]) * gg_ref[g]).astype(BF16))
    y = jnp.concatenate(ys, axis=0)
    x = x_ref[...] + lax.dot_general(y, wo_ref[...], TN, preferred_element_type=F32)
    h = (x * lax.rsqrt(jnp.mean(x * x, axis=-1, keepdims=True) + EPS) * gf_ref[...]).astype(BF16)
    acc = x
    for c in range(D_FF // FF_CHUNK):
        c0, c1 = c * FF_CHUNK, (c + 1) * FF_CHUNK
        a = jnp.maximum(jnp.dot(h, wu_ref[:, c0:c1], preferred_element_type=F32), 0.0)
        acc = acc + jnp.dot((a * a).astype(BF16), wd_ref[c0:c1, :], preferred_element_type=F32)
    if final:
        acc = acc * lax.rsqrt(jnp.mean(acc * acc, axis=-1, keepdims=True) + EPS) * gl_ref[...]
    o_ref[...] = acc


def _post(x, ya, yb, yc, yd, gg_col, w_out, g_ffn, w_up, w_down, g_final, layer, final):
    t = x.shape[0]
    tm = TOKEN_TILE
    const = lambda shape: pl.BlockSpec(shape, lambda i: (0,) * len(shape), pipeline_mode=pl.Buffered(1))
    of_layer = lambda shape: pl.BlockSpec((None,) + shape, lambda i: (layer,) + (0,) * len(shape),
                                          pipeline_mode=pl.Buffered(1))
    ytile = pl.BlockSpec((GROUP_WIDTH, tm), lambda i: (0, i))
    return pl.pallas_call(
        functools.partial(_post_body, final=final),
        grid=(t // tm,),
        in_specs=[
            pl.BlockSpec((tm, D_MODEL), lambda i: (i, 0)),
            ytile, ytile, ytile, ytile,
            const((4, GROUP_WIDTH, 1)),
            of_layer((D_MODEL, D_MODEL)),
            const((1, D_MODEL)),
            of_layer((D_MODEL, D_FF)),
            of_layer((D_FF, D_MODEL)),
            const((1, D_MODEL)),
        ],
        out_specs=pl.BlockSpec((tm, D_MODEL), lambda i: (i, 0)),
        out_shape=jax.ShapeDtypeStruct((t, D_MODEL), F32),
        compiler_params=_params("parallel"),
        name="post",
    )(x, ya, yb, yc, yd, gg_col, w_out, g_ffn, w_up, w_down, g_final)


def _rope_tables(seq, half):
    inv_freq = np.power(ROPE_BASE, -np.arange(half, dtype=np.float64) / half)
    ang = inv_freq[:, None] * np.arange(seq, dtype=np.float64)[None, :]
    return jnp.asarray(np.cos(ang), F32), jnp.asarray(np.sin(ang), F32)


def _inproj_weights(w_in):
    wt = jnp.swapaxes(w_in, 1, 2)
    w_abc = wt[:, :ROWS_ABC].astype(BF16)
    f = wt[:, ROWS_ABC:ROWS_ABC + N_HEADS]
    d = wt[:, ROWS_ABC + N_HEADS:ROWS_ABC + N_HEADS + ROWS_D]
    pad = jnp.zeros((w_in.shape[0], ROWS_F - N_HEADS, D_MODEL), w_in.dtype)
    return w_abc, jnp.concatenate([d, f, pad], axis=1).astype(BF16)


def _mla_weights(w_uq, w_ukv):
    wq = w_uq.T.reshape(N_HEADS, MLA_NOPE + MLA_ROPE, MLA_Q_LORA)
    wq = jnp.pad(wq, ((0, 0), (0, HEAD_PAD - MLA_NOPE - MLA_ROPE), (0, 0)))
    wq = wq.reshape(N_HEADS * HEAD_PAD, MLA_Q_LORA).astype(BF16)
    wkv = w_ukv.T.reshape(N_HEADS, 2 * HEAD_DIM, MLA_KV_LORA)
    wk = wkv[:, :HEAD_DIM].reshape(GROUP_WIDTH, MLA_KV_LORA).astype(BF16)
    wv = wkv[:, HEAD_DIM:].reshape(GROUP_WIDTH, MLA_KV_LORA).astype(BF16)
    return wq, wk, wv


def kernel(x, g_mix_norm, w_in, b_forget, g_sgu, w_spatial, b_spatial, g_mla_q, w_uq, g_mla_kv, w_ukv,
           g_group_out, w_out, g_ffn_norm, w_up, w_down, g_final):
    batch, seq, _ = x.shape
    depth = w_in.shape[0]
    assert seq % TOKEN_TILE == 0 and ATTN_BLOCK == TOKEN_TILE
    cos_b, sin_b = _rope_tables(seq, HEAD_DIM // 2)
    cos_d, sin_d = _rope_tables(seq, MLA_ROPE // 2)
    w_abc, w_df = _inproj_weights(w_in)
    w_out_b, w_up_b, w_down_b = w_out.astype(BF16), w_up.astype(BF16), w_down.astype(BF16)
    xf = x.reshape(batch * seq, D_MODEL)
    for l in range(depth):
        bf_col = jnp.pad(b_forget[l], (0, ROWS_F - N_HEADS))[:, None]
        ya, yb, qt_c, k_c, vt_c, stats_c, qt_d, k_d, vt_d, stats_d = _front(
            xf, g_mix_norm[l][None, :], w_abc, w_df,
            g_sgu[l][:, None], w_spatial[l], b_spatial[l], cos_b, sin_b, bf_col,
            cos_d, sin_d, g_mla_q[l][:, None], g_mla_kv[l][:, None], *_mla_weights(w_uq[l], w_ukv[l]),
            batch, l)
        yc = _attention(qt_c, k_c, vt_c, stats_c, banded=True)
        yd = _attention(qt_d, k_d, vt_d, stats_d, banded=False)
        xf = _post(xf, ya, yb, yc, yd, g_group_out[l].reshape(4, GROUP_WIDTH, 1),
                   w_out_b, g_ffn_norm[l][None, :], w_up_b, w_down_b, g_final[None, :],
                   layer=l, final=(l == depth - 1))
    return xf.reshape(batch, seq, D_MODEL)
```

```python
import functools
import math

import jax
import jax.numpy as jnp
import numpy as np
from jax import lax
from jax.experimental import pallas as pl
from jax.experimental.pallas import tpu as pltpu

F32 = jnp.float32
BF16 = jnp.bfloat16

D_MODEL = 1024
N_HEADS = 4
HEAD_DIM = 64
GROUP_WIDTH = N_HEADS * HEAD_DIM
CHUNK = 128
MLA_Q_LORA = 256
MLA_KV_LORA = 128
MLA_NOPE = 64
MLA_ROPE = 32
ROPE_BASE = 10000.0
D_FF = 4 * D_MODEL
EPS = 1e-6

HEAD_PAD = 128
AUG_ROWS = 8
PV_ROWS = 80
LOG2E = math.log2(math.e)
EXP2_UNDERFLOW = 152.0
STAT_QNORM, STAT_KNORM, STAT_CMAX, STAT_CMIN, STAT_DIAG, STAT_GAP = range(6)
N_STATS = 6
GAP_LIMIT = 64.0
TOKEN_TILE = 512
ATTN_BLOCK = TOKEN_TILE
FF_CHUNK = 512
VMEM_LIMIT = 56 * 1024 * 1024

ROWS_A = 2 * GROUP_WIDTH
ROWS_B = 4 * GROUP_WIDTH
ROWS_C = 3 * GROUP_WIDTH
ROWS_D = MLA_Q_LORA + MLA_KV_LORA + MLA_ROPE
ROWS_F = 8
ROWS_ABC = ROWS_A + ROWS_B + ROWS_C

NT = (((1,), (1,)), ((), ()))
TN = (((0,), (0,)), ((), ()))


def _params(*sem):
    return pltpu.CompilerParams(dimension_semantics=sem, vmem_limit_bytes=VMEM_LIMIT)


def _rot_half_rows(t, cos, sin):
    half = t.shape[0] // 2
    t1, t2 = t[:half], t[half:]
    return jnp.concatenate([t1 * cos - t2 * sin, t1 * sin + t2 * cos], axis=0)


def _standardize_rows(t):
    mu = jnp.mean(t, axis=0, keepdims=True)
    var = jnp.mean(jnp.square(t - mu), axis=0, keepdims=True)
    return (t - mu) * lax.rsqrt(var + EPS)


def _rms_rows(t):
    return t * lax.rsqrt(jnp.mean(t * t, axis=0, keepdims=True) + EPS)


def _sgu(at_ref, gain_ref, ws_ref, bs_ref, o_ref):
    tm = at_ref.shape[1]
    nch = tm // CHUNK
    row = lax.broadcasted_iota(jnp.int32, (CHUNK, CHUNK), 0)
    col = lax.broadcasted_iota(jnp.int32, (CHUNK, CHUNK), 1)
    for h in range(N_HEADS):
        r0, r1 = h * HEAD_DIM, (h + 1) * HEAD_DIM
        u = jax.nn.gelu(at_ref[r0:r1, :].astype(F32))
        v = jax.nn.gelu(at_ref[GROUP_WIDTH + r0:GROUP_WIDTH + r1, :].astype(F32))
        v = _standardize_rows(v) * gain_ref[r0:r1, :]
        w = jnp.where(col <= row, ws_ref[h], 0.0).astype(BF16)
        vs = jnp.concatenate([v[:, c * CHUNK:(c + 1) * CHUNK] for c in range(nch)], axis=0).astype(BF16)
        mixed = lax.dot_general(vs, w, NT, preferred_element_type=F32) + bs_ref[h:h + 1, :]
        for c in range(nch):
            o_ref[r0:r1, c * CHUNK:(c + 1) * CHUNK] = (
                u[:, c * CHUNK:(c + 1) * CHUNK] * mixed[c * HEAD_DIM:(c + 1) * HEAD_DIM])


def _retention(bt_ref, cos_ref, sin_ref, o_ref, st_ref):
    tm = bt_ref.shape[1]
    nch = tm // CHUNK
    cos, sin = cos_ref[...], sin_ref[...]
    srow = lax.broadcasted_iota(jnp.int32, (CHUNK, CHUNK), 0)
    tcol = lax.broadcasted_iota(jnp.int32, (CHUNK, CHUNK), 1)
    rel = (tcol - srow).astype(F32)
    j = lax.broadcasted_iota(jnp.int32, (1, CHUNK), 1).astype(F32)
    for h in range(N_HEADS):
        log_gamma = math.log1p(-(2.0 ** (-5.0 - h)))
        dec_t = jnp.where(rel >= 0, jnp.exp(jnp.maximum(rel, 0.0) * log_gamma), 0.0)
        query_w = jnp.exp((j + 1.0) * log_gamma)
        key_w = jnp.exp((CHUNK - 1.0 - j) * log_gamma)
        chunk_decay = math.exp(CHUNK * log_gamma)
        r0, r1 = h * HEAD_DIM, (h + 1) * HEAD_DIM
        q = _rot_half_rows(bt_ref[r0:r1, :].astype(F32), cos, sin)
        k = _rot_half_rows(bt_ref[GROUP_WIDTH + r0:GROUP_WIDTH + r1, :].astype(F32), cos, sin) * (HEAD_DIM ** -0.5)
        v = bt_ref[2 * GROUP_WIDTH + r0:2 * GROUP_WIDTH + r1, :].astype(F32)
        g = bt_ref[3 * GROUP_WIDTH + r0:3 * GROUP_WIDTH + r1, :].astype(F32)
        st = st_ref[h]
        ys = []
        for c in range(nch):
            sl = slice(c * CHUNK, (c + 1) * CHUNK)
            qc, kc, vc = q[:, sl], k[:, sl], v[:, sl]
            kcb = kc.astype(BF16)
            a_t = lax.dot_general(kcb, qc.astype(BF16), TN, preferred_element_type=F32)
            p_t = (a_t * dec_t).astype(BF16)
            intra = jnp.dot(vc.astype(BF16), p_t, preferred_element_type=F32)
            cross = jnp.dot(st.astype(BF16), (qc * query_w).astype(BF16), preferred_element_type=F32)
            ys.append(intra + cross)
            st = chunk_decay * st + lax.dot_general((vc * key_w).astype(BF16), kcb, NT,
                                                    preferred_element_type=F32)
        st_ref[h] = st
        y = _standardize_rows(jnp.concatenate(ys, axis=1))
        o_ref[r0:r1, :] = jax.nn.silu(g) * y


def _select_rows(rows):
    n = rows[0].shape[1]
    ridx = lax.broadcasted_iota(jnp.int32, (AUG_ROWS, n), 0)
    out = jnp.zeros((AUG_ROWS, n), F32)
    for i, r in enumerate(rows):
        out = jnp.where(ridx == i, jnp.broadcast_to(r, (AUG_ROWS, n)), out)
    return out


def _group(feat, extra):
    n = feat.shape[1]
    pad = HEAD_PAD - feat.shape[0] - extra.shape[0]
    return jnp.concatenate([feat, extra, jnp.zeros((pad, n), F32)], axis=0)


def _split3(x):
    hi = x.astype(BF16).astype(F32)
    mid = (x - hi).astype(BF16).astype(F32)
    lo = (x - hi - mid).astype(BF16).astype(F32)
    return hi, mid, lo


def _logit_bound(qb, kb, kmax_ref, row):
    qnorm = jnp.sqrt(jnp.sum(qb * qb, axis=0, keepdims=True))
    knorm = jnp.sqrt(jnp.max(jnp.sum(kb * kb, axis=0, keepdims=True), axis=1, keepdims=True))
    kmax = jnp.maximum(kmax_ref[row:row + 1, :], knorm)
    kmax_ref[row:row + 1, :] = kmax
    return qnorm * kmax, jnp.sum(qb * kb, axis=0, keepdims=True), qnorm, knorm


def _write_stats(st_ref, per_head):
    stat_row = lax.broadcasted_iota(jnp.int32, (8, 128), 0)
    stat_lane = lax.broadcasted_iota(jnp.int32, (8, 128), 1)
    stats = jnp.zeros((8, 128), F32)
    for h, vals in enumerate(per_head):
        for r, val in enumerate(vals):
            stats = jnp.where((stat_row == r) & (stat_lane == h), val, stats)
    st_ref[0, 0] = stats


def _fox_prep(ct_ref, ft_ref, bf_ref, qt_ref, k_ref, vt_ref, st_ref, carry_ref, kmax_ref):
    tm = ct_ref.shape[1]
    x = ft_ref[...] + bf_ref[...]
    lf = jnp.minimum(x, 0.0) - jnp.log1p(jnp.exp(-jnp.abs(x)))
    hi = lf.astype(BF16)
    mid = (lf - hi.astype(F32)).astype(BF16)
    lo = (lf - hi.astype(F32) - mid.astype(F32)).astype(BF16)
    srow = lax.broadcasted_iota(jnp.int32, (tm, tm), 0)
    tcol = lax.broadcasted_iota(jnp.int32, (tm, tm), 1)
    upper = jnp.where(srow <= tcol, 1.0, 0.0).astype(BF16)
    parts = jnp.dot(jnp.concatenate([hi, mid, lo], axis=0), upper, preferred_element_type=F32)
    cum = parts[0:8] + parts[8:16] + parts[16:24] + carry_ref[...]
    carry_ref[...] = cum[:, tm - 1:tm]

    cum2 = cum * LOG2E
    one = jnp.ones((1, tm), F32)
    ones_row = _select_rows([one])
    stats = []
    for h in range(N_HEADS):
        r0, r1 = h * HEAD_DIM, (h + 1) * HEAD_DIM
        g0, g1 = h * HEAD_PAD, (h + 1) * HEAD_PAD
        q = ct_ref[r0:r1, :].astype(F32) * (HEAD_DIM ** -0.5 * LOG2E)
        k = ct_ref[GROUP_WIDTH + r0:GROUP_WIDTH + r1, :].astype(F32)
        v = ct_ref[2 * GROUP_WIDTH + r0:2 * GROUP_WIDTH + r1, :].astype(F32)
        qb, kb = q.astype(BF16).astype(F32), k.astype(BF16).astype(F32)
        bound, diag, qnorm, knorm = _logit_bound(qb, kb, kmax_ref, h)
        c2 = cum2[h:h + 1]
        q_extra = _select_rows([*_split3(c2 - bound), one, one, one])
        k_extra = _select_rows([one, one, one, *(-part for part in _split3(c2))])
        qt_ref[g0:g1, :] = _group(q, q_extra).astype(BF16)
        k_ref[:, g0:g1] = _group(k, k_extra).T.astype(BF16)
        vt_ref[0, h, 0] = _group(v, ones_row).astype(BF16)
        stats.append((
            jnp.max(qnorm, axis=1, keepdims=True),
            knorm,
            jnp.max(c2, axis=1, keepdims=True),
            jnp.min(c2, axis=1, keepdims=True),
            jnp.min(diag, axis=1, keepdims=True),
            jnp.max(bound - diag, axis=1, keepdims=True),
        ))
    _write_stats(st_ref, stats)


def _mla_prep(dt_ref, cos_ref, sin_ref, gq_ref, gkv_ref, wq_ref, wk_ref, wv_ref, qt_ref, k_ref, vt_ref,
              st_ref, kmax_ref):
    tm = dt_ref.shape[1]
    cos, sin = cos_ref[...], sin_ref[...]
    cq = _rms_rows(dt_ref[0:MLA_Q_LORA, :].astype(F32)) * gq_ref[...]
    ckv = _rms_rows(dt_ref[MLA_Q_LORA:MLA_Q_LORA + MLA_KV_LORA, :].astype(F32)) * gkv_ref[...]
    kr = _rot_half_rows(dt_ref[MLA_Q_LORA + MLA_KV_LORA:ROWS_D, :].astype(F32), cos, sin)
    ckv_b = ckv.astype(BF16)
    q_all = jnp.dot(wq_ref[...], cq.astype(BF16), preferred_element_type=F32)
    q_all = q_all * ((MLA_NOPE + MLA_ROPE) ** -0.5 * LOG2E)
    k_all = jnp.dot(wk_ref[...], ckv_b, preferred_element_type=F32)
    v_all = jnp.dot(wv_ref[...], ckv_b, preferred_element_type=F32)
    one = jnp.ones((1, tm), F32)
    ones_row = _select_rows([one])
    k_extra = _select_rows([one, one, one])
    zero = jnp.zeros((1, 1), F32)
    stats = []
    for h in range(N_HEADS):
        r0, r1 = h * HEAD_DIM, (h + 1) * HEAD_DIM
        g0, g1 = h * HEAD_PAD, (h + 1) * HEAD_PAD
        qg = q_all[g0:g1]
        q_rope = _rot_half_rows(qg[MLA_NOPE:MLA_NOPE + MLA_ROPE], cos, sin)
        q = jnp.concatenate([qg[0:MLA_NOPE], q_rope], axis=0)
        k = jnp.concatenate([k_all[r0:r1], kr], axis=0)
        qb, kb = q.astype(BF16).astype(F32), k.astype(BF16).astype(F32)
        bound, diag, _, _ = _logit_bound(qb, kb, kmax_ref, N_HEADS + h)
        q_extra = _select_rows([*_split3(-bound)])
        qt_ref[g0:g1, :] = _group(q, q_extra).astype(BF16)
        k_ref[:, g0:g1] = _group(k, k_extra).T.astype(BF16)
        vt_ref[0, h, 0] = _group(v_all[r0:r1], ones_row).astype(BF16)
        stats.append((zero,) * STAT_GAP + (jnp.max(bound - diag, axis=1, keepdims=True),))
    _write_stats(st_ref, stats)


def _front_body(x_ref, g_ref, w_ref, wdf_ref,
                gain_ref, ws_ref, bs_ref,
                cosb_ref, sinb_ref,
                bf_ref,
                cosd_ref, sind_ref, gq_ref, gkv_ref, wq_ref, wk_ref, wv_ref,
                ya_ref, yb_ref,
                qtc_ref, kc_ref, vtc_ref, stats_ref,
                qtd_ref, kd_ref, vtd_ref, statsd_ref,
                at_ref, bt_ref, ct_ref, dt_ref, ft_ref, state_ref, carry_ref, kmax_ref, *, tiles_per_seq):
    @pl.when(pl.program_id(0) % tiles_per_seq == 0)
    def _():
        state_ref[...] = jnp.zeros_like(state_ref)
        carry_ref[...] = jnp.zeros_like(carry_ref)
        kmax_ref[...] = jnp.zeros_like(kmax_ref)

    x = x_ref[...]
    h = (x * lax.rsqrt(jnp.mean(x * x, axis=-1, keepdims=True) + EPS) * g_ref[...]).astype(BF16)

    def proj(w):
        return lax.dot_general(w, h, NT, preferred_element_type=F32)

    half_b = ROWS_B // 2
    zdf = proj(wdf_ref[...])
    dt_ref[...] = zdf[0:ROWS_D].astype(BF16)
    ft_ref[...] = zdf[ROWS_D:ROWS_D + ROWS_F]
    at_ref[...] = proj(w_ref[0:ROWS_A, :]).astype(BF16)
    _mla_prep(dt_ref, cosd_ref, sind_ref, gq_ref, gkv_ref, wq_ref, wk_ref, wv_ref, qtd_ref, kd_ref, vtd_ref,
              statsd_ref, kmax_ref)
    ct_ref[...] = proj(w_ref[ROWS_A + ROWS_B:ROWS_ABC, :]).astype(BF16)
    _sgu(at_ref, gain_ref, ws_ref, bs_ref, ya_ref)
    bt_ref[0:half_b, :] = proj(w_ref[ROWS_A:ROWS_A + half_b, :]).astype(BF16)
    _fox_prep(ct_ref, ft_ref, bf_ref, qtc_ref, kc_ref, vtc_ref, stats_ref, carry_ref, kmax_ref)
    bt_ref[half_b:ROWS_B, :] = proj(w_ref[ROWS_A + half_b:ROWS_A + ROWS_B, :]).astype(BF16)
    _retention(bt_ref, cosb_ref, sinb_ref, yb_ref, state_ref)


def _front(x, g, w_t, wdf_t, gain_col, w_s, b_s, cos_b, sin_b, bf_col,
           cos_d, sin_d, gq_col, gkv_col, wq_t, wk_t, wv_t, batch, layer):
    t = x.shape[0]
    tm = TOKEN_TILE
    ns = t // batch // tm
    const = lambda shape: pl.BlockSpec(shape, lambda i: (0,) * len(shape), pipeline_mode=pl.Buffered(1))
    of_layer = lambda shape: pl.BlockSpec((None,) + shape, lambda i: (layer,) + (0,) * len(shape),
                                          pipeline_mode=pl.Buffered(1))
    rows_t = lambda rows: pl.BlockSpec((rows, tm), lambda i: (0, i))
    table = lambda rows: pl.BlockSpec((rows, tm), lambda i: (0, i % ns))
    k_tile = pl.BlockSpec((tm, N_HEADS * HEAD_PAD), lambda i: (i, 0))
    vt_tile = pl.BlockSpec((1, N_HEADS, 1, HEAD_PAD, tm), lambda i: (i // ns, 0, i % ns, 0, 0))
    stats_tile = pl.BlockSpec((1, 1, 8, 128), lambda i: (i // ns, i % ns, 0, 0))
    qkv_shapes = [
        jax.ShapeDtypeStruct((N_HEADS * HEAD_PAD, t), BF16),
        jax.ShapeDtypeStruct((t, N_HEADS * HEAD_PAD), BF16),
        jax.ShapeDtypeStruct((batch, N_HEADS, ns, HEAD_PAD, tm), BF16),
        jax.ShapeDtypeStruct((batch, ns, 8, 128), F32),
    ]
    return pl.pallas_call(
        functools.partial(_front_body, tiles_per_seq=ns),
        grid=(t // tm,),
        in_specs=[
            pl.BlockSpec((tm, D_MODEL), lambda i: (i, 0)),
            const((1, D_MODEL)),
            of_layer((ROWS_ABC, D_MODEL)),
            of_layer((ROWS_D + ROWS_F, D_MODEL)),
            const((GROUP_WIDTH, 1)),
            const((N_HEADS, CHUNK, CHUNK)),
            const((N_HEADS, CHUNK)),
            table(HEAD_DIM // 2), table(HEAD_DIM // 2),
            const((ROWS_F, 1)),
            table(MLA_ROPE // 2), table(MLA_ROPE // 2),
            const((MLA_Q_LORA, 1)),
            const((MLA_KV_LORA, 1)),
            const((N_HEADS * HEAD_PAD, MLA_Q_LORA)),
            const((GROUP_WIDTH, MLA_KV_LORA)),
            const((GROUP_WIDTH, MLA_KV_LORA)),
        ],
        out_specs=[
            rows_t(GROUP_WIDTH), rows_t(GROUP_WIDTH),
            rows_t(N_HEADS * HEAD_PAD), k_tile, vt_tile, stats_tile,
            rows_t(N_HEADS * HEAD_PAD), k_tile, vt_tile, stats_tile,
        ],
        out_shape=[
            jax.ShapeDtypeStruct((GROUP_WIDTH, t), F32),
            jax.ShapeDtypeStruct((GROUP_WIDTH, t), F32),
            *qkv_shapes,
            *qkv_shapes,
        ],
        scratch_shapes=[
            pltpu.VMEM((ROWS_A, tm), BF16),
            pltpu.VMEM((ROWS_B, tm), BF16),
            pltpu.VMEM((ROWS_C, tm), BF16),
            pltpu.VMEM((ROWS_D, tm), BF16),
            pltpu.VMEM((ROWS_F, tm), F32),
            pltpu.VMEM((N_HEADS, HEAD_DIM, HEAD_DIM), F32),
            pltpu.VMEM((ROWS_F, 1), F32),
            pltpu.VMEM((2 * N_HEADS, 1), F32),
        ],
        compiler_params=_params("arbitrary"),
        name="front",
    )(x, g, w_t, wdf_t, gain_col, w_s, b_s, cos_b, sin_b, bf_col,
      cos_d, sin_d, gq_col, gkv_col, wq_t, wk_t, wv_t)


def _first_needed_block(stats_ref, b, qi, nb):
    def stat(j, r, h):
        return stats_ref[((b * nb + j) * N_STATS + r) * N_HEADS + h]

    skipped = jnp.int32(0)
    leading = jnp.bool_(True)
    for j in range(nb - 1):
        zero = j < qi
        for h in range(N_HEADS):
            bound = (stat(qi, STAT_QNORM, h) * stat(j, STAT_KNORM, h)
                     + stat(qi, STAT_CMAX, h) - stat(j, STAT_CMIN, h))
            zero = jnp.logical_and(zero, bound - stat(qi, STAT_DIAG, h) < -EXP2_UNDERFLOW)
        leading = jnp.logical_and(leading, zero)
        skipped = skipped + leading.astype(jnp.int32)
    return skipped


def _attn_body(*refs, banded, online):
    refs = list(refs)
    stats_ref = refs.pop(0) if banded else None
    qt_ref, k_ref, vt_ref, o_ref, acc_ref, s_ref = refs[:6]
    m_ref = refs[6] if online else None
    blk = qt_ref.shape[1]
    nb = vt_ref.shape[2]
    qi = pl.program_id(1)
    first = _first_needed_block(stats_ref, pl.program_id(0), qi, nb) if banded else 0
    acc_ref[...] = jnp.zeros_like(acc_ref)
    if online:
        m_ref[...] = jnp.full(m_ref.shape, -jnp.inf, F32)

    def logits(kj, h):
        start = pl.multiple_of(kj * blk, blk)
        g0, g1 = h * HEAD_PAD, (h + 1) * HEAD_PAD
        return jnp.dot(k_ref[pl.ds(start, blk), g0:g1], qt_ref[g0:g1, :],
                       preferred_element_type=F32)

    def accumulate(s, kj, h):
        acc = acc_ref[h]
        if online:
            m_old = m_ref[h]
            m_new = jnp.maximum(m_old, jnp.max(s, axis=0, keepdims=True))
            m_ref[h] = m_new
            s = s - m_new
            acc = acc * jnp.exp2(m_old - m_new)
        pv = jnp.dot(vt_ref[0, h, kj, 0:PV_ROWS, :], jnp.exp2(s).astype(BF16), preferred_element_type=F32)
        acc_ref[h] = acc + pv[0:HEAD_DIM + AUG_ROWS]

    s_ref[...] = logits(first, 0)

    def full_blocks(kj, count):
        s = s_ref[...]
        for d in range(count):
            for h in range(N_HEADS):
                nxt = (kj + d, h + 1) if h + 1 < N_HEADS else (kj + d + 1, 0)
                s_next = logits(*nxt)
                accumulate(s, kj + d, h)
                s = s_next
        s_ref[...] = s

    n_full = qi - first
    odd = jnp.bitwise_and(n_full, 1)
    pl.when(odd == 1)(lambda: full_blocks(first, 1))

    def pair_step(i, carry):
        full_blocks(first + odd + 2 * i, 2)
        return carry

    lax.fori_loop(0, lax.shift_right_logical(n_full, 1), pair_step, 0)

    if online:
        key_pos = lax.broadcasted_iota(jnp.int32, (blk, blk), 0)
        qry_pos = lax.broadcasted_iota(jnp.int32, (blk, blk), 1)
        visible = key_pos <= qry_pos
        s = s_ref[...]
        for h in range(N_HEADS):
            s_next = logits(qi, h + 1) if h + 1 < N_HEADS else None
            accumulate(jnp.where(visible, s, -jnp.inf), qi, h)
            s = s_next
    else:
        half = blk // 2
        start = pl.multiple_of(qi * blk, blk)
        key_pos = lax.broadcasted_iota(jnp.int32, (half, blk), 0)
        qry_pos = lax.broadcasted_iota(jnp.int32, (half, blk), 1)
        visible_a = key_pos <= qry_pos
        visible_b = visible_a[:, 0:half]

        def quadrant_logits(h):
            g0, g1 = h * HEAD_PAD, (h + 1) * HEAD_PAD
            s_a = jnp.dot(k_ref[pl.ds(start, half), g0:g1], qt_ref[g0:g1, :], preferred_element_type=F32)
            s_b = jnp.dot(k_ref[pl.ds(start + half, half), g0:g1], qt_ref[g0:g1, half:blk],
                          preferred_element_type=F32)
            return s_a, s_b

        s_full = s_ref[...]
        s_a, s_b = s_full[0:half], s_full[half:blk, half:blk]
        for h in range(N_HEADS):
            s_next = quadrant_logits(h + 1) if h + 1 < N_HEADS else None
            p_a = jnp.exp2(jnp.where(visible_a, s_a, -jnp.inf)).astype(BF16)
            p_b = jnp.exp2(jnp.where(visible_b, s_b, -jnp.inf)).astype(BF16)
            pv_a = jnp.dot(vt_ref[0, h, qi, 0:PV_ROWS, 0:half], p_a, preferred_element_type=F32)
            pv_b = jnp.dot(vt_ref[0, h, qi, 0:PV_ROWS, half:blk], p_b, preferred_element_type=F32)
            acc_ref[h] = acc_ref[h] + pv_a[0:HEAD_DIM + AUG_ROWS]
            acc_ref[h, :, half:blk] = acc_ref[h, :, half:blk] + pv_b[0:HEAD_DIM + AUG_ROWS]
            if s_next is not None:
                s_a, s_b = s_next
    for h in range(N_HEADS):
        acc = acc_ref[h]
        o_ref[h * HEAD_DIM:(h + 1) * HEAD_DIM, :] = acc[0:HEAD_DIM] / acc[HEAD_DIM:HEAD_DIM + 1]


def _attention_call(qt, k, vt, skip_stats, online):
    batch, _, nb, _, blk = vt.shape
    t = qt.shape[1]
    seq = t // batch
    banded = skip_stats is not None
    in_specs = [
        pl.BlockSpec((N_HEADS * HEAD_PAD, blk), lambda b, i: (0, b * nb + i)),
        pl.BlockSpec((seq, N_HEADS * HEAD_PAD), lambda b, i: (b, 0)),
        pl.BlockSpec((1, N_HEADS, nb, HEAD_PAD, blk), lambda b, i: (b, 0, 0, 0, 0)),
    ]
    args = (qt, k, vt)
    if banded:
        in_specs = [pl.BlockSpec(memory_space=pltpu.SMEM)] + in_specs
        args = (skip_stats,) + args
    scratch = [pltpu.VMEM((N_HEADS, HEAD_DIM + AUG_ROWS, blk), F32), pltpu.VMEM((blk, blk), F32)]
    if online:
        scratch.append(pltpu.VMEM((N_HEADS, 1, blk), F32))
    return pl.pallas_call(
        functools.partial(_attn_body, banded=banded, online=online),
        grid=(batch, nb),
        in_specs=in_specs,
        out_specs=pl.BlockSpec((GROUP_WIDTH, blk), lambda b, i: (0, b * nb + i)),
        out_shape=jax.ShapeDtypeStruct((GROUP_WIDTH, t), F32),
        scratch_shapes=scratch,
        compiler_params=_params("parallel", "arbitrary"),
        name="attention_online" if online else "attention",
    )(*args)


def _attention(qt, k, vt, stats, banded):
    stats = stats[:, :, :N_STATS, :N_HEADS]
    skip_stats = stats.reshape(-1) if banded else None
    bound_is_tight = jnp.max(stats[:, :, STAT_GAP, :]) <= GAP_LIMIT
    return lax.cond(bound_is_tight,
                    lambda: _attention_call(qt, k, vt, skip_stats, online=False),
                    lambda: _attention_call(qt, k, vt, skip_stats, online=True))


def _post_body(x_ref, ya_ref, yb_ref, yc_ref, yd_ref, gg_ref, wo_ref, gf_ref, wu_ref, wd_ref,
               gl_ref, o_ref, *, final):
    ys = []
    for g, y_ref in enumerate((ya_ref, yb_ref, yc_ref, yd_ref)):
        ys.append((_rms_rows(y_ref[...]) * gg_ref[g]).astype(BF16))
    y = jnp.concatenate(ys, axis=0)
    x = x_ref[...] + lax.dot_general(y, wo_ref[...], TN, preferred_element_type=F32)
    h = (x * lax.rsqrt(jnp.mean(x * x, axis=-1, keepdims=True) + EPS) * gf_ref[...]).astype(BF16)
    acc = x
    for c in range(D_FF // FF_CHUNK):
        c0, c1 = c * FF_CHUNK, (c + 1) * FF_CHUNK
        a = jnp.maximum(jnp.dot(h, wu_ref[:, c0:c1], preferred_element_type=F32), 0.0)
        acc = acc + jnp.dot((a * a).astype(BF16), wd_ref[c0:c1, :], preferred_element_type=F32)
    if final:
        acc = acc * lax.rsqrt(jnp.mean(acc * acc, axis=-1, keepdims=True) + EPS) * gl_ref[...]
    o_ref[...] = acc


def _post(x, ya, yb, yc, yd, gg_col, w_out, g_ffn, w_up, w_down, g_final, layer, final):
    t = x.shape[0]
    tm = TOKEN_TILE
    const = lambda shape: pl.BlockSpec(shape, lambda i: (0,) * len(shape), pipeline_mode=pl.Buffered(1))
    of_layer = lambda shape: pl.BlockSpec((None,) + shape, lambda i: (layer,) + (0,) * len(shape),
                                          pipeline_mode=pl.Buffered(1))
    ytile = pl.BlockSpec((GROUP_WIDTH, tm), lambda i: (0, i))
    return pl.pallas_call(
        functools.partial(_post_body, final=final),
        grid=(t // tm,),
        in_specs=[
            pl.BlockSpec((tm, D_MODEL), lambda i: (i, 0)),
            ytile, ytile, ytile, ytile,
            const((4, GROUP_WIDTH, 1)),
            of_layer((D_MODEL, D_MODEL)),
            const((1, D_MODEL)),
            of_layer((D_MODEL, D_FF)),
            of_layer((D_FF, D_MODEL)),
            const((1, D_MODEL)),
        ],
        out_specs=pl.BlockSpec((tm, D_MODEL), lambda i: (i, 0)),
        out_shape=jax.ShapeDtypeStruct((t, D_MODEL), F32),
        compiler_params=_params("parallel"),
        name="post",
    )(x, ya, yb, yc, yd, gg_col, w_out, g_ffn, w_up, w_down, g_final)


def _rope_tables(seq, half):
    inv_freq = np.power(ROPE_BASE, -np.arange(half, dtype=np.float64) / half)
    ang = inv_freq[:, None] * np.arange(seq, dtype=np.float64)[None, :]
    return jnp.asarray(np.cos(ang), F32), jnp.asarray(np.sin(ang), F32)


def _inproj_weights(w_in):
    wt = jnp.swapaxes(w_in, 1, 2)
    w_abc = wt[:, :ROWS_ABC].astype(BF16)
    f = wt[:, ROWS_ABC:ROWS_ABC + N_HEADS]
    d = wt[:, ROWS_ABC + N_HEADS:ROWS_ABC + N_HEADS + ROWS_D]
    pad = jnp.zeros((w_in.shape[0], ROWS_F - N_HEADS, D_MODEL), w_in.dtype)
    return w_abc, jnp.concatenate([d, f, pad], axis=1).astype(BF16)


def _mla_weights(w_uq, w_ukv):
    wq = w_uq.T.reshape(N_HEADS, MLA_NOPE + MLA_ROPE, MLA_Q_LORA)
    wq = jnp.pad(wq, ((0, 0), (0, HEAD_PAD - MLA_NOPE - MLA_ROPE), (0, 0)))
    wq = wq.reshape(N_HEADS * HEAD_PAD, MLA_Q_LORA).astype(BF16)
    wkv = w_ukv.T.reshape(N_HEADS, 2 * HEAD_DIM, MLA_KV_LORA)
    wk = wkv[:, :HEAD_DIM].reshape(GROUP_WIDTH, MLA_KV_LORA).astype(BF16)
    wv = wkv[:, HEAD_DIM:].reshape(GROUP_WIDTH, MLA_KV_LORA).astype(BF16)
    return wq, wk, wv


def kernel(x, g_mix_norm, w_in, b_forget, g_sgu, w_spatial, b_spatial, g_mla_q, w_uq, g_mla_kv, w_ukv,
           g_group_out, w_out, g_ffn_norm, w_up, w_down, g_final):
    batch, seq, _ = x.shape
    depth = w_in.shape[0]
    assert seq % TOKEN_TILE == 0 and ATTN_BLOCK == TOKEN_TILE
    cos_b, sin_b = _rope_tables(seq, HEAD_DIM // 2)
    cos_d, sin_d = _rope_tables(seq, MLA_ROPE // 2)
    w_abc, w_df = _inproj_weights(w_in)
    w_out_b, w_up_b, w_down_b = w_out.astype(BF16), w_up.astype(BF16), w_down.astype(BF16)
    xf = x.reshape(batch * seq, D_MODEL)
    for l in range(depth):
        bf_col = jnp.pad(b_forget[l], (0, ROWS_F - N_HEADS))[:, None]
        ya, yb, qt_c, k_c, vt_c, stats_c, qt_d, k_d, vt_d, stats_d = _front(
            xf, g_mix_norm[l][None, :], w_abc, w_df,
            g_sgu[l][:, None], w_spatial[l], b_spatial[l], cos_b, sin_b, bf_col,
            cos_d, sin_d, g_mla_q[l][:, None], g_mla_kv[l][:, None], *_mla_weights(w_uq[l], w_ukv[l]),
            batch, l)
        yc = _attention(qt_c, k_c, vt_c, stats_c, banded=True)
        yd = _attention(qt_d, k_d, vt_d, stats_d, banded=False)
        xf = _post(xf, ya, yb, yc, yd, g_group_out[l].reshape(4, GROUP_WIDTH, 1),
                   w_out_b, g_ffn_norm[l][None, :], w_up_b, w_down_b, g_final[None, :],
                   layer=l, final=(l == depth - 1))
    return xf.reshape(batch, seq, D_MODEL)
```

```python
import functools
import math

import jax
import jax.numpy as jnp
import numpy as np
from jax import lax
from jax.experimental import pallas as pl
from jax.experimental.pallas import tpu as pltpu

F32 = jnp.float32
BF16 = jnp.bfloat16

D_MODEL = 1024
N_HEADS = 4
HEAD_DIM = 64
GROUP_WIDTH = N_HEADS * HEAD_DIM
CHUNK = 128
MLA_Q_LORA = 256
MLA_KV_LORA = 128
MLA_NOPE = 64
MLA_ROPE = 32
ROPE_BASE = 10000.0
D_FF = 4 * D_MODEL
EPS = 1e-6

HEAD_PAD = 128
AUG_ROWS = 8
PV_ROWS = 80
LOG2E = math.log2(math.e)
EXP2_UNDERFLOW = 152.0
STAT_QNORM, STAT_KNORM, STAT_CMAX, STAT_CMIN, STAT_DIAG, STAT_GAP = range(6)
N_STATS = 6
GAP_LIMIT = 64.0
TOKEN_TILE = 512
ATTN_BLOCK = TOKEN_TILE
FF_CHUNK = 512
WEIGHT_STAGE_ELEMS = 512 * 1024
VMEM_LIMIT = 56 * 1024 * 1024

ROWS_A = 2 * GROUP_WIDTH
ROWS_B = 4 * GROUP_WIDTH
ROWS_C = 3 * GROUP_WIDTH
ROWS_D = MLA_Q_LORA + MLA_KV_LORA + MLA_ROPE
ROWS_F = 8
ROWS_ABC = ROWS_A + ROWS_B + ROWS_C

NT = (((1,), (1,)), ((), ()))
TN = (((0,), (0,)), ((), ()))


def _params(*sem):
    return pltpu.CompilerParams(dimension_semantics=sem, vmem_limit_bytes=VMEM_LIMIT)


def _rot_half_rows(t, cos, sin):
    half = t.shape[0] // 2
    t1, t2 = t[:half], t[half:]
    return jnp.concatenate([t1 * cos - t2 * sin, t1 * sin + t2 * cos], axis=0)


def _standardize_rows(t):
    mu = jnp.mean(t, axis=0, keepdims=True)
    var = jnp.mean(jnp.square(t - mu), axis=0, keepdims=True)
    return (t - mu) * lax.rsqrt(var + EPS)


def _rms_rows(t):
    return t * lax.rsqrt(jnp.mean(t * t, axis=0, keepdims=True) + EPS)


def _sgu(at_ref, gain_ref, ws_ref, bs_ref, o_ref):
    tm = at_ref.shape[1]
    nch = tm // CHUNK
    row = lax.broadcasted_iota(jnp.int32, (CHUNK, CHUNK), 0)
    col = lax.broadcasted_iota(jnp.int32, (CHUNK, CHUNK), 1)
    for h in range(N_HEADS):
        r0, r1 = h * HEAD_DIM, (h + 1) * HEAD_DIM
        u = jax.nn.gelu(at_ref[r0:r1, :].astype(F32))
        v = jax.nn.gelu(at_ref[GROUP_WIDTH + r0:GROUP_WIDTH + r1, :].astype(F32))
        v = _standardize_rows(v) * gain_ref[r0:r1, :]
        w = jnp.where(col <= row, ws_ref[h], 0.0).astype(BF16)
        vs = jnp.concatenate([v[:, c * CHUNK:(c + 1) * CHUNK] for c in range(nch)], axis=0).astype(BF16)
        mixed = lax.dot_general(vs, w, NT, preferred_element_type=F32) + bs_ref[h:h + 1, :]
        for c in range(nch):
            o_ref[r0:r1, c * CHUNK:(c + 1) * CHUNK] = (
                u[:, c * CHUNK:(c + 1) * CHUNK] * mixed[c * HEAD_DIM:(c + 1) * HEAD_DIM])


def _retention(bt_ref, cos_ref, sin_ref, o_ref, st_ref):
    tm = bt_ref.shape[1]
    nch = tm // CHUNK
    cos, sin = cos_ref[...], sin_ref[...]
    srow = lax.broadcasted_iota(jnp.int32, (CHUNK, CHUNK), 0)
    tcol = lax.broadcasted_iota(jnp.int32, (CHUNK, CHUNK), 1)
    rel = (tcol - srow).astype(F32)
    j = lax.broadcasted_iota(jnp.int32, (1, CHUNK), 1).astype(F32)
    for h in range(N_HEADS):
        log_gamma = math.log1p(-(2.0 ** (-5.0 - h)))
        dec_t = jnp.where(rel >= 0, jnp.exp(jnp.maximum(rel, 0.0) * log_gamma), 0.0)
        query_w = jnp.exp((j + 1.0) * log_gamma)
        key_w = jnp.exp((CHUNK - 1.0 - j) * log_gamma)
        chunk_decay = math.exp(CHUNK * log_gamma)
        r0, r1 = h * HEAD_DIM, (h + 1) * HEAD_DIM
        q = _rot_half_rows(bt_ref[r0:r1, :].astype(F32), cos, sin)
        k = _rot_half_rows(bt_ref[GROUP_WIDTH + r0:GROUP_WIDTH + r1, :].astype(F32), cos, sin) * (HEAD_DIM ** -0.5)
        v = bt_ref[2 * GROUP_WIDTH + r0:2 * GROUP_WIDTH + r1, :].astype(F32)
        g = bt_ref[3 * GROUP_WIDTH + r0:3 * GROUP_WIDTH + r1, :].astype(F32)
        st = st_ref[h]
        ys = []
        for c in range(nch):
            sl = slice(c * CHUNK, (c + 1) * CHUNK)
            qc, kc, vc = q[:, sl], k[:, sl], v[:, sl]
            kcb = kc.astype(BF16)
            a_t = lax.dot_general(kcb, qc.astype(BF16), TN, preferred_element_type=F32)
            p_t = (a_t * dec_t).astype(BF16)
            intra = jnp.dot(vc.astype(BF16), p_t, preferred_element_type=F32)
            cross = jnp.dot(st.astype(BF16), (qc * query_w).astype(BF16), preferred_element_type=F32)
            ys.append(intra + cross)
            st = chunk_decay * st + lax.dot_general((vc * key_w).astype(BF16), kcb, NT,
                                                    preferred_element_type=F32)
        st_ref[h] = st
        y = _standardize_rows(jnp.concatenate(ys, axis=1))
        o_ref[r0:r1, :] = jax.nn.silu(g) * y


def _select_rows(rows):
    n = rows[0].shape[1]
    ridx = lax.broadcasted_iota(jnp.int32, (AUG_ROWS, n), 0)
    out = jnp.zeros((AUG_ROWS, n), F32)
    for i, r in enumerate(rows):
        out = jnp.where(ridx == i, jnp.broadcast_to(r, (AUG_ROWS, n)), out)
    return out


def _group(feat, extra):
    n = feat.shape[1]
    pad = HEAD_PAD - feat.shape[0] - extra.shape[0]
    return jnp.concatenate([feat, extra, jnp.zeros((pad, n), F32)], axis=0)


def _split3(x):
    hi = x.astype(BF16).astype(F32)
    mid = (x - hi).astype(BF16).astype(F32)
    lo = (x - hi - mid).astype(BF16).astype(F32)
    return hi, mid, lo


def _logit_bound(qb, kb, kmax_ref, row):
    qnorm = jnp.sqrt(jnp.sum(qb * qb, axis=0, keepdims=True))
    knorm = jnp.sqrt(jnp.max(jnp.sum(kb * kb, axis=0, keepdims=True), axis=1, keepdims=True))
    kmax = jnp.maximum(kmax_ref[row:row + 1, :], knorm)
    kmax_ref[row:row + 1, :] = kmax
    return qnorm * kmax, jnp.sum(qb * kb, axis=0, keepdims=True), qnorm, knorm


def _write_stats(st_ref, per_head):
    stat_row = lax.broadcasted_iota(jnp.int32, (8, 128), 0)
    stat_lane = lax.broadcasted_iota(jnp.int32, (8, 128), 1)
    stats = jnp.zeros((8, 128), F32)
    for h, vals in enumerate(per_head):
        for r, val in enumerate(vals):
            stats = jnp.where((stat_row == r) & (stat_lane == h), val, stats)
    st_ref[0, 0] = stats


def _fox_prep(ct_ref, ft_ref, bf_ref, qt_ref, k_ref, vt_ref, st_ref, carry_ref, kmax_ref):
    tm = ct_ref.shape[1]
    x = ft_ref[...] + bf_ref[...]
    lf = jnp.minimum(x, 0.0) - jnp.log1p(jnp.exp(-jnp.abs(x)))
    hi = lf.astype(BF16)
    mid = (lf - hi.astype(F32)).astype(BF16)
    lo = (lf - hi.astype(F32) - mid.astype(F32)).astype(BF16)
    srow = lax.broadcasted_iota(jnp.int32, (tm, tm), 0)
    tcol = lax.broadcasted_iota(jnp.int32, (tm, tm), 1)
    upper = jnp.where(srow <= tcol, 1.0, 0.0).astype(BF16)
    parts = jnp.dot(jnp.concatenate([hi, mid, lo], axis=0), upper, preferred_element_type=F32)
    cum = parts[0:8] + parts[8:16] + parts[16:24] + carry_ref[...]
    carry_ref[...] = cum[:, tm - 1:tm]

    cum2 = cum * LOG2E
    one = jnp.ones((1, tm), F32)
    ones_row = _select_rows([one])
    stats = []
    for h in range(N_HEADS):
        r0, r1 = h * HEAD_DIM, (h + 1) * HEAD_DIM
        g0, g1 = h * HEAD_PAD, (h + 1) * HEAD_PAD
        q = ct_ref[r0:r1, :].astype(F32) * (HEAD_DIM ** -0.5 * LOG2E)
        k = ct_ref[GROUP_WIDTH + r0:GROUP_WIDTH + r1, :].astype(F32)
        v = ct_ref[2 * GROUP_WIDTH + r0:2 * GROUP_WIDTH + r1, :].astype(F32)
        qb, kb = q.astype(BF16).astype(F32), k.astype(BF16).astype(F32)
        bound, diag, qnorm, knorm = _logit_bound(qb, kb, kmax_ref, h)
        c2 = cum2[h:h + 1]
        q_extra = _select_rows([*_split3(c2 - bound), one, one, one])
        k_extra = _select_rows([one, one, one, *(-part for part in _split3(c2))])
        qt_ref[g0:g1, :] = _group(q, q_extra).astype(BF16)
        k_ref[:, g0:g1] = _group(k, k_extra).T.astype(BF16)
        vt_ref[0, h, 0] = _group(v, ones_row).astype(BF16)
        stats.append((
            jnp.max(qnorm, axis=1, keepdims=True),
            knorm,
            jnp.max(c2, axis=1, keepdims=True),
            jnp.min(c2, axis=1, keepdims=True),
            jnp.min(diag, axis=1, keepdims=True),
            jnp.max(bound - diag, axis=1, keepdims=True),
        ))
    _write_stats(st_ref, stats)


def _mla_prep(dt_ref, cos_ref, sin_ref, gq_ref, gkv_ref, wq_ref, wk_ref, wv_ref, qt_ref, k_ref, vt_ref,
              st_ref, kmax_ref):
    tm = dt_ref.shape[1]
    cos, sin = cos_ref[...], sin_ref[...]
    cq = _rms_rows(dt_ref[0:MLA_Q_LORA, :].astype(F32)) * gq_ref[...]
    ckv = _rms_rows(dt_ref[MLA_Q_LORA:MLA_Q_LORA + MLA_KV_LORA, :].astype(F32)) * gkv_ref[...]
    kr = _rot_half_rows(dt_ref[MLA_Q_LORA + MLA_KV_LORA:ROWS_D, :].astype(F32), cos, sin)
    ckv_b = ckv.astype(BF16)
    q_all = jnp.dot(wq_ref[...], cq.astype(BF16), preferred_element_type=F32)
    q_all = q_all * ((MLA_NOPE + MLA_ROPE) ** -0.5 * LOG2E)
    k_all = jnp.dot(wk_ref[...], ckv_b, preferred_element_type=F32)
    v_all = jnp.dot(wv_ref[...], ckv_b, preferred_element_type=F32)
    one = jnp.ones((1, tm), F32)
    ones_row = _select_rows([one])
    k_extra = _select_rows([one, one, one])
    zero = jnp.zeros((1, 1), F32)
    stats = []
    for h in range(N_HEADS):
        r0, r1 = h * HEAD_DIM, (h + 1) * HEAD_DIM
        g0, g1 = h * HEAD_PAD, (h + 1) * HEAD_PAD
        qg = q_all[g0:g1]
        q_rope = _rot_half_rows(qg[MLA_NOPE:MLA_NOPE + MLA_ROPE], cos, sin)
        q = jnp.concatenate([qg[0:MLA_NOPE], q_rope], axis=0)
        k = jnp.concatenate([k_all[r0:r1], kr], axis=0)
        qb, kb = q.astype(BF16).astype(F32), k.astype(BF16).astype(F32)
        bound, diag, _, _ = _logit_bound(qb, kb, kmax_ref, N_HEADS + h)
        q_extra = _select_rows([*_split3(-bound)])
        qt_ref[g0:g1, :] = _group(q, q_extra).astype(BF16)
        k_ref[:, g0:g1] = _group(k, k_extra).T.astype(BF16)
        vt_ref[0, h, 0] = _group(v_all[r0:r1], ones_row).astype(BF16)
        stats.append((zero,) * STAT_GAP + (jnp.max(bound - diag, axis=1, keepdims=True),))
    _write_stats(st_ref, stats)


def _front_body(x_ref, g_ref, w_ref, wdf_ref,
                gain_ref, ws_ref, bs_ref,
                cosb_ref, sinb_ref,
                bf_ref,
                cosd_ref, sind_ref, gq_ref, gkv_ref, wq_ref, wk_ref, wv_ref,
                ya_ref, yb_ref,
                qtc_ref, kc_ref, vtc_ref, stats_ref,
                qtd_ref, kd_ref, vtd_ref, statsd_ref,
                at_ref, bt_ref, ct_ref, dt_ref, ft_ref, state_ref, carry_ref, kmax_ref, *, tiles_per_seq):
    @pl.when(pl.program_id(0) % tiles_per_seq == 0)
    def _():
        state_ref[...] = jnp.zeros_like(state_ref)
        carry_ref[...] = jnp.zeros_like(carry_ref)
        kmax_ref[...] = jnp.zeros_like(kmax_ref)

    x = x_ref[...]
    h = (x * lax.rsqrt(jnp.mean(x * x, axis=-1, keepdims=True) + EPS) * g_ref[...]).astype(BF16)

    def proj(w):
        return lax.dot_general(w, h, NT, preferred_element_type=F32)

    half_b = ROWS_B // 2
    zdf = proj(wdf_ref[...])
    dt_ref[...] = zdf[0:ROWS_D].astype(BF16)
    ft_ref[...] = zdf[ROWS_D:ROWS_D + ROWS_F]
    at_ref[...] = proj(w_ref[0:ROWS_A, :]).astype(BF16)
    _mla_prep(dt_ref, cosd_ref, sind_ref, gq_ref, gkv_ref, wq_ref, wk_ref, wv_ref, qtd_ref, kd_ref, vtd_ref,
              statsd_ref, kmax_ref)
    ct_ref[...] = proj(w_ref[ROWS_A + ROWS_B:ROWS_ABC, :]).astype(BF16)
    _sgu(at_ref, gain_ref, ws_ref, bs_ref, ya_ref)
    bt_ref[0:half_b, :] = proj(w_ref[ROWS_A:ROWS_A + half_b, :]).astype(BF16)
    _fox_prep(ct_ref, ft_ref, bf_ref, qtc_ref, kc_ref, vtc_ref, stats_ref, carry_ref, kmax_ref)
    bt_ref[half_b:ROWS_B, :] = proj(w_ref[ROWS_A + half_b:ROWS_A + ROWS_B, :]).astype(BF16)
    _retention(bt_ref, cosb_ref, sinb_ref, yb_ref, state_ref)


def _front(x, g, w_t, wdf_t, gain_col, w_s, b_s, cos_b, sin_b, bf_col,
           cos_d, sin_d, gq_col, gkv_col, wq_t, wk_t, wv_t, batch, layer):
    t = x.shape[0]
    tm = TOKEN_TILE
    ns = t // batch // tm
    const = lambda shape: pl.BlockSpec(shape, lambda i: (0,) * len(shape), pipeline_mode=pl.Buffered(1))
    of_layer = lambda shape: pl.BlockSpec((None,) + shape, lambda i: (layer,) + (0,) * len(shape),
                                          pipeline_mode=pl.Buffered(1))
    rows_t = lambda rows: pl.BlockSpec((rows, tm), lambda i: (0, i))
    table = lambda rows: pl.BlockSpec((rows, tm), lambda i: (0, i % ns))
    k_tile = pl.BlockSpec((tm, N_HEADS * HEAD_PAD), lambda i: (i, 0))
    vt_tile = pl.BlockSpec((1, N_HEADS, 1, HEAD_PAD, tm), lambda i: (i // ns, 0, i % ns, 0, 0))
    stats_tile = pl.BlockSpec((1, 1, 8, 128), lambda i: (i // ns, i % ns, 0, 0))
    qkv_shapes = [
        jax.ShapeDtypeStruct((N_HEADS * HEAD_PAD, t), BF16),
        jax.ShapeDtypeStruct((t, N_HEADS * HEAD_PAD), BF16),
        jax.ShapeDtypeStruct((batch, N_HEADS, ns, HEAD_PAD, tm), BF16),
        jax.ShapeDtypeStruct((batch, ns, 8, 128), F32),
    ]
    return pl.pallas_call(
        functools.partial(_front_body, tiles_per_seq=ns),
        grid=(t // tm,),
        in_specs=[
            pl.BlockSpec((tm, D_MODEL), lambda i: (i, 0)),
            const((1, D_MODEL)),
            of_layer((ROWS_ABC, D_MODEL)),
            of_layer((ROWS_D + ROWS_F, D_MODEL)),
            const((GROUP_WIDTH, 1)),
            const((N_HEADS, CHUNK, CHUNK)),
            const((N_HEADS, CHUNK)),
            table(HEAD_DIM // 2), table(HEAD_DIM // 2),
            const((ROWS_F, 1)),
            table(MLA_ROPE // 2), table(MLA_ROPE // 2),
            const((MLA_Q_LORA, 1)),
            const((MLA_KV_LORA, 1)),
            const((N_HEADS * HEAD_PAD, MLA_Q_LORA)),
            const((GROUP_WIDTH, MLA_KV_LORA)),
            const((GROUP_WIDTH, MLA_KV_LORA)),
        ],
        out_specs=[
            rows_t(GROUP_WIDTH), rows_t(GROUP_WIDTH),
            rows_t(N_HEADS * HEAD_PAD), k_tile, vt_tile, stats_tile,
            rows_t(N_HEADS * HEAD_PAD), k_tile, vt_tile, stats_tile,
        ],
        out_shape=[
            jax.ShapeDtypeStruct((GROUP_WIDTH, t), F32),
            jax.ShapeDtypeStruct((GROUP_WIDTH, t), F32),
            *qkv_shapes,
            *qkv_shapes,
        ],
        scratch_shapes=[
            pltpu.VMEM((ROWS_A, tm), BF16),
            pltpu.VMEM((ROWS_B, tm), BF16),
            pltpu.VMEM((ROWS_C, tm), BF16),
            pltpu.VMEM((ROWS_D, tm), BF16),
            pltpu.VMEM((ROWS_F, tm), F32),
            pltpu.VMEM((N_HEADS, HEAD_DIM, HEAD_DIM), F32),
            pltpu.VMEM((ROWS_F, 1), F32),
            pltpu.VMEM((2 * N_HEADS, 1), F32),
        ],
        compiler_params=_params("arbitrary"),
        name="front",
    )(x, g, w_t, wdf_t, gain_col, w_s, b_s, cos_b, sin_b, bf_col,
      cos_d, sin_d, gq_col, gkv_col, wq_t, wk_t, wv_t)


def _first_needed_block(stats_ref, b, qi, nb):
    def stat(j, r, h):
        return stats_ref[((b * nb + j) * N_STATS + r) * N_HEADS + h]

    skipped = jnp.int32(0)
    leading = jnp.bool_(True)
    for j in range(nb - 1):
        zero = j < qi
        for h in range(N_HEADS):
            bound = (stat(qi, STAT_QNORM, h) * stat(j, STAT_KNORM, h)
                     + stat(qi, STAT_CMAX, h) - stat(j, STAT_CMIN, h))
            zero = jnp.logical_and(zero, bound - stat(qi, STAT_DIAG, h) < -EXP2_UNDERFLOW)
        leading = jnp.logical_and(leading, zero)
        skipped = skipped + leading.astype(jnp.int32)
    return skipped


def _attn_body(*refs, banded, online):
    refs = list(refs)
    stats_ref = refs.pop(0) if banded else None
    qt_ref, k_ref, vt_ref, o_ref, acc_ref, s_ref = refs[:6]
    m_ref = refs[6] if online else None
    blk = qt_ref.shape[1]
    nb = vt_ref.shape[2]
    qi = pl.program_id(1)
    first = _first_needed_block(stats_ref, pl.program_id(0), qi, nb) if banded else 0
    acc_ref[...] = jnp.zeros_like(acc_ref)
    if online:
        m_ref[...] = jnp.full(m_ref.shape, -jnp.inf, F32)

    def logits(kj, h):
        start = pl.multiple_of(kj * blk, blk)
        g0, g1 = h * HEAD_PAD, (h + 1) * HEAD_PAD
        return jnp.dot(k_ref[pl.ds(start, blk), g0:g1], qt_ref[g0:g1, :],
                       preferred_element_type=F32)

    def accumulate(s, kj, h):
        acc = acc_ref[h]
        if online:
            m_old = m_ref[h]
            m_new = jnp.maximum(m_old, jnp.max(s, axis=0, keepdims=True))
            m_ref[h] = m_new
            s = s - m_new
            acc = acc * jnp.exp2(m_old - m_new)
        pv = jnp.dot(vt_ref[0, h, kj, 0:PV_ROWS, :], jnp.exp2(s).astype(BF16), preferred_element_type=F32)
        acc_ref[h] = acc + pv[0:HEAD_DIM + AUG_ROWS]

    s_ref[...] = logits(first, 0)

    def full_blocks(kj, count):
        s = s_ref[...]
        for d in range(count):
            for h in range(N_HEADS):
                nxt = (kj + d, h + 1) if h + 1 < N_HEADS else (kj + d + 1, 0)
                s_next = logits(*nxt)
                accumulate(s, kj + d, h)
                s = s_next
        s_ref[...] = s

    n_full = qi - first
    odd = jnp.bitwise_and(n_full, 1)
    pl.when(odd == 1)(lambda: full_blocks(first, 1))

    def pair_step(i, carry):
        full_blocks(first + odd + 2 * i, 2)
        return carry

    lax.fori_loop(0, lax.shift_right_logical(n_full, 1), pair_step, 0)

    if online:
        key_pos = lax.broadcasted_iota(jnp.int32, (blk, blk), 0)
        qry_pos = lax.broadcasted_iota(jnp.int32, (blk, blk), 1)
        visible = key_pos <= qry_pos
        s = s_ref[...]
        for h in range(N_HEADS):
            s_next = logits(qi, h + 1) if h + 1 < N_HEADS else None
            accumulate(jnp.where(visible, s, -jnp.inf), qi, h)
            s = s_next
    else:
        half = blk // 2
        start = pl.multiple_of(qi * blk, blk)
        key_pos = lax.broadcasted_iota(jnp.int32, (half, blk), 0)
        qry_pos = lax.broadcasted_iota(jnp.int32, (half, blk), 1)
        visible_a = key_pos <= qry_pos
        visible_b = visible_a[:, 0:half]

        def quadrant_logits(h):
            g0, g1 = h * HEAD_PAD, (h + 1) * HEAD_PAD
            s_a = jnp.dot(k_ref[pl.ds(start, half), g0:g1], qt_ref[g0:g1, :], preferred_element_type=F32)
            s_b = jnp.dot(k_ref[pl.ds(start + half, half), g0:g1], qt_ref[g0:g1, half:blk],
                          preferred_element_type=F32)
            return s_a, s_b

        s_full = s_ref[...]
        s_a, s_b = s_full[0:half], s_full[half:blk, half:blk]
        for h in range(N_HEADS):
            s_next = quadrant_logits(h + 1) if h + 1 < N_HEADS else None
            p_a = jnp.exp2(jnp.where(visible_a, s_a, -jnp.inf)).astype(BF16)
            p_b = jnp.exp2(jnp.where(visible_b, s_b, -jnp.inf)).astype(BF16)
            pv_a = jnp.dot(vt_ref[0, h, qi, 0:PV_ROWS, 0:half], p_a, preferred_element_type=F32)
            pv_b = jnp.dot(vt_ref[0, h, qi, 0:PV_ROWS, half:blk], p_b, preferred_element_type=F32)
            acc_ref[h] = acc_ref[h] + pv_a[0:HEAD_DIM + AUG_ROWS]
            acc_ref[h, :, half:blk] = acc_ref[h, :, half:blk] + pv_b[0:HEAD_DIM + AUG_ROWS]
            if s_next is not None:
                s_a, s_b = s_next
    for h in range(N_HEADS):
        acc = acc_ref[h]
        o_ref[h * HEAD_DIM:(h + 1) * HEAD_DIM, :] = acc[0:HEAD_DIM] / acc[HEAD_DIM:HEAD_DIM + 1]


def _attention_call(qt, k, vt, skip_stats, online):
    batch, _, nb, _, blk = vt.shape
    t = qt.shape[1]
    seq = t // batch
    banded = skip_stats is not None
    in_specs = [
        pl.BlockSpec((N_HEADS * HEAD_PAD, blk), lambda b, i: (0, b * nb + i)),
        pl.BlockSpec((seq, N_HEADS * HEAD_PAD), lambda b, i: (b, 0)),
        pl.BlockSpec((1, N_HEADS, nb, HEAD_PAD, blk), lambda b, i: (b, 0, 0, 0, 0)),
    ]
    args = (qt, k, vt)
    if banded:
        in_specs = [pl.BlockSpec(memory_space=pltpu.SMEM)] + in_specs
        args = (skip_stats,) + args
    scratch = [pltpu.VMEM((N_HEADS, HEAD_DIM + AUG_ROWS, blk), F32), pltpu.VMEM((blk, blk), F32)]
    if online:
        scratch.append(pltpu.VMEM((N_HEADS, 1, blk), F32))
    return pl.pallas_call(
        functools.partial(_attn_body, banded=banded, online=online),
        grid=(batch, nb),
        in_specs=in_specs,
        out_specs=pl.BlockSpec((GROUP_WIDTH, blk), lambda b, i: (0, b * nb + i)),
        out_shape=jax.ShapeDtypeStruct((GROUP_WIDTH, t), F32),
        scratch_shapes=scratch,
        compiler_params=_params("parallel", "arbitrary"),
        name="attention_online" if online else "attention",
    )(*args)


def _attention(qt, k, vt, stats, banded):
    stats = stats[:, :, :N_STATS, :N_HEADS]
    skip_stats = stats.reshape(-1) if banded else None
    bound_is_tight = jnp.max(stats[:, :, STAT_GAP, :]) <= GAP_LIMIT
    return lax.cond(bound_is_tight,
                    lambda: _attention_call(qt, k, vt, skip_stats, online=False),
                    lambda: _attention_call(qt, k, vt, skip_stats, online=True))


def _load_weights_bf16(copies, stage_refs, sem):
    def dma(i):
        src, _, kind = copies[i]
        return pltpu.make_async_copy(src, stage_refs[kind].at[i % 2], sem.at[i % 2])

    dma(0).start()
    for i, (_, dst, kind) in enumerate(copies):
        if i + 1 < len(copies):
            dma(i + 1).start()
        dma(i).wait()
        dst[...] = stage_refs[kind][i % 2].astype(BF16)


def _post_body(x_ref, ya_ref, yb_ref, yc_ref, yd_ref, gg_ref, wo_hbm, gf_ref, wu_hbm, wd_hbm, gl_ref,
               o_ref, wo_ref, wu_ref, wd_ref, stage_wide, stage_tall, sem, *, layer, final):
    @pl.when(pl.program_id(0) == 0)
    def _():
        copies = []
        rows = stage_wide.shape[1]
        for r in range(0, D_MODEL, rows):
            copies.append((wu_hbm.at[layer, pl.ds(r, rows), :], wu_ref.at[pl.ds(r, rows), :], 0))
        rows = stage_tall.shape[1]
        for r in range(0, D_FF, rows):
            copies.append((wd_hbm.at[layer, pl.ds(r, rows), :], wd_ref.at[pl.ds(r, rows), :], 1))
        for r in range(0, D_MODEL, rows):
            copies.append((wo_hbm.at[layer, pl.ds(r, rows), :], wo_ref.at[pl.ds(r, rows), :], 1))
        _load_weights_bf16(copies, (stage_wide, stage_tall), sem)

    ys = []
    for g, y_ref in enumerate((ya_ref, yb_ref, yc_ref, yd_ref)):
        ys.append((_rms_rows(y_ref[...]) * gg_ref[g]).astype(BF16))
    y = jnp.concatenate(ys, axis=0)
    x = x_ref[...] + lax.dot_general(y, wo_ref[...], TN, preferred_element_type=F32)
    h = (x * lax.rsqrt(jnp.mean(x * x, axis=-1, keepdims=True) + EPS) * gf_ref[...]).astype(BF16)
    acc = x
    for c in range(D_FF // FF_CHUNK):
        c0, c1 = c * FF_CHUNK, (c + 1) * FF_CHUNK
        a = jnp.maximum(jnp.dot(h, wu_ref[:, c0:c1], preferred_element_type=F32), 0.0)
        acc = acc + jnp.dot((a * a).astype(BF16), wd_ref[c0:c1, :], preferred_element_type=F32)
    if final:
        acc = acc * lax.rsqrt(jnp.mean(acc * acc, axis=-1, keepdims=True) + EPS) * gl_ref[...]
    o_ref[...] = acc


def _post(x, ya, yb, yc, yd, gg_col, w_out, g_ffn, w_up, w_down, g_final, layer, final):
    t = x.shape[0]
    tm = TOKEN_TILE
    const = lambda shape: pl.BlockSpec(shape, lambda i: (0,) * len(shape), pipeline_mode=pl.Buffered(1))
    in_hbm = pl.BlockSpec(memory_space=pl.ANY)
    ytile = pl.BlockSpec((GROUP_WIDTH, tm), lambda i: (0, i))
    return pl.pallas_call(
        functools.partial(_post_body, layer=layer, final=final),
        grid=(t // tm,),
        in_specs=[
            pl.BlockSpec((tm, D_MODEL), lambda i: (i, 0)),
            ytile, ytile, ytile, ytile,
            const((4, GROUP_WIDTH, 1)),
            in_hbm,
            const((1, D_MODEL)),
            in_hbm,
            in_hbm,
            const((1, D_MODEL)),
        ],
        out_specs=pl.BlockSpec((tm, D_MODEL), lambda i: (i, 0)),
        out_shape=jax.ShapeDtypeStruct((t, D_MODEL), F32),
        scratch_shapes=[
            pltpu.VMEM((D_MODEL, D_MODEL), BF16),
            pltpu.VMEM((D_MODEL, D_FF), BF16),
            pltpu.VMEM((D_FF, D_MODEL), BF16),
            pltpu.VMEM((2, WEIGHT_STAGE_ELEMS // D_FF, D_FF), F32),
            pltpu.VMEM((2, WEIGHT_STAGE_ELEMS // D_MODEL, D_MODEL), F32),
            pltpu.SemaphoreType.DMA((2,)),
        ],
        compiler_params=_params("arbitrary"),
        name="post",
    )(x, ya, yb, yc, yd, gg_col, w_out, g_ffn, w_up, w_down, g_final)


def _rope_tables(seq, half):
    inv_freq = np.power(ROPE_BASE, -np.arange(half, dtype=np.float64) / half)
    ang = inv_freq[:, None] * np.arange(seq, dtype=np.float64)[None, :]
    return jnp.asarray(np.cos(ang), F32), jnp.asarray(np.sin(ang), F32)


def _inproj_weights(w_in):
    wt = jnp.swapaxes(w_in, 1, 2)
    w_abc = wt[:, :ROWS_ABC].astype(BF16)
    f = wt[:, ROWS_ABC:ROWS_ABC + N_HEADS]
    d = wt[:, ROWS_ABC + N_HEADS:ROWS_ABC + N_HEADS + ROWS_D]
    pad = jnp.zeros((w_in.shape[0], ROWS_F - N_HEADS, D_MODEL), w_in.dtype)
    return w_abc, jnp.concatenate([d, f, pad], axis=1).astype(BF16)


def _mla_weights(w_uq, w_ukv):
    wq = w_uq.T.reshape(N_HEADS, MLA_NOPE + MLA_ROPE, MLA_Q_LORA)
    wq = jnp.pad(wq, ((0, 0), (0, HEAD_PAD - MLA_NOPE - MLA_ROPE), (0, 0)))
    wq = wq.reshape(N_HEADS * HEAD_PAD, MLA_Q_LORA).astype(BF16)
    wkv = w_ukv.T.reshape(N_HEADS, 2 * HEAD_DIM, MLA_KV_LORA)
    wk = wkv[:, :HEAD_DIM].reshape(GROUP_WIDTH, MLA_KV_LORA).astype(BF16)
    wv = wkv[:, HEAD_DIM:].reshape(GROUP_WIDTH, MLA_KV_LORA).astype(BF16)
    return wq, wk, wv


def kernel(x, g_mix_norm, w_in, b_forget, g_sgu, w_spatial, b_spatial, g_mla_q, w_uq, g_mla_kv, w_ukv,
           g_group_out, w_out, g_ffn_norm, w_up, w_down, g_final):
    batch, seq, _ = x.shape
    depth = w_in.shape[0]
    assert seq % TOKEN_TILE == 0 and ATTN_BLOCK == TOKEN_TILE
    cos_b, sin_b = _rope_tables(seq, HEAD_DIM // 2)
    cos_d, sin_d = _rope_tables(seq, MLA_ROPE // 2)
    w_abc, w_df = _inproj_weights(w_in)
    xf = x.reshape(batch * seq, D_MODEL)
    for l in range(depth):
        bf_col = jnp.pad(b_forget[l], (0, ROWS_F - N_HEADS))[:, None]
        ya, yb, qt_c, k_c, vt_c, stats_c, qt_d, k_d, vt_d, stats_d = _front(
            xf, g_mix_norm[l][None, :], w_abc, w_df,
            g_sgu[l][:, None], w_spatial[l], b_spatial[l], cos_b, sin_b, bf_col,
            cos_d, sin_d, g_mla_q[l][:, None], g_mla_kv[l][:, None], *_mla_weights(w_uq[l], w_ukv[l]),
            batch, l)
        yc = _attention(qt_c, k_c, vt_c, stats_c, banded=True)
        yd = _attention(qt_d, k_d, vt_d, stats_d, banded=False)
        xf = _post(xf, ya, yb, yc, yd, g_group_out[l].reshape(4, GROUP_WIDTH, 1),
                   w_out, g_ffn_norm[l][None, :], w_up, w_down, g_final[None, :],
                   layer=l, final=(l == depth - 1))
    return xf.reshape(batch, seq, D_MODEL)
```

```python
import functools
import math

import jax
import jax.numpy as jnp
import numpy as np
from jax import lax
from jax.experimental import pallas as pl
from jax.experimental.pallas import tpu as pltpu

F32 = jnp.float32
BF16 = jnp.bfloat16

D_MODEL = 1024
N_HEADS = 4
HEAD_DIM = 64
GROUP_WIDTH = N_HEADS * HEAD_DIM
CHUNK = 128
MLA_Q_LORA = 256
MLA_KV_LORA = 128
MLA_NOPE = 64
MLA_ROPE = 32
ROPE_BASE = 10000.0
D_FF = 4 * D_MODEL
EPS = 1e-6

HEAD_PAD = 128
AUG_ROWS = 8
PV_ROWS = 80
LOG2E = math.log2(math.e)
EXP2_UNDERFLOW = 152.0
STAT_QNORM, STAT_KNORM, STAT_CMAX, STAT_CMIN, STAT_DIAG, STAT_GAP = range(6)
N_STATS = 6
GAP_LIMIT = 64.0
TOKEN_TILE = 512
ATTN_BLOCK = TOKEN_TILE
FF_CHUNK = 512
WEIGHT_STAGE_ELEMS = 512 * 1024
VMEM_LIMIT = 56 * 1024 * 1024

ROWS_A = 2 * GROUP_WIDTH
ROWS_B = 4 * GROUP_WIDTH
ROWS_C = 3 * GROUP_WIDTH
ROWS_D = MLA_Q_LORA + MLA_KV_LORA + MLA_ROPE
ROWS_F = 8
ROWS_ABC = ROWS_A + ROWS_B + ROWS_C

NT = (((1,), (1,)), ((), ()))
TN = (((0,), (0,)), ((), ()))


def _params(*sem):
    return pltpu.CompilerParams(dimension_semantics=sem, vmem_limit_bytes=VMEM_LIMIT)


def _rot_half_rows(t, cos, sin):
    half = t.shape[0] // 2
    t1, t2 = t[:half], t[half:]
    return jnp.concatenate([t1 * cos - t2 * sin, t1 * sin + t2 * cos], axis=0)


def _standardize_rows(t):
    mu = jnp.mean(t, axis=0, keepdims=True)
    var = jnp.mean(jnp.square(t - mu), axis=0, keepdims=True)
    return (t - mu) * lax.rsqrt(var + EPS)


def _rms_rows(t):
    return t * lax.rsqrt(jnp.mean(t * t, axis=0, keepdims=True) + EPS)


def _sgu(at_ref, gain_ref, ws_ref, bs_ref, o_ref):
    tm = at_ref.shape[1]
    nch = tm // CHUNK
    row = lax.broadcasted_iota(jnp.int32, (CHUNK, CHUNK), 0)
    col = lax.broadcasted_iota(jnp.int32, (CHUNK, CHUNK), 1)
    for h in range(N_HEADS):
        r0, r1 = h * HEAD_DIM, (h + 1) * HEAD_DIM
        u = jax.nn.gelu(at_ref[r0:r1, :].astype(F32))
        v = jax.nn.gelu(at_ref[GROUP_WIDTH + r0:GROUP_WIDTH + r1, :].astype(F32))
        v = _standardize_rows(v) * gain_ref[r0:r1, :]
        w = jnp.where(col <= row, ws_ref[h], 0.0).astype(BF16)
        vs = jnp.concatenate([v[:, c * CHUNK:(c + 1) * CHUNK] for c in range(nch)], axis=0).astype(BF16)
        mixed = lax.dot_general(vs, w, NT, preferred_element_type=F32) + bs_ref[h:h + 1, :]
        for c in range(nch):
            o_ref[r0:r1, c * CHUNK:(c + 1) * CHUNK] = (
                u[:, c * CHUNK:(c + 1) * CHUNK] * mixed[c * HEAD_DIM:(c + 1) * HEAD_DIM])


def _retention(bt_ref, cos_ref, sin_ref, o_ref, st_ref):
    tm = bt_ref.shape[1]
    nch = tm // CHUNK
    cos, sin = cos_ref[...], sin_ref[...]
    srow = lax.broadcasted_iota(jnp.int32, (CHUNK, CHUNK), 0)
    tcol = lax.broadcasted_iota(jnp.int32, (CHUNK, CHUNK), 1)
    rel = (tcol - srow).astype(F32)
    j = lax.broadcasted_iota(jnp.int32, (1, CHUNK), 1).astype(F32)
    for h in range(N_HEADS):
        log_gamma = math.log1p(-(2.0 ** (-5.0 - h)))
        dec_t = jnp.where(rel >= 0, jnp.exp(jnp.maximum(rel, 0.0) * log_gamma), 0.0)
        query_w = jnp.exp((j + 1.0) * log_gamma)
        key_w = jnp.exp((CHUNK - 1.0 - j) * log_gamma)
        chunk_decay = math.exp(CHUNK * log_gamma)
        r0, r1 = h * HEAD_DIM, (h + 1) * HEAD_DIM
        q = _rot_half_rows(bt_ref[r0:r1, :].astype(F32), cos, sin)
        k = _rot_half_rows(bt_ref[GROUP_WIDTH + r0:GROUP_WIDTH + r1, :].astype(F32), cos, sin) * (HEAD_DIM ** -0.5)
        v = bt_ref[2 * GROUP_WIDTH + r0:2 * GROUP_WIDTH + r1, :].astype(F32)
        g = bt_ref[3 * GROUP_WIDTH + r0:3 * GROUP_WIDTH + r1, :].astype(F32)
        st = st_ref[h]
        ys = []
        for c in range(nch):
            sl = slice(c * CHUNK, (c + 1) * CHUNK)
            qc, kc, vc = q[:, sl], k[:, sl], v[:, sl]
            kcb = kc.astype(BF16)
            a_t = lax.dot_general(kcb, qc.astype(BF16), TN, preferred_element_type=F32)
            p_t = (a_t * dec_t).astype(BF16)
            intra = jnp.dot(vc.astype(BF16), p_t, preferred_element_type=F32)
            cross = jnp.dot(st.astype(BF16), (qc * query_w).astype(BF16), preferred_element_type=F32)
            ys.append(intra + cross)
            st = chunk_decay * st + lax.dot_general((vc * key_w).astype(BF16), kcb, NT,
                                                    preferred_element_type=F32)
        st_ref[h] = st
        y = _standardize_rows(jnp.concatenate(ys, axis=1))
        o_ref[r0:r1, :] = jax.nn.silu(g) * y


def _select_rows(rows):
    n = rows[0].shape[1]
    ridx = lax.broadcasted_iota(jnp.int32, (AUG_ROWS, n), 0)
    out = jnp.zeros((AUG_ROWS, n), F32)
    for i, r in enumerate(rows):
        out = jnp.where(ridx == i, jnp.broadcast_to(r, (AUG_ROWS, n)), out)
    return out


def _group(feat, extra):
    n = feat.shape[1]
    pad = HEAD_PAD - feat.shape[0] - extra.shape[0]
    return jnp.concatenate([feat, extra, jnp.zeros((pad, n), F32)], axis=0)


def _split3(x):
    hi = x.astype(BF16).astype(F32)
    mid = (x - hi).astype(BF16).astype(F32)
    lo = (x - hi - mid).astype(BF16).astype(F32)
    return hi, mid, lo


def _logit_bound(qb, kb, kmax_ref, row):
    qnorm = jnp.sqrt(jnp.sum(qb * qb, axis=0, keepdims=True))
    knorm = jnp.sqrt(jnp.max(jnp.sum(kb * kb, axis=0, keepdims=True), axis=1, keepdims=True))
    kmax = jnp.maximum(kmax_ref[row:row + 1, :], knorm)
    kmax_ref[row:row + 1, :] = kmax
    return qnorm * kmax, jnp.sum(qb * kb, axis=0, keepdims=True), qnorm, knorm


def _write_stats(st_ref, per_head):
    stat_row = lax.broadcasted_iota(jnp.int32, (8, 128), 0)
    stat_lane = lax.broadcasted_iota(jnp.int32, (8, 128), 1)
    stats = jnp.zeros((8, 128), F32)
    for h, vals in enumerate(per_head):
        for r, val in enumerate(vals):
            stats = jnp.where((stat_row == r) & (stat_lane == h), val, stats)
    st_ref[0, 0] = stats


def _fox_prep(ct_ref, ft_ref, bf_ref, qt_ref, k_ref, vt_ref, st_ref, carry_ref, kmax_ref):
    tm = ct_ref.shape[1]
    x = ft_ref[...] + bf_ref[...]
    lf = jnp.minimum(x, 0.0) - jnp.log1p(jnp.exp(-jnp.abs(x)))
    hi = lf.astype(BF16)
    mid = (lf - hi.astype(F32)).astype(BF16)
    lo = (lf - hi.astype(F32) - mid.astype(F32)).astype(BF16)
    srow = lax.broadcasted_iota(jnp.int32, (tm, tm), 0)
    tcol = lax.broadcasted_iota(jnp.int32, (tm, tm), 1)
    upper = jnp.where(srow <= tcol, 1.0, 0.0).astype(BF16)
    parts = jnp.dot(jnp.concatenate([hi, mid, lo], axis=0), upper, preferred_element_type=F32)
    cum = parts[0:8] + parts[8:16] + parts[16:24] + carry_ref[...]
    carry_ref[...] = cum[:, tm - 1:tm]

    cum2 = cum * LOG2E
    one = jnp.ones((1, tm), F32)
    ones_row = _select_rows([one])
    stats = []
    for h in range(N_HEADS):
        r0, r1 = h * HEAD_DIM, (h + 1) * HEAD_DIM
        g0, g1 = h * HEAD_PAD, (h + 1) * HEAD_PAD
        q = ct_ref[r0:r1, :].astype(F32) * (HEAD_DIM ** -0.5 * LOG2E)
        k = ct_ref[GROUP_WIDTH + r0:GROUP_WIDTH + r1, :].astype(F32)
        v = ct_ref[2 * GROUP_WIDTH + r0:2 * GROUP_WIDTH + r1, :].astype(F32)
        qb, kb = q.astype(BF16).astype(F32), k.astype(BF16).astype(F32)
        bound, diag, qnorm, knorm = _logit_bound(qb, kb, kmax_ref, h)
        c2 = cum2[h:h + 1]
        q_extra = _select_rows([*_split3(c2 - bound), one, one, one])
        k_extra = _select_rows([one, one, one, *(-part for part in _split3(c2))])
        qt_ref[0, g0:g1, :] = _group(q, q_extra).astype(BF16)
        k_ref[:, g0:g1] = _group(k, k_extra).T.astype(BF16)
        vt_ref[0, h, 0] = _group(v, ones_row).astype(BF16)
        stats.append((
            jnp.max(qnorm, axis=1, keepdims=True),
            knorm,
            jnp.max(c2, axis=1, keepdims=True),
            jnp.min(c2, axis=1, keepdims=True),
            jnp.min(diag, axis=1, keepdims=True),
            jnp.max(bound - diag, axis=1, keepdims=True),
        ))
    _write_stats(st_ref, stats)


def _mla_prep(dt_ref, cos_ref, sin_ref, gq_ref, gkv_ref, wq_ref, wk_ref, wv_ref, qt_ref, k_ref, vt_ref,
              st_ref, kmax_ref):
    tm = dt_ref.shape[1]
    cos, sin = cos_ref[...], sin_ref[...]
    cq = _rms_rows(dt_ref[0:MLA_Q_LORA, :].astype(F32)) * gq_ref[...]
    ckv = _rms_rows(dt_ref[MLA_Q_LORA:MLA_Q_LORA + MLA_KV_LORA, :].astype(F32)) * gkv_ref[...]
    kr = _rot_half_rows(dt_ref[MLA_Q_LORA + MLA_KV_LORA:ROWS_D, :].astype(F32), cos, sin)
    ckv_b = ckv.astype(BF16)
    q_all = jnp.dot(wq_ref[...], cq.astype(BF16), preferred_element_type=F32)
    q_all = q_all * ((MLA_NOPE + MLA_ROPE) ** -0.5 * LOG2E)
    k_all = jnp.dot(wk_ref[...], ckv_b, preferred_element_type=F32)
    v_all = jnp.dot(wv_ref[...], ckv_b, preferred_element_type=F32)
    one = jnp.ones((1, tm), F32)
    ones_row = _select_rows([one])
    k_extra = _select_rows([one, one, one])
    zero = jnp.zeros((1, 1), F32)
    stats = []
    for h in range(N_HEADS):
        r0, r1 = h * HEAD_DIM, (h + 1) * HEAD_DIM
        g0, g1 = h * HEAD_PAD, (h + 1) * HEAD_PAD
        qg = q_all[g0:g1]
        q_rope = _rot_half_rows(qg[MLA_NOPE:MLA_NOPE + MLA_ROPE], cos, sin)
        q = jnp.concatenate([qg[0:MLA_NOPE], q_rope], axis=0)
        k = jnp.concatenate([k_all[r0:r1], kr], axis=0)
        qb, kb = q.astype(BF16).astype(F32), k.astype(BF16).astype(F32)
        bound, diag, _, _ = _logit_bound(qb, kb, kmax_ref, N_HEADS + h)
        q_extra = _select_rows([*_split3(-bound)])
        qt_ref[0, g0:g1, :] = _group(q, q_extra).astype(BF16)
        k_ref[:, g0:g1] = _group(k, k_extra).T.astype(BF16)
        vt_ref[0, h, 0] = _group(v_all[r0:r1], ones_row).astype(BF16)
        stats.append((zero,) * STAT_GAP + (jnp.max(bound - diag, axis=1, keepdims=True),))
    _write_stats(st_ref, stats)


def _front_body(x_ref, g_ref, w_ref, wdf_ref,
                gain_ref, ws_ref, bs_ref,
                cosb_ref, sinb_ref,
                bf_ref,
                cosd_ref, sind_ref, gq_ref, gkv_ref, wq_ref, wk_ref, wv_ref,
                ya_ref, yb_ref,
                qtc_ref, kc_ref, vtc_ref, stats_ref,
                qtd_ref, kd_ref, vtd_ref, statsd_ref,
                at_ref, bt_ref, ct_ref, dt_ref, ft_ref, state_ref, carry_ref, kmax_ref, *, tiles_per_seq):
    @pl.when(pl.program_id(0) % tiles_per_seq == 0)
    def _():
        state_ref[...] = jnp.zeros_like(state_ref)
        carry_ref[...] = jnp.zeros_like(carry_ref)
        kmax_ref[...] = jnp.zeros_like(kmax_ref)

    x = x_ref[...]
    h = (x * lax.rsqrt(jnp.mean(x * x, axis=-1, keepdims=True) + EPS) * g_ref[...]).astype(BF16)

    def proj(w):
        return lax.dot_general(w, h, NT, preferred_element_type=F32)

    half_b = ROWS_B // 2
    zdf = proj(wdf_ref[...])
    dt_ref[...] = zdf[0:ROWS_D].astype(BF16)
    ft_ref[...] = zdf[ROWS_D:ROWS_D + ROWS_F]
    at_ref[...] = proj(w_ref[0:ROWS_A, :]).astype(BF16)
    _mla_prep(dt_ref, cosd_ref, sind_ref, gq_ref, gkv_ref, wq_ref, wk_ref, wv_ref, qtd_ref, kd_ref, vtd_ref,
              statsd_ref, kmax_ref)
    ct_ref[...] = proj(w_ref[ROWS_A + ROWS_B:ROWS_ABC, :]).astype(BF16)
    _sgu(at_ref, gain_ref, ws_ref, bs_ref, ya_ref)
    bt_ref[0:half_b, :] = proj(w_ref[ROWS_A:ROWS_A + half_b, :]).astype(BF16)
    _fox_prep(ct_ref, ft_ref, bf_ref, qtc_ref, kc_ref, vtc_ref, stats_ref, carry_ref, kmax_ref)
    bt_ref[half_b:ROWS_B, :] = proj(w_ref[ROWS_A + half_b:ROWS_A + ROWS_B, :]).astype(BF16)
    _retention(bt_ref, cosb_ref, sinb_ref, yb_ref, state_ref)


def _front(x, g, w_t, wdf_t, gain_col, w_s, b_s, cos_b, sin_b, bf_col,
           cos_d, sin_d, gq_col, gkv_col, wq_t, wk_t, wv_t, batch, layer):
    t = x.shape[0]
    tm = TOKEN_TILE
    ns = t // batch // tm
    const = lambda shape: pl.BlockSpec(shape, lambda i: (0,) * len(shape), pipeline_mode=pl.Buffered(1))
    of_layer = lambda shape: pl.BlockSpec((None,) + shape, lambda i: (layer,) + (0,) * len(shape),
                                          pipeline_mode=pl.Buffered(1))
    rows_t = lambda rows: pl.BlockSpec((rows, tm), lambda i: (0, i))
    table = lambda rows: pl.BlockSpec((rows, tm), lambda i: (0, i % ns))
    k_tile = pl.BlockSpec((tm, N_HEADS * HEAD_PAD), lambda i: (i, 0))
    vt_tile = pl.BlockSpec((1, N_HEADS, 1, HEAD_PAD, tm), lambda i: (i // ns, 0, i % ns, 0, 0))
    stats_tile = pl.BlockSpec((1, 1, 8, 128), lambda i: (i // ns, i % ns, 0, 0))
    qt_tile = pl.BlockSpec((1, N_HEADS * HEAD_PAD, tm), lambda i: (i, 0, 0))
    qkv_shapes = [
        jax.ShapeDtypeStruct((t // tm, N_HEADS * HEAD_PAD, tm), BF16),
        jax.ShapeDtypeStruct((t, N_HEADS * HEAD_PAD), BF16),
        jax.ShapeDtypeStruct((batch, N_HEADS, ns, HEAD_PAD, tm), BF16),
        jax.ShapeDtypeStruct((batch, ns, 8, 128), F32),
    ]
    return pl.pallas_call(
        functools.partial(_front_body, tiles_per_seq=ns),
        grid=(t // tm,),
        in_specs=[
            pl.BlockSpec((tm, D_MODEL), lambda i: (i, 0)),
            const((1, D_MODEL)),
            of_layer((ROWS_ABC, D_MODEL)),
            of_layer((ROWS_D + ROWS_F, D_MODEL)),
            const((GROUP_WIDTH, 1)),
            const((N_HEADS, CHUNK, CHUNK)),
            const((N_HEADS, CHUNK)),
            table(HEAD_DIM // 2), table(HEAD_DIM // 2),
            const((ROWS_F, 1)),
            table(MLA_ROPE // 2), table(MLA_ROPE // 2),
            const((MLA_Q_LORA, 1)),
            const((MLA_KV_LORA, 1)),
            const((N_HEADS * HEAD_PAD, MLA_Q_LORA)),
            const((GROUP_WIDTH, MLA_KV_LORA)),
            const((GROUP_WIDTH, MLA_KV_LORA)),
        ],
        out_specs=[
            rows_t(GROUP_WIDTH), rows_t(GROUP_WIDTH),
            qt_tile, k_tile, vt_tile, stats_tile,
            qt_tile, k_tile, vt_tile, stats_tile,
        ],
        out_shape=[
            jax.ShapeDtypeStruct((GROUP_WIDTH, t), F32),
            jax.ShapeDtypeStruct((GROUP_WIDTH, t), F32),
            *qkv_shapes,
            *qkv_shapes,
        ],
        scratch_shapes=[
            pltpu.VMEM((ROWS_A, tm), BF16),
            pltpu.VMEM((ROWS_B, tm), BF16),
            pltpu.VMEM((ROWS_C, tm), BF16),
            pltpu.VMEM((ROWS_D, tm), BF16),
            pltpu.VMEM((ROWS_F, tm), F32),
            pltpu.VMEM((N_HEADS, HEAD_DIM, HEAD_DIM), F32),
            pltpu.VMEM((ROWS_F, 1), F32),
            pltpu.VMEM((2 * N_HEADS, 1), F32),
        ],
        compiler_params=_params("arbitrary"),
        name="front",
    )(x, g, w_t, wdf_t, gain_col, w_s, b_s, cos_b, sin_b, bf_col,
      cos_d, sin_d, gq_col, gkv_col, wq_t, wk_t, wv_t)


def _first_needed_block(stats_ref, b, qi, nb):
    def stat(j, r, h):
        return stats_ref[((b * nb + j) * N_STATS + r) * N_HEADS + h]

    skipped = jnp.int32(0)
    leading = jnp.bool_(True)
    for j in range(nb - 1):
        zero = j < qi
        for h in range(N_HEADS):
            bound = (stat(qi, STAT_QNORM, h) * stat(j, STAT_KNORM, h)
                     + stat(qi, STAT_CMAX, h) - stat(j, STAT_CMIN, h))
            zero = jnp.logical_and(zero, bound - stat(qi, STAT_DIAG, h) < -EXP2_UNDERFLOW)
        leading = jnp.logical_and(leading, zero)
        skipped = skipped + leading.astype(jnp.int32)
    return skipped


def _attn_body(*refs, banded, online):
    refs = list(refs)
    stats_ref = refs.pop(0) if banded else None
    qt_ref, k_ref, vt_ref, o_ref, acc_ref, s_ref = refs[:6]
    m_ref = refs[6] if online else None
    nb, _, blk = qt_ref.shape
    half = blk // 2
    batch = pl.program_id(0)

    def first_block(qi):
        return _first_needed_block(stats_ref, batch, qi, nb) if banded else jnp.int32(0)

    def logits(qi, kj, h):
        start = pl.multiple_of(kj * blk, blk)
        g0, g1 = h * HEAD_PAD, (h + 1) * HEAD_PAD
        return jnp.dot(k_ref[pl.ds(start, blk), g0:g1], qt_ref[qi, g0:g1, :],
                       preferred_element_type=F32)

    def accumulate(s, kj, h):
        acc = acc_ref[h]
        if online:
            m_old = m_ref[h]
            m_new = jnp.maximum(m_old, jnp.max(s, axis=0, keepdims=True))
            m_ref[h] = m_new
            s = s - m_new
            acc = acc * jnp.exp2(m_old - m_new)
        pv = jnp.dot(vt_ref[0, h, kj, 0:PV_ROWS, :], jnp.exp2(s).astype(BF16), preferred_element_type=F32)
        acc_ref[h] = acc + pv[0:HEAD_DIM + AUG_ROWS]

    s_ref[...] = logits(0, 0, 0)

    def query_block(qi, first):
        acc_ref[...] = jnp.zeros_like(acc_ref)
        if online:
            m_ref[...] = jnp.full(m_ref.shape, -jnp.inf, F32)
        nxt_qi = jnp.minimum(qi + 1, nb - 1)
        nxt_first = first_block(nxt_qi)

        def full_blocks(kj, count):
            s = s_ref[...]
            for d in range(count):
                for h in range(N_HEADS):
                    nxt = (kj + d, h + 1) if h + 1 < N_HEADS else (kj + d + 1, 0)
                    s_next = logits(qi, *nxt)
                    accumulate(s, kj + d, h)
                    s = s_next
            s_ref[...] = s

        n_full = qi - first
        odd = jnp.bitwise_and(n_full, 1)
        pl.when(odd == 1)(lambda: full_blocks(first, 1))

        def pair_step(i, carry):
            full_blocks(first + odd + 2 * i, 2)
            return carry

        lax.fori_loop(0, lax.shift_right_logical(n_full, 1), pair_step, 0)

        if online:
            key_pos = lax.broadcasted_iota(jnp.int32, (blk, blk), 0)
            qry_pos = lax.broadcasted_iota(jnp.int32, (blk, blk), 1)
            visible = key_pos <= qry_pos
            s = s_ref[...]
            for h in range(N_HEADS):
                s_next = logits(qi, qi, h + 1) if h + 1 < N_HEADS else logits(nxt_qi, nxt_first, 0)
                accumulate(jnp.where(visible, s, -jnp.inf), qi, h)
                s = s_next
            s_ref[...] = s
        else:
            start = pl.multiple_of(qi * blk, blk)
            key_pos = lax.broadcasted_iota(jnp.int32, (half, blk), 0)
            qry_pos = lax.broadcasted_iota(jnp.int32, (half, blk), 1)
            visible_a = key_pos <= qry_pos
            visible_b = visible_a[:, 0:half]

            def quadrant_logits(h):
                g0, g1 = h * HEAD_PAD, (h + 1) * HEAD_PAD
                s_a = jnp.dot(k_ref[pl.ds(start, half), g0:g1], qt_ref[qi, g0:g1, :],
                              preferred_element_type=F32)
                s_b = jnp.dot(k_ref[pl.ds(start + half, half), g0:g1], qt_ref[qi, g0:g1, half:blk],
                              preferred_element_type=F32)
                return s_a, s_b

            s_full = s_ref[...]
            s_a, s_b = s_full[0:half], s_full[half:blk, half:blk]
            for h in range(N_HEADS):
                if h + 1 < N_HEADS:
                    s_next = quadrant_logits(h + 1)
                else:
                    s_ref[...] = logits(nxt_qi, nxt_first, 0)
                p_a = jnp.exp2(jnp.where(visible_a, s_a, -jnp.inf)).astype(BF16)
                p_b = jnp.exp2(jnp.where(visible_b, s_b, -jnp.inf)).astype(BF16)
                pv_a = jnp.dot(vt_ref[0, h, qi, 0:PV_ROWS, 0:half], p_a, preferred_element_type=F32)
                pv_b = jnp.dot(vt_ref[0, h, qi, 0:PV_ROWS, half:blk], p_b, preferred_element_type=F32)
                acc_ref[h] = acc_ref[h] + pv_a[0:HEAD_DIM + AUG_ROWS]
                acc_ref[h, :, half:blk] = acc_ref[h, :, half:blk] + pv_b[0:HEAD_DIM + AUG_ROWS]
                if h + 1 < N_HEADS:
                    s_a, s_b = s_next
        for h in range(N_HEADS):
            acc = acc_ref[h]
            o_ref[qi, h * HEAD_DIM:(h + 1) * HEAD_DIM, :] = acc[0:HEAD_DIM] / acc[HEAD_DIM:HEAD_DIM + 1]
        return nxt_first

    lax.fori_loop(0, nb, query_block, jnp.int32(0))


def _attention_call(qt, k, vt, skip_stats, online):
    batch, _, nb, _, blk = vt.shape
    seq = nb * blk
    banded = skip_stats is not None
    in_specs = [
        pl.BlockSpec((nb, N_HEADS * HEAD_PAD, blk), lambda b: (b, 0, 0)),
        pl.BlockSpec((seq, N_HEADS * HEAD_PAD), lambda b: (b, 0)),
        pl.BlockSpec((1, N_HEADS, nb, HEAD_PAD, blk), lambda b: (b, 0, 0, 0, 0)),
    ]
    args = (qt, k, vt)
    if banded:
        in_specs = [pl.BlockSpec(memory_space=pltpu.SMEM)] + in_specs
        args = (skip_stats,) + args
    scratch = [pltpu.VMEM((N_HEADS, HEAD_DIM + AUG_ROWS, blk), F32), pltpu.VMEM((blk, blk), F32)]
    if online:
        scratch.append(pltpu.VMEM((N_HEADS, 1, blk), F32))
    return pl.pallas_call(
        functools.partial(_attn_body, banded=banded, online=online),
        grid=(batch,),
        in_specs=in_specs,
        out_specs=pl.BlockSpec((nb, GROUP_WIDTH, blk), lambda b: (b, 0, 0)),
        out_shape=jax.ShapeDtypeStruct((batch * nb, GROUP_WIDTH, blk), F32),
        scratch_shapes=scratch,
        compiler_params=_params("parallel"),
        name="attention_online" if online else "attention",
    )(*args)


def _attention(qt, k, vt, stats, banded):
    stats = stats[:, :, :N_STATS, :N_HEADS]
    skip_stats = stats.reshape(-1) if banded else None
    bound_is_tight = jnp.max(stats[:, :, STAT_GAP, :]) <= GAP_LIMIT
    return lax.cond(bound_is_tight,
                    lambda: _attention_call(qt, k, vt, skip_stats, online=False),
                    lambda: _attention_call(qt, k, vt, skip_stats, online=True))


def _load_weights_bf16(copies, stage_refs, sem):
    def dma(i):
        src, _, kind = copies[i]
        return pltpu.make_async_copy(src, stage_refs[kind].at[i % 2], sem.at[i % 2])

    dma(0).start()
    for i, (_, dst, kind) in enumerate(copies):
        if i + 1 < len(copies):
            dma(i + 1).start()
        dma(i).wait()
        dst[...] = stage_refs[kind][i % 2].astype(BF16)


def _post_body(x_ref, ya_ref, yb_ref, yc_ref, yd_ref, gg_ref, wo_hbm, gf_ref, wu_hbm, wd_hbm, gl_ref,
               o_ref, wo_ref, wu_ref, wd_ref, stage_wide, stage_tall, sem, *, layer, final):
    @pl.when(pl.program_id(0) == 0)
    def _():
        copies = []
        rows = stage_wide.shape[1]
        for r in range(0, D_MODEL, rows):
            copies.append((wu_hbm.at[layer, pl.ds(r, rows), :], wu_ref.at[pl.ds(r, rows), :], 0))
        rows = stage_tall.shape[1]
        for r in range(0, D_FF, rows):
            copies.append((wd_hbm.at[layer, pl.ds(r, rows), :], wd_ref.at[pl.ds(r, rows), :], 1))
        for r in range(0, D_MODEL, rows):
            copies.append((wo_hbm.at[layer, pl.ds(r, rows), :], wo_ref.at[pl.ds(r, rows), :], 1))
        _load_weights_bf16(copies, (stage_wide, stage_tall), sem)

    ys = []
    for g, y_ref in enumerate((ya_ref, yb_ref, yc_ref, yd_ref)):
        ys.append((_rms_rows(y_ref[...]) * gg_ref[g]).astype(BF16))
    y = jnp.concatenate(ys, axis=0)
    x = x_ref[...] + lax.dot_general(y, wo_ref[...], TN, preferred_element_type=F32)
    h = (x * lax.rsqrt(jnp.mean(x * x, axis=-1, keepdims=True) + EPS) * gf_ref[...]).astype(BF16)
    acc = x
    for c in range(D_FF // FF_CHUNK):
        c0, c1 = c * FF_CHUNK, (c + 1) * FF_CHUNK
        a = jnp.maximum(jnp.dot(h, wu_ref[:, c0:c1], preferred_element_type=F32), 0.0)
        acc = acc + jnp.dot((a * a).astype(BF16), wd_ref[c0:c1, :], preferred_element_type=F32)
    if final:
        acc = acc * lax.rsqrt(jnp.mean(acc * acc, axis=-1, keepdims=True) + EPS) * gl_ref[...]
    o_ref[...] = acc


def _post(x, ya, yb, yc, yd, gg_col, w_out, g_ffn, w_up, w_down, g_final, layer, final):
    t = x.shape[0]
    tm = TOKEN_TILE
    const = lambda shape: pl.BlockSpec(shape, lambda i: (0,) * len(shape), pipeline_mode=pl.Buffered(1))
    in_hbm = pl.BlockSpec(memory_space=pl.ANY)
    ytile = pl.BlockSpec((GROUP_WIDTH, tm), lambda i: (0, i))
    yblock = pl.BlockSpec((None, GROUP_WIDTH, tm), lambda i: (i, 0, 0))
    return pl.pallas_call(
        functools.partial(_post_body, layer=layer, final=final),
        grid=(t // tm,),
        in_specs=[
            pl.BlockSpec((tm, D_MODEL), lambda i: (i, 0)),
            ytile, ytile, yblock, yblock,
            const((4, GROUP_WIDTH, 1)),
            in_hbm,
            const((1, D_MODEL)),
            in_hbm,
            in_hbm,
            const((1, D_MODEL)),
        ],
        out_specs=pl.BlockSpec((tm, D_MODEL), lambda i: (i, 0)),
        out_shape=jax.ShapeDtypeStruct((t, D_MODEL), F32),
        scratch_shapes=[
            pltpu.VMEM((D_MODEL, D_MODEL), BF16),
            pltpu.VMEM((D_MODEL, D_FF), BF16),
            pltpu.VMEM((D_FF, D_MODEL), BF16),
            pltpu.VMEM((2, WEIGHT_STAGE_ELEMS // D_FF, D_FF), F32),
            pltpu.VMEM((2, WEIGHT_STAGE_ELEMS // D_MODEL, D_MODEL), F32),
            pltpu.SemaphoreType.DMA((2,)),
        ],
        compiler_params=_params("arbitrary"),
        name="post",
    )(x, ya, yb, yc, yd, gg_col, w_out, g_ffn, w_up, w_down, g_final)


def _rope_tables(seq, half):
    inv_freq = np.power(ROPE_BASE, -np.arange(half, dtype=np.float64) / half)
    ang = inv_freq[:, None] * np.arange(seq, dtype=np.float64)[None, :]
    return jnp.asarray(np.cos(ang), F32), jnp.asarray(np.sin(ang), F32)


def _inproj_weights(w_in):
    wt = jnp.swapaxes(w_in, 1, 2)
    w_abc = wt[:, :ROWS_ABC].astype(BF16)
    f = wt[:, ROWS_ABC:ROWS_ABC + N_HEADS]
    d = wt[:, ROWS_ABC + N_HEADS:ROWS_ABC + N_HEADS + ROWS_D]
    pad = jnp.zeros((w_in.shape[0], ROWS_F - N_HEADS, D_MODEL), w_in.dtype)
    return w_abc, jnp.concatenate([d, f, pad], axis=1).astype(BF16)


def _mla_weights(w_uq, w_ukv):
    wq = w_uq.T.reshape(N_HEADS, MLA_NOPE + MLA_ROPE, MLA_Q_LORA)
    wq = jnp.pad(wq, ((0, 0), (0, HEAD_PAD - MLA_NOPE - MLA_ROPE), (0, 0)))
    wq = wq.reshape(N_HEADS * HEAD_PAD, MLA_Q_LORA).astype(BF16)
    wkv = w_ukv.T.reshape(N_HEADS, 2 * HEAD_DIM, MLA_KV_LORA)
    wk = wkv[:, :HEAD_DIM].reshape(GROUP_WIDTH, MLA_KV_LORA).astype(BF16)
    wv = wkv[:, HEAD_DIM:].reshape(GROUP_WIDTH, MLA_KV_LORA).astype(BF16)
    return wq, wk, wv


def kernel(x, g_mix_norm, w_in, b_forget, g_sgu, w_spatial, b_spatial, g_mla_q, w_uq, g_mla_kv, w_ukv,
           g_group_out, w_out, g_ffn_norm, w_up, w_down, g_final):
    batch, seq, _ = x.shape
    depth = w_in.shape[0]
    assert seq % TOKEN_TILE == 0 and ATTN_BLOCK == TOKEN_TILE
    cos_b, sin_b = _rope_tables(seq, HEAD_DIM // 2)
    cos_d, sin_d = _rope_tables(seq, MLA_ROPE // 2)
    w_abc, w_df = _inproj_weights(w_in)
    xf = x.reshape(batch * seq, D_MODEL)
    for l in range(depth):
        bf_col = jnp.pad(b_forget[l], (0, ROWS_F - N_HEADS))[:, None]
        ya, yb, qt_c, k_c, vt_c, stats_c, qt_d, k_d, vt_d, stats_d = _front(
            xf, g_mix_norm[l][None, :], w_abc, w_df,
            g_sgu[l][:, None], w_spatial[l], b_spatial[l], cos_b, sin_b, bf_col,
            cos_d, sin_d, g_mla_q[l][:, None], g_mla_kv[l][:, None], *_mla_weights(w_uq[l], w_ukv[l]),
            batch, l)
        yc = _attention(qt_c, k_c, vt_c, stats_c, banded=True)
        yd = _attention(qt_d, k_d, vt_d, stats_d, banded=False)
        xf = _post(xf, ya, yb, yc, yd, g_group_out[l].reshape(4, GROUP_WIDTH, 1),
                   w_out, g_ffn_norm[l][None, :], w_up, w_down, g_final[None, :],
                   layer=l, final=(l == depth - 1))
    return xf.reshape(batch, seq, D_MODEL)
```

```python
import functools
import math

import jax
import jax.numpy as jnp
import numpy as np
from jax import lax
from jax.experimental import pallas as pl
from jax.experimental.pallas import tpu as pltpu

F32 = jnp.float32
BF16 = jnp.bfloat16

D_MODEL = 1024
N_HEADS = 4
HEAD_DIM = 64
GROUP_WIDTH = N_HEADS * HEAD_DIM
CHUNK = 128
MLA_Q_LORA = 256
MLA_KV_LORA = 128
MLA_NOPE = 64
MLA_ROPE = 32
ROPE_BASE = 10000.0
D_FF = 4 * D_MODEL
EPS = 1e-6

HEAD_PAD = 128
AUG_ROWS = 8
PV_ROWS = 80
LOG2E = math.log2(math.e)
EXP2_UNDERFLOW = 152.0
STAT_QNORM, STAT_KNORM, STAT_CMAX, STAT_CMIN, STAT_DIAG, STAT_GAP = range(6)
N_STATS = 6
GAP_LIMIT = 64.0
TOKEN_TILE = 512
ATTN_BLOCK = TOKEN_TILE
FF_CHUNK = 512
VMEM_LIMIT = 56 * 1024 * 1024

ROWS_A = 2 * GROUP_WIDTH
ROWS_B = 4 * GROUP_WIDTH
ROWS_C = 3 * GROUP_WIDTH
ROWS_D = MLA_Q_LORA + MLA_KV_LORA + MLA_ROPE
ROWS_F = 8
ROWS_ABC = ROWS_A + ROWS_B + ROWS_C

NT = (((1,), (1,)), ((), ()))
TN = (((0,), (0,)), ((), ()))


def _params(*sem):
    return pltpu.CompilerParams(dimension_semantics=sem, vmem_limit_bytes=VMEM_LIMIT)


def _rot_half_rows(t, cos, sin):
    half = t.shape[0] // 2
    t1, t2 = t[:half], t[half:]
    return jnp.concatenate([t1 * cos - t2 * sin, t1 * sin + t2 * cos], axis=0)


def _standardize_rows(t):
    mu = jnp.mean(t, axis=0, keepdims=True)
    var = jnp.mean(jnp.square(t - mu), axis=0, keepdims=True)
    return (t - mu) * lax.rsqrt(var + EPS)


def _rms_rows(t):
    return t * lax.rsqrt(jnp.mean(t * t, axis=0, keepdims=True) + EPS)


def _sgu(at_ref, gain_ref, ws_ref, bs_ref, o_ref):
    tm = at_ref.shape[1]
    nch = tm // CHUNK
    row = lax.broadcasted_iota(jnp.int32, (CHUNK, CHUNK), 0)
    col = lax.broadcasted_iota(jnp.int32, (CHUNK, CHUNK), 1)
    for h in range(N_HEADS):
        r0, r1 = h * HEAD_DIM, (h + 1) * HEAD_DIM
        u = jax.nn.gelu(at_ref[r0:r1, :].astype(F32))
        v = jax.nn.gelu(at_ref[GROUP_WIDTH + r0:GROUP_WIDTH + r1, :].astype(F32))
        v = _standardize_rows(v) * gain_ref[r0:r1, :]
        w = jnp.where(col <= row, ws_ref[h], 0.0).astype(BF16)
        vs = jnp.concatenate([v[:, c * CHUNK:(c + 1) * CHUNK] for c in range(nch)], axis=0).astype(BF16)
        mixed = lax.dot_general(vs, w, NT, preferred_element_type=F32) + bs_ref[h:h + 1, :]
        for c in range(nch):
            o_ref[r0:r1, c * CHUNK:(c + 1) * CHUNK] = (
                u[:, c * CHUNK:(c + 1) * CHUNK] * mixed[c * HEAD_DIM:(c + 1) * HEAD_DIM])


def _retention(bt_ref, cos_ref, sin_ref, o_ref, st_ref):
    tm = bt_ref.shape[1]
    nch = tm // CHUNK
    cos, sin = cos_ref[...], sin_ref[...]
    srow = lax.broadcasted_iota(jnp.int32, (CHUNK, CHUNK), 0)
    tcol = lax.broadcasted_iota(jnp.int32, (CHUNK, CHUNK), 1)
    rel = (tcol - srow).astype(F32)
    j = lax.broadcasted_iota(jnp.int32, (1, CHUNK), 1).astype(F32)
    for h in range(N_HEADS):
        log_gamma = math.log1p(-(2.0 ** (-5.0 - h)))
        dec_t = jnp.where(rel >= 0, jnp.exp(jnp.maximum(rel, 0.0) * log_gamma), 0.0)
        query_w = jnp.exp((j + 1.0) * log_gamma)
        key_w = jnp.exp((CHUNK - 1.0 - j) * log_gamma)
        chunk_decay = math.exp(CHUNK * log_gamma)
        r0, r1 = h * HEAD_DIM, (h + 1) * HEAD_DIM
        q = _rot_half_rows(bt_ref[r0:r1, :].astype(F32), cos, sin)
        k = _rot_half_rows(bt_ref[GROUP_WIDTH + r0:GROUP_WIDTH + r1, :].astype(F32), cos, sin) * (HEAD_DIM ** -0.5)
        v = bt_ref[2 * GROUP_WIDTH + r0:2 * GROUP_WIDTH + r1, :].astype(F32)
        g = bt_ref[3 * GROUP_WIDTH + r0:3 * GROUP_WIDTH + r1, :].astype(F32)
        st = st_ref[h]
        ys = []
        for c in range(nch):
            sl = slice(c * CHUNK, (c + 1) * CHUNK)
            qc, kc, vc = q[:, sl], k[:, sl], v[:, sl]
            kcb = kc.astype(BF16)
            a_t = lax.dot_general(kcb, qc.astype(BF16), TN, preferred_element_type=F32)
            p_t = (a_t * dec_t).astype(BF16)
            intra = jnp.dot(vc.astype(BF16), p_t, preferred_element_type=F32)
            cross = jnp.dot(st.astype(BF16), (qc * query_w).astype(BF16), preferred_element_type=F32)
            ys.append(intra + cross)
            st = chunk_decay * st + lax.dot_general((vc * key_w).astype(BF16), kcb, NT,
                                                    preferred_element_type=F32)
        st_ref[h] = st
        y = _standardize_rows(jnp.concatenate(ys, axis=1))
        o_ref[r0:r1, :] = jax.nn.silu(g) * y


def _select_rows(rows):
    n = rows[0].shape[1]
    ridx = lax.broadcasted_iota(jnp.int32, (AUG_ROWS, n), 0)
    out = jnp.zeros((AUG_ROWS, n), F32)
    for i, r in enumerate(rows):
        out = jnp.where(ridx == i, jnp.broadcast_to(r, (AUG_ROWS, n)), out)
    return out


def _group(feat, extra):
    n = feat.shape[1]
    pad = HEAD_PAD - feat.shape[0] - extra.shape[0]
    return jnp.concatenate([feat, extra, jnp.zeros((pad, n), F32)], axis=0)


def _split3(x):
    hi = x.astype(BF16).astype(F32)
    mid = (x - hi).astype(BF16).astype(F32)
    lo = (x - hi - mid).astype(BF16).astype(F32)
    return hi, mid, lo


def _logit_bound(qb, kb, kmax_ref, row):
    qnorm = jnp.sqrt(jnp.sum(qb * qb, axis=0, keepdims=True))
    knorm = jnp.sqrt(jnp.max(jnp.sum(kb * kb, axis=0, keepdims=True), axis=1, keepdims=True))
    kmax = jnp.maximum(kmax_ref[row:row + 1, :], knorm)
    kmax_ref[row:row + 1, :] = kmax
    return qnorm * kmax, jnp.sum(qb * kb, axis=0, keepdims=True), qnorm, knorm


def _write_stats(st_ref, per_head):
    stat_row = lax.broadcasted_iota(jnp.int32, (8, 128), 0)
    stat_lane = lax.broadcasted_iota(jnp.int32, (8, 128), 1)
    stats = jnp.zeros((8, 128), F32)
    for h, vals in enumerate(per_head):
        for r, val in enumerate(vals):
            stats = jnp.where((stat_row == r) & (stat_lane == h), val, stats)
    st_ref[0, 0] = stats


def _fox_prep(ct_ref, ft_ref, bf_ref, qt_ref, k_ref, vt_ref, st_ref, carry_ref, kmax_ref):
    tm = ct_ref.shape[1]
    x = ft_ref[...] + bf_ref[...]
    lf = jnp.minimum(x, 0.0) - jnp.log1p(jnp.exp(-jnp.abs(x)))
    hi = lf.astype(BF16)
    mid = (lf - hi.astype(F32)).astype(BF16)
    lo = (lf - hi.astype(F32) - mid.astype(F32)).astype(BF16)
    srow = lax.broadcasted_iota(jnp.int32, (tm, tm), 0)
    tcol = lax.broadcasted_iota(jnp.int32, (tm, tm), 1)
    upper = jnp.where(srow <= tcol, 1.0, 0.0).astype(BF16)
    parts = jnp.dot(jnp.concatenate([hi, mid, lo], axis=0), upper, preferred_element_type=F32)
    cum = parts[0:8] + parts[8:16] + parts[16:24] + carry_ref[...]
    carry_ref[...] = cum[:, tm - 1:tm]

    cum2 = cum * LOG2E
    one = jnp.ones((1, tm), F32)
    ones_row = _select_rows([one])
    stats = []
    for h in range(N_HEADS):
        r0, r1 = h * HEAD_DIM, (h + 1) * HEAD_DIM
        g0, g1 = h * HEAD_PAD, (h + 1) * HEAD_PAD
        q = ct_ref[r0:r1, :].astype(F32) * (HEAD_DIM ** -0.5 * LOG2E)
        k = ct_ref[GROUP_WIDTH + r0:GROUP_WIDTH + r1, :].astype(F32)
        v = ct_ref[2 * GROUP_WIDTH + r0:2 * GROUP_WIDTH + r1, :].astype(F32)
        qb, kb = q.astype(BF16).astype(F32), k.astype(BF16).astype(F32)
        bound, diag, qnorm, knorm = _logit_bound(qb, kb, kmax_ref, h)
        c2 = cum2[h:h + 1]
        q_extra = _select_rows([*_split3(c2 - bound), one, one, one])
        k_extra = _select_rows([one, one, one, *(-part for part in _split3(c2))])
        qt_ref[0, g0:g1, :] = _group(q, q_extra).astype(BF16)
        k_ref[:, g0:g1] = _group(k, k_extra).T.astype(BF16)
        vt_ref[0, h, 0] = _group(v, ones_row).astype(BF16)
        stats.append((
            jnp.max(qnorm, axis=1, keepdims=True),
            knorm,
            jnp.max(c2, axis=1, keepdims=True),
            jnp.min(c2, axis=1, keepdims=True),
            jnp.min(diag, axis=1, keepdims=True),
            jnp.max(bound - diag, axis=1, keepdims=True),
        ))
    _write_stats(st_ref, stats)


def _mla_prep(dt_ref, cos_ref, sin_ref, gq_ref, gkv_ref, wq_ref, wk_ref, wv_ref, qt_ref, k_ref, vt_ref,
              st_ref, kmax_ref):
    tm = dt_ref.shape[1]
    cos, sin = cos_ref[...], sin_ref[...]
    cq = _rms_rows(dt_ref[0:MLA_Q_LORA, :].astype(F32)) * gq_ref[...]
    ckv = _rms_rows(dt_ref[MLA_Q_LORA:MLA_Q_LORA + MLA_KV_LORA, :].astype(F32)) * gkv_ref[...]
    kr = _rot_half_rows(dt_ref[MLA_Q_LORA + MLA_KV_LORA:ROWS_D, :].astype(F32), cos, sin)
    ckv_b = ckv.astype(BF16)
    q_all = jnp.dot(wq_ref[...], cq.astype(BF16), preferred_element_type=F32)
    q_all = q_all * ((MLA_NOPE + MLA_ROPE) ** -0.5 * LOG2E)
    k_all = jnp.dot(wk_ref[...], ckv_b, preferred_element_type=F32)
    v_all = jnp.dot(wv_ref[...], ckv_b, preferred_element_type=F32)
    one = jnp.ones((1, tm), F32)
    ones_row = _select_rows([one])
    k_extra = _select_rows([one, one, one])
    zero = jnp.zeros((1, 1), F32)
    stats = []
    for h in range(N_HEADS):
        r0, r1 = h * HEAD_DIM, (h + 1) * HEAD_DIM
        g0, g1 = h * HEAD_PAD, (h + 1) * HEAD_PAD
        qg = q_all[g0:g1]
        q_rope = _rot_half_rows(qg[MLA_NOPE:MLA_NOPE + MLA_ROPE], cos, sin)
        q = jnp.concatenate([qg[0:MLA_NOPE], q_rope], axis=0)
        k = jnp.concatenate([k_all[r0:r1], kr], axis=0)
        qb, kb = q.astype(BF16).astype(F32), k.astype(BF16).astype(F32)
        bound, diag, _, _ = _logit_bound(qb, kb, kmax_ref, N_HEADS + h)
        q_extra = _select_rows([*_split3(-bound)])
        qt_ref[0, g0:g1, :] = _group(q, q_extra).astype(BF16)
        k_ref[:, g0:g1] = _group(k, k_extra).T.astype(BF16)
        vt_ref[0, h, 0] = _group(v_all[r0:r1], ones_row).astype(BF16)
        stats.append((zero,) * STAT_GAP + (jnp.max(bound - diag, axis=1, keepdims=True),))
    _write_stats(st_ref, stats)


def _front_body(x_ref, g_ref, w_ref, wdf_ref,
                gain_ref, ws_ref, bs_ref,
                cosb_ref, sinb_ref,
                bf_ref,
                cosd_ref, sind_ref, gq_ref, gkv_ref, wq_ref, wk_ref, wv_ref,
                ya_ref, yb_ref,
                qtc_ref, kc_ref, vtc_ref, stats_ref,
                qtd_ref, kd_ref, vtd_ref, statsd_ref,
                at_ref, bt_ref, ct_ref, dt_ref, ft_ref, state_ref, carry_ref, kmax_ref, *, tiles_per_seq):
    @pl.when(pl.program_id(0) % tiles_per_seq == 0)
    def _():
        state_ref[...] = jnp.zeros_like(state_ref)
        carry_ref[...] = jnp.zeros_like(carry_ref)
        kmax_ref[...] = jnp.zeros_like(kmax_ref)

    x = x_ref[...]
    h = (x * lax.rsqrt(jnp.mean(x * x, axis=-1, keepdims=True) + EPS) * g_ref[...]).astype(BF16)

    def proj(w):
        return lax.dot_general(w, h, NT, preferred_element_type=F32)

    half_b = ROWS_B // 2
    zdf = proj(wdf_ref[...])
    dt_ref[...] = zdf[0:ROWS_D].astype(BF16)
    ft_ref[...] = zdf[ROWS_D:ROWS_D + ROWS_F]
    at_ref[...] = proj(w_ref[0:ROWS_A, :]).astype(BF16)
    _mla_prep(dt_ref, cosd_ref, sind_ref, gq_ref, gkv_ref, wq_ref, wk_ref, wv_ref, qtd_ref, kd_ref, vtd_ref,
              statsd_ref, kmax_ref)
    ct_ref[...] = proj(w_ref[ROWS_A + ROWS_B:ROWS_ABC, :]).astype(BF16)
    _sgu(at_ref, gain_ref, ws_ref, bs_ref, ya_ref)
    bt_ref[0:half_b, :] = proj(w_ref[ROWS_A:ROWS_A + half_b, :]).astype(BF16)
    _fox_prep(ct_ref, ft_ref, bf_ref, qtc_ref, kc_ref, vtc_ref, stats_ref, carry_ref, kmax_ref)
    bt_ref[half_b:ROWS_B, :] = proj(w_ref[ROWS_A + half_b:ROWS_A + ROWS_B, :]).astype(BF16)
    _retention(bt_ref, cosb_ref, sinb_ref, yb_ref, state_ref)


def _front(x, g, w_t, wdf_t, gain_col, w_s, b_s, cos_b, sin_b, bf_col,
           cos_d, sin_d, gq_col, gkv_col, wq_t, wk_t, wv_t, batch, layer):
    t = x.shape[0]
    tm = TOKEN_TILE
    ns = t // batch // tm
    const = lambda shape: pl.BlockSpec(shape, lambda i: (0,) * len(shape), pipeline_mode=pl.Buffered(1))
    of_layer = lambda shape: pl.BlockSpec((None,) + shape, lambda i: (layer,) + (0,) * len(shape),
                                          pipeline_mode=pl.Buffered(1))
    rows_t = lambda rows: pl.BlockSpec((rows, tm), lambda i: (0, i))
    table = lambda rows: pl.BlockSpec((rows, tm), lambda i: (0, i % ns))
    k_tile = pl.BlockSpec((tm, N_HEADS * HEAD_PAD), lambda i: (i, 0))
    vt_tile = pl.BlockSpec((1, N_HEADS, 1, HEAD_PAD, tm), lambda i: (i // ns, 0, i % ns, 0, 0))
    stats_tile = pl.BlockSpec((1, 1, 8, 128), lambda i: (i // ns, i % ns, 0, 0))
    qt_tile = pl.BlockSpec((1, N_HEADS * HEAD_PAD, tm), lambda i: (i, 0, 0))
    qkv_shapes = [
        jax.ShapeDtypeStruct((t // tm, N_HEADS * HEAD_PAD, tm), BF16),
        jax.ShapeDtypeStruct((t, N_HEADS * HEAD_PAD), BF16),
        jax.ShapeDtypeStruct((batch, N_HEADS, ns, HEAD_PAD, tm), BF16),
        jax.ShapeDtypeStruct((batch, ns, 8, 128), F32),
    ]
    return pl.pallas_call(
        functools.partial(_front_body, tiles_per_seq=ns),
        grid=(t // tm,),
        in_specs=[
            pl.BlockSpec((tm, D_MODEL), lambda i: (i, 0)),
            const((1, D_MODEL)),
            of_layer((ROWS_ABC, D_MODEL)),
            of_layer((ROWS_D + ROWS_F, D_MODEL)),
            const((GROUP_WIDTH, 1)),
            const((N_HEADS, CHUNK, CHUNK)),
            const((N_HEADS, CHUNK)),
            table(HEAD_DIM // 2), table(HEAD_DIM // 2),
            const((ROWS_F, 1)),
            table(MLA_ROPE // 2), table(MLA_ROPE // 2),
            const((MLA_Q_LORA, 1)),
            const((MLA_KV_LORA, 1)),
            const((N_HEADS * HEAD_PAD, MLA_Q_LORA)),
            const((GROUP_WIDTH, MLA_KV_LORA)),
            const((GROUP_WIDTH, MLA_KV_LORA)),
        ],
        out_specs=[
            rows_t(GROUP_WIDTH), rows_t(GROUP_WIDTH),
            qt_tile, k_tile, vt_tile, stats_tile,
            qt_tile, k_tile, vt_tile, stats_tile,
        ],
        out_shape=[
            jax.ShapeDtypeStruct((GROUP_WIDTH, t), F32),
            jax.ShapeDtypeStruct((GROUP_WIDTH, t), F32),
            *qkv_shapes,
            *qkv_shapes,
        ],
        scratch_shapes=[
            pltpu.VMEM((ROWS_A, tm), BF16),
            pltpu.VMEM((ROWS_B, tm), BF16),
            pltpu.VMEM((ROWS_C, tm), BF16),
            pltpu.VMEM((ROWS_D, tm), BF16),
            pltpu.VMEM((ROWS_F, tm), F32),
            pltpu.VMEM((N_HEADS, HEAD_DIM, HEAD_DIM), F32),
            pltpu.VMEM((ROWS_F, 1), F32),
            pltpu.VMEM((2 * N_HEADS, 1), F32),
        ],
        compiler_params=_params("arbitrary"),
        name="front",
    )(x, g, w_t, wdf_t, gain_col, w_s, b_s, cos_b, sin_b, bf_col,
      cos_d, sin_d, gq_col, gkv_col, wq_t, wk_t, wv_t)


def _first_needed_block(stats_ref, b, qi, nb):
    def stat(j, r, h):
        return stats_ref[((b * nb + j) * N_STATS + r) * N_HEADS + h]

    skipped = jnp.int32(0)
    leading = jnp.bool_(True)
    for j in range(nb - 1):
        zero = j < qi
        for h in range(N_HEADS):
            bound = (stat(qi, STAT_QNORM, h) * stat(j, STAT_KNORM, h)
                     + stat(qi, STAT_CMAX, h) - stat(j, STAT_CMIN, h))
            zero = jnp.logical_and(zero, bound - stat(qi, STAT_DIAG, h) < -EXP2_UNDERFLOW)
        leading = jnp.logical_and(leading, zero)
        skipped = skipped + leading.astype(jnp.int32)
    return skipped


def _attn_body(*refs, banded, online):
    refs = list(refs)
    stats_ref = refs.pop(0) if banded else None
    qt_ref, k_ref, vt_ref, o_ref, acc_ref, s_ref = refs[:6]
    m_ref = refs[6] if online else None
    nb, _, blk = qt_ref.shape
    half = blk // 2
    batch = pl.program_id(0)

    def first_block(qi):
        return _first_needed_block(stats_ref, batch, qi, nb) if banded else jnp.int32(0)

    def logits(qi, kj, h):
        start = pl.multiple_of(kj * blk, blk)
        g0, g1 = h * HEAD_PAD, (h + 1) * HEAD_PAD
        return jnp.dot(k_ref[pl.ds(start, blk), g0:g1], qt_ref[qi, g0:g1, :],
                       preferred_element_type=F32)

    def accumulate(s, kj, h):
        acc = acc_ref[h]
        if online:
            m_old = m_ref[h]
            m_new = jnp.maximum(m_old, jnp.max(s, axis=0, keepdims=True))
            m_ref[h] = m_new
            s = s - m_new
            acc = acc * jnp.exp2(m_old - m_new)
        pv = jnp.dot(vt_ref[0, h, kj, 0:PV_ROWS, :], jnp.exp2(s).astype(BF16), preferred_element_type=F32)
        acc_ref[h] = acc + pv[0:HEAD_DIM + AUG_ROWS]

    s_ref[...] = logits(0, 0, 0)

    def query_block(qi, first):
        acc_ref[...] = jnp.zeros_like(acc_ref)
        if online:
            m_ref[...] = jnp.full(m_ref.shape, -jnp.inf, F32)
        nxt_qi = jnp.minimum(qi + 1, nb - 1)
        nxt_first = first_block(nxt_qi)

        def full_blocks(kj, count):
            s = s_ref[...]
            for d in range(count):
                for h in range(N_HEADS):
                    nxt = (kj + d, h + 1) if h + 1 < N_HEADS else (kj + d + 1, 0)
                    s_next = logits(qi, *nxt)
                    accumulate(s, kj + d, h)
                    s = s_next
            s_ref[...] = s

        n_full = qi - first
        odd = jnp.bitwise_and(n_full, 1)
        pl.when(odd == 1)(lambda: full_blocks(first, 1))

        def pair_step(i, carry):
            full_blocks(first + odd + 2 * i, 2)
            return carry

        lax.fori_loop(0, lax.shift_right_logical(n_full, 1), pair_step, 0)

        if online:
            key_pos = lax.broadcasted_iota(jnp.int32, (blk, blk), 0)
            qry_pos = lax.broadcasted_iota(jnp.int32, (blk, blk), 1)
            visible = key_pos <= qry_pos
            s = s_ref[...]
            for h in range(N_HEADS):
                s_next = logits(qi, qi, h + 1) if h + 1 < N_HEADS else logits(nxt_qi, nxt_first, 0)
                accumulate(jnp.where(visible, s, -jnp.inf), qi, h)
                s = s_next
            s_ref[...] = s
        else:
            start = pl.multiple_of(qi * blk, blk)
            key_pos = lax.broadcasted_iota(jnp.int32, (half, blk), 0)
            qry_pos = lax.broadcasted_iota(jnp.int32, (half, blk), 1)
            visible_a = key_pos <= qry_pos
            visible_b = visible_a[:, 0:half]

            def quadrant_logits(h):
                g0, g1 = h * HEAD_PAD, (h + 1) * HEAD_PAD
                s_a = jnp.dot(k_ref[pl.ds(start, half), g0:g1], qt_ref[qi, g0:g1, :],
                              preferred_element_type=F32)
                s_b = jnp.dot(k_ref[pl.ds(start + half, half), g0:g1], qt_ref[qi, g0:g1, half:blk],
                              preferred_element_type=F32)
                return s_a, s_b

            s_full = s_ref[...]
            s_a, s_b = s_full[0:half], s_full[half:blk, half:blk]
            for h in range(N_HEADS):
                if h + 1 < N_HEADS:
                    s_next = quadrant_logits(h + 1)
                else:
                    s_ref[...] = logits(nxt_qi, nxt_first, 0)
                p_a = jnp.exp2(jnp.where(visible_a, s_a, -jnp.inf)).astype(BF16)
                p_b = jnp.exp2(jnp.where(visible_b, s_b, -jnp.inf)).astype(BF16)
                pv_a = jnp.dot(vt_ref[0, h, qi, 0:PV_ROWS, 0:half], p_a, preferred_element_type=F32)
                pv_b = jnp.dot(vt_ref[0, h, qi, 0:PV_ROWS, half:blk], p_b, preferred_element_type=F32)
                acc_ref[h] = acc_ref[h] + pv_a[0:HEAD_DIM + AUG_ROWS]
                acc_ref[h, :, half:blk] = acc_ref[h, :, half:blk] + pv_b[0:HEAD_DIM + AUG_ROWS]
                if h + 1 < N_HEADS:
                    s_a, s_b = s_next
        for h in range(N_HEADS):
            acc = acc_ref[h]
            o_ref[qi, h * HEAD_DIM:(h + 1) * HEAD_DIM, :] = acc[0:HEAD_DIM] / acc[HEAD_DIM:HEAD_DIM + 1]
        return nxt_first

    lax.fori_loop(0, nb, query_block, jnp.int32(0))


def _attention_call(qt, k, vt, skip_stats, online):
    batch, _, nb, _, blk = vt.shape
    seq = nb * blk
    banded = skip_stats is not None
    in_specs = [
        pl.BlockSpec((nb, N_HEADS * HEAD_PAD, blk), lambda b: (b, 0, 0)),
        pl.BlockSpec((seq, N_HEADS * HEAD_PAD), lambda b: (b, 0)),
        pl.BlockSpec((1, N_HEADS, nb, HEAD_PAD, blk), lambda b: (b, 0, 0, 0, 0)),
    ]
    args = (qt, k, vt)
    if banded:
        in_specs = [pl.BlockSpec(memory_space=pltpu.SMEM)] + in_specs
        args = (skip_stats,) + args
    scratch = [pltpu.VMEM((N_HEADS, HEAD_DIM + AUG_ROWS, blk), F32), pltpu.VMEM((blk, blk), F32)]
    if online:
        scratch.append(pltpu.VMEM((N_HEADS, 1, blk), F32))
    return pl.pallas_call(
        functools.partial(_attn_body, banded=banded, online=online),
        grid=(batch,),
        in_specs=in_specs,
        out_specs=pl.BlockSpec((nb, GROUP_WIDTH, blk), lambda b: (b, 0, 0)),
        out_shape=jax.ShapeDtypeStruct((batch * nb, GROUP_WIDTH, blk), F32),
        scratch_shapes=scratch,
        compiler_params=_params("parallel"),
        name="attention_online" if online else "attention",
    )(*args)


def _attention(qt, k, vt, stats, banded):
    stats = stats[:, :, :N_STATS, :N_HEADS]
    skip_stats = stats.reshape(-1) if banded else None
    bound_is_tight = jnp.max(stats[:, :, STAT_GAP, :]) <= GAP_LIMIT
    return lax.cond(bound_is_tight,
                    lambda: _attention_call(qt, k, vt, skip_stats, online=False),
                    lambda: _attention_call(qt, k, vt, skip_stats, online=True))


def _post_body(x_ref, ya_ref, yb_ref, yc_ref, yd_ref, gg_ref, wo_hbm, gf_ref, wu_hbm, wd_hbm, gl_ref,
               o_ref, wo_ref, wu_ref, wd_ref, stage_tall, stage_wide, sem, *, layer, final):
    n_chunks = D_FF // FF_CHUNK
    tall = [(wo_hbm.at[layer, pl.ds(r, FF_CHUNK), :], wo_ref.at[pl.ds(r, FF_CHUNK), :])
            for r in range(0, D_MODEL, FF_CHUNK)]
    n_out = len(tall)
    tall += [(wd_hbm.at[layer, pl.ds(c * FF_CHUNK, FF_CHUNK), :], wd_ref.at[pl.ds(c * FF_CHUNK, FF_CHUNK), :])
             for c in range(n_chunks)]
    wide = [(wu_hbm.at[layer, :, pl.ds(c * FF_CHUNK, FF_CHUNK)], wu_ref.at[:, pl.ds(c * FF_CHUNK, FF_CHUNK)])
            for c in range(n_chunks)]
    queues = ((tall, stage_tall), (wide, stage_wide))

    def dma(kind, k):
        blocks, stage = queues[kind]
        return pltpu.make_async_copy(blocks[k][0], stage.at[k % 2], sem.at[kind, k % 2])

    def land(kind, k):
        blocks, stage = queues[kind]
        dma(kind, k).wait()
        blocks[k][1][...] = stage[k % 2].astype(BF16)
        if k + 2 < len(blocks):
            dma(kind, k + 2).start()

    def body(load):
        if load:
            for kind in range(2):
                dma(kind, 0).start()
                dma(kind, 1).start()
        ys = []
        for g, y_ref in enumerate((ya_ref, yb_ref, yc_ref, yd_ref)):
            ys.append((_rms_rows(y_ref[...]) * gg_ref[g]).astype(BF16))
        y = jnp.concatenate(ys, axis=0)
        if load:
            for k in range(n_out):
                land(0, k)
        x = x_ref[...] + lax.dot_general(y, wo_ref[...], TN, preferred_element_type=F32)
        h = (x * lax.rsqrt(jnp.mean(x * x, axis=-1, keepdims=True) + EPS) * gf_ref[...]).astype(BF16)
        acc = x
        for c in range(n_chunks):
            c0, c1 = c * FF_CHUNK, (c + 1) * FF_CHUNK
            if load:
                land(1, c)
                land(0, n_out + c)
            a = jnp.maximum(jnp.dot(h, wu_ref[:, c0:c1], preferred_element_type=F32), 0.0)
            acc = acc + jnp.dot((a * a).astype(BF16), wd_ref[c0:c1, :], preferred_element_type=F32)
        if final:
            acc = acc * lax.rsqrt(jnp.mean(acc * acc, axis=-1, keepdims=True) + EPS) * gl_ref[...]
        o_ref[...] = acc

    first_step = pl.program_id(0) == 0
    pl.when(first_step)(functools.partial(body, True))
    pl.when(jnp.logical_not(first_step))(functools.partial(body, False))


def _post(x, ya, yb, yc, yd, gg_col, w_out, g_ffn, w_up, w_down, g_final, layer, final):
    t = x.shape[0]
    tm = TOKEN_TILE
    const = lambda shape: pl.BlockSpec(shape, lambda i: (0,) * len(shape), pipeline_mode=pl.Buffered(1))
    in_hbm = pl.BlockSpec(memory_space=pl.ANY)
    ytile = pl.BlockSpec((GROUP_WIDTH, tm), lambda i: (0, i))
    yblock = pl.BlockSpec((None, GROUP_WIDTH, tm), lambda i: (i, 0, 0))
    return pl.pallas_call(
        functools.partial(_post_body, layer=layer, final=final),
        grid=(t // tm,),
        in_specs=[
            pl.BlockSpec((tm, D_MODEL), lambda i: (i, 0)),
            ytile, ytile, yblock, yblock,
            const((4, GROUP_WIDTH, 1)),
            in_hbm,
            const((1, D_MODEL)),
            in_hbm,
            in_hbm,
            const((1, D_MODEL)),
        ],
        out_specs=pl.BlockSpec((tm, D_MODEL), lambda i: (i, 0)),
        out_shape=jax.ShapeDtypeStruct((t, D_MODEL), F32),
        scratch_shapes=[
            pltpu.VMEM((D_MODEL, D_MODEL), BF16),
            pltpu.VMEM((D_MODEL, D_FF), BF16),
            pltpu.VMEM((D_FF, D_MODEL), BF16),
            pltpu.VMEM((2, FF_CHUNK, D_MODEL), F32),
            pltpu.VMEM((2, D_MODEL, FF_CHUNK), F32),
            pltpu.SemaphoreType.DMA((2, 2)),
        ],
        compiler_params=_params("arbitrary"),
        name="post",
    )(x, ya, yb, yc, yd, gg_col, w_out, g_ffn, w_up, w_down, g_final)


def _rope_tables(seq, half):
    inv_freq = np.power(ROPE_BASE, -np.arange(half, dtype=np.float64) / half)
    ang = inv_freq[:, None] * np.arange(seq, dtype=np.float64)[None, :]
    return jnp.asarray(np.cos(ang), F32), jnp.asarray(np.sin(ang), F32)


def _inproj_weights(w_in):
    wt = jnp.swapaxes(w_in, 1, 2)
    w_abc = wt[:, :ROWS_ABC].astype(BF16)
    f = wt[:, ROWS_ABC:ROWS_ABC + N_HEADS]
    d = wt[:, ROWS_ABC + N_HEADS:ROWS_ABC + N_HEADS + ROWS_D]
    pad = jnp.zeros((w_in.shape[0], ROWS_F - N_HEADS, D_MODEL), w_in.dtype)
    return w_abc, jnp.concatenate([d, f, pad], axis=1).astype(BF16)


def _mla_weights(w_uq, w_ukv):
    wq = w_uq.T.reshape(N_HEADS, MLA_NOPE + MLA_ROPE, MLA_Q_LORA)
    wq = jnp.pad(wq, ((0, 0), (0, HEAD_PAD - MLA_NOPE - MLA_ROPE), (0, 0)))
    wq = wq.reshape(N_HEADS * HEAD_PAD, MLA_Q_LORA).astype(BF16)
    wkv = w_ukv.T.reshape(N_HEADS, 2 * HEAD_DIM, MLA_KV_LORA)
    wk = wkv[:, :HEAD_DIM].reshape(GROUP_WIDTH, MLA_KV_LORA).astype(BF16)
    wv = wkv[:, HEAD_DIM:].reshape(GROUP_WIDTH, MLA_KV_LORA).astype(BF16)
    return wq, wk, wv


def kernel(x, g_mix_norm, w_in, b_forget, g_sgu, w_spatial, b_spatial, g_mla_q, w_uq, g_mla_kv, w_ukv,
           g_group_out, w_out, g_ffn_norm, w_up, w_down, g_final):
    batch, seq, _ = x.shape
    depth = w_in.shape[0]
    assert seq % TOKEN_TILE == 0 and ATTN_BLOCK == TOKEN_TILE
    cos_b, sin_b = _rope_tables(seq, HEAD_DIM // 2)
    cos_d, sin_d = _rope_tables(seq, MLA_ROPE // 2)
    w_abc, w_df = _inproj_weights(w_in)
    xf = x.reshape(batch * seq, D_MODEL)
    for l in range(depth):
        bf_col = jnp.pad(b_forget[l], (0, ROWS_F - N_HEADS))[:, None]
        ya, yb, qt_c, k_c, vt_c, stats_c, qt_d, k_d, vt_d, stats_d = _front(
            xf, g_mix_norm[l][None, :], w_abc, w_df,
            g_sgu[l][:, None], w_spatial[l], b_spatial[l], cos_b, sin_b, bf_col,
            cos_d, sin_d, g_mla_q[l][:, None], g_mla_kv[l][:, None], *_mla_weights(w_uq[l], w_ukv[l]),
            batch, l)
        yc = _attention(qt_c, k_c, vt_c, stats_c, banded=True)
        yd = _attention(qt_d, k_d, vt_d, stats_d, banded=False)
        xf = _post(xf, ya, yb, yc, yd, g_group_out[l].reshape(4, GROUP_WIDTH, 1),
                   w_out, g_ffn_norm[l][None, :], w_up, w_down, g_final[None, :],
                   layer=l, final=(l == depth - 1))
    return xf.reshape(batch, seq, D_MODEL)
```

```python
import functools
import math

import jax
import jax.numpy as jnp
import numpy as np
from jax import lax
from jax.experimental import pallas as pl
from jax.experimental.pallas import tpu as pltpu

F32 = jnp.float32
BF16 = jnp.bfloat16

D_MODEL = 1024
N_HEADS = 4
HEAD_DIM = 64
GROUP_WIDTH = N_HEADS * HEAD_DIM
CHUNK = 128
MLA_Q_LORA = 256
MLA_KV_LORA = 128
MLA_NOPE = 64
MLA_ROPE = 32
ROPE_BASE = 10000.0
D_FF = 4 * D_MODEL
EPS = 1e-6

HEAD_PAD = 128
AUG_ROWS = 8
PV_ROWS = 80
LOG2E = math.log2(math.e)
EXP2_UNDERFLOW = 152.0
STAT_QNORM, STAT_KNORM, STAT_CMAX, STAT_CMIN, STAT_DIAG, STAT_GAP = range(6)
N_STATS = 6
GAP_LIMIT = 64.0
TOKEN_TILE = 512
ATTN_BLOCK = TOKEN_TILE
FF_CHUNK = 512
VMEM_LIMIT = 56 * 1024 * 1024

ROWS_A = 2 * GROUP_WIDTH
ROWS_B = 4 * GROUP_WIDTH
ROWS_C = 3 * GROUP_WIDTH
ROWS_D = MLA_Q_LORA + MLA_KV_LORA + MLA_ROPE
ROWS_F = 8
ROWS_ABC = ROWS_A + ROWS_B + ROWS_C

NT = (((1,), (1,)), ((), ()))
TN = (((0,), (0,)), ((), ()))


def _params(*sem):
    return pltpu.CompilerParams(dimension_semantics=sem, vmem_limit_bytes=VMEM_LIMIT)


def _rot_half_rows(t, cos, sin):
    half = t.shape[0] // 2
    t1, t2 = t[:half], t[half:]
    return jnp.concatenate([t1 * cos - t2 * sin, t1 * sin + t2 * cos], axis=0)


def _standardize_rows(t):
    mu = jnp.mean(t, axis=0, keepdims=True)
    var = jnp.mean(jnp.square(t - mu), axis=0, keepdims=True)
    return (t - mu) * lax.rsqrt(var + EPS)


def _rms_rows(t):
    return t * lax.rsqrt(jnp.mean(t * t, axis=0, keepdims=True) + EPS)


def _sgu(at_ref, gain_ref, ws_ref, bs_ref, o_ref):
    tm = at_ref.shape[1]
    nch = tm // CHUNK
    row = lax.broadcasted_iota(jnp.int32, (CHUNK, CHUNK), 0)
    col = lax.broadcasted_iota(jnp.int32, (CHUNK, CHUNK), 1)
    for h in range(N_HEADS):
        r0, r1 = h * HEAD_DIM, (h + 1) * HEAD_DIM
        u = jax.nn.gelu(at_ref[r0:r1, :].astype(F32))
        v = jax.nn.gelu(at_ref[GROUP_WIDTH + r0:GROUP_WIDTH + r1, :].astype(F32))
        v = _standardize_rows(v) * gain_ref[r0:r1, :]
        w = jnp.where(col <= row, ws_ref[h], 0.0).astype(BF16)
        vs = jnp.concatenate([v[:, c * CHUNK:(c + 1) * CHUNK] for c in range(nch)], axis=0).astype(BF16)
        mixed = lax.dot_general(vs, w, NT, preferred_element_type=F32) + bs_ref[h:h + 1, :]
        for c in range(nch):
            o_ref[r0:r1, c * CHUNK:(c + 1) * CHUNK] = (
                u[:, c * CHUNK:(c + 1) * CHUNK] * mixed[c * HEAD_DIM:(c + 1) * HEAD_DIM])


def _retention(bt_ref, cos_ref, sin_ref, o_ref, st_ref):
    tm = bt_ref.shape[1]
    nch = tm // CHUNK
    cos, sin = cos_ref[...], sin_ref[...]
    srow = lax.broadcasted_iota(jnp.int32, (CHUNK, CHUNK), 0)
    tcol = lax.broadcasted_iota(jnp.int32, (CHUNK, CHUNK), 1)
    rel = (tcol - srow).astype(F32)
    j = lax.broadcasted_iota(jnp.int32, (1, CHUNK), 1).astype(F32)
    for h in range(N_HEADS):
        log_gamma = math.log1p(-(2.0 ** (-5.0 - h)))
        dec_t = jnp.where(rel >= 0, jnp.exp(jnp.maximum(rel, 0.0) * log_gamma), 0.0)
        query_w = jnp.exp((j + 1.0) * log_gamma)
        key_w = jnp.exp((CHUNK - 1.0 - j) * log_gamma)
        chunk_decay = math.exp(CHUNK * log_gamma)
        r0, r1 = h * HEAD_DIM, (h + 1) * HEAD_DIM
        q = _rot_half_rows(bt_ref[r0:r1, :].astype(F32), cos, sin)
        k = _rot_half_rows(bt_ref[GROUP_WIDTH + r0:GROUP_WIDTH + r1, :].astype(F32), cos, sin) * (HEAD_DIM ** -0.5)
        v = bt_ref[2 * GROUP_WIDTH + r0:2 * GROUP_WIDTH + r1, :].astype(F32)
        g = bt_ref[3 * GROUP_WIDTH + r0:3 * GROUP_WIDTH + r1, :].astype(F32)
        st = st_ref[h]
        ys = []
        for c in range(nch):
            sl = slice(c * CHUNK, (c + 1) * CHUNK)
            qc, kc, vc = q[:, sl], k[:, sl], v[:, sl]
            kcb = kc.astype(BF16)
            a_t = lax.dot_general(kcb, qc.astype(BF16), TN, preferred_element_type=F32)
            p_t = (a_t * dec_t).astype(BF16)
            intra = jnp.dot(vc.astype(BF16), p_t, preferred_element_type=F32)
            cross = jnp.dot(st.astype(BF16), (qc * query_w).astype(BF16), preferred_element_type=F32)
            ys.append(intra + cross)
            st = chunk_decay * st + lax.dot_general((vc * key_w).astype(BF16), kcb, NT,
                                                    preferred_element_type=F32)
        st_ref[h] = st
        y = _standardize_rows(jnp.concatenate(ys, axis=1))
        o_ref[r0:r1, :] = jax.nn.silu(g) * y


def _select_rows(rows):
    n = rows[0].shape[1]
    ridx = lax.broadcasted_iota(jnp.int32, (AUG_ROWS, n), 0)
    out = jnp.zeros((AUG_ROWS, n), F32)
    for i, r in enumerate(rows):
        out = jnp.where(ridx == i, jnp.broadcast_to(r, (AUG_ROWS, n)), out)
    return out


def _group(feat, extra):
    n = feat.shape[1]
    pad = HEAD_PAD - feat.shape[0] - extra.shape[0]
    return jnp.concatenate([feat, extra, jnp.zeros((pad, n), F32)], axis=0)


def _split3(x):
    hi = x.astype(BF16).astype(F32)
    mid = (x - hi).astype(BF16).astype(F32)
    lo = (x - hi - mid).astype(BF16).astype(F32)
    return hi, mid, lo


def _logit_bound(qb, kb, kmax_ref, row):
    qnorm = jnp.sqrt(jnp.sum(qb * qb, axis=0, keepdims=True))
    knorm = jnp.sqrt(jnp.max(jnp.sum(kb * kb, axis=0, keepdims=True), axis=1, keepdims=True))
    kmax = jnp.maximum(kmax_ref[row:row + 1, :], knorm)
    kmax_ref[row:row + 1, :] = kmax
    return qnorm * kmax, jnp.sum(qb * kb, axis=0, keepdims=True), qnorm, knorm


def _write_stats(st_ref, per_head):
    stat_row = lax.broadcasted_iota(jnp.int32, (8, 128), 0)
    stat_lane = lax.broadcasted_iota(jnp.int32, (8, 128), 1)
    stats = jnp.zeros((8, 128), F32)
    for h, vals in enumerate(per_head):
        for r, val in enumerate(vals):
            stats = jnp.where((stat_row == r) & (stat_lane == h), val, stats)
    st_ref[0, 0] = stats


def _fox_prep(ct_ref, ft_ref, bf_ref, qt_ref, k_ref, vt_ref, st_ref, carry_ref, kmax_ref):
    tm = ct_ref.shape[1]
    x = ft_ref[...] + bf_ref[...]
    lf = jnp.minimum(x, 0.0) - jnp.log1p(jnp.exp(-jnp.abs(x)))
    hi = lf.astype(BF16)
    mid = (lf - hi.astype(F32)).astype(BF16)
    lo = (lf - hi.astype(F32) - mid.astype(F32)).astype(BF16)
    srow = lax.broadcasted_iota(jnp.int32, (tm, tm), 0)
    tcol = lax.broadcasted_iota(jnp.int32, (tm, tm), 1)
    upper = jnp.where(srow <= tcol, 1.0, 0.0).astype(BF16)
    parts = jnp.dot(jnp.concatenate([hi, mid, lo], axis=0), upper, preferred_element_type=F32)
    cum = parts[0:8] + parts[8:16] + parts[16:24] + carry_ref[...]
    carry_ref[...] = cum[:, tm - 1:tm]

    cum2 = cum * LOG2E
    one = jnp.ones((1, tm), F32)
    ones_row = _select_rows([one])
    stats = []
    for h in range(N_HEADS):
        r0, r1 = h * HEAD_DIM, (h + 1) * HEAD_DIM
        g0, g1 = h * HEAD_PAD, (h + 1) * HEAD_PAD
        q = ct_ref[r0:r1, :].astype(F32) * (HEAD_DIM ** -0.5 * LOG2E)
        k = ct_ref[GROUP_WIDTH + r0:GROUP_WIDTH + r1, :].astype(F32)
        v = ct_ref[2 * GROUP_WIDTH + r0:2 * GROUP_WIDTH + r1, :].astype(F32)
        qb, kb = q.astype(BF16).astype(F32), k.astype(BF16).astype(F32)
        bound, diag, qnorm, knorm = _logit_bound(qb, kb, kmax_ref, h)
        c2 = cum2[h:h + 1]
        q_extra = _select_rows([*_split3(c2 - bound), one, one, one])
        k_extra = _select_rows([one, one, one, *(-part for part in _split3(c2))])
        qt_ref[0, g0:g1, :] = _group(q, q_extra).astype(BF16)
        k_ref[:, g0:g1] = _group(k, k_extra).T.astype(BF16)
        vt_ref[0, h, 0] = _group(v, ones_row).astype(BF16)
        stats.append((
            jnp.max(qnorm, axis=1, keepdims=True),
            knorm,
            jnp.max(c2, axis=1, keepdims=True),
            jnp.min(c2, axis=1, keepdims=True),
            jnp.min(diag, axis=1, keepdims=True),
            jnp.max(bound - diag, axis=1, keepdims=True),
        ))
    _write_stats(st_ref, stats)


def _mla_prep(dt_ref, cos_ref, sin_ref, gq_ref, gkv_ref, wq_ref, wk_ref, wv_ref, qt_ref, k_ref, vt_ref,
              st_ref, kmax_ref):
    tm = dt_ref.shape[1]
    cos, sin = cos_ref[...], sin_ref[...]
    cq = _rms_rows(dt_ref[0:MLA_Q_LORA, :].astype(F32)) * gq_ref[...]
    ckv = _rms_rows(dt_ref[MLA_Q_LORA:MLA_Q_LORA + MLA_KV_LORA, :].astype(F32)) * gkv_ref[...]
    kr = _rot_half_rows(dt_ref[MLA_Q_LORA + MLA_KV_LORA:ROWS_D, :].astype(F32), cos, sin)
    ckv_b = ckv.astype(BF16)
    q_all = jnp.dot(wq_ref[...], cq.astype(BF16), preferred_element_type=F32)
    q_all = q_all * ((MLA_NOPE + MLA_ROPE) ** -0.5 * LOG2E)
    k_all = jnp.dot(wk_ref[...], ckv_b, preferred_element_type=F32)
    v_all = jnp.dot(wv_ref[...], ckv_b, preferred_element_type=F32)
    one = jnp.ones((1, tm), F32)
    ones_row = _select_rows([one])
    k_extra = _select_rows([one, one, one])
    zero = jnp.zeros((1, 1), F32)
    stats = []
    for h in range(N_HEADS):
        r0, r1 = h * HEAD_DIM, (h + 1) * HEAD_DIM
        g0, g1 = h * HEAD_PAD, (h + 1) * HEAD_PAD
        qg = q_all[g0:g1]
        q_rope = _rot_half_rows(qg[MLA_NOPE:MLA_NOPE + MLA_ROPE], cos, sin)
        q = jnp.concatenate([qg[0:MLA_NOPE], q_rope], axis=0)
        k = jnp.concatenate([k_all[r0:r1], kr], axis=0)
        qb, kb = q.astype(BF16).astype(F32), k.astype(BF16).astype(F32)
        bound, diag, _, _ = _logit_bound(qb, kb, kmax_ref, N_HEADS + h)
        q_extra = _select_rows([*_split3(-bound)])
        qt_ref[0, g0:g1, :] = _group(q, q_extra).astype(BF16)
        k_ref[:, g0:g1] = _group(k, k_extra).T.astype(BF16)
        vt_ref[0, h, 0] = _group(v_all[r0:r1], ones_row).astype(BF16)
        stats.append((zero,) * STAT_GAP + (jnp.max(bound - diag, axis=1, keepdims=True),))
    _write_stats(st_ref, stats)


def _front_body(x_ref, g_ref, w_ref, wdf_ref,
                gain_ref, ws_ref, bs_ref,
                cosb_ref, sinb_ref,
                bf_ref,
                cosd_ref, sind_ref, gq_ref, gkv_ref, wq_ref, wk_ref, wv_ref,
                ya_ref, yb_ref,
                qtc_ref, kc_ref, vtc_ref, stats_ref,
                qtd_ref, kd_ref, vtd_ref, statsd_ref,
                at_ref, bt_ref, ct_ref, dt_ref, ft_ref, state_ref, carry_ref, kmax_ref, *, tiles_per_seq):
    @pl.when(pl.program_id(0) % tiles_per_seq == 0)
    def _():
        state_ref[...] = jnp.zeros_like(state_ref)
        carry_ref[...] = jnp.zeros_like(carry_ref)
        kmax_ref[...] = jnp.zeros_like(kmax_ref)

    x = x_ref[...]
    h = (x * lax.rsqrt(jnp.mean(x * x, axis=-1, keepdims=True) + EPS) * g_ref[...]).astype(BF16)

    def proj(w):
        return lax.dot_general(w, h, NT, preferred_element_type=F32)

    half_b = ROWS_B // 2
    zdf = proj(wdf_ref[...])
    dt_ref[...] = zdf[0:ROWS_D].astype(BF16)
    ft_ref[...] = zdf[ROWS_D:ROWS_D + ROWS_F]
    at_ref[...] = proj(w_ref[0:ROWS_A, :]).astype(BF16)
    _mla_prep(dt_ref, cosd_ref, sind_ref, gq_ref, gkv_ref, wq_ref, wk_ref, wv_ref, qtd_ref, kd_ref, vtd_ref,
              statsd_ref, kmax_ref)
    ct_ref[...] = proj(w_ref[ROWS_A + ROWS_B:ROWS_ABC, :]).astype(BF16)
    _sgu(at_ref, gain_ref, ws_ref, bs_ref, ya_ref)
    bt_ref[0:half_b, :] = proj(w_ref[ROWS_A:ROWS_A + half_b, :]).astype(BF16)
    _fox_prep(ct_ref, ft_ref, bf_ref, qtc_ref, kc_ref, vtc_ref, stats_ref, carry_ref, kmax_ref)
    bt_ref[half_b:ROWS_B, :] = proj(w_ref[ROWS_A + half_b:ROWS_A + ROWS_B, :]).astype(BF16)
    _retention(bt_ref, cosb_ref, sinb_ref, yb_ref, state_ref)


def _front(x, g, w_t, wdf_t, gain_col, w_s, b_s, cos_b, sin_b, bf_col,
           cos_d, sin_d, gq_col, gkv_col, wq_t, wk_t, wv_t, batch, layer):
    t = x.shape[0]
    tm = TOKEN_TILE
    ns = t // batch // tm
    const = lambda shape: pl.BlockSpec(shape, lambda i: (0,) * len(shape), pipeline_mode=pl.Buffered(1))
    of_layer = lambda shape: pl.BlockSpec((None,) + shape, lambda i: (layer,) + (0,) * len(shape),
                                          pipeline_mode=pl.Buffered(1))
    rows_t = lambda rows: pl.BlockSpec((rows, tm), lambda i: (0, i))
    table = lambda rows: pl.BlockSpec((rows, tm), lambda i: (0, i % ns))
    k_tile = pl.BlockSpec((tm, N_HEADS * HEAD_PAD), lambda i: (i, 0))
    vt_tile = pl.BlockSpec((1, N_HEADS, 1, HEAD_PAD, tm), lambda i: (i // ns, 0, i % ns, 0, 0))
    stats_tile = pl.BlockSpec((1, 1, 8, 128), lambda i: (i // ns, i % ns, 0, 0))
    qt_tile = pl.BlockSpec((1, N_HEADS * HEAD_PAD, tm), lambda i: (i, 0, 0))
    qkv_shapes = [
        jax.ShapeDtypeStruct((t // tm, N_HEADS * HEAD_PAD, tm), BF16),
        jax.ShapeDtypeStruct((t, N_HEADS * HEAD_PAD), BF16),
        jax.ShapeDtypeStruct((batch, N_HEADS, ns, HEAD_PAD, tm), BF16),
        jax.ShapeDtypeStruct((batch, ns, 8, 128), F32),
    ]
    return pl.pallas_call(
        functools.partial(_front_body, tiles_per_seq=ns),
        grid=(t // tm,),
        in_specs=[
            pl.BlockSpec((tm, D_MODEL), lambda i: (i, 0)),
            const((1, D_MODEL)),
            of_layer((ROWS_ABC, D_MODEL)),
            of_layer((ROWS_D + ROWS_F, D_MODEL)),
            const((GROUP_WIDTH, 1)),
            const((N_HEADS, CHUNK, CHUNK)),
            const((N_HEADS, CHUNK)),
            table(HEAD_DIM // 2), table(HEAD_DIM // 2),
            const((ROWS_F, 1)),
            table(MLA_ROPE // 2), table(MLA_ROPE // 2),
            const((MLA_Q_LORA, 1)),
            const((MLA_KV_LORA, 1)),
            const((N_HEADS * HEAD_PAD, MLA_Q_LORA)),
            const((GROUP_WIDTH, MLA_KV_LORA)),
            const((GROUP_WIDTH, MLA_KV_LORA)),
        ],
        out_specs=[
            rows_t(GROUP_WIDTH), rows_t(GROUP_WIDTH),
            qt_tile, k_tile, vt_tile, stats_tile,
            qt_tile, k_tile, vt_tile, stats_tile,
        ],
        out_shape=[
            jax.ShapeDtypeStruct((GROUP_WIDTH, t), F32),
            jax.ShapeDtypeStruct((GROUP_WIDTH, t), F32),
            *qkv_shapes,
            *qkv_shapes,
        ],
        scratch_shapes=[
            pltpu.VMEM((ROWS_A, tm), BF16),
            pltpu.VMEM((ROWS_B, tm), BF16),
            pltpu.VMEM((ROWS_C, tm), BF16),
            pltpu.VMEM((ROWS_D, tm), BF16),
            pltpu.VMEM((ROWS_F, tm), F32),
            pltpu.VMEM((N_HEADS, HEAD_DIM, HEAD_DIM), F32),
            pltpu.VMEM((ROWS_F, 1), F32),
            pltpu.VMEM((2 * N_HEADS, 1), F32),
        ],
        compiler_params=_params("arbitrary"),
        name="front",
    )(x, g, w_t, wdf_t, gain_col, w_s, b_s, cos_b, sin_b, bf_col,
      cos_d, sin_d, gq_col, gkv_col, wq_t, wk_t, wv_t)


def _first_needed_block(stats_ref, b, qi, nb):
    def stat(j, r, h):
        return stats_ref[((b * nb + j) * N_STATS + r) * N_HEADS + h]

    skipped = jnp.int32(0)
    leading = jnp.bool_(True)
    for j in range(nb - 1):
        zero = j < qi
        for h in range(N_HEADS):
            bound = (stat(qi, STAT_QNORM, h) * stat(j, STAT_KNORM, h)
                     + stat(qi, STAT_CMAX, h) - stat(j, STAT_CMIN, h))
            zero = jnp.logical_and(zero, bound - stat(qi, STAT_DIAG, h) < -EXP2_UNDERFLOW)
        leading = jnp.logical_and(leading, zero)
        skipped = skipped + leading.astype(jnp.int32)
    return skipped


def _attn_body(*refs, banded, online):
    refs = list(refs)
    stats_ref = refs.pop(0) if banded else None
    qt_ref, k_ref, vt_ref, o_ref, acc_ref, s_ref = refs[:6]
    m_ref = refs[6] if online else None
    nb, _, blk = qt_ref.shape
    half = blk // 2
    batch = pl.program_id(0)

    def first_block(qi):
        return _first_needed_block(stats_ref, batch, qi, nb) if banded else jnp.int32(0)

    def logits(qi, kj, h):
        start = pl.multiple_of(kj * blk, blk)
        g0, g1 = h * HEAD_PAD, (h + 1) * HEAD_PAD
        return jnp.dot(k_ref[pl.ds(start, blk), g0:g1], qt_ref[qi, g0:g1, :],
                       preferred_element_type=F32)

    def accumulate(s, kj, h):
        acc = acc_ref[h]
        if online:
            m_old = m_ref[h]
            m_new = jnp.maximum(m_old, jnp.max(s, axis=0, keepdims=True))
            m_ref[h] = m_new
            s = s - m_new
            acc = acc * jnp.exp2(m_old - m_new)
        pv = jnp.dot(vt_ref[0, h, kj, 0:PV_ROWS, :], jnp.exp2(s).astype(BF16), preferred_element_type=F32)
        acc_ref[h] = acc + pv[0:HEAD_DIM + AUG_ROWS]

    s_ref[...] = logits(0, 0, 0)

    def query_block(qi, first):
        acc_ref[...] = jnp.zeros_like(acc_ref)
        if online:
            m_ref[...] = jnp.full(m_ref.shape, -jnp.inf, F32)
        nxt_qi = jnp.minimum(qi + 1, nb - 1)
        nxt_first = first_block(nxt_qi)

        def full_blocks(kj, count):
            s = s_ref[...]
            for d in range(count):
                for h in range(N_HEADS):
                    nxt = (kj + d, h + 1) if h + 1 < N_HEADS else (kj + d + 1, 0)
                    s_next = logits(qi, *nxt)
                    accumulate(s, kj + d, h)
                    s = s_next
            s_ref[...] = s

        n_full = qi - first
        odd = jnp.bitwise_and(n_full, 1)
        pl.when(odd == 1)(lambda: full_blocks(first, 1))

        def pair_step(i, carry):
            full_blocks(first + odd + 2 * i, 2)
            return carry

        lax.fori_loop(0, lax.shift_right_logical(n_full, 1), pair_step, 0)

        if online:
            key_pos = lax.broadcasted_iota(jnp.int32, (blk, blk), 0)
            qry_pos = lax.broadcasted_iota(jnp.int32, (blk, blk), 1)
            visible = key_pos <= qry_pos
            s = s_ref[...]
            for h in range(N_HEADS):
                s_next = logits(qi, qi, h + 1) if h + 1 < N_HEADS else logits(nxt_qi, nxt_first, 0)
                accumulate(jnp.where(visible, s, -jnp.inf), qi, h)
                s = s_next
            s_ref[...] = s
        else:
            start = pl.multiple_of(qi * blk, blk)
            key_pos = lax.broadcasted_iota(jnp.int32, (half, blk), 0)
            qry_pos = lax.broadcasted_iota(jnp.int32, (half, blk), 1)
            visible_a = key_pos <= qry_pos
            visible_b = visible_a[:, 0:half]

            def quadrant_logits(h):
                g0, g1 = h * HEAD_PAD, (h + 1) * HEAD_PAD
                s_a = jnp.dot(k_ref[pl.ds(start, half), g0:g1], qt_ref[qi, g0:g1, :],
                              preferred_element_type=F32)
                s_b = jnp.dot(k_ref[pl.ds(start + half, half), g0:g1], qt_ref[qi, g0:g1, half:blk],
                              preferred_element_type=F32)
                return s_a, s_b

            s_full = s_ref[...]
            s_a, s_b = s_full[0:half], s_full[half:blk, half:blk]
            for h in range(N_HEADS):
                if h + 1 < N_HEADS:
                    s_next = quadrant_logits(h + 1)
                else:
                    s_ref[...] = logits(nxt_qi, nxt_first, 0)
                p_a = jnp.exp2(jnp.where(visible_a, s_a, -jnp.inf)).astype(BF16)
                p_b = jnp.exp2(jnp.where(visible_b, s_b, -jnp.inf)).astype(BF16)
                pv_a = jnp.dot(vt_ref[0, h, qi, 0:PV_ROWS, 0:half], p_a, preferred_element_type=F32)
                pv_b = jnp.dot(vt_ref[0, h, qi, 0:PV_ROWS, half:blk], p_b, preferred_element_type=F32)
                acc_ref[h] = acc_ref[h] + pv_a[0:HEAD_DIM + AUG_ROWS]
                acc_ref[h, :, half:blk] = acc_ref[h, :, half:blk] + pv_b[0:HEAD_DIM + AUG_ROWS]
                if h + 1 < N_HEADS:
                    s_a, s_b = s_next
        for h in range(N_HEADS):
            acc = acc_ref[h]
            o_ref[qi, h * HEAD_DIM:(h + 1) * HEAD_DIM, :] = acc[0:HEAD_DIM] / acc[HEAD_DIM:HEAD_DIM + 1]
        return nxt_first

    lax.fori_loop(0, nb, query_block, jnp.int32(0))


def _attention_call(qt, k, vt, skip_stats, online):
    batch, _, nb, _, blk = vt.shape
    seq = nb * blk
    banded = skip_stats is not None
    in_specs = [
        pl.BlockSpec((nb, N_HEADS * HEAD_PAD, blk), lambda b: (b, 0, 0)),
        pl.BlockSpec((seq, N_HEADS * HEAD_PAD), lambda b: (b, 0)),
        pl.BlockSpec((1, N_HEADS, nb, HEAD_PAD, blk), lambda b: (b, 0, 0, 0, 0)),
    ]
    args = (qt, k, vt)
    if banded:
        in_specs = [pl.BlockSpec(memory_space=pltpu.SMEM)] + in_specs
        args = (skip_stats,) + args
    scratch = [pltpu.VMEM((N_HEADS, HEAD_DIM + AUG_ROWS, blk), F32), pltpu.VMEM((blk, blk), F32)]
    if online:
        scratch.append(pltpu.VMEM((N_HEADS, 1, blk), F32))
    return pl.pallas_call(
        functools.partial(_attn_body, banded=banded, online=online),
        grid=(batch,),
        in_specs=in_specs,
        out_specs=pl.BlockSpec((nb, GROUP_WIDTH, blk), lambda b: (b, 0, 0)),
        out_shape=jax.ShapeDtypeStruct((batch * nb, GROUP_WIDTH, blk), F32),
        scratch_shapes=scratch,
        compiler_params=_params("parallel"),
        name="attention_online" if online else "attention",
    )(*args)


def _attention(qt, k, vt, stats, banded):
    stats = stats[:, :, :N_STATS, :N_HEADS]
    skip_stats = stats.reshape(-1) if banded else None
    bound_is_tight = jnp.max(stats[:, :, STAT_GAP, :]) <= GAP_LIMIT
    return lax.cond(bound_is_tight,
                    lambda: _attention_call(qt, k, vt, skip_stats, online=False),
                    lambda: _attention_call(qt, k, vt, skip_stats, online=True))


def _post_body(x_ref, ya_ref, yb_ref, yc_ref, yd_ref, gg_ref, wo_hbm, gf_ref, wu_hbm, wd_hbm, gl_ref,
               o_ref, wo_ref, wu_ref, wd_ref, stage_tall, stage_wide, sem, *, layer, final):
    n_chunks = D_FF // FF_CHUNK
    tall = [(wo_hbm.at[layer, pl.ds(r, FF_CHUNK), :], wo_ref.at[pl.ds(r, FF_CHUNK), :])
            for r in range(0, D_MODEL, FF_CHUNK)]
    n_out = len(tall)
    tall += [(wd_hbm.at[layer, pl.ds(c * FF_CHUNK, FF_CHUNK), :], wd_ref.at[pl.ds(c * FF_CHUNK, FF_CHUNK), :])
             for c in range(n_chunks)]
    wide = [(wu_hbm.at[layer, :, pl.ds(c * FF_CHUNK, FF_CHUNK)], wu_ref.at[:, pl.ds(c * FF_CHUNK, FF_CHUNK)])
            for c in range(n_chunks)]
    queues = ((tall, stage_tall), (wide, stage_wide))

    def dma(kind, k):
        blocks, stage = queues[kind]
        return pltpu.make_async_copy(blocks[k][0], stage.at[k % 2], sem.at[kind, k % 2])

    def land(kind, k):
        blocks, stage = queues[kind]
        dma(kind, k).wait()
        blocks[k][1][...] = stage[k % 2].astype(BF16)
        if k + 2 < len(blocks):
            dma(kind, k + 2).start()

    def body(load):
        if load:
            for kind in range(2):
                dma(kind, 0).start()
                dma(kind, 1).start()
        ys = []
        for g, y_ref in enumerate((ya_ref, yb_ref, yc_ref, yd_ref)):
            ys.append((_rms_rows(y_ref[...]) * gg_ref[g]).astype(BF16))
        y = jnp.concatenate(ys, axis=0)
        if load:
            for k in range(n_out):
                land(0, k)
        x = x_ref[...] + lax.dot_general(y, wo_ref[...], TN, preferred_element_type=F32)
        h = (x * lax.rsqrt(jnp.mean(x * x, axis=-1, keepdims=True) + EPS) * gf_ref[...]).astype(BF16)
        acc = x
        for c in range(n_chunks):
            c0, c1 = c * FF_CHUNK, (c + 1) * FF_CHUNK
            if load:
                land(1, c)
                land(0, n_out + c)
            a = jnp.maximum(jnp.dot(h, wu_ref[:, c0:c1], preferred_element_type=F32), 0.0)
            acc = acc + jnp.dot((a * a).astype(BF16), wd_ref[c0:c1, :], preferred_element_type=F32)
        if final:
            acc = acc * lax.rsqrt(jnp.mean(acc * acc, axis=-1, keepdims=True) + EPS) * gl_ref[...]
        o_ref[...] = acc

    first_step = pl.program_id(0) == 0
    pl.when(first_step)(functools.partial(body, True))
    pl.when(jnp.logical_not(first_step))(functools.partial(body, False))


def _post(x, ya, yb, yc, yd, gg_col, w_out, g_ffn, w_up, w_down, g_final, layer, final):
    t = x.shape[0]
    tm = TOKEN_TILE
    const = lambda shape: pl.BlockSpec(shape, lambda i: (0,) * len(shape), pipeline_mode=pl.Buffered(1))
    in_hbm = pl.BlockSpec(memory_space=pl.ANY)
    ytile = pl.BlockSpec((GROUP_WIDTH, tm), lambda i: (0, i))
    yblock = pl.BlockSpec((None, GROUP_WIDTH, tm), lambda i: (i, 0, 0))
    return pl.pallas_call(
        functools.partial(_post_body, layer=layer, final=final),
        grid=(t // tm,),
        in_specs=[
            pl.BlockSpec((tm, D_MODEL), lambda i: (i, 0)),
            ytile, ytile, yblock, yblock,
            const((4, GROUP_WIDTH, 1)),
            in_hbm,
            const((1, D_MODEL)),
            in_hbm,
            in_hbm,
            const((1, D_MODEL)),
        ],
        out_specs=pl.BlockSpec((tm, D_MODEL), lambda i: (i, 0)),
        out_shape=jax.ShapeDtypeStruct((t, D_MODEL), F32),
        scratch_shapes=[
            pltpu.VMEM((D_MODEL, D_MODEL), BF16),
            pltpu.VMEM((D_MODEL, D_FF), BF16),
            pltpu.VMEM((D_FF, D_MODEL), BF16),
            pltpu.VMEM((2, FF_CHUNK, D_MODEL), F32),
            pltpu.VMEM((2, D_MODEL, FF_CHUNK), F32),
            pltpu.SemaphoreType.DMA((2, 2)),
        ],
        compiler_params=_params("arbitrary"),
        name="post",
    )(x, ya, yb, yc, yd, gg_col, w_out, g_ffn, w_up, w_down, g_final)


def _rope_tables(seq, half):
    inv_freq = np.power(ROPE_BASE, -np.arange(half, dtype=np.float64) / half)
    ang = inv_freq[:, None] * np.arange(seq, dtype=np.float64)[None, :]
    return jnp.asarray(np.cos(ang), F32), jnp.asarray(np.sin(ang), F32)


def _inproj_weights(w_in):
    wt = jnp.swapaxes(w_in, 1, 2).astype(BF16)
    f = wt[:, ROWS_ABC:ROWS_ABC + N_HEADS]
    d = wt[:, ROWS_ABC + N_HEADS:ROWS_ABC + N_HEADS + ROWS_D]
    pad = jnp.zeros((w_in.shape[0], ROWS_F - N_HEADS, D_MODEL), BF16)
    return wt, jnp.concatenate([d, f, pad], axis=1)


def _mla_weights(w_uq, w_ukv):
    wq = w_uq.T.reshape(N_HEADS, MLA_NOPE + MLA_ROPE, MLA_Q_LORA)
    wq = jnp.pad(wq, ((0, 0), (0, HEAD_PAD - MLA_NOPE - MLA_ROPE), (0, 0)))
    wq = wq.reshape(N_HEADS * HEAD_PAD, MLA_Q_LORA).astype(BF16)
    wkv = w_ukv.T.reshape(N_HEADS, 2 * HEAD_DIM, MLA_KV_LORA)
    wk = wkv[:, :HEAD_DIM].reshape(GROUP_WIDTH, MLA_KV_LORA).astype(BF16)
    wv = wkv[:, HEAD_DIM:].reshape(GROUP_WIDTH, MLA_KV_LORA).astype(BF16)
    return wq, wk, wv


def kernel(x, g_mix_norm, w_in, b_forget, g_sgu, w_spatial, b_spatial, g_mla_q, w_uq, g_mla_kv, w_ukv,
           g_group_out, w_out, g_ffn_norm, w_up, w_down, g_final):
    batch, seq, _ = x.shape
    depth = w_in.shape[0]
    assert seq % TOKEN_TILE == 0 and ATTN_BLOCK == TOKEN_TILE
    cos_b, sin_b = _rope_tables(seq, HEAD_DIM // 2)
    cos_d, sin_d = _rope_tables(seq, MLA_ROPE // 2)
    w_abc, w_df = _inproj_weights(w_in)
    xf = x.reshape(batch * seq, D_MODEL)
    for l in range(depth):
        bf_col = jnp.pad(b_forget[l], (0, ROWS_F - N_HEADS))[:, None]
        ya, yb, qt_c, k_c, vt_c, stats_c, qt_d, k_d, vt_d, stats_d = _front(
            xf, g_mix_norm[l][None, :], w_abc, w_df,
            g_sgu[l][:, None], w_spatial[l], b_spatial[l], cos_b, sin_b, bf_col,
            cos_d, sin_d, g_mla_q[l][:, None], g_mla_kv[l][:, None], *_mla_weights(w_uq[l], w_ukv[l]),
            batch, l)
        yc = _attention(qt_c, k_c, vt_c, stats_c, banded=True)
        yd = _attention(qt_d, k_d, vt_d, stats_d, banded=False)
        xf = _post(xf, ya, yb, yc, yd, g_group_out[l].reshape(4, GROUP_WIDTH, 1),
                   w_out, g_ffn_norm[l][None, :], w_up, w_down, g_final[None, :],
                   layer=l, final=(l == depth - 1))
    return xf.reshape(batch, seq, D_MODEL)
```

```python
import functools
import math

import jax
import jax.numpy as jnp
import numpy as np
from jax import lax
from jax.experimental import pallas as pl
from jax.experimental.pallas import tpu as pltpu

F32 = jnp.float32
BF16 = jnp.bfloat16

D_MODEL = 1024
N_HEADS = 4
HEAD_DIM = 64
GROUP_WIDTH = N_HEADS * HEAD_DIM
CHUNK = 128
MLA_Q_LORA = 256
MLA_KV_LORA = 128
MLA_NOPE = 64
MLA_ROPE = 32
ROPE_BASE = 10000.0
D_FF = 4 * D_MODEL
EPS = 1e-6

HEAD_PAD = 128
AUG_ROWS = 8
PV_ROWS = 80
LOG2E = math.log2(math.e)
EXP2_UNDERFLOW = 152.0
(STAT_QNORM, STAT_CMAX, STAT_DIAG, STAT_GAP,
 STAT_KNORM_LO, STAT_KNORM_HI, STAT_CMIN_LO, STAT_CMIN_HI) = range(8)
N_STATS = 8
GAP_LIMIT = 64.0
TOKEN_TILE = 512
ATTN_BLOCK = TOKEN_TILE
FF_CHUNK = 512
VMEM_LIMIT = 56 * 1024 * 1024

ROWS_A = 2 * GROUP_WIDTH
ROWS_B = 4 * GROUP_WIDTH
ROWS_C = 3 * GROUP_WIDTH
ROWS_D = MLA_Q_LORA + MLA_KV_LORA + MLA_ROPE
ROWS_F = 8
ROWS_ABC = ROWS_A + ROWS_B + ROWS_C

NT = (((1,), (1,)), ((), ()))
TN = (((0,), (0,)), ((), ()))


def _params(*sem):
    return pltpu.CompilerParams(dimension_semantics=sem, vmem_limit_bytes=VMEM_LIMIT)


def _rot_half_rows(t, cos, sin):
    half = t.shape[0] // 2
    t1, t2 = t[:half], t[half:]
    return jnp.concatenate([t1 * cos - t2 * sin, t1 * sin + t2 * cos], axis=0)


def _standardize_rows(t):
    mu = jnp.mean(t, axis=0, keepdims=True)
    var = jnp.mean(jnp.square(t - mu), axis=0, keepdims=True)
    return (t - mu) * lax.rsqrt(var + EPS)


def _rms_rows(t):
    return t * lax.rsqrt(jnp.mean(t * t, axis=0, keepdims=True) + EPS)


def _sgu(at_ref, gain_ref, ws_ref, bs_ref, o_ref):
    tm = at_ref.shape[1]
    nch = tm // CHUNK
    row = lax.broadcasted_iota(jnp.int32, (CHUNK, CHUNK), 0)
    col = lax.broadcasted_iota(jnp.int32, (CHUNK, CHUNK), 1)
    for h in range(N_HEADS):
        r0, r1 = h * HEAD_DIM, (h + 1) * HEAD_DIM
        u = jax.nn.gelu(at_ref[r0:r1, :].astype(F32))
        v = jax.nn.gelu(at_ref[GROUP_WIDTH + r0:GROUP_WIDTH + r1, :].astype(F32))
        v = _standardize_rows(v) * gain_ref[r0:r1, :]
        w = jnp.where(col <= row, ws_ref[h], 0.0).astype(BF16)
        vs = jnp.concatenate([v[:, c * CHUNK:(c + 1) * CHUNK] for c in range(nch)], axis=0).astype(BF16)
        mixed = lax.dot_general(vs, w, NT, preferred_element_type=F32) + bs_ref[h:h + 1, :]
        for c in range(nch):
            o_ref[r0:r1, c * CHUNK:(c + 1) * CHUNK] = (
                u[:, c * CHUNK:(c + 1) * CHUNK] * mixed[c * HEAD_DIM:(c + 1) * HEAD_DIM])


def _retention(bt_ref, cos_ref, sin_ref, o_ref, st_ref):
    tm = bt_ref.shape[1]
    nch = tm // CHUNK
    cos, sin = cos_ref[...], sin_ref[...]
    srow = lax.broadcasted_iota(jnp.int32, (CHUNK, CHUNK), 0)
    tcol = lax.broadcasted_iota(jnp.int32, (CHUNK, CHUNK), 1)
    rel = (tcol - srow).astype(F32)
    j = lax.broadcasted_iota(jnp.int32, (1, CHUNK), 1).astype(F32)
    for h in range(N_HEADS):
        log_gamma = math.log1p(-(2.0 ** (-5.0 - h)))
        dec_t = jnp.where(rel >= 0, jnp.exp(jnp.maximum(rel, 0.0) * log_gamma), 0.0)
        query_w = jnp.exp((j + 1.0) * log_gamma)
        key_w = jnp.exp((CHUNK - 1.0 - j) * log_gamma)
        chunk_decay = math.exp(CHUNK * log_gamma)
        r0, r1 = h * HEAD_DIM, (h + 1) * HEAD_DIM
        q = _rot_half_rows(bt_ref[r0:r1, :].astype(F32), cos, sin)
        k = _rot_half_rows(bt_ref[GROUP_WIDTH + r0:GROUP_WIDTH + r1, :].astype(F32), cos, sin) * (HEAD_DIM ** -0.5)
        v = bt_ref[2 * GROUP_WIDTH + r0:2 * GROUP_WIDTH + r1, :].astype(F32)
        g = bt_ref[3 * GROUP_WIDTH + r0:3 * GROUP_WIDTH + r1, :].astype(F32)
        st = st_ref[h]
        ys = []
        for c in range(nch):
            sl = slice(c * CHUNK, (c + 1) * CHUNK)
            qc, kc, vc = q[:, sl], k[:, sl], v[:, sl]
            kcb = kc.astype(BF16)
            a_t = lax.dot_general(kcb, qc.astype(BF16), TN, preferred_element_type=F32)
            p_t = (a_t * dec_t).astype(BF16)
            intra = jnp.dot(vc.astype(BF16), p_t, preferred_element_type=F32)
            cross = jnp.dot(st.astype(BF16), (qc * query_w).astype(BF16), preferred_element_type=F32)
            ys.append(intra + cross)
            st = chunk_decay * st + lax.dot_general((vc * key_w).astype(BF16), kcb, NT,
                                                    preferred_element_type=F32)
        st_ref[h] = st
        y = _standardize_rows(jnp.concatenate(ys, axis=1))
        o_ref[r0:r1, :] = jax.nn.silu(g) * y


def _select_rows(rows):
    n = rows[0].shape[1]
    ridx = lax.broadcasted_iota(jnp.int32, (AUG_ROWS, n), 0)
    out = jnp.zeros((AUG_ROWS, n), F32)
    for i, r in enumerate(rows):
        out = jnp.where(ridx == i, jnp.broadcast_to(r, (AUG_ROWS, n)), out)
    return out


def _group(feat, extra):
    n = feat.shape[1]
    pad = HEAD_PAD - feat.shape[0] - extra.shape[0]
    return jnp.concatenate([feat, extra, jnp.zeros((pad, n), F32)], axis=0)


def _split3(x):
    hi = x.astype(BF16).astype(F32)
    mid = (x - hi).astype(BF16).astype(F32)
    lo = (x - hi - mid).astype(BF16).astype(F32)
    return hi, mid, lo


def _logit_bound(qb, kb, kmax_ref, row):
    qnorm = jnp.sqrt(jnp.sum(qb * qb, axis=0, keepdims=True))
    ksq = jnp.sum(kb * kb, axis=0, keepdims=True)
    kmax = jnp.maximum(kmax_ref[row:row + 1, :], jnp.sqrt(jnp.max(ksq, axis=1, keepdims=True)))
    kmax_ref[row:row + 1, :] = kmax
    return qnorm * kmax, jnp.sum(qb * kb, axis=0, keepdims=True), qnorm, ksq


def _write_stats(st_ref, per_head):
    stat_row = lax.broadcasted_iota(jnp.int32, (8, 128), 0)
    stat_lane = lax.broadcasted_iota(jnp.int32, (8, 128), 1)
    stats = jnp.zeros((8, 128), F32)
    for h, vals in enumerate(per_head):
        for r, val in enumerate(vals):
            stats = jnp.where((stat_row == r) & (stat_lane == h), val, stats)
    st_ref[0, 0] = stats


def _fox_prep(ct_ref, ft_ref, bf_ref, qt_ref, k_ref, vt_ref, st_ref, carry_ref, kmax_ref):
    tm = ct_ref.shape[1]
    x = ft_ref[...] + bf_ref[...]
    lf = jnp.minimum(x, 0.0) - jnp.log1p(jnp.exp(-jnp.abs(x)))
    hi = lf.astype(BF16)
    mid = (lf - hi.astype(F32)).astype(BF16)
    lo = (lf - hi.astype(F32) - mid.astype(F32)).astype(BF16)
    srow = lax.broadcasted_iota(jnp.int32, (tm, tm), 0)
    tcol = lax.broadcasted_iota(jnp.int32, (tm, tm), 1)
    upper = jnp.where(srow <= tcol, 1.0, 0.0).astype(BF16)
    parts = jnp.dot(jnp.concatenate([hi, mid, lo], axis=0), upper, preferred_element_type=F32)
    cum = parts[0:8] + parts[8:16] + parts[16:24] + carry_ref[...]
    carry_ref[...] = cum[:, tm - 1:tm]

    cum2 = cum * LOG2E
    one = jnp.ones((1, tm), F32)
    ones_row = _select_rows([one])
    stats = []
    for h in range(N_HEADS):
        r0, r1 = h * HEAD_DIM, (h + 1) * HEAD_DIM
        g0, g1 = h * HEAD_PAD, (h + 1) * HEAD_PAD
        q = ct_ref[r0:r1, :].astype(F32) * (HEAD_DIM ** -0.5 * LOG2E)
        k = ct_ref[GROUP_WIDTH + r0:GROUP_WIDTH + r1, :].astype(F32)
        v = ct_ref[2 * GROUP_WIDTH + r0:2 * GROUP_WIDTH + r1, :].astype(F32)
        qb, kb = q.astype(BF16).astype(F32), k.astype(BF16).astype(F32)
        bound, diag, qnorm, ksq = _logit_bound(qb, kb, kmax_ref, h)
        c2 = cum2[h:h + 1]
        q_extra = _select_rows([*_split3(c2 - bound), one, one, one])
        k_extra = _select_rows([one, one, one, *(-part for part in _split3(c2))])
        qt_ref[0, g0:g1, :] = _group(q, q_extra).astype(BF16)
        k_ref[:, g0:g1] = _group(k, k_extra).T.astype(BF16)
        vt_ref[0, h, 0] = _group(v, ones_row).astype(BF16)
        lo, hi = slice(0, tm // 2), slice(tm // 2, tm)
        stats.append((
            jnp.max(qnorm, axis=1, keepdims=True),
            jnp.max(c2, axis=1, keepdims=True),
            jnp.min(diag, axis=1, keepdims=True),
            jnp.max(bound - diag, axis=1, keepdims=True),
            jnp.sqrt(jnp.max(ksq[:, lo], axis=1, keepdims=True)),
            jnp.sqrt(jnp.max(ksq[:, hi], axis=1, keepdims=True)),
            jnp.min(c2[:, lo], axis=1, keepdims=True),
            jnp.min(c2[:, hi], axis=1, keepdims=True),
        ))
    _write_stats(st_ref, stats)


def _mla_prep(dt_ref, cos_ref, sin_ref, gq_ref, gkv_ref, wq_ref, wk_ref, wv_ref, qt_ref, k_ref, vt_ref,
              st_ref, kmax_ref):
    tm = dt_ref.shape[1]
    cos, sin = cos_ref[...], sin_ref[...]
    cq = _rms_rows(dt_ref[0:MLA_Q_LORA, :].astype(F32)) * gq_ref[...]
    ckv = _rms_rows(dt_ref[MLA_Q_LORA:MLA_Q_LORA + MLA_KV_LORA, :].astype(F32)) * gkv_ref[...]
    kr = _rot_half_rows(dt_ref[MLA_Q_LORA + MLA_KV_LORA:ROWS_D, :].astype(F32), cos, sin)
    ckv_b = ckv.astype(BF16)
    q_all = jnp.dot(wq_ref[...], cq.astype(BF16), preferred_element_type=F32)
    q_all = q_all * ((MLA_NOPE + MLA_ROPE) ** -0.5 * LOG2E)
    k_all = jnp.dot(wk_ref[...], ckv_b, preferred_element_type=F32)
    v_all = jnp.dot(wv_ref[...], ckv_b, preferred_element_type=F32)
    one = jnp.ones((1, tm), F32)
    ones_row = _select_rows([one])
    k_extra = _select_rows([one, one, one])
    zero = jnp.zeros((1, 1), F32)
    stats = []
    for h in range(N_HEADS):
        r0, r1 = h * HEAD_DIM, (h + 1) * HEAD_DIM
        g0, g1 = h * HEAD_PAD, (h + 1) * HEAD_PAD
        qg = q_all[g0:g1]
        q_rope = _rot_half_rows(qg[MLA_NOPE:MLA_NOPE + MLA_ROPE], cos, sin)
        q = jnp.concatenate([qg[0:MLA_NOPE], q_rope], axis=0)
        k = jnp.concatenate([k_all[r0:r1], kr], axis=0)
        qb, kb = q.astype(BF16).astype(F32), k.astype(BF16).astype(F32)
        bound, diag, _, _ = _logit_bound(qb, kb, kmax_ref, N_HEADS + h)
        q_extra = _select_rows([*_split3(-bound)])
        qt_ref[0, g0:g1, :] = _group(q, q_extra).astype(BF16)
        k_ref[:, g0:g1] = _group(k, k_extra).T.astype(BF16)
        vt_ref[0, h, 0] = _group(v_all[r0:r1], ones_row).astype(BF16)
        stats.append((zero,) * STAT_GAP + (jnp.max(bound - diag, axis=1, keepdims=True),))
    _write_stats(st_ref, stats)


def _front_body(x_ref, g_ref, w_ref, wdf_ref,
                gain_ref, ws_ref, bs_ref,
                cosb_ref, sinb_ref,
                bf_ref,
                cosd_ref, sind_ref, gq_ref, gkv_ref, wq_ref, wk_ref, wv_ref,
                ya_ref, yb_ref,
                qtc_ref, kc_ref, vtc_ref, stats_ref,
                qtd_ref, kd_ref, vtd_ref, statsd_ref,
                at_ref, bt_ref, ct_ref, dt_ref, ft_ref, state_ref, carry_ref, kmax_ref, *, tiles_per_seq):
    @pl.when(pl.program_id(0) % tiles_per_seq == 0)
    def _():
        state_ref[...] = jnp.zeros_like(state_ref)
        carry_ref[...] = jnp.zeros_like(carry_ref)
        kmax_ref[...] = jnp.zeros_like(kmax_ref)

    x = x_ref[...]
    h = (x * lax.rsqrt(jnp.mean(x * x, axis=-1, keepdims=True) + EPS) * g_ref[...]).astype(BF16)

    def proj(w):
        return lax.dot_general(w, h, NT, preferred_element_type=F32)

    half_b = ROWS_B // 2
    zdf = proj(wdf_ref[...])
    dt_ref[...] = zdf[0:ROWS_D].astype(BF16)
    ft_ref[...] = zdf[ROWS_D:ROWS_D + ROWS_F]
    at_ref[...] = proj(w_ref[0:ROWS_A, :]).astype(BF16)
    _mla_prep(dt_ref, cosd_ref, sind_ref, gq_ref, gkv_ref, wq_ref, wk_ref, wv_ref, qtd_ref, kd_ref, vtd_ref,
              statsd_ref, kmax_ref)
    ct_ref[...] = proj(w_ref[ROWS_A + ROWS_B:ROWS_ABC, :]).astype(BF16)
    _sgu(at_ref, gain_ref, ws_ref, bs_ref, ya_ref)
    bt_ref[0:half_b, :] = proj(w_ref[ROWS_A:ROWS_A + half_b, :]).astype(BF16)
    _fox_prep(ct_ref, ft_ref, bf_ref, qtc_ref, kc_ref, vtc_ref, stats_ref, carry_ref, kmax_ref)
    bt_ref[half_b:ROWS_B, :] = proj(w_ref[ROWS_A + half_b:ROWS_A + ROWS_B, :]).astype(BF16)
    _retention(bt_ref, cosb_ref, sinb_ref, yb_ref, state_ref)


def _front(x, g, w_t, wdf_t, gain_col, w_s, b_s, cos_b, sin_b, bf_col,
           cos_d, sin_d, gq_col, gkv_col, wq_t, wk_t, wv_t, batch, layer):
    t = x.shape[0]
    tm = TOKEN_TILE
    ns = t // batch // tm
    const = lambda shape: pl.BlockSpec(shape, lambda i: (0,) * len(shape), pipeline_mode=pl.Buffered(1))
    of_layer = lambda shape: pl.BlockSpec((None,) + shape, lambda i: (layer,) + (0,) * len(shape),
                                          pipeline_mode=pl.Buffered(1))
    rows_t = lambda rows: pl.BlockSpec((rows, tm), lambda i: (0, i))
    table = lambda rows: pl.BlockSpec((rows, tm), lambda i: (0, i % ns))
    k_tile = pl.BlockSpec((tm, N_HEADS * HEAD_PAD), lambda i: (i, 0))
    vt_tile = pl.BlockSpec((1, N_HEADS, 1, HEAD_PAD, tm), lambda i: (i // ns, 0, i % ns, 0, 0))
    stats_tile = pl.BlockSpec((1, 1, 8, 128), lambda i: (i // ns, i % ns, 0, 0))
    qt_tile = pl.BlockSpec((1, N_HEADS * HEAD_PAD, tm), lambda i: (i, 0, 0))
    qkv_shapes = [
        jax.ShapeDtypeStruct((t // tm, N_HEADS * HEAD_PAD, tm), BF16),
        jax.ShapeDtypeStruct((t, N_HEADS * HEAD_PAD), BF16),
        jax.ShapeDtypeStruct((batch, N_HEADS, ns, HEAD_PAD, tm), BF16),
        jax.ShapeDtypeStruct((batch, ns, 8, 128), F32),
    ]
    return pl.pallas_call(
        functools.partial(_front_body, tiles_per_seq=ns),
        grid=(t // tm,),
        in_specs=[
            pl.BlockSpec((tm, D_MODEL), lambda i: (i, 0)),
            const((1, D_MODEL)),
            of_layer((ROWS_ABC, D_MODEL)),
            of_layer((ROWS_D + ROWS_F, D_MODEL)),
            const((GROUP_WIDTH, 1)),
            const((N_HEADS, CHUNK, CHUNK)),
            const((N_HEADS, CHUNK)),
            table(HEAD_DIM // 2), table(HEAD_DIM // 2),
            const((ROWS_F, 1)),
            table(MLA_ROPE // 2), table(MLA_ROPE // 2),
            const((MLA_Q_LORA, 1)),
            const((MLA_KV_LORA, 1)),
            const((N_HEADS * HEAD_PAD, MLA_Q_LORA)),
            const((GROUP_WIDTH, MLA_KV_LORA)),
            const((GROUP_WIDTH, MLA_KV_LORA)),
        ],
        out_specs=[
            rows_t(GROUP_WIDTH), rows_t(GROUP_WIDTH),
            qt_tile, k_tile, vt_tile, stats_tile,
            qt_tile, k_tile, vt_tile, stats_tile,
        ],
        out_shape=[
            jax.ShapeDtypeStruct((GROUP_WIDTH, t), F32),
            jax.ShapeDtypeStruct((GROUP_WIDTH, t), F32),
            *qkv_shapes,
            *qkv_shapes,
        ],
        scratch_shapes=[
            pltpu.VMEM((ROWS_A, tm), BF16),
            pltpu.VMEM((ROWS_B, tm), BF16),
            pltpu.VMEM((ROWS_C, tm), BF16),
            pltpu.VMEM((ROWS_D, tm), BF16),
            pltpu.VMEM((ROWS_F, tm), F32),
            pltpu.VMEM((N_HEADS, HEAD_DIM, HEAD_DIM), F32),
            pltpu.VMEM((ROWS_F, 1), F32),
            pltpu.VMEM((2 * N_HEADS, 1), F32),
        ],
        compiler_params=_params("arbitrary"),
        name="front",
    )(x, g, w_t, wdf_t, gain_col, w_s, b_s, cos_b, sin_b, bf_col,
      cos_d, sin_d, gq_col, gkv_col, wq_t, wk_t, wv_t)


def _leading_zero_half_blocks(stats_ref, b, qi, nb):
    def stat(j, r, h):
        return stats_ref[((b * nb + j) * N_STATS + r) * N_HEADS + h]

    skipped = jnp.int32(0)
    leading = jnp.bool_(True)
    for j in range(nb - 1):
        for knorm_row, cmin_row in ((STAT_KNORM_LO, STAT_CMIN_LO), (STAT_KNORM_HI, STAT_CMIN_HI)):
            zero = j < qi
            for h in range(N_HEADS):
                bound = (stat(qi, STAT_QNORM, h) * stat(j, knorm_row, h)
                         + stat(qi, STAT_CMAX, h) - stat(j, cmin_row, h))
                zero = jnp.logical_and(zero, bound - stat(qi, STAT_DIAG, h) < -EXP2_UNDERFLOW)
            leading = jnp.logical_and(leading, zero)
            skipped = skipped + leading.astype(jnp.int32)
    return skipped


def _attn_body(*refs, banded, online):
    refs = list(refs)
    stats_ref = refs.pop(0) if banded else None
    qt_ref, k_ref, vt_ref, o_ref, acc_ref, s_ref = refs[:6]
    m_ref = refs[6] if online else None
    nb, _, blk = qt_ref.shape
    half = blk // 2
    batch = pl.program_id(0)

    def skipped_half_blocks(qi):
        return _leading_zero_half_blocks(stats_ref, batch, qi, nb) if banded else jnp.int32(0)

    def logits(qi, kj, h, rows=blk, skip=0):
        start = pl.multiple_of(kj * blk + skip, rows)
        g0, g1 = h * HEAD_PAD, (h + 1) * HEAD_PAD
        return jnp.dot(k_ref[pl.ds(start, rows), g0:g1], qt_ref[qi, g0:g1, :], preferred_element_type=F32)

    def accumulate(s, kj, h, keys=slice(None)):
        acc = acc_ref[h]
        if online:
            m_old = m_ref[h]
            m_new = jnp.maximum(m_old, jnp.max(s, axis=0, keepdims=True))
            m_ref[h] = m_new
            s = s - m_new
            acc = acc * jnp.exp2(m_old - m_new)
        pv = jnp.dot(vt_ref[0, h, kj, 0:PV_ROWS, keys], jnp.exp2(s).astype(BF16), preferred_element_type=F32)
        acc_ref[h] = acc + pv[0:HEAD_DIM + AUG_ROWS]

    s_ref[...] = logits(0, 0, 0)

    def query_block(qi, skipped):
        acc_ref[...] = jnp.zeros_like(acc_ref)
        if online:
            m_ref[...] = jnp.full(m_ref.shape, -jnp.inf, F32)
        nxt_qi = jnp.minimum(qi + 1, nb - 1)
        nxt_skipped = skipped_half_blocks(nxt_qi)
        nxt_first = lax.shift_right_logical(nxt_skipped, 1)
        first = lax.shift_right_logical(skipped, 1)

        def full_blocks(kj, count):
            s = s_ref[...]
            for d in range(count):
                for h in range(N_HEADS):
                    nxt = (kj + d, h + 1) if h + 1 < N_HEADS else (kj + d + 1, 0)
                    s_next = logits(qi, *nxt)
                    accumulate(s, kj + d, h)
                    s = s_next
            s_ref[...] = s

        if banded:
            starts_half = jnp.bitwise_and(skipped, 1)

            def second_half(kj):
                s = s_ref[half:blk, :]
                for h in range(N_HEADS):
                    s_next = (logits(qi, kj, h + 1, rows=half, skip=half) if h + 1 < N_HEADS
                              else logits(qi, kj + 1, 0))
                    accumulate(s, kj, h, keys=slice(half, blk))
                    s = s_next
                s_ref[...] = s

            pl.when(starts_half == 1)(lambda: second_half(first))
            first = first + starts_half

        n_full = qi - first
        odd = jnp.bitwise_and(n_full, 1)
        pl.when(odd == 1)(lambda: full_blocks(first, 1))

        def pair_step(i, carry):
            full_blocks(first + odd + 2 * i, 2)
            return carry

        lax.fori_loop(0, lax.shift_right_logical(n_full, 1), pair_step, 0)

        if online:
            key_pos = lax.broadcasted_iota(jnp.int32, (blk, blk), 0)
            qry_pos = lax.broadcasted_iota(jnp.int32, (blk, blk), 1)
            visible = key_pos <= qry_pos
            s = s_ref[...]
            for h in range(N_HEADS):
                s_next = logits(qi, qi, h + 1) if h + 1 < N_HEADS else logits(nxt_qi, nxt_first, 0)
                accumulate(jnp.where(visible, s, -jnp.inf), qi, h)
                s = s_next
            s_ref[...] = s
        else:
            start = pl.multiple_of(qi * blk, blk)
            key_pos = lax.broadcasted_iota(jnp.int32, (half, blk), 0)
            qry_pos = lax.broadcasted_iota(jnp.int32, (half, blk), 1)
            visible_a = key_pos <= qry_pos
            visible_b = visible_a[:, 0:half]

            def quadrant_logits(h):
                g0, g1 = h * HEAD_PAD, (h + 1) * HEAD_PAD
                s_a = jnp.dot(k_ref[pl.ds(start, half), g0:g1], qt_ref[qi, g0:g1, :],
                              preferred_element_type=F32)
                s_b = jnp.dot(k_ref[pl.ds(start + half, half), g0:g1], qt_ref[qi, g0:g1, half:blk],
                              preferred_element_type=F32)
                return s_a, s_b

            s_full = s_ref[...]
            s_a, s_b = s_full[0:half], s_full[half:blk, half:blk]
            for h in range(N_HEADS):
                if h + 1 < N_HEADS:
                    s_next = quadrant_logits(h + 1)
                else:
                    s_ref[...] = logits(nxt_qi, nxt_first, 0)
                p_a = jnp.exp2(jnp.where(visible_a, s_a, -jnp.inf)).astype(BF16)
                p_b = jnp.exp2(jnp.where(visible_b, s_b, -jnp.inf)).astype(BF16)
                pv_a = jnp.dot(vt_ref[0, h, qi, 0:PV_ROWS, 0:half], p_a, preferred_element_type=F32)
                pv_b = jnp.dot(vt_ref[0, h, qi, 0:PV_ROWS, half:blk], p_b, preferred_element_type=F32)
                acc_ref[h] = acc_ref[h] + pv_a[0:HEAD_DIM + AUG_ROWS]
                acc_ref[h, :, half:blk] = acc_ref[h, :, half:blk] + pv_b[0:HEAD_DIM + AUG_ROWS]
                if h + 1 < N_HEADS:
                    s_a, s_b = s_next
        for h in range(N_HEADS):
            acc = acc_ref[h]
            o_ref[qi, h * HEAD_DIM:(h + 1) * HEAD_DIM, :] = acc[0:HEAD_DIM] / acc[HEAD_DIM:HEAD_DIM + 1]
        return nxt_skipped

    lax.fori_loop(0, nb, query_block, jnp.int32(0))


def _attention_call(qt, k, vt, skip_stats, online):
    batch, _, nb, _, blk = vt.shape
    seq = nb * blk
    banded = skip_stats is not None
    in_specs = [
        pl.BlockSpec((nb, N_HEADS * HEAD_PAD, blk), lambda b: (b, 0, 0)),
        pl.BlockSpec((seq, N_HEADS * HEAD_PAD), lambda b: (b, 0)),
        pl.BlockSpec((1, N_HEADS, nb, HEAD_PAD, blk), lambda b: (b, 0, 0, 0, 0)),
    ]
    args = (qt, k, vt)
    if banded:
        in_specs = [pl.BlockSpec(memory_space=pltpu.SMEM)] + in_specs
        args = (skip_stats,) + args
    scratch = [pltpu.VMEM((N_HEADS, HEAD_DIM + AUG_ROWS, blk), F32), pltpu.VMEM((blk, blk), F32)]
    if online:
        scratch.append(pltpu.VMEM((N_HEADS, 1, blk), F32))
    return pl.pallas_call(
        functools.partial(_attn_body, banded=banded, online=online),
        grid=(batch,),
        in_specs=in_specs,
        out_specs=pl.BlockSpec((nb, GROUP_WIDTH, blk), lambda b: (b, 0, 0)),
        out_shape=jax.ShapeDtypeStruct((batch * nb, GROUP_WIDTH, blk), F32),
        scratch_shapes=scratch,
        compiler_params=_params("parallel"),
        name="attention_online" if online else "attention",
    )(*args)


def _attention(qt, k, vt, stats, banded):
    stats = stats[:, :, :N_STATS, :N_HEADS]
    skip_stats = stats.reshape(-1) if banded else None
    bound_is_tight = jnp.max(stats[:, :, STAT_GAP, :]) <= GAP_LIMIT
    return lax.cond(bound_is_tight,
                    lambda: _attention_call(qt, k, vt, skip_stats, online=False),
                    lambda: _attention_call(qt, k, vt, skip_stats, online=True))


def _post_body(x_ref, ya_ref, yb_ref, yc_ref, yd_ref, gg_ref, wo_hbm, gf_ref, wu_hbm, wd_hbm, gl_ref,
               o_ref, wo_ref, wu_ref, wd_ref, stage_tall, stage_wide, sem, *, layer, final):
    n_chunks = D_FF // FF_CHUNK
    tall = [(wo_hbm.at[layer, pl.ds(r, FF_CHUNK), :], wo_ref.at[pl.ds(r, FF_CHUNK), :])
            for r in range(0, D_MODEL, FF_CHUNK)]
    n_out = len(tall)
    tall += [(wd_hbm.at[layer, pl.ds(c * FF_CHUNK, FF_CHUNK), :], wd_ref.at[pl.ds(c * FF_CHUNK, FF_CHUNK), :])
             for c in range(n_chunks)]
    wide = [(wu_hbm.at[layer, :, pl.ds(c * FF_CHUNK, FF_CHUNK)], wu_ref.at[:, pl.ds(c * FF_CHUNK, FF_CHUNK)])
            for c in range(n_chunks)]
    queues = ((tall, stage_tall), (wide, stage_wide))

    def dma(kind, k):
        blocks, stage = queues[kind]
        return pltpu.make_async_copy(blocks[k][0], stage.at[k % 2], sem.at[kind, k % 2])

    def land(kind, k):
        blocks, stage = queues[kind]
        dma(kind, k).wait()
        blocks[k][1][...] = stage[k % 2].astype(BF16)
        if k + 2 < len(blocks):
            dma(kind, k + 2).start()

    def body(load):
        if load:
            for kind in range(2):
                dma(kind, 0).start()
                dma(kind, 1).start()
        ys = []
        for g, y_ref in enumerate((ya_ref, yb_ref, yc_ref, yd_ref)):
            ys.append((_rms_rows(y_ref[...]) * gg_ref[g]).astype(BF16))
        y = jnp.concatenate(ys, axis=0)
        if load:
            for k in range(n_out):
                land(0, k)
        x = x_ref[...] + lax.dot_general(y, wo_ref[...], TN, preferred_element_type=F32)
        h = (x * lax.rsqrt(jnp.mean(x * x, axis=-1, keepdims=True) + EPS) * gf_ref[...]).astype(BF16)
        acc = x
        for c in range(n_chunks):
            c0, c1 = c * FF_CHUNK, (c + 1) * FF_CHUNK
            if load:
                land(1, c)
                land(0, n_out + c)
            a = jnp.maximum(jnp.dot(h, wu_ref[:, c0:c1], preferred_element_type=F32), 0.0)
            acc = acc + jnp.dot((a * a).astype(BF16), wd_ref[c0:c1, :], preferred_element_type=F32)
        if final:
            acc = acc * lax.rsqrt(jnp.mean(acc * acc, axis=-1, keepdims=True) + EPS) * gl_ref[...]
        o_ref[...] = acc

    first_step = pl.program_id(0) == 0
    pl.when(first_step)(functools.partial(body, True))
    pl.when(jnp.logical_not(first_step))(functools.partial(body, False))


def _post(x, ya, yb, yc, yd, gg_col, w_out, g_ffn, w_up, w_down, g_final, layer, final):
    t = x.shape[0]
    tm = TOKEN_TILE
    const = lambda shape: pl.BlockSpec(shape, lambda i: (0,) * len(shape), pipeline_mode=pl.Buffered(1))
    in_hbm = pl.BlockSpec(memory_space=pl.ANY)
    ytile = pl.BlockSpec((GROUP_WIDTH, tm), lambda i: (0, i))
    yblock = pl.BlockSpec((None, GROUP_WIDTH, tm), lambda i: (i, 0, 0))
    return pl.pallas_call(
        functools.partial(_post_body, layer=layer, final=final),
        grid=(t // tm,),
        in_specs=[
            pl.BlockSpec((tm, D_MODEL), lambda i: (i, 0)),
            ytile, ytile, yblock, yblock,
            const((4, GROUP_WIDTH, 1)),
            in_hbm,
            const((1, D_MODEL)),
            in_hbm,
            in_hbm,
            const((1, D_MODEL)),
        ],
        out_specs=pl.BlockSpec((tm, D_MODEL), lambda i: (i, 0)),
        out_shape=jax.ShapeDtypeStruct((t, D_MODEL), F32),
        scratch_shapes=[
            pltpu.VMEM((D_MODEL, D_MODEL), BF16),
            pltpu.VMEM((D_MODEL, D_FF), BF16),
            pltpu.VMEM((D_FF, D_MODEL), BF16),
            pltpu.VMEM((2, FF_CHUNK, D_MODEL), F32),
            pltpu.VMEM((2, D_MODEL, FF_CHUNK), F32),
            pltpu.SemaphoreType.DMA((2, 2)),
        ],
        compiler_params=_params("arbitrary"),
        name="post",
    )(x, ya, yb, yc, yd, gg_col, w_out, g_ffn, w_up, w_down, g_final)


def _rope_tables(seq, half):
    inv_freq = np.power(ROPE_BASE, -np.arange(half, dtype=np.float64) / half)
    ang = inv_freq[:, None] * np.arange(seq, dtype=np.float64)[None, :]
    return jnp.asarray(np.cos(ang), F32), jnp.asarray(np.sin(ang), F32)


def _inproj_weights(w_in):
    wt = jnp.swapaxes(w_in, 1, 2).astype(BF16)
    f = wt[:, ROWS_ABC:ROWS_ABC + N_HEADS]
    d = wt[:, ROWS_ABC + N_HEADS:ROWS_ABC + N_HEADS + ROWS_D]
    pad = jnp.zeros((w_in.shape[0], ROWS_F - N_HEADS, D_MODEL), BF16)
    return wt, jnp.concatenate([d, f, pad], axis=1)


def _mla_weights(w_uq, w_ukv):
    wq = w_uq.T.reshape(N_HEADS, MLA_NOPE + MLA_ROPE, MLA_Q_LORA)
    wq = jnp.pad(wq, ((0, 0), (0, HEAD_PAD - MLA_NOPE - MLA_ROPE), (0, 0)))
    wq = wq.reshape(N_HEADS * HEAD_PAD, MLA_Q_LORA).astype(BF16)
    wkv = w_ukv.T.reshape(N_HEADS, 2 * HEAD_DIM, MLA_KV_LORA)
    wk = wkv[:, :HEAD_DIM].reshape(GROUP_WIDTH, MLA_KV_LORA).astype(BF16)
    wv = wkv[:, HEAD_DIM:].reshape(GROUP_WIDTH, MLA_KV_LORA).astype(BF16)
    return wq, wk, wv


def kernel(x, g_mix_norm, w_in, b_forget, g_sgu, w_spatial, b_spatial, g_mla_q, w_uq, g_mla_kv, w_ukv,
           g_group_out, w_out, g_ffn_norm, w_up, w_down, g_final):
    batch, seq, _ = x.shape
    depth = w_in.shape[0]
    assert seq % TOKEN_TILE == 0 and ATTN_BLOCK == TOKEN_TILE
    cos_b, sin_b = _rope_tables(seq, HEAD_DIM // 2)
    cos_d, sin_d = _rope_tables(seq, MLA_ROPE // 2)
    w_abc, w_df = _inproj_weights(w_in)
    xf = x.reshape(batch * seq, D_MODEL)
    for l in range(depth):
        bf_col = jnp.pad(b_forget[l], (0, ROWS_F - N_HEADS))[:, None]
        ya, yb, qt_c, k_c, vt_c, stats_c, qt_d, k_d, vt_d, stats_d = _front(
            xf, g_mix_norm[l][None, :], w_abc, w_df,
            g_sgu[l][:, None], w_spatial[l], b_spatial[l], cos_b, sin_b, bf_col,
            cos_d, sin_d, g_mla_q[l][:, None], g_mla_kv[l][:, None], *_mla_weights(w_uq[l], w_ukv[l]),
            batch, l)
        yc = _attention(qt_c, k_c, vt_c, stats_c, banded=True)
        yd = _attention(qt_d, k_d, vt_d, stats_d, banded=False)
        xf = _post(xf, ya, yb, yc, yd, g_group_out[l].reshape(4, GROUP_WIDTH, 1),
                   w_out, g_ffn_norm[l][None, :], w_up, w_down, g_final[None, :],
                   layer=l, final=(l == depth - 1))
    return xf.reshape(batch, seq, D_MODEL)
```

```python
import functools
import math

import jax
import jax.numpy as jnp
import numpy as np
from jax import lax
from jax.experimental import pallas as pl
from jax.experimental.pallas import tpu as pltpu

F32 = jnp.float32
BF16 = jnp.bfloat16

D_MODEL = 1024
N_HEADS = 4
HEAD_DIM = 64
GROUP_WIDTH = N_HEADS * HEAD_DIM
CHUNK = 128
MLA_Q_LORA = 256
MLA_KV_LORA = 128
MLA_NOPE = 64
MLA_ROPE = 32
ROPE_BASE = 10000.0
D_FF = 4 * D_MODEL
EPS = 1e-6

HEAD_PAD = 128
AUG_ROWS = 8
PV_ROWS = 80
LOG2E = math.log2(math.e)
EXP2_UNDERFLOW = 152.0
STAT_QNORM, STAT_KNORM, STAT_CMAX, STAT_CMIN, STAT_DIAG, STAT_GAP = range(6)
N_STATS = 6
GAP_LIMIT = 64.0
TOKEN_TILE = 512
ATTN_BLOCK = TOKEN_TILE
FF_CHUNK = 512
VMEM_LIMIT = 56 * 1024 * 1024

ROWS_A = 2 * GROUP_WIDTH
ROWS_B = 4 * GROUP_WIDTH
ROWS_C = 3 * GROUP_WIDTH
ROWS_D = MLA_Q_LORA + MLA_KV_LORA + MLA_ROPE
ROWS_F = 8
ROWS_ABC = ROWS_A + ROWS_B + ROWS_C

NT = (((1,), (1,)), ((), ()))
TN = (((0,), (0,)), ((), ()))


def _params(*sem):
    return pltpu.CompilerParams(dimension_semantics=sem, vmem_limit_bytes=VMEM_LIMIT)


def _rot_half_rows(t, cos, sin):
    half = t.shape[0] // 2
    t1, t2 = t[:half], t[half:]
    return jnp.concatenate([t1 * cos - t2 * sin, t1 * sin + t2 * cos], axis=0)


def _standardize_rows(t):
    mu = jnp.mean(t, axis=0, keepdims=True)
    var = jnp.mean(jnp.square(t - mu), axis=0, keepdims=True)
    return (t - mu) * lax.rsqrt(var + EPS)


def _rms_rows(t):
    return t * lax.rsqrt(jnp.mean(t * t, axis=0, keepdims=True) + EPS)


def _sgu(at_ref, gain_ref, ws_ref, bs_ref, o_ref):
    tm = at_ref.shape[1]
    nch = tm // CHUNK
    row = lax.broadcasted_iota(jnp.int32, (CHUNK, CHUNK), 0)
    col = lax.broadcasted_iota(jnp.int32, (CHUNK, CHUNK), 1)
    for h in range(N_HEADS):
        r0, r1 = h * HEAD_DIM, (h + 1) * HEAD_DIM
        u = jax.nn.gelu(at_ref[r0:r1, :].astype(F32))
        v = jax.nn.gelu(at_ref[GROUP_WIDTH + r0:GROUP_WIDTH + r1, :].astype(F32))
        v = _standardize_rows(v) * gain_ref[r0:r1, :]
        w = jnp.where(col <= row, ws_ref[h], 0.0).astype(BF16)
        vs = jnp.concatenate([v[:, c * CHUNK:(c + 1) * CHUNK] for c in range(nch)], axis=0).astype(BF16)
        mixed = lax.dot_general(vs, w, NT, preferred_element_type=F32) + bs_ref[h:h + 1, :]
        for c in range(nch):
            o_ref[r0:r1, c * CHUNK:(c + 1) * CHUNK] = (
                u[:, c * CHUNK:(c + 1) * CHUNK] * mixed[c * HEAD_DIM:(c + 1) * HEAD_DIM])


def _retention_tables():
    log_gamma = np.log1p(-np.exp2(-5.0 - np.arange(N_HEADS, dtype=np.float64)))
    j = np.arange(CHUNK, dtype=np.float64)
    rel = j[None, :] - j[:, None]
    decay_t = np.where(rel >= 0, np.exp(np.maximum(rel, 0.0)[None] * log_gamma[:, None, None]), 0.0)
    query_w = np.exp((j + 1.0)[None, None, :] * log_gamma[:, None, None])
    key_w = np.exp((CHUNK - 1.0 - j)[None, None, :] * log_gamma[:, None, None])
    to_f32 = lambda a: jnp.asarray(a, F32)
    return to_f32(decay_t), to_f32(query_w), to_f32(key_w), [float(v) for v in np.exp(CHUNK * log_gamma)]


def _retention(bt_ref, cos_ref, sin_ref, dec_ref, qw_ref, kw_ref, chunk_decay, o_ref, st_ref):
    tm = bt_ref.shape[1]
    nch = tm // CHUNK
    cos, sin = cos_ref[...], sin_ref[...]
    for h in range(N_HEADS):
        dec_t, query_w, key_w = dec_ref[h], qw_ref[h], kw_ref[h]
        r0, r1 = h * HEAD_DIM, (h + 1) * HEAD_DIM
        q = _rot_half_rows(bt_ref[r0:r1, :].astype(F32), cos, sin)
        k = _rot_half_rows(bt_ref[GROUP_WIDTH + r0:GROUP_WIDTH + r1, :].astype(F32), cos, sin) * (HEAD_DIM ** -0.5)
        v = bt_ref[2 * GROUP_WIDTH + r0:2 * GROUP_WIDTH + r1, :].astype(F32)
        g = bt_ref[3 * GROUP_WIDTH + r0:3 * GROUP_WIDTH + r1, :].astype(F32)
        st = st_ref[h]
        ys = []
        for c in range(nch):
            sl = slice(c * CHUNK, (c + 1) * CHUNK)
            qc, kc, vc = q[:, sl], k[:, sl], v[:, sl]
            kcb = kc.astype(BF16)
            a_t = lax.dot_general(kcb, qc.astype(BF16), TN, preferred_element_type=F32)
            p_t = (a_t * dec_t).astype(BF16)
            intra = jnp.dot(vc.astype(BF16), p_t, preferred_element_type=F32)
            cross = jnp.dot(st.astype(BF16), (qc * query_w).astype(BF16), preferred_element_type=F32)
            ys.append(intra + cross)
            st = chunk_decay[h] * st + lax.dot_general((vc * key_w).astype(BF16), kcb, NT,
                                                       preferred_element_type=F32)
        st_ref[h] = st
        y = _standardize_rows(jnp.concatenate(ys, axis=1))
        o_ref[r0:r1, :] = jax.nn.silu(g) * y


def _select_rows(rows):
    n = rows[0].shape[1]
    ridx = lax.broadcasted_iota(jnp.int32, (AUG_ROWS, n), 0)
    out = jnp.zeros((AUG_ROWS, n), F32)
    for i, r in enumerate(rows):
        out = jnp.where(ridx == i, jnp.broadcast_to(r, (AUG_ROWS, n)), out)
    return out


def _group(feat, extra):
    n = feat.shape[1]
    pad = HEAD_PAD - feat.shape[0] - extra.shape[0]
    return jnp.concatenate([feat, extra, jnp.zeros((pad, n), F32)], axis=0)


def _split3(x):
    hi = x.astype(BF16).astype(F32)
    mid = (x - hi).astype(BF16).astype(F32)
    lo = (x - hi - mid).astype(BF16).astype(F32)
    return hi, mid, lo


def _logit_bound(qb, kb, kmax_ref, row):
    qnorm = jnp.sqrt(jnp.sum(qb * qb, axis=0, keepdims=True))
    knorm = jnp.sqrt(jnp.max(jnp.sum(kb * kb, axis=0, keepdims=True), axis=1, keepdims=True))
    kmax = jnp.maximum(kmax_ref[row:row + 1, :], knorm)
    kmax_ref[row:row + 1, :] = kmax
    return qnorm * kmax, jnp.sum(qb * kb, axis=0, keepdims=True), qnorm, knorm


def _write_stats(st_ref, per_head):
    stat_row = lax.broadcasted_iota(jnp.int32, (8, 128), 0)
    stat_lane = lax.broadcasted_iota(jnp.int32, (8, 128), 1)
    stats = jnp.zeros((8, 128), F32)
    for h, vals in enumerate(per_head):
        for r, val in enumerate(vals):
            stats = jnp.where((stat_row == r) & (stat_lane == h), val, stats)
    st_ref[0, 0] = stats


def _fox_prep(ct_ref, ft_ref, bf_ref, tri_ref, qt_ref, k_ref, vt_ref, st_ref, carry_ref, kmax_ref):
    tm = ct_ref.shape[1]
    x = ft_ref[...] + bf_ref[...]
    lf = jnp.minimum(x, 0.0) - jnp.log1p(jnp.exp(-jnp.abs(x)))
    hi = lf.astype(BF16)
    mid = (lf - hi.astype(F32)).astype(BF16)
    lo = (lf - hi.astype(F32) - mid.astype(F32)).astype(BF16)
    parts = jnp.dot(jnp.concatenate([hi, mid, lo], axis=0), tri_ref[...], preferred_element_type=F32)
    cum = parts[0:8] + parts[8:16] + parts[16:24] + carry_ref[...]
    carry_ref[...] = cum[:, tm - 1:tm]

    cum2 = cum * LOG2E
    one = jnp.ones((1, tm), F32)
    ones_row = _select_rows([one])
    stats = []
    for h in range(N_HEADS):
        r0, r1 = h * HEAD_DIM, (h + 1) * HEAD_DIM
        g0, g1 = h * HEAD_PAD, (h + 1) * HEAD_PAD
        q = ct_ref[r0:r1, :].astype(F32) * (HEAD_DIM ** -0.5 * LOG2E)
        k = ct_ref[GROUP_WIDTH + r0:GROUP_WIDTH + r1, :].astype(F32)
        v = ct_ref[2 * GROUP_WIDTH + r0:2 * GROUP_WIDTH + r1, :].astype(F32)
        qb = q.astype(BF16).astype(F32)
        bound, diag, qnorm, knorm = _logit_bound(qb, k, kmax_ref, h)
        c2 = cum2[h:h + 1]
        q_extra = _select_rows([*_split3(c2 - bound), one, one, one])
        k_extra = _select_rows([one, one, one, *(-part for part in _split3(c2))])
        qt_ref[0, g0:g1, :] = _group(q, q_extra).astype(BF16)
        k_ref[:, g0:g1] = _group(k, k_extra).T.astype(BF16)
        vt_ref[0, h, 0] = _group(v, ones_row).astype(BF16)
        stats.append((
            jnp.max(qnorm, axis=1, keepdims=True),
            knorm,
            jnp.max(c2, axis=1, keepdims=True),
            jnp.min(c2, axis=1, keepdims=True),
            jnp.min(diag, axis=1, keepdims=True),
            jnp.max(bound - diag, axis=1, keepdims=True),
        ))
    _write_stats(st_ref, stats)


def _mla_prep(dt_ref, cos_ref, sin_ref, gq_ref, gkv_ref, wq_ref, wk_ref, wv_ref, qt_ref, k_ref, vt_ref,
              st_ref, kmax_ref):
    tm = dt_ref.shape[1]
    cos, sin = cos_ref[...], sin_ref[...]
    cq = _rms_rows(dt_ref[0:MLA_Q_LORA, :].astype(F32)) * gq_ref[...]
    ckv = _rms_rows(dt_ref[MLA_Q_LORA:MLA_Q_LORA + MLA_KV_LORA, :].astype(F32)) * gkv_ref[...]
    kr = _rot_half_rows(dt_ref[MLA_Q_LORA + MLA_KV_LORA:ROWS_D, :].astype(F32), cos, sin)
    ckv_b = ckv.astype(BF16)
    q_all = jnp.dot(wq_ref[...], cq.astype(BF16), preferred_element_type=F32)
    q_all = q_all * ((MLA_NOPE + MLA_ROPE) ** -0.5 * LOG2E)
    k_all = jnp.dot(wk_ref[...], ckv_b, preferred_element_type=F32)
    v_all = jnp.dot(wv_ref[...], ckv_b, preferred_element_type=F32)
    one = jnp.ones((1, tm), F32)
    ones_row = _select_rows([one])
    k_extra = _select_rows([one, one, one])
    zero = jnp.zeros((1, 1), F32)
    stats = []
    for h in range(N_HEADS):
        r0, r1 = h * HEAD_DIM, (h + 1) * HEAD_DIM
        g0, g1 = h * HEAD_PAD, (h + 1) * HEAD_PAD
        qg = q_all[g0:g1]
        q_rope = _rot_half_rows(qg[MLA_NOPE:MLA_NOPE + MLA_ROPE], cos, sin)
        q = jnp.concatenate([qg[0:MLA_NOPE], q_rope], axis=0)
        k = jnp.concatenate([k_all[r0:r1], kr], axis=0)
        qb, kb = q.astype(BF16).astype(F32), k.astype(BF16).astype(F32)
        bound, diag, _, _ = _logit_bound(qb, kb, kmax_ref, N_HEADS + h)
        q_extra = _select_rows([*_split3(-bound)])
        qt_ref[0, g0:g1, :] = _group(q, q_extra).astype(BF16)
        k_ref[:, g0:g1] = _group(k, k_extra).T.astype(BF16)
        vt_ref[0, h, 0] = _group(v_all[r0:r1], ones_row).astype(BF16)
        stats.append((zero,) * STAT_GAP + (jnp.max(bound - diag, axis=1, keepdims=True),))
    _write_stats(st_ref, stats)


def _front_body(x_ref, g_ref, w_ref, wdf_ref,
                gain_ref, ws_ref, bs_ref,
                cosb_ref, sinb_ref, dec_ref, qw_ref, kw_ref,
                bf_ref, tri_ref,
                cosd_ref, sind_ref, gq_ref, gkv_ref, wq_ref, wk_ref, wv_ref,
                ya_ref, yb_ref,
                qtc_ref, kc_ref, vtc_ref, stats_ref,
                qtd_ref, kd_ref, vtd_ref, statsd_ref,
                at_ref, bt_ref, ct_ref, dt_ref, ft_ref, state_ref, carry_ref, kmax_ref, *,
                tiles_per_seq, chunk_decay):
    @pl.when(pl.program_id(0) % tiles_per_seq == 0)
    def _():
        state_ref[...] = jnp.zeros_like(state_ref)
        carry_ref[...] = jnp.zeros_like(carry_ref)
        kmax_ref[...] = jnp.zeros_like(kmax_ref)

    x = x_ref[...]
    h = (x * lax.rsqrt(jnp.mean(x * x, axis=-1, keepdims=True) + EPS) * g_ref[...]).astype(BF16)

    def proj(w):
        return lax.dot_general(w, h, NT, preferred_element_type=F32)

    half_b = ROWS_B // 2
    zdf = proj(wdf_ref[...])
    dt_ref[...] = zdf[0:ROWS_D].astype(BF16)
    ft_ref[...] = zdf[ROWS_D:ROWS_D + ROWS_F]
    at_ref[...] = proj(w_ref[0:ROWS_A, :]).astype(BF16)
    _mla_prep(dt_ref, cosd_ref, sind_ref, gq_ref, gkv_ref, wq_ref, wk_ref, wv_ref, qtd_ref, kd_ref, vtd_ref,
              statsd_ref, kmax_ref)
    ct_ref[...] = proj(w_ref[ROWS_A + ROWS_B:ROWS_ABC, :]).astype(BF16)
    _sgu(at_ref, gain_ref, ws_ref, bs_ref, ya_ref)
    bt_ref[0:half_b, :] = proj(w_ref[ROWS_A:ROWS_A + half_b, :]).astype(BF16)
    _fox_prep(ct_ref, ft_ref, bf_ref, tri_ref, qtc_ref, kc_ref, vtc_ref, stats_ref, carry_ref, kmax_ref)
    bt_ref[half_b:ROWS_B, :] = proj(w_ref[ROWS_A + half_b:ROWS_A + ROWS_B, :]).astype(BF16)
    _retention(bt_ref, cosb_ref, sinb_ref, dec_ref, qw_ref, kw_ref, chunk_decay, yb_ref, state_ref)


def _front(x, g, w_t, wdf_t, gain_col, w_s, b_s, cos_b, sin_b, ret_tables, bf_col, tri,
           cos_d, sin_d, gq_col, gkv_col, wq_t, wk_t, wv_t, batch, layer):
    t = x.shape[0]
    tm = TOKEN_TILE
    ns = t // batch // tm
    decay_t, query_w, key_w, chunk_decay = ret_tables
    const = lambda shape: pl.BlockSpec(shape, lambda i: (0,) * len(shape), pipeline_mode=pl.Buffered(1))
    of_layer = lambda shape: pl.BlockSpec((None,) + shape, lambda i: (layer,) + (0,) * len(shape),
                                          pipeline_mode=pl.Buffered(1))
    rows_t = lambda rows: pl.BlockSpec((rows, tm), lambda i: (0, i))
    table = lambda rows: pl.BlockSpec((rows, tm), lambda i: (0, i % ns))
    k_tile = pl.BlockSpec((tm, N_HEADS * HEAD_PAD), lambda i: (i, 0))
    vt_tile = pl.BlockSpec((1, N_HEADS, 1, HEAD_PAD, tm), lambda i: (i // ns, 0, i % ns, 0, 0))
    stats_tile = pl.BlockSpec((1, 1, 8, 128), lambda i: (i // ns, i % ns, 0, 0))
    qt_tile = pl.BlockSpec((1, N_HEADS * HEAD_PAD, tm), lambda i: (i, 0, 0))
    qkv_shapes = [
        jax.ShapeDtypeStruct((t // tm, N_HEADS * HEAD_PAD, tm), BF16),
        jax.ShapeDtypeStruct((t, N_HEADS * HEAD_PAD), BF16),
        jax.ShapeDtypeStruct((batch, N_HEADS, ns, HEAD_PAD, tm), BF16),
        jax.ShapeDtypeStruct((batch, ns, 8, 128), F32),
    ]
    return pl.pallas_call(
        functools.partial(_front_body, tiles_per_seq=ns, chunk_decay=chunk_decay),
        grid=(t // tm,),
        in_specs=[
            pl.BlockSpec((tm, D_MODEL), lambda i: (i, 0)),
            const((1, D_MODEL)),
            of_layer((ROWS_ABC, D_MODEL)),
            of_layer((ROWS_D + ROWS_F, D_MODEL)),
            const((GROUP_WIDTH, 1)),
            const((N_HEADS, CHUNK, CHUNK)),
            const((N_HEADS, CHUNK)),
            table(HEAD_DIM // 2), table(HEAD_DIM // 2),
            const((N_HEADS, CHUNK, CHUNK)), const((N_HEADS, 1, CHUNK)), const((N_HEADS, 1, CHUNK)),
            const((ROWS_F, 1)),
            const((tm, tm)),
            table(MLA_ROPE // 2), table(MLA_ROPE // 2),
            const((MLA_Q_LORA, 1)),
            const((MLA_KV_LORA, 1)),
            const((N_HEADS * HEAD_PAD, MLA_Q_LORA)),
            const((GROUP_WIDTH, MLA_KV_LORA)),
            const((GROUP_WIDTH, MLA_KV_LORA)),
        ],
        out_specs=[
            rows_t(GROUP_WIDTH), rows_t(GROUP_WIDTH),
            qt_tile, k_tile, vt_tile, stats_tile,
            qt_tile, k_tile, vt_tile, stats_tile,
        ],
        out_shape=[
            jax.ShapeDtypeStruct((GROUP_WIDTH, t), F32),
            jax.ShapeDtypeStruct((GROUP_WIDTH, t), F32),
            *qkv_shapes,
            *qkv_shapes,
        ],
        scratch_shapes=[
            pltpu.VMEM((ROWS_A, tm), BF16),
            pltpu.VMEM((ROWS_B, tm), BF16),
            pltpu.VMEM((ROWS_C, tm), BF16),
            pltpu.VMEM((ROWS_D, tm), BF16),
            pltpu.VMEM((ROWS_F, tm), F32),
            pltpu.VMEM((N_HEADS, HEAD_DIM, HEAD_DIM), F32),
            pltpu.VMEM((ROWS_F, 1), F32),
            pltpu.VMEM((2 * N_HEADS, 1), F32),
        ],
        compiler_params=_params("arbitrary"),
        name="front",
    )(x, g, w_t, wdf_t, gain_col, w_s, b_s, cos_b, sin_b, decay_t, query_w, key_w, bf_col, tri,
      cos_d, sin_d, gq_col, gkv_col, wq_t, wk_t, wv_t)


def _first_needed_block(stats_ref, b, qi, nb):
    def stat(j, r, h):
        return stats_ref[((b * nb + j) * N_STATS + r) * N_HEADS + h]

    skipped = jnp.int32(0)
    leading = jnp.bool_(True)
    for j in range(nb - 1):
        zero = j < qi
        for h in range(N_HEADS):
            bound = (stat(qi, STAT_QNORM, h) * stat(j, STAT_KNORM, h)
                     + stat(qi, STAT_CMAX, h) - stat(j, STAT_CMIN, h))
            zero = jnp.logical_and(zero, bound - stat(qi, STAT_DIAG, h) < -EXP2_UNDERFLOW)
        leading = jnp.logical_and(leading, zero)
        skipped = skipped + leading.astype(jnp.int32)
    return skipped


def _attn_body(*refs, banded, online):
    refs = list(refs)
    stats_ref = refs.pop(0) if banded else None
    qt_ref, k_ref, vt_ref, o_ref, acc_ref, s_ref = refs[:6]
    m_ref = refs[6] if online else None
    nb, _, blk = qt_ref.shape
    half = blk // 2
    batch = pl.program_id(0)

    def first_block(qi):
        return _first_needed_block(stats_ref, batch, qi, nb) if banded else jnp.int32(0)

    def logits(qi, kj, h):
        start = pl.multiple_of(kj * blk, blk)
        g0, g1 = h * HEAD_PAD, (h + 1) * HEAD_PAD
        return jnp.dot(k_ref[pl.ds(start, blk), g0:g1], qt_ref[qi, g0:g1, :],
                       preferred_element_type=F32)

    def accumulate(s, kj, h):
        acc = acc_ref[h]
        if online:
            m_old = m_ref[h]
            m_new = jnp.maximum(m_old, jnp.max(s, axis=0, keepdims=True))
            m_ref[h] = m_new
            s = s - m_new
            acc = acc * jnp.exp2(m_old - m_new)
        pv = jnp.dot(vt_ref[0, h, kj, 0:PV_ROWS, :], jnp.exp2(s).astype(BF16), preferred_element_type=F32)
        acc_ref[h] = acc + pv[0:HEAD_DIM + AUG_ROWS]

    s_ref[...] = logits(0, 0, 0)

    def query_block(qi, first):
        acc_ref[...] = jnp.zeros_like(acc_ref)
        if online:
            m_ref[...] = jnp.full(m_ref.shape, -jnp.inf, F32)
        nxt_qi = jnp.minimum(qi + 1, nb - 1)
        nxt_first = first_block(nxt_qi)

        def full_blocks(kj, count):
            s = s_ref[...]
            for d in range(count):
                for h in range(N_HEADS):
                    nxt = (kj + d, h + 1) if h + 1 < N_HEADS else (kj + d + 1, 0)
                    s_next = logits(qi, *nxt)
                    accumulate(s, kj + d, h)
                    s = s_next
            s_ref[...] = s

        n_full = qi - first
        odd = jnp.bitwise_and(n_full, 1)
        pl.when(odd == 1)(lambda: full_blocks(first, 1))

        def pair_step(i, carry):
            full_blocks(first + odd + 2 * i, 2)
            return carry

        lax.fori_loop(0, lax.shift_right_logical(n_full, 1), pair_step, 0)

        if online:
            key_pos = lax.broadcasted_iota(jnp.int32, (blk, blk), 0)
            qry_pos = lax.broadcasted_iota(jnp.int32, (blk, blk), 1)
            visible = key_pos <= qry_pos
            s = s_ref[...]
            for h in range(N_HEADS):
                s_next = logits(qi, qi, h + 1) if h + 1 < N_HEADS else logits(nxt_qi, nxt_first, 0)
                accumulate(jnp.where(visible, s, -jnp.inf), qi, h)
                s = s_next
            s_ref[...] = s
        else:
            start = pl.multiple_of(qi * blk, blk)
            key_pos = lax.broadcasted_iota(jnp.int32, (half, blk), 0)
            qry_pos = lax.broadcasted_iota(jnp.int32, (half, blk), 1)
            visible_a = key_pos <= qry_pos
            visible_b = visible_a[:, 0:half]

            def quadrant_logits(h):
                g0, g1 = h * HEAD_PAD, (h + 1) * HEAD_PAD
                s_a = jnp.dot(k_ref[pl.ds(start, half), g0:g1], qt_ref[qi, g0:g1, :],
                              preferred_element_type=F32)
                s_b = jnp.dot(k_ref[pl.ds(start + half, half), g0:g1], qt_ref[qi, g0:g1, half:blk],
                              preferred_element_type=F32)
                return s_a, s_b

            s_full = s_ref[...]
            s_a, s_b = s_full[0:half], s_full[half:blk, half:blk]
            for h in range(N_HEADS):
                if h + 1 < N_HEADS:
                    s_next = quadrant_logits(h + 1)
                else:
                    s_ref[...] = logits(nxt_qi, nxt_first, 0)
                p_a = jnp.exp2(jnp.where(visible_a, s_a, -jnp.inf)).astype(BF16)
                p_b = jnp.exp2(jnp.where(visible_b, s_b, -jnp.inf)).astype(BF16)
                pv_a = jnp.dot(vt_ref[0, h, qi, 0:PV_ROWS, 0:half], p_a, preferred_element_type=F32)
                pv_b = jnp.dot(vt_ref[0, h, qi, 0:PV_ROWS, half:blk], p_b, preferred_element_type=F32)
                acc_ref[h] = acc_ref[h] + pv_a[0:HEAD_DIM + AUG_ROWS]
                acc_ref[h, :, half:blk] = acc_ref[h, :, half:blk] + pv_b[0:HEAD_DIM + AUG_ROWS]
                if h + 1 < N_HEADS:
                    s_a, s_b = s_next
        for h in range(N_HEADS):
            acc = acc_ref[h]
            o_ref[qi, h * HEAD_DIM:(h + 1) * HEAD_DIM, :] = acc[0:HEAD_DIM] / acc[HEAD_DIM:HEAD_DIM + 1]
        return nxt_first

    lax.fori_loop(0, nb, query_block, jnp.int32(0))


def _attention_call(qt, k, vt, skip_stats, online):
    batch, _, nb, _, blk = vt.shape
    seq = nb * blk
    banded = skip_stats is not None
    in_specs = [
        pl.BlockSpec((nb, N_HEADS * HEAD_PAD, blk), lambda b: (b, 0, 0)),
        pl.BlockSpec((seq, N_HEADS * HEAD_PAD), lambda b: (b, 0)),
        pl.BlockSpec((1, N_HEADS, nb, HEAD_PAD, blk), lambda b: (b, 0, 0, 0, 0)),
    ]
    args = (qt, k, vt)
    if banded:
        in_specs = [pl.BlockSpec(memory_space=pltpu.SMEM)] + in_specs
        args = (skip_stats,) + args
    scratch = [pltpu.VMEM((N_HEADS, HEAD_DIM + AUG_ROWS, blk), F32), pltpu.VMEM((blk, blk), F32)]
    if online:
        scratch.append(pltpu.VMEM((N_HEADS, 1, blk), F32))
    return pl.pallas_call(
        functools.partial(_attn_body, banded=banded, online=online),
        grid=(batch,),
        in_specs=in_specs,
        out_specs=pl.BlockSpec((nb, GROUP_WIDTH, blk), lambda b: (b, 0, 0)),
        out_shape=jax.ShapeDtypeStruct((batch * nb, GROUP_WIDTH, blk), F32),
        scratch_shapes=scratch,
        compiler_params=_params("parallel"),
        name="attention_online" if online else "attention",
    )(*args)


def _attention(qt, k, vt, stats, banded):
    stats = stats[:, :, :N_STATS, :N_HEADS]
    skip_stats = stats.reshape(-1) if banded else None
    bound_is_tight = jnp.max(stats[:, :, STAT_GAP, :]) <= GAP_LIMIT
    return lax.cond(bound_is_tight,
                    lambda: _attention_call(qt, k, vt, skip_stats, online=False),
                    lambda: _attention_call(qt, k, vt, skip_stats, online=True))


def _post_body(x_ref, ya_ref, yb_ref, yc_ref, yd_ref, gg_ref, wo_hbm, gf_ref, wu_hbm, wd_hbm, gl_ref,
               o_ref, wo_ref, wu_ref, wd_ref, stage_tall, stage_wide, sem, *, layer, final):
    n_chunks = D_FF // FF_CHUNK
    tall = [(wo_hbm.at[layer, pl.ds(r, FF_CHUNK), :], wo_ref.at[pl.ds(r, FF_CHUNK), :])
            for r in range(0, D_MODEL, FF_CHUNK)]
    n_out = len(tall)
    tall += [(wd_hbm.at[layer, pl.ds(c * FF_CHUNK, FF_CHUNK), :], wd_ref.at[pl.ds(c * FF_CHUNK, FF_CHUNK), :])
             for c in range(n_chunks)]
    wide = [(wu_hbm.at[layer, :, pl.ds(c * FF_CHUNK, FF_CHUNK)], wu_ref.at[:, pl.ds(c * FF_CHUNK, FF_CHUNK)])
            for c in range(n_chunks)]
    queues = ((tall, stage_tall), (wide, stage_wide))

    def dma(kind, k):
        blocks, stage = queues[kind]
        return pltpu.make_async_copy(blocks[k][0], stage.at[k % 2], sem.at[kind, k % 2])

    def land(kind, k):
        blocks, stage = queues[kind]
        dma(kind, k).wait()
        blocks[k][1][...] = stage[k % 2].astype(BF16)
        if k + 2 < len(blocks):
            dma(kind, k + 2).start()

    def body(load):
        if load:
            for kind in range(2):
                dma(kind, 0).start()
                dma(kind, 1).start()
        ys = []
        for g, y_ref in enumerate((ya_ref, yb_ref, yc_ref, yd_ref)):
            ys.append((_rms_rows(y_ref[...]) * gg_ref[g]).astype(BF16))
        y = jnp.concatenate(ys, axis=0)
        if load:
            for k in range(n_out):
                land(0, k)
        x = x_ref[...] + lax.dot_general(y, wo_ref[...], TN, preferred_element_type=F32)
        h = (x * lax.rsqrt(jnp.mean(x * x, axis=-1, keepdims=True) + EPS) * gf_ref[...]).astype(BF16)
        acc = x
        for c in range(n_chunks):
            c0, c1 = c * FF_CHUNK, (c + 1) * FF_CHUNK
            if load:
                land(1, c)
                land(0, n_out + c)
            a = jnp.maximum(jnp.dot(h, wu_ref[:, c0:c1], preferred_element_type=F32), 0.0)
            acc = acc + jnp.dot((a * a).astype(BF16), wd_ref[c0:c1, :], preferred_element_type=F32)
        if final:
            acc = acc * lax.rsqrt(jnp.mean(acc * acc, axis=-1, keepdims=True) + EPS) * gl_ref[...]
        o_ref[...] = acc

    first_step = pl.program_id(0) == 0
    pl.when(first_step)(functools.partial(body, True))
    pl.when(jnp.logical_not(first_step))(functools.partial(body, False))


def _post(x, ya, yb, yc, yd, gg_col, w_out, g_ffn, w_up, w_down, g_final, layer, final):
    t = x.shape[0]
    tm = TOKEN_TILE
    const = lambda shape: pl.BlockSpec(shape, lambda i: (0,) * len(shape), pipeline_mode=pl.Buffered(1))
    in_hbm = pl.BlockSpec(memory_space=pl.ANY)
    ytile = pl.BlockSpec((GROUP_WIDTH, tm), lambda i: (0, i))
    yblock = pl.BlockSpec((None, GROUP_WIDTH, tm), lambda i: (i, 0, 0))
    return pl.pallas_call(
        functools.partial(_post_body, layer=layer, final=final),
        grid=(t // tm,),
        in_specs=[
            pl.BlockSpec((tm, D_MODEL), lambda i: (i, 0)),
            ytile, ytile, yblock, yblock,
            const((4, GROUP_WIDTH, 1)),
            in_hbm,
            const((1, D_MODEL)),
            in_hbm,
            in_hbm,
            const((1, D_MODEL)),
        ],
        out_specs=pl.BlockSpec((tm, D_MODEL), lambda i: (i, 0)),
        out_shape=jax.ShapeDtypeStruct((t, D_MODEL), F32),
        scratch_shapes=[
            pltpu.VMEM((D_MODEL, D_MODEL), BF16),
            pltpu.VMEM((D_MODEL, D_FF), BF16),
            pltpu.VMEM((D_FF, D_MODEL), BF16),
            pltpu.VMEM((2, FF_CHUNK, D_MODEL), F32),
            pltpu.VMEM((2, D_MODEL, FF_CHUNK), F32),
            pltpu.SemaphoreType.DMA((2, 2)),
        ],
        compiler_params=_params("arbitrary"),
        name="post",
    )(x, ya, yb, yc, yd, gg_col, w_out, g_ffn, w_up, w_down, g_final)


def _rope_tables(seq, half):
    inv_freq = np.power(ROPE_BASE, -np.arange(half, dtype=np.float64) / half)
    ang = inv_freq[:, None] * np.arange(seq, dtype=np.float64)[None, :]
    return jnp.asarray(np.cos(ang), F32), jnp.asarray(np.sin(ang), F32)


def _inproj_weights(w_in):
    wt = jnp.swapaxes(w_in, 1, 2).astype(BF16)
    f = wt[:, ROWS_ABC:ROWS_ABC + N_HEADS]
    d = wt[:, ROWS_ABC + N_HEADS:ROWS_ABC + N_HEADS + ROWS_D]
    pad = jnp.zeros((w_in.shape[0], ROWS_F - N_HEADS, D_MODEL), BF16)
    return wt, jnp.concatenate([d, f, pad], axis=1)


def _mla_weights(w_uq, w_ukv):
    wq = w_uq.T.reshape(N_HEADS, MLA_NOPE + MLA_ROPE, MLA_Q_LORA)
    wq = jnp.pad(wq, ((0, 0), (0, HEAD_PAD - MLA_NOPE - MLA_ROPE), (0, 0)))
    wq = wq.reshape(N_HEADS * HEAD_PAD, MLA_Q_LORA).astype(BF16)
    wkv = w_ukv.T.reshape(N_HEADS, 2 * HEAD_DIM, MLA_KV_LORA)
    wk = wkv[:, :HEAD_DIM].reshape(GROUP_WIDTH, MLA_KV_LORA).astype(BF16)
    wv = wkv[:, HEAD_DIM:].reshape(GROUP_WIDTH, MLA_KV_LORA).astype(BF16)
    return wq, wk, wv


def kernel(x, g_mix_norm, w_in, b_forget, g_sgu, w_spatial, b_spatial, g_mla_q, w_uq, g_mla_kv, w_ukv,
           g_group_out, w_out, g_ffn_norm, w_up, w_down, g_final):
    batch, seq, _ = x.shape
    depth = w_in.shape[0]
    assert seq % TOKEN_TILE == 0 and ATTN_BLOCK == TOKEN_TILE
    cos_b, sin_b = _rope_tables(seq, HEAD_DIM // 2)
    cos_d, sin_d = _rope_tables(seq, MLA_ROPE // 2)
    w_abc, w_df = _inproj_weights(w_in)
    ret_tables = _retention_tables()
    tri = jnp.asarray(np.triu(np.ones((TOKEN_TILE, TOKEN_TILE), np.float32)), BF16)
    xf = x.reshape(batch * seq, D_MODEL)
    for l in range(depth):
        bf_col = jnp.pad(b_forget[l], (0, ROWS_F - N_HEADS))[:, None]
        ya, yb, qt_c, k_c, vt_c, stats_c, qt_d, k_d, vt_d, stats_d = _front(
            xf, g_mix_norm[l][None, :], w_abc, w_df,
            g_sgu[l][:, None], w_spatial[l], b_spatial[l], cos_b, sin_b, ret_tables, bf_col, tri,
            cos_d, sin_d, g_mla_q[l][:, None], g_mla_kv[l][:, None], *_mla_weights(w_uq[l], w_ukv[l]),
            batch, l)
        yc = _attention(qt_c, k_c, vt_c, stats_c, banded=True)
        yd = _attention(qt_d, k_d, vt_d, stats_d, banded=False)
        xf = _post(xf, ya, yb, yc, yd, g_group_out[l].reshape(4, GROUP_WIDTH, 1),
                   w_out, g_ffn_norm[l][None, :], w_up, w_down, g_final[None, :],
                   layer=l, final=(l == depth - 1))
    return xf.reshape(batch, seq, D_MODEL)
```

```python
import functools
import math

import jax
import jax.numpy as jnp
import numpy as np
from jax import lax
from jax.experimental import pallas as pl
from jax.experimental.pallas import tpu as pltpu

F32 = jnp.float32
BF16 = jnp.bfloat16

D_MODEL = 1024
N_HEADS = 4
HEAD_DIM = 64
GROUP_WIDTH = N_HEADS * HEAD_DIM
CHUNK = 128
MLA_Q_LORA = 256
MLA_KV_LORA = 128
MLA_NOPE = 64
MLA_ROPE = 32
ROPE_BASE = 10000.0
D_FF = 4 * D_MODEL
EPS = 1e-6

HEAD_PAD = 128
AUG_ROWS = 8
PV_ROWS = 80
LOG2E = math.log2(math.e)
EXP2_UNDERFLOW = 152.0
STAT_QNORM, STAT_KNORM, STAT_CMAX, STAT_CMIN, STAT_DIAG, STAT_GAP = range(6)
N_STATS = 6
GAP_LIMIT = 64.0
TOKEN_TILE = 512
ATTN_BLOCK = TOKEN_TILE
FF_CHUNK = 512
VMEM_LIMIT = 56 * 1024 * 1024

ROWS_A = 2 * GROUP_WIDTH
ROWS_B = 4 * GROUP_WIDTH
ROWS_C = 3 * GROUP_WIDTH
ROWS_D = MLA_Q_LORA + MLA_KV_LORA + MLA_ROPE
ROWS_F = 8
ROWS_ABC = ROWS_A + ROWS_B + ROWS_C

NT = (((1,), (1,)), ((), ()))
TN = (((0,), (0,)), ((), ()))


def _params(*sem):
    return pltpu.CompilerParams(dimension_semantics=sem, vmem_limit_bytes=VMEM_LIMIT)


def _rot_half_rows(t, cos, sin):
    half = t.shape[0] // 2
    t1, t2 = t[:half], t[half:]
    return jnp.concatenate([t1 * cos - t2 * sin, t1 * sin + t2 * cos], axis=0)


def _standardize_rows(t):
    mu = jnp.mean(t, axis=0, keepdims=True)
    var = jnp.mean(jnp.square(t - mu), axis=0, keepdims=True)
    return (t - mu) * lax.rsqrt(var + EPS)


def _rms_rows(t):
    return t * lax.rsqrt(jnp.mean(t * t, axis=0, keepdims=True) + EPS)


def _sgu(at_ref, gain_ref, ws_ref, bs_ref, o_ref):
    tm = at_ref.shape[1]
    nch = tm // CHUNK
    row = lax.broadcasted_iota(jnp.int32, (CHUNK, CHUNK), 0)
    col = lax.broadcasted_iota(jnp.int32, (CHUNK, CHUNK), 1)
    for h in range(N_HEADS):
        r0, r1 = h * HEAD_DIM, (h + 1) * HEAD_DIM
        u = jax.nn.gelu(at_ref[r0:r1, :].astype(F32))
        v = jax.nn.gelu(at_ref[GROUP_WIDTH + r0:GROUP_WIDTH + r1, :].astype(F32))
        v = _standardize_rows(v) * gain_ref[r0:r1, :]
        w = jnp.where(col <= row, ws_ref[h], 0.0).astype(BF16)
        vs = jnp.concatenate([v[:, c * CHUNK:(c + 1) * CHUNK] for c in range(nch)], axis=0).astype(BF16)
        mixed = lax.dot_general(vs, w, NT, preferred_element_type=F32) + bs_ref[h:h + 1, :]
        for c in range(nch):
            o_ref[r0:r1, c * CHUNK:(c + 1) * CHUNK] = (
                u[:, c * CHUNK:(c + 1) * CHUNK] * mixed[c * HEAD_DIM:(c + 1) * HEAD_DIM])


def _retention_tables():
    log_gamma = np.log1p(-np.exp2(-5.0 - np.arange(N_HEADS, dtype=np.float64)))
    j = np.arange(CHUNK, dtype=np.float64)
    rel = j[None, :] - j[:, None]
    decay_t = np.where(rel >= 0, np.exp(np.maximum(rel, 0.0)[None] * log_gamma[:, None, None]), 0.0)
    query_w = np.exp((j + 1.0)[None, None, :] * log_gamma[:, None, None])
    key_w = np.exp((CHUNK - 1.0 - j)[None, None, :] * log_gamma[:, None, None])
    to_f32 = lambda a: jnp.asarray(a, F32)
    return to_f32(decay_t), to_f32(query_w), to_f32(key_w), [float(v) for v in np.exp(CHUNK * log_gamma)]


def _retention(bt_ref, cos_ref, sin_ref, dec_ref, qw_ref, kw_ref, chunk_decay, o_ref, st_ref):
    tm = bt_ref.shape[1]
    nch = tm // CHUNK
    cos, sin = cos_ref[...], sin_ref[...]
    for h in range(N_HEADS):
        dec_t, query_w, key_w = dec_ref[h], qw_ref[h], kw_ref[h]
        r0, r1 = h * HEAD_DIM, (h + 1) * HEAD_DIM
        q = _rot_half_rows(bt_ref[r0:r1, :].astype(F32), cos, sin)
        k = _rot_half_rows(bt_ref[GROUP_WIDTH + r0:GROUP_WIDTH + r1, :].astype(F32), cos, sin) * (HEAD_DIM ** -0.5)
        v = bt_ref[2 * GROUP_WIDTH + r0:2 * GROUP_WIDTH + r1, :].astype(F32)
        g = bt_ref[3 * GROUP_WIDTH + r0:3 * GROUP_WIDTH + r1, :].astype(F32)
        st = st_ref[h]
        ys = []
        for c in range(nch):
            sl = slice(c * CHUNK, (c + 1) * CHUNK)
            qc, kc, vc = q[:, sl], k[:, sl], v[:, sl]
            kcb = kc.astype(BF16)
            a_t = lax.dot_general(kcb, qc.astype(BF16), TN, preferred_element_type=F32)
            p_t = (a_t * dec_t).astype(BF16)
            intra = jnp.dot(vc.astype(BF16), p_t, preferred_element_type=F32)
            cross = jnp.dot(st.astype(BF16), (qc * query_w).astype(BF16), preferred_element_type=F32)
            ys.append(intra + cross)
            st = chunk_decay[h] * st + lax.dot_general((vc * key_w).astype(BF16), kcb, NT,
                                                       preferred_element_type=F32)
        st_ref[h] = st
        y = _standardize_rows(jnp.concatenate(ys, axis=1))
        o_ref[r0:r1, :] = jax.nn.silu(g) * y


def _select_rows(rows):
    n = rows[0].shape[1]
    ridx = lax.broadcasted_iota(jnp.int32, (AUG_ROWS, n), 0)
    out = jnp.zeros((AUG_ROWS, n), F32)
    for i, r in enumerate(rows):
        out = jnp.where(ridx == i, jnp.broadcast_to(r, (AUG_ROWS, n)), out)
    return out


def _group(feat, extra):
    n = feat.shape[1]
    pad = HEAD_PAD - feat.shape[0] - extra.shape[0]
    return jnp.concatenate([feat, extra, jnp.zeros((pad, n), F32)], axis=0)


def _split3(x):
    hi = x.astype(BF16).astype(F32)
    mid = (x - hi).astype(BF16).astype(F32)
    lo = (x - hi - mid).astype(BF16).astype(F32)
    return hi, mid, lo


def _logit_bound(qb, kb, kmax_ref, row):
    qnorm = jnp.sqrt(jnp.sum(qb * qb, axis=0, keepdims=True))
    knorm = jnp.sqrt(jnp.max(jnp.sum(kb * kb, axis=0, keepdims=True), axis=1, keepdims=True))
    kmax = jnp.maximum(kmax_ref[row:row + 1, :], knorm)
    kmax_ref[row:row + 1, :] = kmax
    return qnorm * kmax, jnp.sum(qb * kb, axis=0, keepdims=True), qnorm, knorm


def _write_stats(st_ref, per_head):
    stat_row = lax.broadcasted_iota(jnp.int32, (8, 128), 0)
    stat_lane = lax.broadcasted_iota(jnp.int32, (8, 128), 1)
    stats = jnp.zeros((8, 128), F32)
    for h, vals in enumerate(per_head):
        for r, val in enumerate(vals):
            stats = jnp.where((stat_row == r) & (stat_lane == h), val, stats)
    st_ref[0, 0] = stats


def _fox_prep(ct_ref, ft_ref, bf_ref, tri_ref, qt_ref, k_ref, vt_ref, st_ref, carry_ref, kmax_ref):
    tm = ct_ref.shape[1]
    x = ft_ref[...] + bf_ref[...]
    lf = jnp.minimum(x, 0.0) - jnp.log1p(jnp.exp(-jnp.abs(x)))
    hi = lf.astype(BF16)
    mid = (lf - hi.astype(F32)).astype(BF16)
    lo = (lf - hi.astype(F32) - mid.astype(F32)).astype(BF16)
    parts = jnp.dot(jnp.concatenate([hi, mid, lo], axis=0), tri_ref[...], preferred_element_type=F32)
    cum = parts[0:8] + parts[8:16] + parts[16:24] + carry_ref[...]
    carry_ref[...] = cum[:, tm - 1:tm]

    cum2 = cum * LOG2E
    one = jnp.ones((1, tm), F32)
    ones_row = _select_rows([one])
    stats = []
    for h in range(N_HEADS):
        r0, r1 = h * HEAD_DIM, (h + 1) * HEAD_DIM
        g0, g1 = h * HEAD_PAD, (h + 1) * HEAD_PAD
        q = ct_ref[r0:r1, :].astype(F32) * (HEAD_DIM ** -0.5 * LOG2E)
        k = ct_ref[GROUP_WIDTH + r0:GROUP_WIDTH + r1, :].astype(F32)
        v = ct_ref[2 * GROUP_WIDTH + r0:2 * GROUP_WIDTH + r1, :].astype(F32)
        qb = q.astype(BF16).astype(F32)
        bound, diag, qnorm, knorm = _logit_bound(qb, k, kmax_ref, h)
        c2 = cum2[h:h + 1]
        q_extra = _select_rows([*_split3(c2 - bound), one, one, one])
        k_extra = _select_rows([one, one, one, *(-part for part in _split3(c2))])
        qt_ref[0, g0:g1, :] = _group(q, q_extra).astype(BF16)
        k_ref[:, g0:g1] = _group(k, k_extra).T.astype(BF16)
        vt_ref[0, h, 0] = _group(v, ones_row).astype(BF16)
        stats.append((
            jnp.max(qnorm, axis=1, keepdims=True),
            knorm,
            jnp.max(c2, axis=1, keepdims=True),
            jnp.min(c2, axis=1, keepdims=True),
            jnp.min(diag, axis=1, keepdims=True),
            jnp.max(bound - diag, axis=1, keepdims=True),
        ))
    _write_stats(st_ref, stats)


def _mla_prep(dt_ref, cos_ref, sin_ref, gq_ref, gkv_ref, wq_ref, wk_ref, wv_ref, qt_ref, k_ref, vt_ref,
              st_ref, kmax_ref):
    tm = dt_ref.shape[1]
    cos, sin = cos_ref[...], sin_ref[...]
    cq = _rms_rows(dt_ref[0:MLA_Q_LORA, :].astype(F32)) * gq_ref[...]
    ckv = _rms_rows(dt_ref[MLA_Q_LORA:MLA_Q_LORA + MLA_KV_LORA, :].astype(F32)) * gkv_ref[...]
    kr = _rot_half_rows(dt_ref[MLA_Q_LORA + MLA_KV_LORA:ROWS_D, :].astype(F32), cos, sin)
    ckv_b = ckv.astype(BF16)
    q_all = jnp.dot(wq_ref[...], cq.astype(BF16), preferred_element_type=F32)
    q_all = q_all * ((MLA_NOPE + MLA_ROPE) ** -0.5 * LOG2E)
    k_all = jnp.dot(wk_ref[...], ckv_b, preferred_element_type=F32)
    v_all = jnp.dot(wv_ref[...], ckv_b, preferred_element_type=F32)
    one = jnp.ones((1, tm), F32)
    ones_row = _select_rows([one])
    k_extra = _select_rows([one, one, one])
    zero = jnp.zeros((1, 1), F32)
    stats = []
    for h in range(N_HEADS):
        r0, r1 = h * HEAD_DIM, (h + 1) * HEAD_DIM
        g0, g1 = h * HEAD_PAD, (h + 1) * HEAD_PAD
        qg = q_all[g0:g1]
        q_rope = _rot_half_rows(qg[MLA_NOPE:MLA_NOPE + MLA_ROPE], cos, sin)
        q = jnp.concatenate([qg[0:MLA_NOPE], q_rope], axis=0)
        k = jnp.concatenate([k_all[r0:r1], kr], axis=0)
        qb, kb = q.astype(BF16).astype(F32), k.astype(BF16).astype(F32)
        bound, diag, _, _ = _logit_bound(qb, kb, kmax_ref, N_HEADS + h)
        q_extra = _select_rows([*_split3(-bound)])
        qt_ref[0, g0:g1, :] = _group(q, q_extra).astype(BF16)
        k_ref[:, g0:g1] = _group(k, k_extra).T.astype(BF16)
        vt_ref[0, h, 0] = _group(v_all[r0:r1], ones_row).astype(BF16)
        stats.append((zero,) * STAT_GAP + (jnp.max(bound - diag, axis=1, keepdims=True),))
    _write_stats(st_ref, stats)


def _front_body(x_ref, g_ref, w_ref, wdf_ref,
                gain_ref, ws_ref, bs_ref,
                cosb_ref, sinb_ref, dec_ref, qw_ref, kw_ref,
                bf_ref, tri_ref,
                cosd_ref, sind_ref, gq_ref, gkv_ref, wq_ref, wk_ref, wv_ref,
                ya_ref, yb_ref,
                qt_ref, k_ref, vt_ref, stats_ref,
                at_ref, bt_ref, ct_ref, dt_ref, ft_ref, state_ref, carry_ref, kmax_ref, *,
                tiles_per_seq, chunk_decay):
    @pl.when(pl.program_id(0) % tiles_per_seq == 0)
    def _():
        state_ref[...] = jnp.zeros_like(state_ref)
        carry_ref[...] = jnp.zeros_like(carry_ref)
        kmax_ref[...] = jnp.zeros_like(kmax_ref)

    x = x_ref[...]
    h = (x * lax.rsqrt(jnp.mean(x * x, axis=-1, keepdims=True) + EPS) * g_ref[...]).astype(BF16)

    def proj(w):
        return lax.dot_general(w, h, NT, preferred_element_type=F32)

    half_b = ROWS_B // 2
    zdf = proj(wdf_ref[...])
    dt_ref[...] = zdf[0:ROWS_D].astype(BF16)
    ft_ref[...] = zdf[ROWS_D:ROWS_D + ROWS_F]
    at_ref[...] = proj(w_ref[0:ROWS_A, :]).astype(BF16)
    _mla_prep(dt_ref, cosd_ref, sind_ref, gq_ref, gkv_ref, wq_ref, wk_ref, wv_ref,
              qt_ref.at[1], k_ref.at[1], vt_ref.at[1], stats_ref.at[1], kmax_ref)
    ct_ref[...] = proj(w_ref[ROWS_A + ROWS_B:ROWS_ABC, :]).astype(BF16)
    _sgu(at_ref, gain_ref, ws_ref, bs_ref, ya_ref)
    bt_ref[0:half_b, :] = proj(w_ref[ROWS_A:ROWS_A + half_b, :]).astype(BF16)
    _fox_prep(ct_ref, ft_ref, bf_ref, tri_ref, qt_ref.at[0], k_ref.at[0], vt_ref.at[0], stats_ref.at[0],
              carry_ref, kmax_ref)
    bt_ref[half_b:ROWS_B, :] = proj(w_ref[ROWS_A + half_b:ROWS_A + ROWS_B, :]).astype(BF16)
    _retention(bt_ref, cosb_ref, sinb_ref, dec_ref, qw_ref, kw_ref, chunk_decay, yb_ref, state_ref)


def _front(x, g, w_t, wdf_t, gain_col, w_s, b_s, cos_b, sin_b, ret_tables, bf_col, tri,
           cos_d, sin_d, gq_col, gkv_col, wq_t, wk_t, wv_t, batch, layer):
    t = x.shape[0]
    tm = TOKEN_TILE
    ns = t // batch // tm
    decay_t, query_w, key_w, chunk_decay = ret_tables
    const = lambda shape: pl.BlockSpec(shape, lambda i: (0,) * len(shape), pipeline_mode=pl.Buffered(1))
    of_layer = lambda shape: pl.BlockSpec((None,) + shape, lambda i: (layer,) + (0,) * len(shape),
                                          pipeline_mode=pl.Buffered(1))
    rows_t = lambda rows: pl.BlockSpec((rows, tm), lambda i: (0, i))
    table = lambda rows: pl.BlockSpec((rows, tm), lambda i: (0, i % ns))
    k_tile = pl.BlockSpec((2, tm, N_HEADS * HEAD_PAD), lambda i: (0, i, 0))
    vt_tile = pl.BlockSpec((2, 1, N_HEADS, 1, HEAD_PAD, tm), lambda i: (0, i // ns, 0, i % ns, 0, 0))
    stats_tile = pl.BlockSpec((2, 1, 1, 8, 128), lambda i: (0, i // ns, i % ns, 0, 0))
    qt_tile = pl.BlockSpec((2, 1, N_HEADS * HEAD_PAD, tm), lambda i: (0, i, 0, 0))
    qkv_shapes = [
        jax.ShapeDtypeStruct((2, t // tm, N_HEADS * HEAD_PAD, tm), BF16),
        jax.ShapeDtypeStruct((2, t, N_HEADS * HEAD_PAD), BF16),
        jax.ShapeDtypeStruct((2, batch, N_HEADS, ns, HEAD_PAD, tm), BF16),
        jax.ShapeDtypeStruct((2, batch, ns, 8, 128), F32),
    ]
    return pl.pallas_call(
        functools.partial(_front_body, tiles_per_seq=ns, chunk_decay=chunk_decay),
        grid=(t // tm,),
        in_specs=[
            pl.BlockSpec((tm, D_MODEL), lambda i: (i, 0)),
            const((1, D_MODEL)),
            of_layer((ROWS_ABC, D_MODEL)),
            of_layer((ROWS_D + ROWS_F, D_MODEL)),
            const((GROUP_WIDTH, 1)),
            const((N_HEADS, CHUNK, CHUNK)),
            const((N_HEADS, CHUNK)),
            table(HEAD_DIM // 2), table(HEAD_DIM // 2),
            const((N_HEADS, CHUNK, CHUNK)), const((N_HEADS, 1, CHUNK)), const((N_HEADS, 1, CHUNK)),
            const((ROWS_F, 1)),
            const((tm, tm)),
            table(MLA_ROPE // 2), table(MLA_ROPE // 2),
            const((MLA_Q_LORA, 1)),
            const((MLA_KV_LORA, 1)),
            const((N_HEADS * HEAD_PAD, MLA_Q_LORA)),
            const((GROUP_WIDTH, MLA_KV_LORA)),
            const((GROUP_WIDTH, MLA_KV_LORA)),
        ],
        out_specs=[
            rows_t(GROUP_WIDTH), rows_t(GROUP_WIDTH),
            qt_tile, k_tile, vt_tile, stats_tile,
        ],
        out_shape=[
            jax.ShapeDtypeStruct((GROUP_WIDTH, t), F32),
            jax.ShapeDtypeStruct((GROUP_WIDTH, t), F32),
            *qkv_shapes,
        ],
        scratch_shapes=[
            pltpu.VMEM((ROWS_A, tm), BF16),
            pltpu.VMEM((ROWS_B, tm), BF16),
            pltpu.VMEM((ROWS_C, tm), BF16),
            pltpu.VMEM((ROWS_D, tm), BF16),
            pltpu.VMEM((ROWS_F, tm), F32),
            pltpu.VMEM((N_HEADS, HEAD_DIM, HEAD_DIM), F32),
            pltpu.VMEM((ROWS_F, 1), F32),
            pltpu.VMEM((2 * N_HEADS, 1), F32),
        ],
        compiler_params=_params("arbitrary"),
        name="front",
    )(x, g, w_t, wdf_t, gain_col, w_s, b_s, cos_b, sin_b, decay_t, query_w, key_w, bf_col, tri,
      cos_d, sin_d, gq_col, gkv_col, wq_t, wk_t, wv_t)


def _first_needed_block(stats_ref, b, qi, nb):
    def stat(j, r, h):
        return stats_ref[((b * nb + j) * N_STATS + r) * N_HEADS + h]

    skipped = jnp.int32(0)
    leading = jnp.bool_(True)
    for j in range(nb - 1):
        zero = j < qi
        for h in range(N_HEADS):
            bound = (stat(qi, STAT_QNORM, h) * stat(j, STAT_KNORM, h)
                     + stat(qi, STAT_CMAX, h) - stat(j, STAT_CMIN, h))
            zero = jnp.logical_and(zero, bound - stat(qi, STAT_DIAG, h) < -EXP2_UNDERFLOW)
        leading = jnp.logical_and(leading, zero)
        skipped = skipped + leading.astype(jnp.int32)
    return skipped


def _attn_body(*refs, online):
    stats_ref, qt_ref, k_ref, vt_ref, o_ref, acc_ref, s_ref = refs[:7]
    m_ref = refs[7] if online else None
    nb, _, blk = qt_ref.shape
    half = blk // 2
    batch = pl.program_id(0)

    def first_block(qi):
        return _first_needed_block(stats_ref, batch, qi, nb)

    def logits(qi, kj, h):
        start = pl.multiple_of(kj * blk, blk)
        g0, g1 = h * HEAD_PAD, (h + 1) * HEAD_PAD
        return jnp.dot(k_ref[pl.ds(start, blk), g0:g1], qt_ref[qi, g0:g1, :],
                       preferred_element_type=F32)

    def accumulate(s, kj, h):
        acc = acc_ref[h]
        if online:
            m_old = m_ref[h]
            m_new = jnp.maximum(m_old, jnp.max(s, axis=0, keepdims=True))
            m_ref[h] = m_new
            s = s - m_new
            acc = acc * jnp.exp2(m_old - m_new)
        pv = jnp.dot(vt_ref[0, h, kj, 0:PV_ROWS, :], jnp.exp2(s).astype(BF16), preferred_element_type=F32)
        acc_ref[h] = acc + pv[0:HEAD_DIM + AUG_ROWS]

    s_ref[...] = logits(0, 0, 0)

    def query_block(qi, first):
        acc_ref[...] = jnp.zeros_like(acc_ref)
        if online:
            m_ref[...] = jnp.full(m_ref.shape, -jnp.inf, F32)

        def full_blocks(kj, count):
            s = s_ref[...]
            for d in range(count):
                for h in range(N_HEADS):
                    nxt = (kj + d, h + 1) if h + 1 < N_HEADS else (kj + d + 1, 0)
                    s_next = logits(qi, *nxt)
                    accumulate(s, kj + d, h)
                    s = s_next
            s_ref[...] = s

        n_full = qi - first
        odd = jnp.bitwise_and(n_full, 1)
        pl.when(odd == 1)(lambda: full_blocks(first, 1))

        def pair_step(i, carry):
            full_blocks(first + odd + 2 * i, 2)
            return carry

        lax.fori_loop(0, lax.shift_right_logical(n_full, 1), pair_step, 0)

        nxt_qi = jnp.minimum(qi + 1, nb - 1)
        nxt_first = first_block(nxt_qi)

        if online:
            key_pos = lax.broadcasted_iota(jnp.int32, (blk, blk), 0)
            qry_pos = lax.broadcasted_iota(jnp.int32, (blk, blk), 1)
            visible = key_pos <= qry_pos
            s = s_ref[...]
            for h in range(N_HEADS):
                s_next = logits(qi, qi, h + 1) if h + 1 < N_HEADS else logits(nxt_qi, nxt_first, 0)
                accumulate(jnp.where(visible, s, -jnp.inf), qi, h)
                s = s_next
            s_ref[...] = s
        else:
            start = pl.multiple_of(qi * blk, blk)
            key_pos = lax.broadcasted_iota(jnp.int32, (half, blk), 0)
            qry_pos = lax.broadcasted_iota(jnp.int32, (half, blk), 1)
            visible_a = key_pos <= qry_pos
            visible_b = visible_a[:, 0:half]

            def quadrant_logits(h):
                g0, g1 = h * HEAD_PAD, (h + 1) * HEAD_PAD
                s_a = jnp.dot(k_ref[pl.ds(start, half), g0:g1], qt_ref[qi, g0:g1, :],
                              preferred_element_type=F32)
                s_b = jnp.dot(k_ref[pl.ds(start + half, half), g0:g1], qt_ref[qi, g0:g1, half:blk],
                              preferred_element_type=F32)
                return s_a, s_b

            s_full = s_ref[...]
            s_a, s_b = s_full[0:half], s_full[half:blk, half:blk]
            for h in range(N_HEADS):
                if h + 1 < N_HEADS:
                    s_next = quadrant_logits(h + 1)
                else:
                    s_ref[...] = logits(nxt_qi, nxt_first, 0)
                p_a = jnp.exp2(jnp.where(visible_a, s_a, -jnp.inf)).astype(BF16)
                p_b = jnp.exp2(jnp.where(visible_b, s_b, -jnp.inf)).astype(BF16)
                pv_a = jnp.dot(vt_ref[0, h, qi, 0:PV_ROWS, 0:half], p_a, preferred_element_type=F32)
                pv_b = jnp.dot(vt_ref[0, h, qi, 0:PV_ROWS, half:blk], p_b, preferred_element_type=F32)
                acc_ref[h] = acc_ref[h] + pv_a[0:HEAD_DIM + AUG_ROWS]
                acc_ref[h, :, half:blk] = acc_ref[h, :, half:blk] + pv_b[0:HEAD_DIM + AUG_ROWS]
                if h + 1 < N_HEADS:
                    s_a, s_b = s_next
        for h in range(N_HEADS):
            acc = acc_ref[h]
            o_ref[qi, h * HEAD_DIM:(h + 1) * HEAD_DIM, :] = acc[0:HEAD_DIM] / acc[HEAD_DIM:HEAD_DIM + 1]
        return nxt_first

    lax.fori_loop(0, nb, query_block, jnp.int32(0))


def _attention_call(qt, k, vt, skip_stats, online):
    n_mix, batch, _, nb, _, blk = vt.shape
    seq = nb * blk
    scratch = [pltpu.VMEM((N_HEADS, HEAD_DIM + AUG_ROWS, blk), F32), pltpu.VMEM((blk, blk), F32)]
    if online:
        scratch.append(pltpu.VMEM((N_HEADS, 1, blk), F32))
    return pl.pallas_call(
        functools.partial(_attn_body, online=online),
        grid=(n_mix * batch,),
        in_specs=[
            pl.BlockSpec(memory_space=pltpu.SMEM),
            pl.BlockSpec((None, nb, N_HEADS * HEAD_PAD, blk), lambda i: (i // batch, i % batch, 0, 0)),
            pl.BlockSpec((None, seq, N_HEADS * HEAD_PAD), lambda i: (i // batch, i % batch, 0)),
            pl.BlockSpec((None, 1, N_HEADS, nb, HEAD_PAD, blk), lambda i: (i // batch, i % batch, 0, 0, 0, 0)),
        ],
        out_specs=pl.BlockSpec((nb, GROUP_WIDTH, blk), lambda i: (i, 0, 0)),
        out_shape=jax.ShapeDtypeStruct((n_mix * batch * nb, GROUP_WIDTH, blk), F32),
        scratch_shapes=scratch,
        compiler_params=_params("parallel"),
        name="attention_online" if online else "attention",
    )(skip_stats, qt, k, vt)


def _attention(qt, k, vt, stats):
    stats = stats[:, :, :, :N_STATS, :N_HEADS]
    bound_is_tight = jnp.max(stats[:, :, :, STAT_GAP, :]) <= GAP_LIMIT
    return lax.cond(bound_is_tight,
                    lambda: _attention_call(qt, k, vt, stats.reshape(-1), online=False),
                    lambda: _attention_call(qt, k, vt, stats.reshape(-1), online=True))


def _post_body(x_ref, ya_ref, yb_ref, yc_ref, yd_ref, gg_ref, wo_hbm, gf_ref, wu_hbm, wd_hbm, gl_ref,
               o_ref, wo_ref, wu_ref, wd_ref, stage_tall, stage_wide, sem, *, layer, final):
    n_chunks = D_FF // FF_CHUNK
    tall = [(wo_hbm.at[layer, pl.ds(r, FF_CHUNK), :], wo_ref.at[pl.ds(r, FF_CHUNK), :])
            for r in range(0, D_MODEL, FF_CHUNK)]
    n_out = len(tall)
    tall += [(wd_hbm.at[layer, pl.ds(c * FF_CHUNK, FF_CHUNK), :], wd_ref.at[pl.ds(c * FF_CHUNK, FF_CHUNK), :])
             for c in range(n_chunks)]
    wide = [(wu_hbm.at[layer, :, pl.ds(c * FF_CHUNK, FF_CHUNK)], wu_ref.at[:, pl.ds(c * FF_CHUNK, FF_CHUNK)])
            for c in range(n_chunks)]
    queues = ((tall, stage_tall), (wide, stage_wide))

    def dma(kind, k):
        blocks, stage = queues[kind]
        return pltpu.make_async_copy(blocks[k][0], stage.at[k % 2], sem.at[kind, k % 2])

    def land(kind, k):
        blocks, stage = queues[kind]
        dma(kind, k).wait()
        blocks[k][1][...] = stage[k % 2].astype(BF16)
        if k + 2 < len(blocks):
            dma(kind, k + 2).start()

    def body(load):
        if load:
            for kind in range(2):
                dma(kind, 0).start()
                dma(kind, 1).start()
        ys = []
        for g, y_ref in enumerate((ya_ref, yb_ref, yc_ref, yd_ref)):
            ys.append((_rms_rows(y_ref[...]) * gg_ref[g]).astype(BF16))
        y = jnp.concatenate(ys, axis=0)
        if load:
            for k in range(n_out):
                land(0, k)
        x = x_ref[...] + lax.dot_general(y, wo_ref[...], TN, preferred_element_type=F32)
        h = (x * lax.rsqrt(jnp.mean(x * x, axis=-1, keepdims=True) + EPS) * gf_ref[...]).astype(BF16)
        acc = x
        for c in range(n_chunks):
            c0, c1 = c * FF_CHUNK, (c + 1) * FF_CHUNK
            if load:
                land(1, c)
                land(0, n_out + c)
            a = jnp.maximum(jnp.dot(h, wu_ref[:, c0:c1], preferred_element_type=F32), 0.0)
            acc = acc + jnp.dot((a * a).astype(BF16), wd_ref[c0:c1, :], preferred_element_type=F32)
        if final:
            acc = acc * lax.rsqrt(jnp.mean(acc * acc, axis=-1, keepdims=True) + EPS) * gl_ref[...]
        o_ref[...] = acc

    first_step = pl.program_id(0) == 0
    pl.when(first_step)(functools.partial(body, True))
    pl.when(jnp.logical_not(first_step))(functools.partial(body, False))


def _post(x, ya, yb, ycd, gg_col, w_out, g_ffn, w_up, w_down, g_final, layer, final):
    t = x.shape[0]
    tm = TOKEN_TILE
    const = lambda shape: pl.BlockSpec(shape, lambda i: (0,) * len(shape), pipeline_mode=pl.Buffered(1))
    in_hbm = pl.BlockSpec(memory_space=pl.ANY)
    ytile = pl.BlockSpec((GROUP_WIDTH, tm), lambda i: (0, i))
    n_tiles = t // tm
    yblock = lambda m: pl.BlockSpec((None, GROUP_WIDTH, tm), lambda i: (m * n_tiles + i, 0, 0))
    return pl.pallas_call(
        functools.partial(_post_body, layer=layer, final=final),
        grid=(t // tm,),
        in_specs=[
            pl.BlockSpec((tm, D_MODEL), lambda i: (i, 0)),
            ytile, ytile, yblock(0), yblock(1),
            const((4, GROUP_WIDTH, 1)),
            in_hbm,
            const((1, D_MODEL)),
            in_hbm,
            in_hbm,
            const((1, D_MODEL)),
        ],
        out_specs=pl.BlockSpec((tm, D_MODEL), lambda i: (i, 0)),
        out_shape=jax.ShapeDtypeStruct((t, D_MODEL), F32),
        scratch_shapes=[
            pltpu.VMEM((D_MODEL, D_MODEL), BF16),
            pltpu.VMEM((D_MODEL, D_FF), BF16),
            pltpu.VMEM((D_FF, D_MODEL), BF16),
            pltpu.VMEM((2, FF_CHUNK, D_MODEL), F32),
            pltpu.VMEM((2, D_MODEL, FF_CHUNK), F32),
            pltpu.SemaphoreType.DMA((2, 2)),
        ],
        compiler_params=_params("arbitrary"),
        name="post",
    )(x, ya, yb, ycd, ycd, gg_col, w_out, g_ffn, w_up, w_down, g_final)


def _rope_tables(seq, half):
    inv_freq = np.power(ROPE_BASE, -np.arange(half, dtype=np.float64) / half)
    ang = inv_freq[:, None] * np.arange(seq, dtype=np.float64)[None, :]
    return jnp.asarray(np.cos(ang), F32), jnp.asarray(np.sin(ang), F32)


def _inproj_weights(w_in):
    wt = jnp.swapaxes(w_in, 1, 2).astype(BF16)
    f = wt[:, ROWS_ABC:ROWS_ABC + N_HEADS]
    d = wt[:, ROWS_ABC + N_HEADS:ROWS_ABC + N_HEADS + ROWS_D]
    pad = jnp.zeros((w_in.shape[0], ROWS_F - N_HEADS, D_MODEL), BF16)
    return wt, jnp.concatenate([d, f, pad], axis=1)


def _mla_weights(w_uq, w_ukv):
    wq = w_uq.T.reshape(N_HEADS, MLA_NOPE + MLA_ROPE, MLA_Q_LORA)
    wq = jnp.pad(wq, ((0, 0), (0, HEAD_PAD - MLA_NOPE - MLA_ROPE), (0, 0)))
    wq = wq.reshape(N_HEADS * HEAD_PAD, MLA_Q_LORA).astype(BF16)
    wkv = w_ukv.T.reshape(N_HEADS, 2 * HEAD_DIM, MLA_KV_LORA)
    wk = wkv[:, :HEAD_DIM].reshape(GROUP_WIDTH, MLA_KV_LORA).astype(BF16)
    wv = wkv[:, HEAD_DIM:].reshape(GROUP_WIDTH, MLA_KV_LORA).astype(BF16)
    return wq, wk, wv


def kernel(x, g_mix_norm, w_in, b_forget, g_sgu, w_spatial, b_spatial, g_mla_q, w_uq, g_mla_kv, w_ukv,
           g_group_out, w_out, g_ffn_norm, w_up, w_down, g_final):
    batch, seq, _ = x.shape
    depth = w_in.shape[0]
    assert seq % TOKEN_TILE == 0 and ATTN_BLOCK == TOKEN_TILE
    cos_b, sin_b = _rope_tables(seq, HEAD_DIM // 2)
    cos_d, sin_d = _rope_tables(seq, MLA_ROPE // 2)
    w_abc, w_df = _inproj_weights(w_in)
    ret_tables = _retention_tables()
    tri = jnp.asarray(np.triu(np.ones((TOKEN_TILE, TOKEN_TILE), np.float32)), BF16)
    xf = x.reshape(batch * seq, D_MODEL)
    for l in range(depth):
        bf_col = jnp.pad(b_forget[l], (0, ROWS_F - N_HEADS))[:, None]
        ya, yb, qt, k, vt, stats = _front(
            xf, g_mix_norm[l][None, :], w_abc, w_df,
            g_sgu[l][:, None], w_spatial[l], b_spatial[l], cos_b, sin_b, ret_tables, bf_col, tri,
            cos_d, sin_d, g_mla_q[l][:, None], g_mla_kv[l][:, None], *_mla_weights(w_uq[l], w_ukv[l]),
            batch, l)
        ycd = _attention(qt, k, vt, stats)
        xf = _post(xf, ya, yb, ycd, g_group_out[l].reshape(4, GROUP_WIDTH, 1),
                   w_out, g_ffn_norm[l][None, :], w_up, w_down, g_final[None, :],
                   layer=l, final=(l == depth - 1))
    return xf.reshape(batch, seq, D_MODEL)
```

```python
import functools
import math

import jax
import jax.numpy as jnp
import numpy as np
from jax import lax
from jax.experimental import pallas as pl
from jax.experimental.pallas import tpu as pltpu

F32 = jnp.float32
BF16 = jnp.bfloat16

D_MODEL = 1024
N_HEADS = 4
HEAD_DIM = 64
GROUP_WIDTH = N_HEADS * HEAD_DIM
CHUNK = 128
MLA_Q_LORA = 256
MLA_KV_LORA = 128
MLA_NOPE = 64
MLA_ROPE = 32
ROPE_BASE = 10000.0
D_FF = 4 * D_MODEL
EPS = 1e-6

HEAD_PAD = 128
AUG_ROWS = 8
PV_ROWS = 80
LOG2E = math.log2(math.e)
EXP2_UNDERFLOW = 152.0
STAT_QNORM, STAT_KNORM, STAT_CMAX, STAT_CMIN, STAT_DIAG, STAT_GAP = range(6)
N_STATS = 6
GAP_LIMIT = 64.0
TOKEN_TILE = 512
ATTN_BLOCK = TOKEN_TILE
FF_CHUNK = 512
VMEM_LIMIT = 56 * 1024 * 1024

ROWS_A = 2 * GROUP_WIDTH
ROWS_B = 4 * GROUP_WIDTH
ROWS_C = 3 * GROUP_WIDTH
ROWS_D = MLA_Q_LORA + MLA_KV_LORA + MLA_ROPE
ROWS_F = 8
ROWS_ABC = ROWS_A + ROWS_B + ROWS_C

COL_GAIN = 0
COL_BF = COL_GAIN + GROUP_WIDTH
COL_GQ = COL_BF + ROWS_F
COL_GKV = COL_GQ + MLA_Q_LORA
N_COLS = COL_GKV + MLA_KV_LORA

NT = (((1,), (1,)), ((), ()))
TN = (((0,), (0,)), ((), ()))


def _params(*sem):
    return pltpu.CompilerParams(dimension_semantics=sem, vmem_limit_bytes=VMEM_LIMIT)


def _rot_half_rows(t, cos, sin):
    half = t.shape[0] // 2
    t1, t2 = t[:half], t[half:]
    return jnp.concatenate([t1 * cos - t2 * sin, t1 * sin + t2 * cos], axis=0)


def _standardize_rows(t):
    mu = jnp.mean(t, axis=0, keepdims=True)
    var = jnp.mean(jnp.square(t - mu), axis=0, keepdims=True)
    return (t - mu) * lax.rsqrt(var + EPS)


def _rms_rows(t):
    return t * lax.rsqrt(jnp.mean(t * t, axis=0, keepdims=True) + EPS)


def _sgu(at_ref, gain_ref, ws_ref, bs_ref, o_ref):
    tm = at_ref.shape[1]
    nch = tm // CHUNK
    row = lax.broadcasted_iota(jnp.int32, (CHUNK, CHUNK), 0)
    col = lax.broadcasted_iota(jnp.int32, (CHUNK, CHUNK), 1)
    for h in range(N_HEADS):
        r0, r1 = h * HEAD_DIM, (h + 1) * HEAD_DIM
        u = jax.nn.gelu(at_ref[r0:r1, :].astype(F32))
        v = jax.nn.gelu(at_ref[GROUP_WIDTH + r0:GROUP_WIDTH + r1, :].astype(F32))
        v = _standardize_rows(v) * gain_ref[r0:r1, :]
        w = jnp.where(col <= row, ws_ref[h], 0.0).astype(BF16)
        vs = jnp.concatenate([v[:, c * CHUNK:(c + 1) * CHUNK] for c in range(nch)], axis=0).astype(BF16)
        mixed = lax.dot_general(vs, w, NT, preferred_element_type=F32) + bs_ref[h:h + 1, :]
        for c in range(nch):
            o_ref[r0:r1, c * CHUNK:(c + 1) * CHUNK] = (
                u[:, c * CHUNK:(c + 1) * CHUNK] * mixed[c * HEAD_DIM:(c + 1) * HEAD_DIM])


def _retention_tables():
    log_gamma = np.log1p(-np.exp2(-5.0 - np.arange(N_HEADS, dtype=np.float64)))
    j = np.arange(CHUNK, dtype=np.float64)
    rel = j[None, :] - j[:, None]
    decay_t = np.where(rel >= 0, np.exp(np.maximum(rel, 0.0)[None] * log_gamma[:, None, None]), 0.0)
    query_w = np.exp((j + 1.0)[None, None, :] * log_gamma[:, None, None])
    key_w = np.exp((CHUNK - 1.0 - j)[None, None, :] * log_gamma[:, None, None])
    to_f32 = lambda a: jnp.asarray(a, F32)
    return to_f32(decay_t), to_f32(query_w), to_f32(key_w), [float(v) for v in np.exp(CHUNK * log_gamma)]


def _retention(bt_ref, cos_ref, sin_ref, dec_ref, qw_ref, kw_ref, chunk_decay, o_ref, st_ref):
    tm = bt_ref.shape[1]
    nch = tm // CHUNK
    cos, sin = cos_ref[...], sin_ref[...]
    for h in range(N_HEADS):
        dec_t, query_w, key_w = dec_ref[h], qw_ref[h], kw_ref[h]
        r0, r1 = h * HEAD_DIM, (h + 1) * HEAD_DIM
        q = _rot_half_rows(bt_ref[r0:r1, :].astype(F32), cos, sin)
        k = _rot_half_rows(bt_ref[GROUP_WIDTH + r0:GROUP_WIDTH + r1, :].astype(F32), cos, sin) * (HEAD_DIM ** -0.5)
        v = bt_ref[2 * GROUP_WIDTH + r0:2 * GROUP_WIDTH + r1, :].astype(F32)
        g = bt_ref[3 * GROUP_WIDTH + r0:3 * GROUP_WIDTH + r1, :].astype(F32)
        st = st_ref[h]
        ys = []
        for c in range(nch):
            sl = slice(c * CHUNK, (c + 1) * CHUNK)
            qc, kc, vc = q[:, sl], k[:, sl], v[:, sl]
            kcb = kc.astype(BF16)
            a_t = lax.dot_general(kcb, qc.astype(BF16), TN, preferred_element_type=F32)
            p_t = (a_t * dec_t).astype(BF16)
            intra = jnp.dot(vc.astype(BF16), p_t, preferred_element_type=F32)
            cross = jnp.dot(st.astype(BF16), (qc * query_w).astype(BF16), preferred_element_type=F32)
            ys.append(intra + cross)
            st = chunk_decay[h] * st + lax.dot_general((vc * key_w).astype(BF16), kcb, NT,
                                                       preferred_element_type=F32)
        st_ref[h] = st
        y = _standardize_rows(jnp.concatenate(ys, axis=1))
        o_ref[r0:r1, :] = jax.nn.silu(g) * y


def _select_rows(rows):
    n = rows[0].shape[1]
    ridx = lax.broadcasted_iota(jnp.int32, (AUG_ROWS, n), 0)
    out = jnp.zeros((AUG_ROWS, n), F32)
    for i, r in enumerate(rows):
        out = jnp.where(ridx == i, jnp.broadcast_to(r, (AUG_ROWS, n)), out)
    return out


def _group(feat, extra):
    n = feat.shape[1]
    pad = HEAD_PAD - feat.shape[0] - extra.shape[0]
    return jnp.concatenate([feat, extra, jnp.zeros((pad, n), F32)], axis=0)


def _split3(x):
    hi = x.astype(BF16).astype(F32)
    mid = (x - hi).astype(BF16).astype(F32)
    lo = (x - hi - mid).astype(BF16).astype(F32)
    return hi, mid, lo


def _logit_bound(qb, kb, kmax_ref, row):
    qnorm = jnp.sqrt(jnp.sum(qb * qb, axis=0, keepdims=True))
    knorm = jnp.sqrt(jnp.max(jnp.sum(kb * kb, axis=0, keepdims=True), axis=1, keepdims=True))
    kmax = jnp.maximum(kmax_ref[row:row + 1, :], knorm)
    kmax_ref[row:row + 1, :] = kmax
    return qnorm * kmax, jnp.sum(qb * kb, axis=0, keepdims=True), qnorm, knorm


def _write_stats(st_ref, per_head):
    stat_row = lax.broadcasted_iota(jnp.int32, (8, 128), 0)
    stat_lane = lax.broadcasted_iota(jnp.int32, (8, 128), 1)
    stats = jnp.zeros((8, 128), F32)
    for h, vals in enumerate(per_head):
        for r, val in enumerate(vals):
            stats = jnp.where((stat_row == r) & (stat_lane == h), val, stats)
    st_ref[0, 0] = stats


def _fox_prep(ct_ref, ft_ref, bf_ref, tri_ref, qt_ref, k_ref, vt_ref, st_ref, carry_ref, kmax_ref):
    tm = ct_ref.shape[1]
    x = ft_ref[...] + bf_ref[...]
    lf = jnp.minimum(x, 0.0) - jnp.log1p(jnp.exp(-jnp.abs(x)))
    hi = lf.astype(BF16)
    mid = (lf - hi.astype(F32)).astype(BF16)
    lo = (lf - hi.astype(F32) - mid.astype(F32)).astype(BF16)
    parts = jnp.dot(jnp.concatenate([hi, mid, lo], axis=0), tri_ref[...], preferred_element_type=F32)
    cum = parts[0:8] + parts[8:16] + parts[16:24] + carry_ref[...]
    carry_ref[...] = cum[:, tm - 1:tm]

    cum2 = cum * LOG2E
    one = jnp.ones((1, tm), F32)
    ones_row = _select_rows([one])
    stats = []
    for h in range(N_HEADS):
        r0, r1 = h * HEAD_DIM, (h + 1) * HEAD_DIM
        g0, g1 = h * HEAD_PAD, (h + 1) * HEAD_PAD
        q = ct_ref[r0:r1, :].astype(F32) * (HEAD_DIM ** -0.5 * LOG2E)
        k = ct_ref[GROUP_WIDTH + r0:GROUP_WIDTH + r1, :].astype(F32)
        v = ct_ref[2 * GROUP_WIDTH + r0:2 * GROUP_WIDTH + r1, :].astype(F32)
        qb = q.astype(BF16).astype(F32)
        bound, diag, qnorm, knorm = _logit_bound(qb, k, kmax_ref, h)
        c2 = cum2[h:h + 1]
        q_extra = _select_rows([*_split3(c2 - bound), one, one, one])
        k_extra = _select_rows([one, one, one, *(-part for part in _split3(c2))])
        qt_ref[0, g0:g1, :] = _group(q, q_extra).astype(BF16)
        k_ref[:, g0:g1] = _group(k, k_extra).T.astype(BF16)
        vt_ref[0, h, 0] = _group(v, ones_row).astype(BF16)
        stats.append((
            jnp.max(qnorm, axis=1, keepdims=True),
            knorm,
            jnp.max(c2, axis=1, keepdims=True),
            jnp.min(c2, axis=1, keepdims=True),
            jnp.min(diag, axis=1, keepdims=True),
            jnp.max(bound - diag, axis=1, keepdims=True),
        ))
    _write_stats(st_ref, stats)


def _mla_prep(dt_ref, cos_ref, sin_ref, gq_ref, gkv_ref, wq_ref, wk_ref, wv_ref, qt_ref, k_ref, vt_ref,
              st_ref, kmax_ref):
    tm = dt_ref.shape[1]
    cos, sin = cos_ref[...], sin_ref[...]
    cq = _rms_rows(dt_ref[0:MLA_Q_LORA, :].astype(F32)) * gq_ref[...]
    ckv = _rms_rows(dt_ref[MLA_Q_LORA:MLA_Q_LORA + MLA_KV_LORA, :].astype(F32)) * gkv_ref[...]
    kr = _rot_half_rows(dt_ref[MLA_Q_LORA + MLA_KV_LORA:ROWS_D, :].astype(F32), cos, sin)
    ckv_b = ckv.astype(BF16)
    q_all = jnp.dot(wq_ref[...], cq.astype(BF16), preferred_element_type=F32)
    q_all = q_all * ((MLA_NOPE + MLA_ROPE) ** -0.5 * LOG2E)
    k_all = jnp.dot(wk_ref[...], ckv_b, preferred_element_type=F32)
    v_all = jnp.dot(wv_ref[...], ckv_b, preferred_element_type=F32)
    one = jnp.ones((1, tm), F32)
    ones_row = _select_rows([one])
    k_extra = _select_rows([one, one, one])
    zero = jnp.zeros((1, 1), F32)
    stats = []
    for h in range(N_HEADS):
        r0, r1 = h * HEAD_DIM, (h + 1) * HEAD_DIM
        g0, g1 = h * HEAD_PAD, (h + 1) * HEAD_PAD
        qg = q_all[g0:g1]
        q_rope = _rot_half_rows(qg[MLA_NOPE:MLA_NOPE + MLA_ROPE], cos, sin)
        q = jnp.concatenate([qg[0:MLA_NOPE], q_rope], axis=0)
        k = jnp.concatenate([k_all[r0:r1], kr], axis=0)
        qb, kb = q.astype(BF16).astype(F32), k.astype(BF16).astype(F32)
        bound, diag, _, _ = _logit_bound(qb, kb, kmax_ref, N_HEADS + h)
        q_extra = _select_rows([*_split3(-bound)])
        qt_ref[0, g0:g1, :] = _group(q, q_extra).astype(BF16)
        k_ref[:, g0:g1] = _group(k, k_extra).T.astype(BF16)
        vt_ref[0, h, 0] = _group(v_all[r0:r1], ones_row).astype(BF16)
        stats.append((zero,) * STAT_GAP + (jnp.max(bound - diag, axis=1, keepdims=True),))
    _write_stats(st_ref, stats)


def _front_body(x_ref, g_ref, w_ref, wdf_ref,
                cols_ref, ws_ref, bs_ref,
                cosb_ref, sinb_ref, dec_ref, qw_ref, kw_ref,
                tri_ref,
                cosd_ref, sind_ref, wq_ref, wk_ref, wv_ref,
                ya_ref, yb_ref,
                qt_ref, k_ref, vt_ref, stats_ref,
                at_ref, bt_ref, ct_ref, dt_ref, ft_ref, state_ref, carry_ref, kmax_ref, *,
                tiles_per_seq, chunk_decay):
    @pl.when(pl.program_id(0) % tiles_per_seq == 0)
    def _():
        state_ref[...] = jnp.zeros_like(state_ref)
        carry_ref[...] = jnp.zeros_like(carry_ref)
        kmax_ref[...] = jnp.zeros_like(kmax_ref)

    gain_ref = cols_ref.at[pl.ds(COL_GAIN, GROUP_WIDTH)]
    bf_ref = cols_ref.at[pl.ds(COL_BF, ROWS_F)]
    gq_ref = cols_ref.at[pl.ds(COL_GQ, MLA_Q_LORA)]
    gkv_ref = cols_ref.at[pl.ds(COL_GKV, MLA_KV_LORA)]

    x = x_ref[...]
    h = (x * lax.rsqrt(jnp.mean(x * x, axis=-1, keepdims=True) + EPS) * g_ref[...]).astype(BF16)

    def proj(w):
        return lax.dot_general(w, h, NT, preferred_element_type=F32)

    half_b = ROWS_B // 2
    zdf = proj(wdf_ref[...])
    dt_ref[...] = zdf[0:ROWS_D].astype(BF16)
    ft_ref[...] = zdf[ROWS_D:ROWS_D + ROWS_F]
    at_ref[...] = proj(w_ref[0:ROWS_A, :]).astype(BF16)
    _mla_prep(dt_ref, cosd_ref, sind_ref, gq_ref, gkv_ref, wq_ref, wk_ref, wv_ref,
              qt_ref.at[1], k_ref.at[1], vt_ref.at[1], stats_ref.at[1], kmax_ref)
    ct_ref[...] = proj(w_ref[ROWS_A + ROWS_B:ROWS_ABC, :]).astype(BF16)
    _sgu(at_ref, gain_ref, ws_ref, bs_ref, ya_ref)
    bt_ref[0:half_b, :] = proj(w_ref[ROWS_A:ROWS_A + half_b, :]).astype(BF16)
    _fox_prep(ct_ref, ft_ref, bf_ref, tri_ref, qt_ref.at[0], k_ref.at[0], vt_ref.at[0], stats_ref.at[0],
              carry_ref, kmax_ref)
    bt_ref[half_b:ROWS_B, :] = proj(w_ref[ROWS_A + half_b:ROWS_A + ROWS_B, :]).astype(BF16)
    _retention(bt_ref, cosb_ref, sinb_ref, dec_ref, qw_ref, kw_ref, chunk_decay, yb_ref, state_ref)


def _front(x, g, w_t, wdf_t, cols, w_s, b_s, cos_b, sin_b, ret_tables, tri,
           cos_d, sin_d, wq_t, wk_t, wv_t, batch, layer):
    t = x.shape[0]
    tm = TOKEN_TILE
    ns = t // batch // tm
    decay_t, query_w, key_w, chunk_decay = ret_tables
    const = lambda shape: pl.BlockSpec(shape, lambda i: (0,) * len(shape), pipeline_mode=pl.Buffered(1))
    of_layer = lambda shape: pl.BlockSpec((None,) + shape, lambda i: (layer,) + (0,) * len(shape),
                                          pipeline_mode=pl.Buffered(1))
    rows_t = lambda rows: pl.BlockSpec((rows, tm), lambda i: (0, i))
    table = lambda rows: pl.BlockSpec((rows, tm), lambda i: (0, i % ns))
    k_tile = pl.BlockSpec((2, tm, N_HEADS * HEAD_PAD), lambda i: (0, i, 0))
    vt_tile = pl.BlockSpec((2, 1, N_HEADS, 1, HEAD_PAD, tm), lambda i: (0, i // ns, 0, i % ns, 0, 0))
    stats_tile = pl.BlockSpec((2, 1, 1, 8, 128), lambda i: (0, i // ns, i % ns, 0, 0))
    qt_tile = pl.BlockSpec((2, 1, N_HEADS * HEAD_PAD, tm), lambda i: (0, i, 0, 0))
    qkv_shapes = [
        jax.ShapeDtypeStruct((2, t // tm, N_HEADS * HEAD_PAD, tm), BF16),
        jax.ShapeDtypeStruct((2, t, N_HEADS * HEAD_PAD), BF16),
        jax.ShapeDtypeStruct((2, batch, N_HEADS, ns, HEAD_PAD, tm), BF16),
        jax.ShapeDtypeStruct((2, batch, ns, 8, 128), F32),
    ]
    return pl.pallas_call(
        functools.partial(_front_body, tiles_per_seq=ns, chunk_decay=chunk_decay),
        grid=(t // tm,),
        in_specs=[
            pl.BlockSpec((tm, D_MODEL), lambda i: (i, 0)),
            of_layer((1, D_MODEL)),
            of_layer((ROWS_ABC, D_MODEL)),
            of_layer((ROWS_D + ROWS_F, D_MODEL)),
            of_layer((N_COLS, 1)),
            of_layer((N_HEADS, CHUNK, CHUNK)),
            of_layer((N_HEADS, CHUNK)),
            table(HEAD_DIM // 2), table(HEAD_DIM // 2),
            const((N_HEADS, CHUNK, CHUNK)), const((N_HEADS, 1, CHUNK)), const((N_HEADS, 1, CHUNK)),
            const((tm, tm)),
            table(MLA_ROPE // 2), table(MLA_ROPE // 2),
            of_layer((N_HEADS * HEAD_PAD, MLA_Q_LORA)),
            of_layer((GROUP_WIDTH, MLA_KV_LORA)),
            of_layer((GROUP_WIDTH, MLA_KV_LORA)),
        ],
        out_specs=[
            rows_t(GROUP_WIDTH), rows_t(GROUP_WIDTH),
            qt_tile, k_tile, vt_tile, stats_tile,
        ],
        out_shape=[
            jax.ShapeDtypeStruct((GROUP_WIDTH, t), F32),
            jax.ShapeDtypeStruct((GROUP_WIDTH, t), F32),
            *qkv_shapes,
        ],
        scratch_shapes=[
            pltpu.VMEM((ROWS_A, tm), BF16),
            pltpu.VMEM((ROWS_B, tm), BF16),
            pltpu.VMEM((ROWS_C, tm), BF16),
            pltpu.VMEM((ROWS_D, tm), BF16),
            pltpu.VMEM((ROWS_F, tm), F32),
            pltpu.VMEM((N_HEADS, HEAD_DIM, HEAD_DIM), F32),
            pltpu.VMEM((ROWS_F, 1), F32),
            pltpu.VMEM((2 * N_HEADS, 1), F32),
        ],
        compiler_params=_params("arbitrary"),
        name="front",
    )(x, g, w_t, wdf_t, cols, w_s, b_s, cos_b, sin_b, decay_t, query_w, key_w, tri,
      cos_d, sin_d, wq_t, wk_t, wv_t)


def _first_needed_block(stats_ref, b, qi, nb):
    def stat(j, r, h):
        return stats_ref[((b * nb + j) * N_STATS + r) * N_HEADS + h]

    skipped = jnp.int32(0)
    leading = jnp.bool_(True)
    for j in range(nb - 1):
        zero = j < qi
        for h in range(N_HEADS):
            bound = (stat(qi, STAT_QNORM, h) * stat(j, STAT_KNORM, h)
                     + stat(qi, STAT_CMAX, h) - stat(j, STAT_CMIN, h))
            zero = jnp.logical_and(zero, bound - stat(qi, STAT_DIAG, h) < -EXP2_UNDERFLOW)
        leading = jnp.logical_and(leading, zero)
        skipped = skipped + leading.astype(jnp.int32)
    return skipped


def _attn_body(*refs, online):
    stats_ref, qt_ref, k_ref, vt_ref, o_ref, acc_ref, s_ref = refs[:7]
    m_ref = refs[7] if online else None
    nb, _, blk = qt_ref.shape
    half = blk // 2
    batch = pl.program_id(0)

    def first_block(qi):
        return _first_needed_block(stats_ref, batch, qi, nb)

    def logits(qi, kj, h):
        start = pl.multiple_of(kj * blk, blk)
        g0, g1 = h * HEAD_PAD, (h + 1) * HEAD_PAD
        return jnp.dot(k_ref[pl.ds(start, blk), g0:g1], qt_ref[qi, g0:g1, :],
                       preferred_element_type=F32)

    def accumulate(s, kj, h):
        acc = acc_ref[h]
        if online:
            m_old = m_ref[h]
            m_new = jnp.maximum(m_old, jnp.max(s, axis=0, keepdims=True))
            m_ref[h] = m_new
            s = s - m_new
            acc = acc * jnp.exp2(m_old - m_new)
        pv = jnp.dot(vt_ref[0, h, kj, 0:PV_ROWS, :], jnp.exp2(s).astype(BF16), preferred_element_type=F32)
        acc_ref[h] = acc + pv[0:HEAD_DIM + AUG_ROWS]

    s_ref[...] = logits(0, 0, 0)

    def query_block(qi, first):
        acc_ref[...] = jnp.zeros_like(acc_ref)
        if online:
            m_ref[...] = jnp.full(m_ref.shape, -jnp.inf, F32)

        def full_blocks(kj, count):
            s = s_ref[...]
            for d in range(count):
                for h in range(N_HEADS):
                    nxt = (kj + d, h + 1) if h + 1 < N_HEADS else (kj + d + 1, 0)
                    s_next = logits(qi, *nxt)
                    accumulate(s, kj + d, h)
                    s = s_next
            s_ref[...] = s

        n_full = qi - first
        odd = jnp.bitwise_and(n_full, 1)
        pl.when(odd == 1)(lambda: full_blocks(first, 1))

        def pair_step(i, carry):
            full_blocks(first + odd + 2 * i, 2)
            return carry

        lax.fori_loop(0, lax.shift_right_logical(n_full, 1), pair_step, 0)

        nxt_qi = jnp.minimum(qi + 1, nb - 1)
        nxt_first = first_block(nxt_qi)

        if online:
            key_pos = lax.broadcasted_iota(jnp.int32, (blk, blk), 0)
            qry_pos = lax.broadcasted_iota(jnp.int32, (blk, blk), 1)
            visible = key_pos <= qry_pos
            s = s_ref[...]
            for h in range(N_HEADS):
                s_next = logits(qi, qi, h + 1) if h + 1 < N_HEADS else logits(nxt_qi, nxt_first, 0)
                accumulate(jnp.where(visible, s, -jnp.inf), qi, h)
                s = s_next
            s_ref[...] = s
        else:
            start = pl.multiple_of(qi * blk, blk)
            key_pos = lax.broadcasted_iota(jnp.int32, (half, blk), 0)
            qry_pos = lax.broadcasted_iota(jnp.int32, (half, blk), 1)
            visible_a = key_pos <= qry_pos
            visible_b = visible_a[:, 0:half]

            def quadrant_logits(h):
                g0, g1 = h * HEAD_PAD, (h + 1) * HEAD_PAD
                s_a = jnp.dot(k_ref[pl.ds(start, half), g0:g1], qt_ref[qi, g0:g1, :],
                              preferred_element_type=F32)
                s_b = jnp.dot(k_ref[pl.ds(start + half, half), g0:g1], qt_ref[qi, g0:g1, half:blk],
                              preferred_element_type=F32)
                return s_a, s_b

            s_full = s_ref[...]
            s_a, s_b = s_full[0:half], s_full[half:blk, half:blk]
            for h in range(N_HEADS):
                if h + 1 < N_HEADS:
                    s_next = quadrant_logits(h + 1)
                else:
                    s_ref[...] = logits(nxt_qi, nxt_first, 0)
                p_a = jnp.exp2(jnp.where(visible_a, s_a, -jnp.inf)).astype(BF16)
                p_b = jnp.exp2(jnp.where(visible_b, s_b, -jnp.inf)).astype(BF16)
                pv_a = jnp.dot(vt_ref[0, h, qi, 0:PV_ROWS, 0:half], p_a, preferred_element_type=F32)
                pv_b = jnp.dot(vt_ref[0, h, qi, 0:PV_ROWS, half:blk], p_b, preferred_element_type=F32)
                acc_ref[h] = acc_ref[h] + pv_a[0:HEAD_DIM + AUG_ROWS]
                acc_ref[h, :, half:blk] = acc_ref[h, :, half:blk] + pv_b[0:HEAD_DIM + AUG_ROWS]
                if h + 1 < N_HEADS:
                    s_a, s_b = s_next
        for h in range(N_HEADS):
            acc = acc_ref[h]
            o_ref[qi, h * HEAD_DIM:(h + 1) * HEAD_DIM, :] = acc[0:HEAD_DIM] / acc[HEAD_DIM:HEAD_DIM + 1]
        return nxt_first

    lax.fori_loop(0, nb, query_block, jnp.int32(0))


def _attention_call(qt, k, vt, skip_stats, online):
    n_mix, batch, _, nb, _, blk = vt.shape
    seq = nb * blk
    scratch = [pltpu.VMEM((N_HEADS, HEAD_DIM + AUG_ROWS, blk), F32), pltpu.VMEM((blk, blk), F32)]
    if online:
        scratch.append(pltpu.VMEM((N_HEADS, 1, blk), F32))
    return pl.pallas_call(
        functools.partial(_attn_body, online=online),
        grid=(n_mix * batch,),
        in_specs=[
            pl.BlockSpec(memory_space=pltpu.SMEM),
            pl.BlockSpec((None, nb, N_HEADS * HEAD_PAD, blk), lambda i: (i // batch, i % batch, 0, 0)),
            pl.BlockSpec((None, seq, N_HEADS * HEAD_PAD), lambda i: (i // batch, i % batch, 0)),
            pl.BlockSpec((None, 1, N_HEADS, nb, HEAD_PAD, blk), lambda i: (i // batch, i % batch, 0, 0, 0, 0)),
        ],
        out_specs=pl.BlockSpec((nb, GROUP_WIDTH, blk), lambda i: (i, 0, 0)),
        out_shape=jax.ShapeDtypeStruct((n_mix * batch * nb, GROUP_WIDTH, blk), F32),
        scratch_shapes=scratch,
        compiler_params=_params("parallel"),
        name="attention_online" if online else "attention",
    )(skip_stats, qt, k, vt)


def _attention(qt, k, vt, stats):
    stats = stats[:, :, :, :N_STATS, :N_HEADS]
    bound_is_tight = jnp.max(stats[:, :, :, STAT_GAP, :]) <= GAP_LIMIT
    return lax.cond(bound_is_tight,
                    lambda: _attention_call(qt, k, vt, stats.reshape(-1), online=False),
                    lambda: _attention_call(qt, k, vt, stats.reshape(-1), online=True))


def _post_body(x_ref, ya_ref, yb_ref, yc_ref, yd_ref, gg_ref, wo_hbm, gf_ref, wu_hbm, wd_hbm, gl_ref,
               o_ref, wo_ref, wu_ref, wd_ref, stage_tall, stage_wide, sem, *, layer, final):
    n_chunks = D_FF // FF_CHUNK
    tall = [(wo_hbm.at[layer, pl.ds(r, FF_CHUNK), :], wo_ref.at[pl.ds(r, FF_CHUNK), :])
            for r in range(0, D_MODEL, FF_CHUNK)]
    n_out = len(tall)
    tall += [(wd_hbm.at[layer, pl.ds(c * FF_CHUNK, FF_CHUNK), :], wd_ref.at[pl.ds(c * FF_CHUNK, FF_CHUNK), :])
             for c in range(n_chunks)]
    wide = [(wu_hbm.at[layer, :, pl.ds(c * FF_CHUNK, FF_CHUNK)], wu_ref.at[:, pl.ds(c * FF_CHUNK, FF_CHUNK)])
            for c in range(n_chunks)]
    queues = ((tall, stage_tall), (wide, stage_wide))

    def dma(kind, k):
        blocks, stage = queues[kind]
        return pltpu.make_async_copy(blocks[k][0], stage.at[k % 2], sem.at[kind, k % 2])

    def land(kind, k):
        blocks, stage = queues[kind]
        dma(kind, k).wait()
        blocks[k][1][...] = stage[k % 2].astype(BF16)
        if k + 2 < len(blocks):
            dma(kind, k + 2).start()

    def body(load):
        if load:
            for kind in range(2):
                dma(kind, 0).start()
                dma(kind, 1).start()
        ys = []
        for g, y_ref in enumerate((ya_ref, yb_ref, yc_ref, yd_ref)):
            ys.append((_rms_rows(y_ref[...]) * gg_ref[g]).astype(BF16))
        y = jnp.concatenate(ys, axis=0)
        if load:
            for k in range(n_out):
                land(0, k)
        x = x_ref[...] + lax.dot_general(y, wo_ref[...], TN, preferred_element_type=F32)
        h = (x * lax.rsqrt(jnp.mean(x * x, axis=-1, keepdims=True) + EPS) * gf_ref[...]).astype(BF16)
        acc = x
        for c in range(n_chunks):
            c0, c1 = c * FF_CHUNK, (c + 1) * FF_CHUNK
            if load:
                land(1, c)
                land(0, n_out + c)
            a = jnp.maximum(jnp.dot(h, wu_ref[:, c0:c1], preferred_element_type=F32), 0.0)
            acc = acc + jnp.dot((a * a).astype(BF16), wd_ref[c0:c1, :], preferred_element_type=F32)
        if final:
            acc = acc * lax.rsqrt(jnp.mean(acc * acc, axis=-1, keepdims=True) + EPS) * gl_ref[...]
        o_ref[...] = acc

    first_step = pl.program_id(0) == 0
    pl.when(first_step)(functools.partial(body, True))
    pl.when(jnp.logical_not(first_step))(functools.partial(body, False))


def _post(x, ya, yb, ycd, gg_col, w_out, g_ffn, w_up, w_down, g_final, layer, final):
    t = x.shape[0]
    tm = TOKEN_TILE
    const = lambda shape: pl.BlockSpec(shape, lambda i: (0,) * len(shape), pipeline_mode=pl.Buffered(1))
    of_layer = lambda shape: pl.BlockSpec((None,) + shape, lambda i: (layer,) + (0,) * len(shape),
                                          pipeline_mode=pl.Buffered(1))
    in_hbm = pl.BlockSpec(memory_space=pl.ANY)
    ytile = pl.BlockSpec((GROUP_WIDTH, tm), lambda i: (0, i))
    n_tiles = t // tm
    yblock = lambda m: pl.BlockSpec((None, GROUP_WIDTH, tm), lambda i: (m * n_tiles + i, 0, 0))
    return pl.pallas_call(
        functools.partial(_post_body, layer=layer, final=final),
        grid=(t // tm,),
        in_specs=[
            pl.BlockSpec((tm, D_MODEL), lambda i: (i, 0)),
            ytile, ytile, yblock(0), yblock(1),
            of_layer((4, GROUP_WIDTH, 1)),
            in_hbm,
            of_layer((1, D_MODEL)),
            in_hbm,
            in_hbm,
            const((1, D_MODEL)),
        ],
        out_specs=pl.BlockSpec((tm, D_MODEL), lambda i: (i, 0)),
        out_shape=jax.ShapeDtypeStruct((t, D_MODEL), F32),
        scratch_shapes=[
            pltpu.VMEM((D_MODEL, D_MODEL), BF16),
            pltpu.VMEM((D_MODEL, D_FF), BF16),
            pltpu.VMEM((D_FF, D_MODEL), BF16),
            pltpu.VMEM((2, FF_CHUNK, D_MODEL), F32),
            pltpu.VMEM((2, D_MODEL, FF_CHUNK), F32),
            pltpu.SemaphoreType.DMA((2, 2)),
        ],
        compiler_params=_params("arbitrary"),
        name="post",
    )(x, ya, yb, ycd, ycd, gg_col, w_out, g_ffn, w_up, w_down, g_final)


def _rope_tables(seq, half):
    inv_freq = np.power(ROPE_BASE, -np.arange(half, dtype=np.float64) / half)
    ang = inv_freq[:, None] * np.arange(seq, dtype=np.float64)[None, :]
    return jnp.asarray(np.cos(ang), F32), jnp.asarray(np.sin(ang), F32)


def _inproj_weights(w_in):
    wt = jnp.swapaxes(w_in, 1, 2).astype(BF16)
    f = wt[:, ROWS_ABC:ROWS_ABC + N_HEADS]
    d = wt[:, ROWS_ABC + N_HEADS:ROWS_ABC + N_HEADS + ROWS_D]
    pad = jnp.zeros((w_in.shape[0], ROWS_F - N_HEADS, D_MODEL), BF16)
    return wt, jnp.concatenate([d, f, pad], axis=1)


def _mla_weights(w_uq, w_ukv):
    depth = w_uq.shape[0]
    wq = jnp.swapaxes(w_uq, 1, 2).reshape(depth, N_HEADS, MLA_NOPE + MLA_ROPE, MLA_Q_LORA)
    wq = jnp.pad(wq, ((0, 0), (0, 0), (0, HEAD_PAD - MLA_NOPE - MLA_ROPE), (0, 0)))
    wq = wq.reshape(depth, N_HEADS * HEAD_PAD, MLA_Q_LORA).astype(BF16)
    wkv = jnp.swapaxes(w_ukv, 1, 2).reshape(depth, N_HEADS, 2 * HEAD_DIM, MLA_KV_LORA)
    wk = wkv[:, :, :HEAD_DIM].reshape(depth, GROUP_WIDTH, MLA_KV_LORA).astype(BF16)
    wv = wkv[:, :, HEAD_DIM:].reshape(depth, GROUP_WIDTH, MLA_KV_LORA).astype(BF16)
    return wq, wk, wv


def kernel(x, g_mix_norm, w_in, b_forget, g_sgu, w_spatial, b_spatial, g_mla_q, w_uq, g_mla_kv, w_ukv,
           g_group_out, w_out, g_ffn_norm, w_up, w_down, g_final):
    batch, seq, _ = x.shape
    depth = w_in.shape[0]
    assert seq % TOKEN_TILE == 0 and ATTN_BLOCK == TOKEN_TILE
    cos_b, sin_b = _rope_tables(seq, HEAD_DIM // 2)
    cos_d, sin_d = _rope_tables(seq, MLA_ROPE // 2)
    w_abc, w_df = _inproj_weights(w_in)
    ret_tables = _retention_tables()
    tri = jnp.asarray(np.triu(np.ones((TOKEN_TILE, TOKEN_TILE), np.float32)), BF16)
    cols = jnp.concatenate([g_sgu, jnp.pad(b_forget, ((0, 0), (0, ROWS_F - N_HEADS))), g_mla_q, g_mla_kv],
                           axis=1)[:, :, None]
    mla_w = _mla_weights(w_uq, w_ukv)
    g_mix, g_ffn = g_mix_norm[:, None, :], g_ffn_norm[:, None, :]
    g_group = g_group_out.reshape(depth, 4, GROUP_WIDTH, 1)
    xf = x.reshape(batch * seq, D_MODEL)
    for l in range(depth):
        ya, yb, qt, k, vt, stats = _front(
            xf, g_mix, w_abc, w_df, cols, w_spatial, b_spatial, cos_b, sin_b, ret_tables, tri,
            cos_d, sin_d, *mla_w, batch, l)
        ycd = _attention(qt, k, vt, stats)
        xf = _post(xf, ya, yb, ycd, g_group, w_out, g_ffn, w_up, w_down, g_final[None, :],
                   layer=l, final=(l == depth - 1))
    return xf.reshape(batch, seq, D_MODEL)
```

```python
import functools
import math

import jax
import jax.numpy as jnp
import numpy as np
from jax import lax
from jax.experimental import pallas as pl
from jax.experimental.pallas import tpu as pltpu

F32 = jnp.float32
BF16 = jnp.bfloat16

D_MODEL = 1024
N_HEADS = 4
HEAD_DIM = 64
GROUP_WIDTH = N_HEADS * HEAD_DIM
CHUNK = 128
MLA_Q_LORA = 256
MLA_KV_LORA = 128
MLA_NOPE = 64
MLA_ROPE = 32
ROPE_BASE = 10000.0
D_FF = 4 * D_MODEL
EPS = 1e-6

HEAD_PAD = 128
AUG_ROWS = 8
PV_ROWS = 80
LOG2E = math.log2(math.e)
EXP2_UNDERFLOW = 152.0
STAT_QNORM, STAT_KNORM, STAT_CMAX, STAT_CMIN, STAT_DIAG, STAT_GAP = range(6)
N_STATS = 6
GAP_LIMIT = 64.0
TOKEN_TILE = 512
ATTN_BLOCK = TOKEN_TILE
FF_CHUNK = 512
VMEM_LIMIT = 56 * 1024 * 1024

ROWS_A = 2 * GROUP_WIDTH
ROWS_B = 4 * GROUP_WIDTH
ROWS_C = 3 * GROUP_WIDTH
ROWS_D = MLA_Q_LORA + MLA_KV_LORA + MLA_ROPE
ROWS_F = 8
ROWS_ABC = ROWS_A + ROWS_B + ROWS_C

COL_GAIN = 0
COL_BF = COL_GAIN + GROUP_WIDTH
COL_GQ = COL_BF + ROWS_F
COL_GKV = COL_GQ + MLA_Q_LORA
N_COLS = COL_GKV + MLA_KV_LORA

NT = (((1,), (1,)), ((), ()))
TN = (((0,), (0,)), ((), ()))


def _params(*sem):
    return pltpu.CompilerParams(dimension_semantics=sem, vmem_limit_bytes=VMEM_LIMIT)


def _rot_half_rows(t, cos, sin):
    half = t.shape[0] // 2
    t1, t2 = t[:half], t[half:]
    return jnp.concatenate([t1 * cos - t2 * sin, t1 * sin + t2 * cos], axis=0)


def _standardize_rows(t):
    mu = jnp.mean(t, axis=0, keepdims=True)
    var = jnp.mean(jnp.square(t - mu), axis=0, keepdims=True)
    return (t - mu) * lax.rsqrt(var + EPS)


def _rms_rows(t):
    return t * lax.rsqrt(jnp.mean(t * t, axis=0, keepdims=True) + EPS)


def _sgu(at_ref, gain_ref, ws_ref, bs_ref, o_ref):
    tm = at_ref.shape[1]
    nch = tm // CHUNK
    row = lax.broadcasted_iota(jnp.int32, (CHUNK, CHUNK), 0)
    col = lax.broadcasted_iota(jnp.int32, (CHUNK, CHUNK), 1)
    for h in range(N_HEADS):
        r0, r1 = h * HEAD_DIM, (h + 1) * HEAD_DIM
        u = jax.nn.gelu(at_ref[r0:r1, :].astype(F32))
        v = jax.nn.gelu(at_ref[GROUP_WIDTH + r0:GROUP_WIDTH + r1, :].astype(F32))
        v = _standardize_rows(v) * gain_ref[r0:r1, :]
        w = jnp.where(col <= row, ws_ref[h], 0.0).astype(BF16)
        vs = jnp.concatenate([v[:, c * CHUNK:(c + 1) * CHUNK] for c in range(nch)], axis=0).astype(BF16)
        mixed = lax.dot_general(vs, w, NT, preferred_element_type=F32) + bs_ref[h:h + 1, :]
        for c in range(nch):
            o_ref[r0:r1, c * CHUNK:(c + 1) * CHUNK] = (
                u[:, c * CHUNK:(c + 1) * CHUNK] * mixed[c * HEAD_DIM:(c + 1) * HEAD_DIM])


def _retention_tables():
    log_gamma = np.log1p(-np.exp2(-5.0 - np.arange(N_HEADS, dtype=np.float64)))
    j = np.arange(CHUNK, dtype=np.float64)
    rel = j[None, :] - j[:, None]
    decay_t = np.where(rel >= 0, np.exp(np.maximum(rel, 0.0)[None] * log_gamma[:, None, None]), 0.0)
    query_w = np.exp((j + 1.0)[None, None, :] * log_gamma[:, None, None])
    key_w = np.exp((CHUNK - 1.0 - j)[None, None, :] * log_gamma[:, None, None])
    to_f32 = lambda a: jnp.asarray(a, F32)
    return to_f32(decay_t), to_f32(query_w), to_f32(key_w), [float(v) for v in np.exp(CHUNK * log_gamma)]


def _retention(bt_ref, cos_ref, sin_ref, dec_ref, qw_ref, kw_ref, chunk_decay, o_ref, st_ref):
    tm = bt_ref.shape[1]
    nch = tm // CHUNK
    cos, sin = cos_ref[...], sin_ref[...]
    for h in range(N_HEADS):
        dec_t, query_w, key_w = dec_ref[h], qw_ref[h], kw_ref[h]
        r0, r1 = h * HEAD_DIM, (h + 1) * HEAD_DIM
        q = _rot_half_rows(bt_ref[r0:r1, :].astype(F32), cos, sin)
        k = _rot_half_rows(bt_ref[GROUP_WIDTH + r0:GROUP_WIDTH + r1, :].astype(F32), cos, sin) * (HEAD_DIM ** -0.5)
        v = bt_ref[2 * GROUP_WIDTH + r0:2 * GROUP_WIDTH + r1, :].astype(F32)
        g = bt_ref[3 * GROUP_WIDTH + r0:3 * GROUP_WIDTH + r1, :].astype(F32)
        st = st_ref[h]
        ys = []
        for c in range(nch):
            sl = slice(c * CHUNK, (c + 1) * CHUNK)
            qc, kc, vc = q[:, sl], k[:, sl], v[:, sl]
            kcb = kc.astype(BF16)
            a_t = lax.dot_general(kcb, qc.astype(BF16), TN, preferred_element_type=F32)
            p_t = (a_t * dec_t).astype(BF16)
            intra = jnp.dot(vc.astype(BF16), p_t, preferred_element_type=F32)
            cross = jnp.dot(st.astype(BF16), (qc * query_w).astype(BF16), preferred_element_type=F32)
            ys.append(intra + cross)
            st = chunk_decay[h] * st + lax.dot_general((vc * key_w).astype(BF16), kcb, NT,
                                                       preferred_element_type=F32)
        st_ref[h] = st
        y = _standardize_rows(jnp.concatenate(ys, axis=1))
        o_ref[r0:r1, :] = jax.nn.silu(g) * y


def _select_rows(rows):
    n = rows[0].shape[1]
    ridx = lax.broadcasted_iota(jnp.int32, (AUG_ROWS, n), 0)
    out = jnp.zeros((AUG_ROWS, n), F32)
    for i, r in enumerate(rows):
        out = jnp.where(ridx == i, jnp.broadcast_to(r, (AUG_ROWS, n)), out)
    return out


def _group(feat, extra):
    n = feat.shape[1]
    pad = HEAD_PAD - feat.shape[0] - extra.shape[0]
    return jnp.concatenate([feat, extra, jnp.zeros((pad, n), F32)], axis=0)


def _split3(x):
    hi = x.astype(BF16).astype(F32)
    mid = (x - hi).astype(BF16).astype(F32)
    lo = (x - hi - mid).astype(BF16).astype(F32)
    return hi, mid, lo


def _logit_bound(qb, kb, kmax_ref, row):
    qnorm = jnp.sqrt(jnp.sum(qb * qb, axis=0, keepdims=True))
    knorm = jnp.sqrt(jnp.max(jnp.sum(kb * kb, axis=0, keepdims=True), axis=1, keepdims=True))
    kmax = jnp.maximum(kmax_ref[row:row + 1, :], knorm)
    kmax_ref[row:row + 1, :] = kmax
    return qnorm * kmax, jnp.sum(qb * kb, axis=0, keepdims=True), qnorm, knorm


def _write_stats(st_ref, per_head):
    stat_row = lax.broadcasted_iota(jnp.int32, (8, 128), 0)
    stat_lane = lax.broadcasted_iota(jnp.int32, (8, 128), 1)
    stats = jnp.zeros((8, 128), F32)
    for h, vals in enumerate(per_head):
        for r, val in enumerate(vals):
            stats = jnp.where((stat_row == r) & (stat_lane == h), val, stats)
    st_ref[0, 0] = stats


def _fox_prep(ct_ref, ft_ref, bf_ref, tri_ref, qt_ref, k_ref, vt_ref, st_ref, carry_ref, kmax_ref):
    tm = ct_ref.shape[1]
    x = ft_ref[...] + bf_ref[...]
    lf = jnp.minimum(x, 0.0) - jnp.log1p(jnp.exp(-jnp.abs(x)))
    hi = lf.astype(BF16)
    mid = (lf - hi.astype(F32)).astype(BF16)
    lo = (lf - hi.astype(F32) - mid.astype(F32)).astype(BF16)
    parts = jnp.dot(jnp.concatenate([hi, mid, lo], axis=0), tri_ref[...], preferred_element_type=F32)
    cum = parts[0:8] + parts[8:16] + parts[16:24] + carry_ref[...]
    carry_ref[...] = cum[:, tm - 1:tm]

    cum2 = cum * LOG2E
    one = jnp.ones((1, tm), F32)
    ones_row = _select_rows([one])
    stats = []
    for h in range(N_HEADS):
        r0, r1 = h * HEAD_DIM, (h + 1) * HEAD_DIM
        g0, g1 = h * HEAD_PAD, (h + 1) * HEAD_PAD
        q = ct_ref[r0:r1, :].astype(F32) * (HEAD_DIM ** -0.5 * LOG2E)
        k = ct_ref[GROUP_WIDTH + r0:GROUP_WIDTH + r1, :].astype(F32)
        v = ct_ref[2 * GROUP_WIDTH + r0:2 * GROUP_WIDTH + r1, :].astype(F32)
        qb = q.astype(BF16).astype(F32)
        bound, diag, qnorm, knorm = _logit_bound(qb, k, kmax_ref, h)
        c2 = cum2[h:h + 1]
        q_extra = _select_rows([*_split3(c2 - bound), one, one, one])
        k_extra = _select_rows([one, one, one, *(-part for part in _split3(c2))])
        qt_ref[0, g0:g1, :] = _group(q, q_extra).astype(BF16)
        k_ref[:, g0:g1] = _group(k, k_extra).T.astype(BF16)
        vt_ref[0, h, 0] = _group(v, ones_row).astype(BF16)
        stats.append((
            jnp.max(qnorm, axis=1, keepdims=True),
            knorm,
            jnp.max(c2, axis=1, keepdims=True),
            jnp.min(c2, axis=1, keepdims=True),
            jnp.min(diag, axis=1, keepdims=True),
            jnp.max(bound - diag, axis=1, keepdims=True),
        ))
    _write_stats(st_ref, stats)


def _mla_prep(dt_ref, cos_ref, sin_ref, gq_ref, gkv_ref, wq_ref, wk_ref, wv_ref, qt_ref, k_ref, vt_ref,
              st_ref, kmax_ref):
    tm = dt_ref.shape[1]
    cos, sin = cos_ref[...], sin_ref[...]
    cq = _rms_rows(dt_ref[0:MLA_Q_LORA, :].astype(F32)) * gq_ref[...]
    ckv = _rms_rows(dt_ref[MLA_Q_LORA:MLA_Q_LORA + MLA_KV_LORA, :].astype(F32)) * gkv_ref[...]
    kr = _rot_half_rows(dt_ref[MLA_Q_LORA + MLA_KV_LORA:ROWS_D, :].astype(F32), cos, sin)
    ckv_b = ckv.astype(BF16)
    q_all = jnp.dot(wq_ref[...], cq.astype(BF16), preferred_element_type=F32)
    q_all = q_all * ((MLA_NOPE + MLA_ROPE) ** -0.5 * LOG2E)
    k_all = jnp.dot(wk_ref[...], ckv_b, preferred_element_type=F32)
    v_all = jnp.dot(wv_ref[...], ckv_b, preferred_element_type=F32)
    one = jnp.ones((1, tm), F32)
    ones_row = _select_rows([one])
    k_extra = _select_rows([one, one, one])
    zero = jnp.zeros((1, 1), F32)
    stats = []
    for h in range(N_HEADS):
        r0, r1 = h * HEAD_DIM, (h + 1) * HEAD_DIM
        g0, g1 = h * HEAD_PAD, (h + 1) * HEAD_PAD
        qg = q_all[g0:g1]
        q_rope = _rot_half_rows(qg[MLA_NOPE:MLA_NOPE + MLA_ROPE], cos, sin)
        q = jnp.concatenate([qg[0:MLA_NOPE], q_rope], axis=0)
        k = jnp.concatenate([k_all[r0:r1], kr], axis=0)
        qb, kb = q.astype(BF16).astype(F32), k.astype(BF16).astype(F32)
        bound, diag, _, _ = _logit_bound(qb, kb, kmax_ref, N_HEADS + h)
        q_extra = _select_rows([*_split3(-bound)])
        qt_ref[0, g0:g1, :] = _group(q, q_extra).astype(BF16)
        k_ref[:, g0:g1] = _group(k, k_extra).T.astype(BF16)
        vt_ref[0, h, 0] = _group(v_all[r0:r1], ones_row).astype(BF16)
        stats.append((zero,) * STAT_GAP + (jnp.max(bound - diag, axis=1, keepdims=True),))
    _write_stats(st_ref, stats)


def _front_body(x_ref, g_ref, w_ref, wdf_ref,
                cols_ref, ws_ref, bs_ref,
                cosb_ref, sinb_ref, dec_ref, qw_ref, kw_ref,
                tri_ref,
                cosd_ref, sind_ref, wq_ref, wk_ref, wv_ref,
                ya_ref, yb_ref,
                qt_ref, k_ref, vt_ref, stats_ref,
                at_ref, bt_ref, ct_ref, dt_ref, ft_ref, state_ref, carry_ref, kmax_ref, *,
                tiles_per_seq, chunk_decay):
    @pl.when(pl.program_id(0) % tiles_per_seq == 0)
    def _():
        state_ref[...] = jnp.zeros_like(state_ref)
        carry_ref[...] = jnp.zeros_like(carry_ref)
        kmax_ref[...] = jnp.zeros_like(kmax_ref)

    gain_ref = cols_ref.at[pl.ds(COL_GAIN, GROUP_WIDTH)]
    bf_ref = cols_ref.at[pl.ds(COL_BF, ROWS_F)]
    gq_ref = cols_ref.at[pl.ds(COL_GQ, MLA_Q_LORA)]
    gkv_ref = cols_ref.at[pl.ds(COL_GKV, MLA_KV_LORA)]

    x = x_ref[...]
    h = (x * lax.rsqrt(jnp.mean(x * x, axis=-1, keepdims=True) + EPS) * g_ref[...]).astype(BF16)

    def proj(w):
        return lax.dot_general(w, h, NT, preferred_element_type=F32)

    half_b = ROWS_B // 2
    zdf = proj(wdf_ref[...])
    dt_ref[...] = zdf[0:ROWS_D].astype(BF16)
    ft_ref[...] = zdf[ROWS_D:ROWS_D + ROWS_F]
    at_ref[...] = proj(w_ref[0:ROWS_A, :]).astype(BF16)
    _mla_prep(dt_ref, cosd_ref, sind_ref, gq_ref, gkv_ref, wq_ref, wk_ref, wv_ref,
              qt_ref.at[1], k_ref.at[1], vt_ref.at[1], stats_ref.at[1], kmax_ref)
    ct_ref[...] = proj(w_ref[ROWS_A + ROWS_B:ROWS_ABC, :]).astype(BF16)
    _sgu(at_ref, gain_ref, ws_ref, bs_ref, ya_ref)
    bt_ref[0:half_b, :] = proj(w_ref[ROWS_A:ROWS_A + half_b, :]).astype(BF16)
    _fox_prep(ct_ref, ft_ref, bf_ref, tri_ref, qt_ref.at[0], k_ref.at[0], vt_ref.at[0], stats_ref.at[0],
              carry_ref, kmax_ref)
    bt_ref[half_b:ROWS_B, :] = proj(w_ref[ROWS_A + half_b:ROWS_A + ROWS_B, :]).astype(BF16)
    _retention(bt_ref, cosb_ref, sinb_ref, dec_ref, qw_ref, kw_ref, chunk_decay, yb_ref, state_ref)


def _front(x, g, w_t, wdf_t, cols, w_s, b_s, cos_b, sin_b, ret_tables, tri,
           cos_d, sin_d, wq_t, wk_t, wv_t, batch, layer):
    t = x.shape[0]
    tm = TOKEN_TILE
    ns = t // batch // tm
    decay_t, query_w, key_w, chunk_decay = ret_tables
    const = lambda shape: pl.BlockSpec(shape, lambda i: (0,) * len(shape), pipeline_mode=pl.Buffered(1))
    of_layer = lambda shape: pl.BlockSpec((None,) + shape, lambda i: (layer,) + (0,) * len(shape),
                                          pipeline_mode=pl.Buffered(1))
    rows_t = lambda rows: pl.BlockSpec((rows, tm), lambda i: (0, i))
    table = lambda rows: pl.BlockSpec((rows, tm), lambda i: (0, i % ns))
    k_tile = pl.BlockSpec((2, tm, N_HEADS * HEAD_PAD), lambda i: (0, i, 0))
    vt_tile = pl.BlockSpec((2, 1, N_HEADS, 1, HEAD_PAD, tm), lambda i: (0, i // ns, 0, i % ns, 0, 0))
    stats_tile = pl.BlockSpec((2, 1, 1, 8, 128), lambda i: (0, i // ns, i % ns, 0, 0))
    qt_tile = pl.BlockSpec((2, 1, N_HEADS * HEAD_PAD, tm), lambda i: (0, i, 0, 0))
    qkv_shapes = [
        jax.ShapeDtypeStruct((2, t // tm, N_HEADS * HEAD_PAD, tm), BF16),
        jax.ShapeDtypeStruct((2, t, N_HEADS * HEAD_PAD), BF16),
        jax.ShapeDtypeStruct((2, batch, N_HEADS, ns, HEAD_PAD, tm), BF16),
        jax.ShapeDtypeStruct((2, batch, ns, 8, 128), F32),
    ]
    return pl.pallas_call(
        functools.partial(_front_body, tiles_per_seq=ns, chunk_decay=chunk_decay),
        grid=(t // tm,),
        in_specs=[
            pl.BlockSpec((tm, D_MODEL), lambda i: (i, 0)),
            of_layer((1, D_MODEL)),
            of_layer((ROWS_ABC, D_MODEL)),
            of_layer((ROWS_D + ROWS_F, D_MODEL)),
            of_layer((N_COLS, 1)),
            of_layer((N_HEADS, CHUNK, CHUNK)),
            of_layer((N_HEADS, CHUNK)),
            table(HEAD_DIM // 2), table(HEAD_DIM // 2),
            const((N_HEADS, CHUNK, CHUNK)), const((N_HEADS, 1, CHUNK)), const((N_HEADS, 1, CHUNK)),
            const((tm, tm)),
            table(MLA_ROPE // 2), table(MLA_ROPE // 2),
            of_layer((N_HEADS * HEAD_PAD, MLA_Q_LORA)),
            of_layer((GROUP_WIDTH, MLA_KV_LORA)),
            of_layer((GROUP_WIDTH, MLA_KV_LORA)),
        ],
        out_specs=[
            rows_t(GROUP_WIDTH), rows_t(GROUP_WIDTH),
            qt_tile, k_tile, vt_tile, stats_tile,
        ],
        out_shape=[
            jax.ShapeDtypeStruct((GROUP_WIDTH, t), F32),
            jax.ShapeDtypeStruct((GROUP_WIDTH, t), F32),
            *qkv_shapes,
        ],
        scratch_shapes=[
            pltpu.VMEM((ROWS_A, tm), BF16),
            pltpu.VMEM((ROWS_B, tm), BF16),
            pltpu.VMEM((ROWS_C, tm), BF16),
            pltpu.VMEM((ROWS_D, tm), BF16),
            pltpu.VMEM((ROWS_F, tm), F32),
            pltpu.VMEM((N_HEADS, HEAD_DIM, HEAD_DIM), F32),
            pltpu.VMEM((ROWS_F, 1), F32),
            pltpu.VMEM((2 * N_HEADS, 1), F32),
        ],
        compiler_params=_params("arbitrary"),
        name="front",
    )(x, g, w_t, wdf_t, cols, w_s, b_s, cos_b, sin_b, decay_t, query_w, key_w, tri,
      cos_d, sin_d, wq_t, wk_t, wv_t)


def _first_needed_block(stats_ref, b, qi, nb):
    def stat(j, r, h):
        return stats_ref[((b * nb + j) * N_STATS + r) * N_HEADS + h]

    skipped = jnp.int32(0)
    leading = jnp.bool_(True)
    for j in range(nb - 1):
        zero = j < qi
        for h in range(N_HEADS):
            bound = (stat(qi, STAT_QNORM, h) * stat(j, STAT_KNORM, h)
                     + stat(qi, STAT_CMAX, h) - stat(j, STAT_CMIN, h))
            zero = jnp.logical_and(zero, bound - stat(qi, STAT_DIAG, h) < -EXP2_UNDERFLOW)
        leading = jnp.logical_and(leading, zero)
        skipped = skipped + leading.astype(jnp.int32)
    return skipped


def _attn_body(*refs, online):
    stats_ref, qt_ref, k_ref, vt_ref, o_ref, acc_ref, s_ref = refs[:7]
    m_ref = refs[7] if online else None
    nb, _, blk = qt_ref.shape
    half = blk // 2
    batch = pl.program_id(0)

    def first_block(qi):
        return _first_needed_block(stats_ref, batch, qi, nb)

    def logits(qi, kj, h):
        start = pl.multiple_of(kj * blk, blk)
        g0, g1 = h * HEAD_PAD, (h + 1) * HEAD_PAD
        return jnp.dot(k_ref[pl.ds(start, blk), g0:g1], qt_ref[qi, g0:g1, :],
                       preferred_element_type=F32)

    def accumulate(s, kj, h):
        acc = acc_ref[h]
        if online:
            m_old = m_ref[h]
            m_new = jnp.maximum(m_old, jnp.max(s, axis=0, keepdims=True))
            m_ref[h] = m_new
            s = s - m_new
            acc = acc * jnp.exp2(m_old - m_new)
        pv = jnp.dot(vt_ref[0, h, kj, 0:PV_ROWS, :], jnp.exp2(s).astype(BF16), preferred_element_type=F32)
        acc_ref[h] = acc + pv[0:HEAD_DIM + AUG_ROWS]

    s_ref[...] = logits(0, 0, 0)

    def query_block(qi, first):
        acc_ref[...] = jnp.zeros_like(acc_ref)
        if online:
            m_ref[...] = jnp.full(m_ref.shape, -jnp.inf, F32)

        def full_blocks(kj, count):
            s = s_ref[...]
            for d in range(count):
                for h in range(N_HEADS):
                    nxt = (kj + d, h + 1) if h + 1 < N_HEADS else (kj + d + 1, 0)
                    s_next = logits(qi, *nxt)
                    accumulate(s, kj + d, h)
                    s = s_next
            s_ref[...] = s

        n_full = qi - first
        odd = jnp.bitwise_and(n_full, 1)
        pl.when(odd == 1)(lambda: full_blocks(first, 1))

        def pair_step(i, carry):
            full_blocks(first + odd + 2 * i, 2)
            return carry

        lax.fori_loop(0, lax.shift_right_logical(n_full, 1), pair_step, 0)

        nxt_qi = jnp.minimum(qi + 1, nb - 1)
        nxt_first = first_block(nxt_qi)

        if online:
            key_pos = lax.broadcasted_iota(jnp.int32, (blk, blk), 0)
            qry_pos = lax.broadcasted_iota(jnp.int32, (blk, blk), 1)
            visible = key_pos <= qry_pos
            s = s_ref[...]
            for h in range(N_HEADS):
                s_next = logits(qi, qi, h + 1) if h + 1 < N_HEADS else logits(nxt_qi, nxt_first, 0)
                accumulate(jnp.where(visible, s, -jnp.inf), qi, h)
                s = s_next
            s_ref[...] = s
        else:
            start = pl.multiple_of(qi * blk, blk)
            key_pos = lax.broadcasted_iota(jnp.int32, (half, blk), 0)
            qry_pos = lax.broadcasted_iota(jnp.int32, (half, blk), 1)
            visible_a = key_pos <= qry_pos
            visible_b = visible_a[:, 0:half]

            def quadrant_logits(h):
                g0, g1 = h * HEAD_PAD, (h + 1) * HEAD_PAD
                s_a = jnp.dot(k_ref[pl.ds(start, half), g0:g1], qt_ref[qi, g0:g1, :],
                              preferred_element_type=F32)
                s_b = jnp.dot(k_ref[pl.ds(start + half, half), g0:g1], qt_ref[qi, g0:g1, half:blk],
                              preferred_element_type=F32)
                return s_a, s_b

            s_full = s_ref[...]
            s_a, s_b = s_full[0:half], s_full[half:blk, half:blk]
            for h in range(N_HEADS):
                if h + 1 < N_HEADS:
                    s_next = quadrant_logits(h + 1)
                else:
                    s_ref[...] = logits(nxt_qi, nxt_first, 0)
                p_a = jnp.exp2(jnp.where(visible_a, s_a, -jnp.inf)).astype(BF16)
                p_b = jnp.exp2(jnp.where(visible_b, s_b, -jnp.inf)).astype(BF16)
                pv_a = jnp.dot(vt_ref[0, h, qi, 0:PV_ROWS, 0:half], p_a, preferred_element_type=F32)
                pv_b = jnp.dot(vt_ref[0, h, qi, 0:PV_ROWS, half:blk], p_b, preferred_element_type=F32)
                acc_ref[h] = acc_ref[h] + pv_a[0:HEAD_DIM + AUG_ROWS]
                acc_ref[h, :, half:blk] = acc_ref[h, :, half:blk] + pv_b[0:HEAD_DIM + AUG_ROWS]
                if h + 1 < N_HEADS:
                    s_a, s_b = s_next
        for h in range(N_HEADS):
            acc = acc_ref[h]
            o_ref[qi, h * HEAD_DIM:(h + 1) * HEAD_DIM, :] = acc[0:HEAD_DIM] / acc[HEAD_DIM:HEAD_DIM + 1]
        return nxt_first

    lax.fori_loop(0, nb, query_block, jnp.int32(0))


def _attention_call(qt, k, vt, skip_stats, online):
    n_mix, batch, _, nb, _, blk = vt.shape
    seq = nb * blk
    scratch = [pltpu.VMEM((N_HEADS, HEAD_DIM + AUG_ROWS, blk), F32), pltpu.VMEM((blk, blk), F32)]
    if online:
        scratch.append(pltpu.VMEM((N_HEADS, 1, blk), F32))
    return pl.pallas_call(
        functools.partial(_attn_body, online=online),
        grid=(n_mix * batch,),
        in_specs=[
            pl.BlockSpec(memory_space=pltpu.SMEM),
            pl.BlockSpec((None, nb, N_HEADS * HEAD_PAD, blk), lambda i: (i // batch, i % batch, 0, 0)),
            pl.BlockSpec((None, seq, N_HEADS * HEAD_PAD), lambda i: (i // batch, i % batch, 0)),
            pl.BlockSpec((None, 1, N_HEADS, nb, HEAD_PAD, blk), lambda i: (i // batch, i % batch, 0, 0, 0, 0)),
        ],
        out_specs=pl.BlockSpec((nb, GROUP_WIDTH, blk), lambda i: (i, 0, 0)),
        out_shape=jax.ShapeDtypeStruct((n_mix * batch * nb, GROUP_WIDTH, blk), F32),
        scratch_shapes=scratch,
        compiler_params=_params("parallel"),
        name="attention_online" if online else "attention",
    )(skip_stats, qt, k, vt)


def _attention(qt, k, vt, stats):
    stats = stats[:, :, :, :N_STATS, :N_HEADS]
    bound_is_tight = jnp.max(stats[:, :, :, STAT_GAP, :]) <= GAP_LIMIT
    return lax.cond(bound_is_tight,
                    lambda: _attention_call(qt, k, vt, stats.reshape(-1), online=False),
                    lambda: _attention_call(qt, k, vt, stats.reshape(-1), online=True))


def _post_body(x_ref, ya_ref, yb_ref, yc_ref, yd_ref, gg_ref, wo_hbm, gf_ref, wu_hbm, wd_hbm, gl_ref,
               o_ref, wo_ref, wu_ref, wd_ref, stage_tall, stage_wide, sem, *, layer, final):
    n_chunks = D_FF // FF_CHUNK
    tall = [(wo_hbm.at[layer, pl.ds(r, FF_CHUNK), :], wo_ref.at[pl.ds(r, FF_CHUNK), :])
            for r in range(0, D_MODEL, FF_CHUNK)]
    n_out = len(tall)
    tall += [(wd_hbm.at[layer, pl.ds(c * FF_CHUNK, FF_CHUNK), :], wd_ref.at[pl.ds(c * FF_CHUNK, FF_CHUNK), :])
             for c in range(n_chunks)]
    wide = [(wu_hbm.at[layer, :, pl.ds(c * FF_CHUNK, FF_CHUNK)], wu_ref.at[:, pl.ds(c * FF_CHUNK, FF_CHUNK)])
            for c in range(n_chunks)]
    queues = ((tall, stage_tall), (wide, stage_wide))

    def dma(kind, k):
        blocks, stage = queues[kind]
        return pltpu.make_async_copy(blocks[k][0], stage.at[k % 2], sem.at[kind, k % 2])

    def land(kind, k):
        blocks, stage = queues[kind]
        dma(kind, k).wait()
        blocks[k][1][...] = stage[k % 2].astype(BF16)
        if k + 2 < len(blocks):
            dma(kind, k + 2).start()

    def body(load):
        if load:
            for kind in range(2):
                dma(kind, 0).start()
                dma(kind, 1).start()
        ys = []
        for g, y_ref in enumerate((ya_ref, yb_ref, yc_ref, yd_ref)):
            ys.append((_rms_rows(y_ref[...]) * gg_ref[g]).astype(BF16))
        y = jnp.concatenate(ys, axis=0)
        if load:
            for k in range(n_out):
                land(0, k)
        x = x_ref[...] + lax.dot_general(y, wo_ref[...], TN, preferred_element_type=F32)
        h = (x * lax.rsqrt(jnp.mean(x * x, axis=-1, keepdims=True) + EPS) * gf_ref[...]).astype(BF16)
        acc = x
        for c in range(n_chunks):
            c0, c1 = c * FF_CHUNK, (c + 1) * FF_CHUNK
            if load:
                land(1, c)
                land(0, n_out + c)
            a = jnp.maximum(jnp.dot(h, wu_ref[:, c0:c1], preferred_element_type=F32), 0.0)
            acc = acc + jnp.dot((a * a).astype(BF16), wd_ref[c0:c1, :], preferred_element_type=F32)
        if final:
            acc = acc * lax.rsqrt(jnp.mean(acc * acc, axis=-1, keepdims=True) + EPS) * gl_ref[...]
        o_ref[...] = acc

    first_step = pl.program_id(0) == 0
    pl.when(first_step)(functools.partial(body, True))
    pl.when(jnp.logical_not(first_step))(functools.partial(body, False))


def _post(x, ya, yb, ycd, gg_col, w_out, g_ffn, w_up, w_down, g_final, layer, final):
    t = x.shape[0]
    tm = TOKEN_TILE
    const = lambda shape: pl.BlockSpec(shape, lambda i: (0,) * len(shape), pipeline_mode=pl.Buffered(1))
    of_layer = lambda shape: pl.BlockSpec((None,) + shape, lambda i: (layer,) + (0,) * len(shape),
                                          pipeline_mode=pl.Buffered(1))
    in_hbm = pl.BlockSpec(memory_space=pl.ANY)
    ytile = pl.BlockSpec((GROUP_WIDTH, tm), lambda i: (0, i))
    n_tiles = t // tm
    yblock = lambda m: pl.BlockSpec((None, GROUP_WIDTH, tm), lambda i: (m * n_tiles + i, 0, 0))
    return pl.pallas_call(
        functools.partial(_post_body, layer=layer, final=final),
        grid=(t // tm,),
        in_specs=[
            pl.BlockSpec((tm, D_MODEL), lambda i: (i, 0)),
            ytile, ytile, yblock(0), yblock(1),
            of_layer((4, GROUP_WIDTH, 1)),
            in_hbm,
            of_layer((1, D_MODEL)),
            in_hbm,
            in_hbm,
            const((1, D_MODEL)),
        ],
        out_specs=pl.BlockSpec((tm, D_MODEL), lambda i: (i, 0)),
        out_shape=jax.ShapeDtypeStruct((t, D_MODEL), F32),
        scratch_shapes=[
            pltpu.VMEM((D_MODEL, D_MODEL), BF16),
            pltpu.VMEM((D_MODEL, D_FF), BF16),
            pltpu.VMEM((D_FF, D_MODEL), BF16),
            pltpu.VMEM((2, FF_CHUNK, D_MODEL), F32),
            pltpu.VMEM((2, D_MODEL, FF_CHUNK), F32),
            pltpu.SemaphoreType.DMA((2, 2)),
        ],
        compiler_params=_params("arbitrary"),
        name="post",
    )(x, ya, yb, ycd, ycd, gg_col, w_out, g_ffn, w_up, w_down, g_final)


def _rope_tables(seq, half):
    inv_freq = np.power(ROPE_BASE, -np.arange(half, dtype=np.float64) / half)
    ang = inv_freq[:, None] * np.arange(seq, dtype=np.float64)[None, :]
    return jnp.asarray(np.cos(ang), F32), jnp.asarray(np.sin(ang), F32)


def _transpose_body(w_ref, o_ref):
    o_ref[...] = w_ref[...].T.astype(BF16)


def _inproj_weights(w_in):
    depth = w_in.shape[0]
    cols = ROWS_ABC // 3
    w_abc = pl.pallas_call(
        _transpose_body,
        grid=(depth, ROWS_ABC // cols),
        in_specs=[pl.BlockSpec((None, D_MODEL, cols), lambda l, j: (l, 0, j))],
        out_specs=pl.BlockSpec((None, cols, D_MODEL), lambda l, j: (l, j, 0)),
        out_shape=jax.ShapeDtypeStruct((depth, ROWS_ABC, D_MODEL), BF16),
        compiler_params=_params("arbitrary", "arbitrary"),
        name="transpose_weights",
    )(w_in)
    tail = jnp.swapaxes(w_in[:, :, ROWS_ABC:], 1, 2).astype(BF16)
    f = tail[:, 0:N_HEADS]
    d = tail[:, N_HEADS:N_HEADS + ROWS_D]
    pad = jnp.zeros((depth, ROWS_F - N_HEADS, D_MODEL), BF16)
    return w_abc, jnp.concatenate([d, f, pad], axis=1)


def _mla_weights(w_uq, w_ukv):
    depth = w_uq.shape[0]
    wq = jnp.swapaxes(w_uq, 1, 2).reshape(depth, N_HEADS, MLA_NOPE + MLA_ROPE, MLA_Q_LORA)
    wq = jnp.pad(wq, ((0, 0), (0, 0), (0, HEAD_PAD - MLA_NOPE - MLA_ROPE), (0, 0)))
    wq = wq.reshape(depth, N_HEADS * HEAD_PAD, MLA_Q_LORA).astype(BF16)
    wkv = jnp.swapaxes(w_ukv, 1, 2).reshape(depth, N_HEADS, 2 * HEAD_DIM, MLA_KV_LORA)
    wk = wkv[:, :, :HEAD_DIM].reshape(depth, GROUP_WIDTH, MLA_KV_LORA).astype(BF16)
    wv = wkv[:, :, HEAD_DIM:].reshape(depth, GROUP_WIDTH, MLA_KV_LORA).astype(BF16)
    return wq, wk, wv


def kernel(x, g_mix_norm, w_in, b_forget, g_sgu, w_spatial, b_spatial, g_mla_q, w_uq, g_mla_kv, w_ukv,
           g_group_out, w_out, g_ffn_norm, w_up, w_down, g_final):
    batch, seq, _ = x.shape
    depth = w_in.shape[0]
    assert seq % TOKEN_TILE == 0 and ATTN_BLOCK == TOKEN_TILE
    cos_b, sin_b = _rope_tables(seq, HEAD_DIM // 2)
    cos_d, sin_d = _rope_tables(seq, MLA_ROPE // 2)
    w_abc, w_df = _inproj_weights(w_in)
    ret_tables = _retention_tables()
    tri = jnp.asarray(np.triu(np.ones((TOKEN_TILE, TOKEN_TILE), np.float32)), BF16)
    cols = jnp.concatenate([g_sgu, jnp.pad(b_forget, ((0, 0), (0, ROWS_F - N_HEADS))), g_mla_q, g_mla_kv],
                           axis=1)[:, :, None]
    mla_w = _mla_weights(w_uq, w_ukv)
    g_mix, g_ffn = g_mix_norm[:, None, :], g_ffn_norm[:, None, :]
    g_group = g_group_out.reshape(depth, 4, GROUP_WIDTH, 1)
    xf = x.reshape(batch * seq, D_MODEL)
    for l in range(depth):
        ya, yb, qt, k, vt, stats = _front(
            xf, g_mix, w_abc, w_df, cols, w_spatial, b_spatial, cos_b, sin_b, ret_tables, tri,
            cos_d, sin_d, *mla_w, batch, l)
        ycd = _attention(qt, k, vt, stats)
        xf = _post(xf, ya, yb, ycd, g_group, w_out, g_ffn, w_up, w_down, g_final[None, :],
                   layer=l, final=(l == depth - 1))
    return xf.reshape(batch, seq, D_MODEL)
```

```python
import functools
import math

import jax
import jax.numpy as jnp
import numpy as np
from jax import lax
from jax.experimental import pallas as pl
from jax.experimental.pallas import tpu as pltpu

F32 = jnp.float32
BF16 = jnp.bfloat16

D_MODEL = 1024
N_HEADS = 4
HEAD_DIM = 64
GROUP_WIDTH = N_HEADS * HEAD_DIM
CHUNK = 128
MLA_Q_LORA = 256
MLA_KV_LORA = 128
MLA_NOPE = 64
MLA_ROPE = 32
ROPE_BASE = 10000.0
D_FF = 4 * D_MODEL
EPS = 1e-6

HEAD_PAD = 128
AUG_ROWS = 8
PV_ROWS = 80
LOG2E = math.log2(math.e)
EXP2_UNDERFLOW = 152.0
STAT_QNORM, STAT_KNORM, STAT_CMAX, STAT_CMIN, STAT_DIAG, STAT_GAP = range(6)
N_STATS = 6
GAP_LIMIT = 64.0
TOKEN_TILE = 512
ATTN_BLOCK = TOKEN_TILE
FF_CHUNK = 512
VMEM_LIMIT = 56 * 1024 * 1024

ROWS_A = 2 * GROUP_WIDTH
ROWS_B = 4 * GROUP_WIDTH
ROWS_C = 3 * GROUP_WIDTH
ROWS_D = MLA_Q_LORA + MLA_KV_LORA + MLA_ROPE
ROWS_F = 8
ROWS_ABC = ROWS_A + ROWS_B + ROWS_C

COL_GAIN = 0
COL_BF = COL_GAIN + GROUP_WIDTH
COL_GQ = COL_BF + ROWS_F
COL_GKV = COL_GQ + MLA_Q_LORA
N_COLS = COL_GKV + MLA_KV_LORA

NT = (((1,), (1,)), ((), ()))
TN = (((0,), (0,)), ((), ()))


def _params(*sem):
    return pltpu.CompilerParams(dimension_semantics=sem, vmem_limit_bytes=VMEM_LIMIT)


def _rot_half_rows(t, cos, sin):
    half = t.shape[0] // 2
    t1, t2 = t[:half], t[half:]
    return jnp.concatenate([t1 * cos - t2 * sin, t1 * sin + t2 * cos], axis=0)


def _standardize_rows(t):
    mu = jnp.mean(t, axis=0, keepdims=True)
    var = jnp.mean(jnp.square(t - mu), axis=0, keepdims=True)
    return (t - mu) * lax.rsqrt(var + EPS)


def _rms_rows(t):
    return t * lax.rsqrt(jnp.mean(t * t, axis=0, keepdims=True) + EPS)


def _sgu(at_ref, gain_ref, ws_ref, bs_ref, o_ref):
    tm = at_ref.shape[1]
    nch = tm // CHUNK
    row = lax.broadcasted_iota(jnp.int32, (CHUNK, CHUNK), 0)
    col = lax.broadcasted_iota(jnp.int32, (CHUNK, CHUNK), 1)
    for h in range(N_HEADS):
        r0, r1 = h * HEAD_DIM, (h + 1) * HEAD_DIM
        u = jax.nn.gelu(at_ref[r0:r1, :].astype(F32))
        v = jax.nn.gelu(at_ref[GROUP_WIDTH + r0:GROUP_WIDTH + r1, :].astype(F32))
        v = _standardize_rows(v) * gain_ref[r0:r1, :]
        w = jnp.where(col <= row, ws_ref[h], 0.0).astype(BF16)
        vs = jnp.concatenate([v[:, c * CHUNK:(c + 1) * CHUNK] for c in range(nch)], axis=0).astype(BF16)
        mixed = lax.dot_general(vs, w, NT, preferred_element_type=F32) + bs_ref[h:h + 1, :]
        for c in range(nch):
            o_ref[r0:r1, c * CHUNK:(c + 1) * CHUNK] = (
                u[:, c * CHUNK:(c + 1) * CHUNK] * mixed[c * HEAD_DIM:(c + 1) * HEAD_DIM])


def _retention_tables():
    log_gamma = np.log1p(-np.exp2(-5.0 - np.arange(N_HEADS, dtype=np.float64)))
    j = np.arange(CHUNK, dtype=np.float64)
    rel = j[None, :] - j[:, None]
    decay_t = np.where(rel >= 0, np.exp(np.maximum(rel, 0.0)[None] * log_gamma[:, None, None]), 0.0)
    query_w = np.exp((j + 1.0)[None, None, :] * log_gamma[:, None, None])
    key_w = np.exp((CHUNK - 1.0 - j)[None, None, :] * log_gamma[:, None, None])
    to_f32 = lambda a: jnp.asarray(a, F32)
    return to_f32(decay_t), to_f32(query_w), to_f32(key_w), [float(v) for v in np.exp(CHUNK * log_gamma)]


def _retention(bt_ref, cos_ref, sin_ref, dec_ref, qw_ref, kw_ref, chunk_decay, o_ref, st_ref):
    tm = bt_ref.shape[1]
    nch = tm // CHUNK
    cos, sin = cos_ref[...], sin_ref[...]
    for h in range(N_HEADS):
        dec_t, query_w, key_w = dec_ref[h], qw_ref[h], kw_ref[h]
        r0, r1 = h * HEAD_DIM, (h + 1) * HEAD_DIM
        q = _rot_half_rows(bt_ref[r0:r1, :].astype(F32), cos, sin)
        k = _rot_half_rows(bt_ref[GROUP_WIDTH + r0:GROUP_WIDTH + r1, :].astype(F32), cos, sin) * (HEAD_DIM ** -0.5)
        v = bt_ref[2 * GROUP_WIDTH + r0:2 * GROUP_WIDTH + r1, :].astype(F32)
        g = bt_ref[3 * GROUP_WIDTH + r0:3 * GROUP_WIDTH + r1, :].astype(F32)
        st = st_ref[h]
        ys = []
        for c in range(nch):
            sl = slice(c * CHUNK, (c + 1) * CHUNK)
            qc, kc, vc = q[:, sl], k[:, sl], v[:, sl]
            kcb = kc.astype(BF16)
            a_t = lax.dot_general(kcb, qc.astype(BF16), TN, preferred_element_type=F32)
            p_t = (a_t * dec_t).astype(BF16)
            intra = jnp.dot(vc.astype(BF16), p_t, preferred_element_type=F32)
            cross = jnp.dot(st.astype(BF16), (qc * query_w).astype(BF16), preferred_element_type=F32)
            ys.append(intra + cross)
            st = chunk_decay[h] * st + lax.dot_general((vc * key_w).astype(BF16), kcb, NT,
                                                       preferred_element_type=F32)
        st_ref[h] = st
        y = _standardize_rows(jnp.concatenate(ys, axis=1))
        o_ref[r0:r1, :] = jax.nn.silu(g) * y


def _select_rows(rows):
    n = rows[0].shape[1]
    ridx = lax.broadcasted_iota(jnp.int32, (AUG_ROWS, n), 0)
    out = jnp.zeros((AUG_ROWS, n), F32)
    for i, r in enumerate(rows):
        out = jnp.where(ridx == i, jnp.broadcast_to(r, (AUG_ROWS, n)), out)
    return out


def _group(feat, extra):
    n = feat.shape[1]
    pad = HEAD_PAD - feat.shape[0] - extra.shape[0]
    return jnp.concatenate([feat, extra, jnp.zeros((pad, n), F32)], axis=0)


def _split3(x):
    hi = x.astype(BF16).astype(F32)
    mid = (x - hi).astype(BF16).astype(F32)
    lo = (x - hi - mid).astype(BF16).astype(F32)
    return hi, mid, lo


def _logit_bound(qb, kb, kmax_ref, row):
    qnorm = jnp.sqrt(jnp.sum(qb * qb, axis=0, keepdims=True))
    knorm = jnp.sqrt(jnp.max(jnp.sum(kb * kb, axis=0, keepdims=True), axis=1, keepdims=True))
    kmax = jnp.maximum(kmax_ref[row:row + 1, :], knorm)
    kmax_ref[row:row + 1, :] = kmax
    return qnorm * kmax, jnp.sum(qb * kb, axis=0, keepdims=True), qnorm, knorm


def _write_stats(st_ref, per_head):
    stat_row = lax.broadcasted_iota(jnp.int32, (8, 128), 0)
    stat_lane = lax.broadcasted_iota(jnp.int32, (8, 128), 1)
    stats = jnp.zeros((8, 128), F32)
    for h, vals in enumerate(per_head):
        for r, val in enumerate(vals):
            stats = jnp.where((stat_row == r) & (stat_lane == h), val, stats)
    st_ref[0, 0] = stats


def _fox_prep(ct_ref, ft_ref, bf_ref, tri_ref, qt_ref, k_ref, vt_ref, st_ref, carry_ref, kmax_ref):
    tm = ct_ref.shape[1]
    x = ft_ref[...] + bf_ref[...]
    lf = jnp.minimum(x, 0.0) - jnp.log1p(jnp.exp(-jnp.abs(x)))
    hi = lf.astype(BF16)
    mid = (lf - hi.astype(F32)).astype(BF16)
    lo = (lf - hi.astype(F32) - mid.astype(F32)).astype(BF16)
    parts = jnp.dot(jnp.concatenate([hi, mid, lo], axis=0), tri_ref[...], preferred_element_type=F32)
    cum = parts[0:8] + parts[8:16] + parts[16:24] + carry_ref[...]
    carry_ref[...] = cum[:, tm - 1:tm]

    cum2 = cum * LOG2E
    one = jnp.ones((1, tm), F32)
    ones_row = _select_rows([one])
    stats = []
    for h in range(N_HEADS):
        r0, r1 = h * HEAD_DIM, (h + 1) * HEAD_DIM
        g0, g1 = h * HEAD_PAD, (h + 1) * HEAD_PAD
        q = ct_ref[r0:r1, :].astype(F32) * (HEAD_DIM ** -0.5 * LOG2E)
        k = ct_ref[GROUP_WIDTH + r0:GROUP_WIDTH + r1, :].astype(F32)
        v = ct_ref[2 * GROUP_WIDTH + r0:2 * GROUP_WIDTH + r1, :].astype(F32)
        qb = q.astype(BF16).astype(F32)
        bound, diag, qnorm, knorm = _logit_bound(qb, k, kmax_ref, h)
        c2 = cum2[h:h + 1]
        q_extra = _select_rows([*_split3(c2 - bound), one, one, one])
        k_extra = _select_rows([one, one, one, *(-part for part in _split3(c2))])
        qt_ref[0, g0:g1, :] = _group(q, q_extra).astype(BF16)
        k_ref[:, g0:g1] = _group(k, k_extra).T.astype(BF16)
        vt_ref[0, h, 0] = _group(v, ones_row).astype(BF16)
        stats.append((
            jnp.max(qnorm, axis=1, keepdims=True),
            knorm,
            jnp.max(c2, axis=1, keepdims=True),
            jnp.min(c2, axis=1, keepdims=True),
            jnp.min(diag, axis=1, keepdims=True),
            jnp.max(bound - diag, axis=1, keepdims=True),
        ))
    _write_stats(st_ref, stats)


def _mla_prep(dt_ref, cos_ref, sin_ref, gq_ref, gkv_ref, wq_ref, wk_ref, wv_ref, qt_ref, k_ref, vt_ref,
              st_ref, kmax_ref):
    tm = dt_ref.shape[1]
    cos, sin = cos_ref[...], sin_ref[...]
    cq = _rms_rows(dt_ref[0:MLA_Q_LORA, :].astype(F32)) * gq_ref[...]
    ckv = _rms_rows(dt_ref[MLA_Q_LORA:MLA_Q_LORA + MLA_KV_LORA, :].astype(F32)) * gkv_ref[...]
    kr = _rot_half_rows(dt_ref[MLA_Q_LORA + MLA_KV_LORA:ROWS_D, :].astype(F32), cos, sin)
    ckv_b = ckv.astype(BF16)
    q_all = jnp.dot(wq_ref[...], cq.astype(BF16), preferred_element_type=F32)
    q_all = q_all * ((MLA_NOPE + MLA_ROPE) ** -0.5 * LOG2E)
    k_all = jnp.dot(wk_ref[...], ckv_b, preferred_element_type=F32)
    v_all = jnp.dot(wv_ref[...], ckv_b, preferred_element_type=F32)
    one = jnp.ones((1, tm), F32)
    ones_row = _select_rows([one])
    k_extra = _select_rows([one, one, one])
    zero = jnp.zeros((1, 1), F32)
    stats = []
    for h in range(N_HEADS):
        r0, r1 = h * HEAD_DIM, (h + 1) * HEAD_DIM
        g0, g1 = h * HEAD_PAD, (h + 1) * HEAD_PAD
        qg = q_all[g0:g1]
        q_rope = _rot_half_rows(qg[MLA_NOPE:MLA_NOPE + MLA_ROPE], cos, sin)
        q = jnp.concatenate([qg[0:MLA_NOPE], q_rope], axis=0)
        k = jnp.concatenate([k_all[r0:r1], kr], axis=0)
        qb, kb = q.astype(BF16).astype(F32), k.astype(BF16).astype(F32)
        bound, diag, _, _ = _logit_bound(qb, kb, kmax_ref, N_HEADS + h)
        q_extra = _select_rows([*_split3(-bound)])
        qt_ref[0, g0:g1, :] = _group(q, q_extra).astype(BF16)
        k_ref[:, g0:g1] = _group(k, k_extra).T.astype(BF16)
        vt_ref[0, h, 0] = _group(v_all[r0:r1], ones_row).astype(BF16)
        stats.append((zero,) * STAT_GAP + (jnp.max(bound - diag, axis=1, keepdims=True),))
    _write_stats(st_ref, stats)


def _front_body(x_ref, g_ref, w_ref, wdf_ref,
                cols_ref, ws_ref, bs_ref,
                cosb_ref, sinb_ref, dec_ref, qw_ref, kw_ref,
                tri_ref,
                cosd_ref, sind_ref, wq_ref, wk_ref, wv_ref,
                ya_ref, yb_ref,
                qt_ref, k_ref, vt_ref, stats_ref,
                at_ref, bt_ref, ct_ref, dt_ref, ft_ref, state_ref, carry_ref, kmax_ref, *,
                tiles_per_seq, chunk_decay):
    @pl.when(pl.program_id(0) % tiles_per_seq == 0)
    def _():
        state_ref[...] = jnp.zeros_like(state_ref)
        carry_ref[...] = jnp.zeros_like(carry_ref)
        kmax_ref[...] = jnp.zeros_like(kmax_ref)

    gain_ref = cols_ref.at[pl.ds(COL_GAIN, GROUP_WIDTH)]
    bf_ref = cols_ref.at[pl.ds(COL_BF, ROWS_F)]
    gq_ref = cols_ref.at[pl.ds(COL_GQ, MLA_Q_LORA)]
    gkv_ref = cols_ref.at[pl.ds(COL_GKV, MLA_KV_LORA)]

    x = x_ref[...]
    h = (x * lax.rsqrt(jnp.mean(x * x, axis=-1, keepdims=True) + EPS) * g_ref[...]).astype(BF16)

    def proj(w):
        return lax.dot_general(w, h, NT, preferred_element_type=F32)

    half_b = ROWS_B // 2
    zdf = proj(wdf_ref[...])
    dt_ref[...] = zdf[0:ROWS_D].astype(BF16)
    ft_ref[...] = zdf[ROWS_D:ROWS_D + ROWS_F]
    at_ref[...] = proj(w_ref[0:ROWS_A, :]).astype(BF16)
    _mla_prep(dt_ref, cosd_ref, sind_ref, gq_ref, gkv_ref, wq_ref, wk_ref, wv_ref,
              qt_ref.at[1], k_ref.at[1], vt_ref.at[1], stats_ref.at[1], kmax_ref)
    ct_ref[...] = proj(w_ref[ROWS_A + ROWS_B:ROWS_ABC, :]).astype(BF16)
    _sgu(at_ref, gain_ref, ws_ref, bs_ref, ya_ref)
    bt_ref[0:half_b, :] = proj(w_ref[ROWS_A:ROWS_A + half_b, :]).astype(BF16)
    _fox_prep(ct_ref, ft_ref, bf_ref, tri_ref, qt_ref.at[0], k_ref.at[0], vt_ref.at[0], stats_ref.at[0],
              carry_ref, kmax_ref)
    bt_ref[half_b:ROWS_B, :] = proj(w_ref[ROWS_A + half_b:ROWS_A + ROWS_B, :]).astype(BF16)
    _retention(bt_ref, cosb_ref, sinb_ref, dec_ref, qw_ref, kw_ref, chunk_decay, yb_ref, state_ref)


def _front(x, g, w_t, wdf_t, cols, w_s, b_s, cos_b, sin_b, ret_tables, tri,
           cos_d, sin_d, wq_t, wk_t, wv_t, batch, layer):
    t = x.shape[0]
    tm = TOKEN_TILE
    ns = t // batch // tm
    decay_t, query_w, key_w, chunk_decay = ret_tables
    const = lambda shape: pl.BlockSpec(shape, lambda i: (0,) * len(shape), pipeline_mode=pl.Buffered(1))
    of_layer = lambda shape: pl.BlockSpec((None,) + shape, lambda i: (layer,) + (0,) * len(shape),
                                          pipeline_mode=pl.Buffered(1))
    rows_t = lambda rows: pl.BlockSpec((rows, tm), lambda i: (0, i))
    table = lambda rows: pl.BlockSpec((rows, tm), lambda i: (0, i % ns))
    k_tile = pl.BlockSpec((2, tm, N_HEADS * HEAD_PAD), lambda i: (0, i, 0))
    vt_tile = pl.BlockSpec((2, 1, N_HEADS, 1, HEAD_PAD, tm), lambda i: (0, i // ns, 0, i % ns, 0, 0))
    stats_tile = pl.BlockSpec((2, 1, 1, 8, 128), lambda i: (0, i // ns, i % ns, 0, 0))
    qt_tile = pl.BlockSpec((2, 1, N_HEADS * HEAD_PAD, tm), lambda i: (0, i, 0, 0))
    qkv_shapes = [
        jax.ShapeDtypeStruct((2, t // tm, N_HEADS * HEAD_PAD, tm), BF16),
        jax.ShapeDtypeStruct((2, t, N_HEADS * HEAD_PAD), BF16),
        jax.ShapeDtypeStruct((2, batch, N_HEADS, ns, HEAD_PAD, tm), BF16),
        jax.ShapeDtypeStruct((2, batch, ns, 8, 128), F32),
    ]
    return pl.pallas_call(
        functools.partial(_front_body, tiles_per_seq=ns, chunk_decay=chunk_decay),
        grid=(t // tm,),
        in_specs=[
            pl.BlockSpec((tm, D_MODEL), lambda i: (i, 0)),
            of_layer((1, D_MODEL)),
            of_layer((ROWS_ABC, D_MODEL)),
            of_layer((ROWS_D + ROWS_F, D_MODEL)),
            of_layer((N_COLS, 1)),
            of_layer((N_HEADS, CHUNK, CHUNK)),
            of_layer((N_HEADS, CHUNK)),
            table(HEAD_DIM // 2), table(HEAD_DIM // 2),
            const((N_HEADS, CHUNK, CHUNK)), const((N_HEADS, 1, CHUNK)), const((N_HEADS, 1, CHUNK)),
            const((tm, tm)),
            table(MLA_ROPE // 2), table(MLA_ROPE // 2),
            of_layer((N_HEADS * HEAD_PAD, MLA_Q_LORA)),
            of_layer((GROUP_WIDTH, MLA_KV_LORA)),
            of_layer((GROUP_WIDTH, MLA_KV_LORA)),
        ],
        out_specs=[
            rows_t(GROUP_WIDTH), rows_t(GROUP_WIDTH),
            qt_tile, k_tile, vt_tile, stats_tile,
        ],
        out_shape=[
            jax.ShapeDtypeStruct((GROUP_WIDTH, t), F32),
            jax.ShapeDtypeStruct((GROUP_WIDTH, t), F32),
            *qkv_shapes,
        ],
        scratch_shapes=[
            pltpu.VMEM((ROWS_A, tm), BF16),
            pltpu.VMEM((ROWS_B, tm), BF16),
            pltpu.VMEM((ROWS_C, tm), BF16),
            pltpu.VMEM((ROWS_D, tm), BF16),
            pltpu.VMEM((ROWS_F, tm), F32),
            pltpu.VMEM((N_HEADS, HEAD_DIM, HEAD_DIM), F32),
            pltpu.VMEM((ROWS_F, 1), F32),
            pltpu.VMEM((2 * N_HEADS, 1), F32),
        ],
        compiler_params=_params("arbitrary"),
        name="front",
    )(x, g, w_t, wdf_t, cols, w_s, b_s, cos_b, sin_b, decay_t, query_w, key_w, tri,
      cos_d, sin_d, wq_t, wk_t, wv_t)


def _first_needed_block(stats_ref, b, qi, nb):
    def stat(j, r, h):
        return stats_ref[((b * nb + j) * N_STATS + r) * N_HEADS + h]

    skipped = jnp.int32(0)
    leading = jnp.bool_(True)
    for j in range(nb - 1):
        zero = j < qi
        for h in range(N_HEADS):
            bound = (stat(qi, STAT_QNORM, h) * stat(j, STAT_KNORM, h)
                     + stat(qi, STAT_CMAX, h) - stat(j, STAT_CMIN, h))
            zero = jnp.logical_and(zero, bound - stat(qi, STAT_DIAG, h) < -EXP2_UNDERFLOW)
        leading = jnp.logical_and(leading, zero)
        skipped = skipped + leading.astype(jnp.int32)
    return skipped


def _attn_body(*refs, online):
    stats_ref, qt_ref, k_ref, vt_ref, o_ref, acc_ref, s_ref, first_ref = refs[:8]
    m_ref = refs[8] if online else None
    nb, _, blk = qt_ref.shape
    half = blk // 2
    batch = pl.program_id(0)

    def first_block(qi):
        return _first_needed_block(stats_ref, batch, qi, nb)

    def logits(qi, kj, h):
        start = pl.multiple_of(kj * blk, blk)
        g0, g1 = h * HEAD_PAD, (h + 1) * HEAD_PAD
        return jnp.dot(k_ref[pl.ds(start, blk), g0:g1], qt_ref[qi, g0:g1, :],
                       preferred_element_type=F32)

    def accumulate(s, kj, h):
        acc = acc_ref[h]
        if online:
            m_old = m_ref[h]
            m_new = jnp.maximum(m_old, jnp.max(s, axis=0, keepdims=True))
            m_ref[h] = m_new
            s = s - m_new
            acc = acc * jnp.exp2(m_old - m_new)
        pv = jnp.dot(vt_ref[0, h, kj, 0:PV_ROWS, :], jnp.exp2(s).astype(BF16), preferred_element_type=F32)
        acc_ref[h] = acc + pv[0:HEAD_DIM + AUG_ROWS]

    s_ref[...] = logits(0, 0, 0)

    def query_block(qi, carry):
        first = first_ref[0]
        acc_ref[...] = jnp.zeros_like(acc_ref)
        if online:
            m_ref[...] = jnp.full(m_ref.shape, -jnp.inf, F32)

        def full_blocks(kj, count):
            s = s_ref[...]
            for d in range(count):
                for h in range(N_HEADS):
                    nxt = (kj + d, h + 1) if h + 1 < N_HEADS else (kj + d + 1, 0)
                    s_next = logits(qi, *nxt)
                    accumulate(s, kj + d, h)
                    s = s_next
            s_ref[...] = s

        n_full = qi - first
        odd = jnp.bitwise_and(n_full, 1)

        def pair_step(i, carry):
            full_blocks(first + 2 * i, 2)
            return carry

        lax.fori_loop(0, lax.shift_right_logical(n_full, 1), pair_step, 0)

        def closing(with_full):
            nxt_qi = jnp.minimum(qi + 1, nb - 1)
            nxt_first = first_block(nxt_qi)
            first_ref[0] = nxt_first
            s = s_ref[...]
            if with_full:
                for h in range(N_HEADS):
                    s_next = logits(qi, qi - 1, h + 1) if h + 1 < N_HEADS else logits(qi, qi, 0)
                    accumulate(s, qi - 1, h)
                    s = s_next
            if online:
                key_pos = lax.broadcasted_iota(jnp.int32, (blk, blk), 0)
                qry_pos = lax.broadcasted_iota(jnp.int32, (blk, blk), 1)
                visible = key_pos <= qry_pos
                for h in range(N_HEADS):
                    s_next = logits(qi, qi, h + 1) if h + 1 < N_HEADS else logits(nxt_qi, nxt_first, 0)
                    accumulate(jnp.where(visible, s, -jnp.inf), qi, h)
                    s = s_next
                s_ref[...] = s
            else:
                start = pl.multiple_of(qi * blk, blk)
                key_pos = lax.broadcasted_iota(jnp.int32, (half, blk), 0)
                qry_pos = lax.broadcasted_iota(jnp.int32, (half, blk), 1)
                visible_a = key_pos <= qry_pos
                visible_b = visible_a[:, 0:half]

                def quadrant_logits(h):
                    g0, g1 = h * HEAD_PAD, (h + 1) * HEAD_PAD
                    s_a = jnp.dot(k_ref[pl.ds(start, half), g0:g1], qt_ref[qi, g0:g1, :],
                                  preferred_element_type=F32)
                    s_b = jnp.dot(k_ref[pl.ds(start + half, half), g0:g1], qt_ref[qi, g0:g1, half:blk],
                                  preferred_element_type=F32)
                    return s_a, s_b

                s_a, s_b = s[0:half], s[half:blk, half:blk]
                for h in range(N_HEADS):
                    if h + 1 < N_HEADS:
                        s_next = quadrant_logits(h + 1)
                    else:
                        s_ref[...] = logits(nxt_qi, nxt_first, 0)
                    p_a = jnp.exp2(jnp.where(visible_a, s_a, -jnp.inf)).astype(BF16)
                    p_b = jnp.exp2(jnp.where(visible_b, s_b, -jnp.inf)).astype(BF16)
                    pv_a = jnp.dot(vt_ref[0, h, qi, 0:PV_ROWS, 0:half], p_a, preferred_element_type=F32)
                    pv_b = jnp.dot(vt_ref[0, h, qi, 0:PV_ROWS, half:blk], p_b, preferred_element_type=F32)
                    acc_ref[h] = acc_ref[h] + pv_a[0:HEAD_DIM + AUG_ROWS]
                    acc_ref[h, :, half:blk] = acc_ref[h, :, half:blk] + pv_b[0:HEAD_DIM + AUG_ROWS]
                    if h + 1 < N_HEADS:
                        s_a, s_b = s_next
            for h in range(N_HEADS):
                acc = acc_ref[h]
                o_ref[qi, h * HEAD_DIM:(h + 1) * HEAD_DIM, :] = acc[0:HEAD_DIM] / acc[HEAD_DIM:HEAD_DIM + 1]

        pl.when(odd == 1)(functools.partial(closing, True))
        pl.when(odd == 0)(functools.partial(closing, False))
        return carry

    first_ref[0] = jnp.int32(0)
    lax.fori_loop(0, nb, query_block, 0)


def _attention_call(qt, k, vt, skip_stats, online):
    n_mix, batch, _, nb, _, blk = vt.shape
    seq = nb * blk
    scratch = [pltpu.VMEM((N_HEADS, HEAD_DIM + AUG_ROWS, blk), F32), pltpu.VMEM((blk, blk), F32),
               pltpu.SMEM((1,), jnp.int32)]
    if online:
        scratch.append(pltpu.VMEM((N_HEADS, 1, blk), F32))
    return pl.pallas_call(
        functools.partial(_attn_body, online=online),
        grid=(n_mix * batch,),
        in_specs=[
            pl.BlockSpec(memory_space=pltpu.SMEM),
            pl.BlockSpec((None, nb, N_HEADS * HEAD_PAD, blk), lambda i: (i // batch, i % batch, 0, 0)),
            pl.BlockSpec((None, seq, N_HEADS * HEAD_PAD), lambda i: (i // batch, i % batch, 0)),
            pl.BlockSpec((None, 1, N_HEADS, nb, HEAD_PAD, blk), lambda i: (i // batch, i % batch, 0, 0, 0, 0)),
        ],
        out_specs=pl.BlockSpec((nb, GROUP_WIDTH, blk), lambda i: (i, 0, 0)),
        out_shape=jax.ShapeDtypeStruct((n_mix * batch * nb, GROUP_WIDTH, blk), F32),
        scratch_shapes=scratch,
        compiler_params=_params("parallel"),
        name="attention_online" if online else "attention",
    )(skip_stats, qt, k, vt)


def _attention(qt, k, vt, stats):
    stats = stats[:, :, :, :N_STATS, :N_HEADS]
    bound_is_tight = jnp.max(stats[:, :, :, STAT_GAP, :]) <= GAP_LIMIT
    return lax.cond(bound_is_tight,
                    lambda: _attention_call(qt, k, vt, stats.reshape(-1), online=False),
                    lambda: _attention_call(qt, k, vt, stats.reshape(-1), online=True))


def _post_body(x_ref, ya_ref, yb_ref, yc_ref, yd_ref, gg_ref, wo_hbm, gf_ref, wu_hbm, wd_hbm, gl_ref,
               o_ref, wo_ref, wu_ref, wd_ref, stage_tall, stage_wide, sem, *, layer, final):
    n_chunks = D_FF // FF_CHUNK
    tall = [(wo_hbm.at[layer, pl.ds(r, FF_CHUNK), :], wo_ref.at[pl.ds(r, FF_CHUNK), :])
            for r in range(0, D_MODEL, FF_CHUNK)]
    n_out = len(tall)
    tall += [(wd_hbm.at[layer, pl.ds(c * FF_CHUNK, FF_CHUNK), :], wd_ref.at[pl.ds(c * FF_CHUNK, FF_CHUNK), :])
             for c in range(n_chunks)]
    wide = [(wu_hbm.at[layer, :, pl.ds(c * FF_CHUNK, FF_CHUNK)], wu_ref.at[:, pl.ds(c * FF_CHUNK, FF_CHUNK)])
            for c in range(n_chunks)]
    queues = ((tall, stage_tall), (wide, stage_wide))

    def dma(kind, k):
        blocks, stage = queues[kind]
        return pltpu.make_async_copy(blocks[k][0], stage.at[k % 2], sem.at[kind, k % 2])

    def land(kind, k):
        blocks, stage = queues[kind]
        dma(kind, k).wait()
        blocks[k][1][...] = stage[k % 2].astype(BF16)
        if k + 2 < len(blocks):
            dma(kind, k + 2).start()

    def body(load):
        if load:
            for kind in range(2):
                dma(kind, 0).start()
                dma(kind, 1).start()
        ys = []
        for g, y_ref in enumerate((ya_ref, yb_ref, yc_ref, yd_ref)):
            ys.append((_rms_rows(y_ref[...]) * gg_ref[g]).astype(BF16))
        y = jnp.concatenate(ys, axis=0)
        if load:
            for k in range(n_out):
                land(0, k)
        x = x_ref[...] + lax.dot_general(y, wo_ref[...], TN, preferred_element_type=F32)
        h = (x * lax.rsqrt(jnp.mean(x * x, axis=-1, keepdims=True) + EPS) * gf_ref[...]).astype(BF16)
        acc = x
        for c in range(n_chunks):
            c0, c1 = c * FF_CHUNK, (c + 1) * FF_CHUNK
            if load:
                land(1, c)
                land(0, n_out + c)
            a = jnp.maximum(jnp.dot(h, wu_ref[:, c0:c1], preferred_element_type=F32), 0.0)
            acc = acc + jnp.dot((a * a).astype(BF16), wd_ref[c0:c1, :], preferred_element_type=F32)
        if final:
            acc = acc * lax.rsqrt(jnp.mean(acc * acc, axis=-1, keepdims=True) + EPS) * gl_ref[...]
        o_ref[...] = acc

    first_step = pl.program_id(0) == 0
    pl.when(first_step)(functools.partial(body, True))
    pl.when(jnp.logical_not(first_step))(functools.partial(body, False))


def _post(x, ya, yb, ycd, gg_col, w_out, g_ffn, w_up, w_down, g_final, layer, final):
    t = x.shape[0]
    tm = TOKEN_TILE
    const = lambda shape: pl.BlockSpec(shape, lambda i: (0,) * len(shape), pipeline_mode=pl.Buffered(1))
    of_layer = lambda shape: pl.BlockSpec((None,) + shape, lambda i: (layer,) + (0,) * len(shape),
                                          pipeline_mode=pl.Buffered(1))
    in_hbm = pl.BlockSpec(memory_space=pl.ANY)
    ytile = pl.BlockSpec((GROUP_WIDTH, tm), lambda i: (0, i))
    n_tiles = t // tm
    yblock = lambda m: pl.BlockSpec((None, GROUP_WIDTH, tm), lambda i: (m * n_tiles + i, 0, 0))
    return pl.pallas_call(
        functools.partial(_post_body, layer=layer, final=final),
        grid=(t // tm,),
        in_specs=[
            pl.BlockSpec((tm, D_MODEL), lambda i: (i, 0)),
            ytile, ytile, yblock(0), yblock(1),
            of_layer((4, GROUP_WIDTH, 1)),
            in_hbm,
            of_layer((1, D_MODEL)),
            in_hbm,
            in_hbm,
            const((1, D_MODEL)),
        ],
        out_specs=pl.BlockSpec((tm, D_MODEL), lambda i: (i, 0)),
        out_shape=jax.ShapeDtypeStruct((t, D_MODEL), F32),
        scratch_shapes=[
            pltpu.VMEM((D_MODEL, D_MODEL), BF16),
            pltpu.VMEM((D_MODEL, D_FF), BF16),
            pltpu.VMEM((D_FF, D_MODEL), BF16),
            pltpu.VMEM((2, FF_CHUNK, D_MODEL), F32),
            pltpu.VMEM((2, D_MODEL, FF_CHUNK), F32),
            pltpu.SemaphoreType.DMA((2, 2)),
        ],
        compiler_params=_params("arbitrary"),
        name="post",
    )(x, ya, yb, ycd, ycd, gg_col, w_out, g_ffn, w_up, w_down, g_final)


def _rope_tables(seq, half):
    inv_freq = np.power(ROPE_BASE, -np.arange(half, dtype=np.float64) / half)
    ang = inv_freq[:, None] * np.arange(seq, dtype=np.float64)[None, :]
    return jnp.asarray(np.cos(ang), F32), jnp.asarray(np.sin(ang), F32)


def _inproj_weights(w_in):
    wt = jnp.swapaxes(w_in, 1, 2).astype(BF16)
    f = wt[:, ROWS_ABC:ROWS_ABC + N_HEADS]
    d = wt[:, ROWS_ABC + N_HEADS:ROWS_ABC + N_HEADS + ROWS_D]
    pad = jnp.zeros((w_in.shape[0], ROWS_F - N_HEADS, D_MODEL), BF16)
    return wt, jnp.concatenate([d, f, pad], axis=1)


def _mla_weights(w_uq, w_ukv):
    depth = w_uq.shape[0]
    wq = jnp.swapaxes(w_uq, 1, 2).reshape(depth, N_HEADS, MLA_NOPE + MLA_ROPE, MLA_Q_LORA)
    wq = jnp.pad(wq, ((0, 0), (0, 0), (0, HEAD_PAD - MLA_NOPE - MLA_ROPE), (0, 0)))
    wq = wq.reshape(depth, N_HEADS * HEAD_PAD, MLA_Q_LORA).astype(BF16)
    wkv = jnp.swapaxes(w_ukv, 1, 2).reshape(depth, N_HEADS, 2 * HEAD_DIM, MLA_KV_LORA)
    wk = wkv[:, :, :HEAD_DIM].reshape(depth, GROUP_WIDTH, MLA_KV_LORA).astype(BF16)
    wv = wkv[:, :, HEAD_DIM:].reshape(depth, GROUP_WIDTH, MLA_KV_LORA).astype(BF16)
    return wq, wk, wv


def kernel(x, g_mix_norm, w_in, b_forget, g_sgu, w_spatial, b_spatial, g_mla_q, w_uq, g_mla_kv, w_ukv,
           g_group_out, w_out, g_ffn_norm, w_up, w_down, g_final):
    batch, seq, _ = x.shape
    depth = w_in.shape[0]
    assert seq % TOKEN_TILE == 0 and ATTN_BLOCK == TOKEN_TILE
    cos_b, sin_b = _rope_tables(seq, HEAD_DIM // 2)
    cos_d, sin_d = _rope_tables(seq, MLA_ROPE // 2)
    w_abc, w_df = _inproj_weights(w_in)
    ret_tables = _retention_tables()
    tri = jnp.asarray(np.triu(np.ones((TOKEN_TILE, TOKEN_TILE), np.float32)), BF16)
    cols = jnp.concatenate([g_sgu, jnp.pad(b_forget, ((0, 0), (0, ROWS_F - N_HEADS))), g_mla_q, g_mla_kv],
                           axis=1)[:, :, None]
    mla_w = _mla_weights(w_uq, w_ukv)
    g_mix, g_ffn = g_mix_norm[:, None, :], g_ffn_norm[:, None, :]
    g_group = g_group_out.reshape(depth, 4, GROUP_WIDTH, 1)
    xf = x.reshape(batch * seq, D_MODEL)
    for l in range(depth):
        ya, yb, qt, k, vt, stats = _front(
            xf, g_mix, w_abc, w_df, cols, w_spatial, b_spatial, cos_b, sin_b, ret_tables, tri,
            cos_d, sin_d, *mla_w, batch, l)
        ycd = _attention(qt, k, vt, stats)
        xf = _post(xf, ya, yb, ycd, g_group, w_out, g_ffn, w_up, w_down, g_final[None, :],
                   layer=l, final=(l == depth - 1))
    return xf.reshape(batch, seq, D_MODEL)
```

```python
import functools
import math

import jax
import jax.numpy as jnp
import numpy as np
from jax import lax
from jax.experimental import pallas as pl
from jax.experimental.pallas import tpu as pltpu

F32 = jnp.float32
BF16 = jnp.bfloat16

D_MODEL = 1024
N_HEADS = 4
HEAD_DIM = 64
GROUP_WIDTH = N_HEADS * HEAD_DIM
CHUNK = 128
MLA_Q_LORA = 256
MLA_KV_LORA = 128
MLA_NOPE = 64
MLA_ROPE = 32
ROPE_BASE = 10000.0
D_FF = 4 * D_MODEL
EPS = 1e-6

HEAD_PAD = 128
AUG_ROWS = 8
PV_ROWS = 80
LOG2E = math.log2(math.e)
EXP2_UNDERFLOW = 152.0
STAT_QNORM, STAT_KNORM, STAT_CMAX, STAT_CMIN, STAT_DIAG, STAT_GAP = range(6)
N_STATS = 6
GAP_LIMIT = 64.0
TOKEN_TILE = 512
ATTN_BLOCK = TOKEN_TILE
KEY_UNROLL = 3
FF_CHUNK = 512
VMEM_LIMIT = 56 * 1024 * 1024

ROWS_A = 2 * GROUP_WIDTH
ROWS_B = 4 * GROUP_WIDTH
ROWS_C = 3 * GROUP_WIDTH
ROWS_D = MLA_Q_LORA + MLA_KV_LORA + MLA_ROPE
ROWS_F = 8
ROWS_ABC = ROWS_A + ROWS_B + ROWS_C

COL_GAIN = 0
COL_BF = COL_GAIN + GROUP_WIDTH
COL_GQ = COL_BF + ROWS_F
COL_GKV = COL_GQ + MLA_Q_LORA
N_COLS = COL_GKV + MLA_KV_LORA

NT = (((1,), (1,)), ((), ()))
TN = (((0,), (0,)), ((), ()))


def _params(*sem):
    return pltpu.CompilerParams(dimension_semantics=sem, vmem_limit_bytes=VMEM_LIMIT)


def _rot_half_rows(t, cos, sin):
    half = t.shape[0] // 2
    t1, t2 = t[:half], t[half:]
    return jnp.concatenate([t1 * cos - t2 * sin, t1 * sin + t2 * cos], axis=0)


def _standardize_rows(t):
    mu = jnp.mean(t, axis=0, keepdims=True)
    var = jnp.mean(jnp.square(t - mu), axis=0, keepdims=True)
    return (t - mu) * lax.rsqrt(var + EPS)


def _rms_rows(t):
    return t * lax.rsqrt(jnp.mean(t * t, axis=0, keepdims=True) + EPS)


def _sgu(at_ref, gain_ref, ws_ref, bs_ref, o_ref):
    tm = at_ref.shape[1]
    nch = tm // CHUNK
    row = lax.broadcasted_iota(jnp.int32, (CHUNK, CHUNK), 0)
    col = lax.broadcasted_iota(jnp.int32, (CHUNK, CHUNK), 1)
    for h in range(N_HEADS):
        r0, r1 = h * HEAD_DIM, (h + 1) * HEAD_DIM
        u = jax.nn.gelu(at_ref[r0:r1, :].astype(F32))
        v = jax.nn.gelu(at_ref[GROUP_WIDTH + r0:GROUP_WIDTH + r1, :].astype(F32))
        v = _standardize_rows(v) * gain_ref[r0:r1, :]
        w = jnp.where(col <= row, ws_ref[h], 0.0).astype(BF16)
        vs = jnp.concatenate([v[:, c * CHUNK:(c + 1) * CHUNK] for c in range(nch)], axis=0).astype(BF16)
        mixed = lax.dot_general(vs, w, NT, preferred_element_type=F32) + bs_ref[h:h + 1, :]
        for c in range(nch):
            o_ref[r0:r1, c * CHUNK:(c + 1) * CHUNK] = (
                u[:, c * CHUNK:(c + 1) * CHUNK] * mixed[c * HEAD_DIM:(c + 1) * HEAD_DIM])


def _retention_tables():
    log_gamma = np.log1p(-np.exp2(-5.0 - np.arange(N_HEADS, dtype=np.float64)))
    j = np.arange(CHUNK, dtype=np.float64)
    rel = j[None, :] - j[:, None]
    decay_t = np.where(rel >= 0, np.exp(np.maximum(rel, 0.0)[None] * log_gamma[:, None, None]), 0.0)
    query_w = np.exp((j + 1.0)[None, None, :] * log_gamma[:, None, None])
    key_w = np.exp((CHUNK - 1.0 - j)[None, None, :] * log_gamma[:, None, None])
    to_f32 = lambda a: jnp.asarray(a, F32)
    return to_f32(decay_t), to_f32(query_w), to_f32(key_w), [float(v) for v in np.exp(CHUNK * log_gamma)]


def _retention(bt_ref, cos_ref, sin_ref, dec_ref, qw_ref, kw_ref, chunk_decay, o_ref, st_ref):
    tm = bt_ref.shape[1]
    nch = tm // CHUNK
    cos, sin = cos_ref[...], sin_ref[...]
    for h in range(N_HEADS):
        dec_t, query_w, key_w = dec_ref[h], qw_ref[h], kw_ref[h]
        r0, r1 = h * HEAD_DIM, (h + 1) * HEAD_DIM
        q = _rot_half_rows(bt_ref[r0:r1, :].astype(F32), cos, sin)
        k = _rot_half_rows(bt_ref[GROUP_WIDTH + r0:GROUP_WIDTH + r1, :].astype(F32), cos, sin) * (HEAD_DIM ** -0.5)
        v = bt_ref[2 * GROUP_WIDTH + r0:2 * GROUP_WIDTH + r1, :].astype(F32)
        g = bt_ref[3 * GROUP_WIDTH + r0:3 * GROUP_WIDTH + r1, :].astype(F32)
        st = st_ref[h]
        ys = []
        for c in range(nch):
            sl = slice(c * CHUNK, (c + 1) * CHUNK)
            qc, kc, vc = q[:, sl], k[:, sl], v[:, sl]
            kcb = kc.astype(BF16)
            a_t = lax.dot_general(kcb, qc.astype(BF16), TN, preferred_element_type=F32)
            p_t = (a_t * dec_t).astype(BF16)
            intra = jnp.dot(vc.astype(BF16), p_t, preferred_element_type=F32)
            cross = jnp.dot(st.astype(BF16), (qc * query_w).astype(BF16), preferred_element_type=F32)
            ys.append(intra + cross)
            st = chunk_decay[h] * st + lax.dot_general((vc * key_w).astype(BF16), kcb, NT,
                                                       preferred_element_type=F32)
        st_ref[h] = st
        y = _standardize_rows(jnp.concatenate(ys, axis=1))
        o_ref[r0:r1, :] = jax.nn.silu(g) * y


def _select_rows(rows):
    n = rows[0].shape[1]
    ridx = lax.broadcasted_iota(jnp.int32, (AUG_ROWS, n), 0)
    out = jnp.zeros((AUG_ROWS, n), F32)
    for i, r in enumerate(rows):
        out = jnp.where(ridx == i, jnp.broadcast_to(r, (AUG_ROWS, n)), out)
    return out


def _group(feat, extra):
    n = feat.shape[1]
    pad = HEAD_PAD - feat.shape[0] - extra.shape[0]
    return jnp.concatenate([feat, extra, jnp.zeros((pad, n), F32)], axis=0)


def _split3(x):
    hi = x.astype(BF16).astype(F32)
    mid = (x - hi).astype(BF16).astype(F32)
    lo = (x - hi - mid).astype(BF16).astype(F32)
    return hi, mid, lo


def _logit_bound(qb, kb, kmax_ref, row):
    qnorm = jnp.sqrt(jnp.sum(qb * qb, axis=0, keepdims=True))
    knorm = jnp.sqrt(jnp.max(jnp.sum(kb * kb, axis=0, keepdims=True), axis=1, keepdims=True))
    kmax = jnp.maximum(kmax_ref[row:row + 1, :], knorm)
    kmax_ref[row:row + 1, :] = kmax
    return qnorm * kmax, jnp.sum(qb * kb, axis=0, keepdims=True), qnorm, knorm


def _write_stats(st_ref, per_head):
    stat_row = lax.broadcasted_iota(jnp.int32, (8, 128), 0)
    stat_lane = lax.broadcasted_iota(jnp.int32, (8, 128), 1)
    stats = jnp.zeros((8, 128), F32)
    for h, vals in enumerate(per_head):
        for r, val in enumerate(vals):
            stats = jnp.where((stat_row == r) & (stat_lane == h), val, stats)
    st_ref[0, 0] = stats


def _fox_prep(ct_ref, ft_ref, bf_ref, tri_ref, qt_ref, k_ref, vt_ref, st_ref, carry_ref, kmax_ref):
    tm = ct_ref.shape[1]
    x = ft_ref[...] + bf_ref[...]
    lf = jnp.minimum(x, 0.0) - jnp.log1p(jnp.exp(-jnp.abs(x)))
    hi = lf.astype(BF16)
    mid = (lf - hi.astype(F32)).astype(BF16)
    lo = (lf - hi.astype(F32) - mid.astype(F32)).astype(BF16)
    parts = jnp.dot(jnp.concatenate([hi, mid, lo], axis=0), tri_ref[...], preferred_element_type=F32)
    cum = parts[0:8] + parts[8:16] + parts[16:24] + carry_ref[...]
    carry_ref[...] = cum[:, tm - 1:tm]

    cum2 = cum * LOG2E
    one = jnp.ones((1, tm), F32)
    ones_row = _select_rows([one])
    stats = []
    for h in range(N_HEADS):
        r0, r1 = h * HEAD_DIM, (h + 1) * HEAD_DIM
        g0, g1 = h * HEAD_PAD, (h + 1) * HEAD_PAD
        q = ct_ref[r0:r1, :].astype(F32) * (HEAD_DIM ** -0.5 * LOG2E)
        k = ct_ref[GROUP_WIDTH + r0:GROUP_WIDTH + r1, :].astype(F32)
        v = ct_ref[2 * GROUP_WIDTH + r0:2 * GROUP_WIDTH + r1, :].astype(F32)
        qb = q.astype(BF16).astype(F32)
        bound, diag, qnorm, knorm = _logit_bound(qb, k, kmax_ref, h)
        c2 = cum2[h:h + 1]
        q_extra = _select_rows([*_split3(c2 - bound), one, one, one])
        k_extra = _select_rows([one, one, one, *(-part for part in _split3(c2))])
        qt_ref[0, g0:g1, :] = _group(q, q_extra).astype(BF16)
        k_ref[:, g0:g1] = _group(k, k_extra).T.astype(BF16)
        vt_ref[0, h, 0] = _group(v, ones_row).astype(BF16)
        stats.append((
            jnp.max(qnorm, axis=1, keepdims=True),
            knorm,
            jnp.max(c2, axis=1, keepdims=True),
            jnp.min(c2, axis=1, keepdims=True),
            jnp.min(diag, axis=1, keepdims=True),
            jnp.max(bound - diag, axis=1, keepdims=True),
        ))
    _write_stats(st_ref, stats)


def _mla_prep(dt_ref, cos_ref, sin_ref, gq_ref, gkv_ref, wq_ref, wk_ref, wv_ref, qt_ref, k_ref, vt_ref,
              st_ref, kmax_ref):
    tm = dt_ref.shape[1]
    cos, sin = cos_ref[...], sin_ref[...]
    cq = _rms_rows(dt_ref[0:MLA_Q_LORA, :].astype(F32)) * gq_ref[...]
    ckv = _rms_rows(dt_ref[MLA_Q_LORA:MLA_Q_LORA + MLA_KV_LORA, :].astype(F32)) * gkv_ref[...]
    kr = _rot_half_rows(dt_ref[MLA_Q_LORA + MLA_KV_LORA:ROWS_D, :].astype(F32), cos, sin)
    ckv_b = ckv.astype(BF16)
    q_all = jnp.dot(wq_ref[...], cq.astype(BF16), preferred_element_type=F32)
    q_all = q_all * ((MLA_NOPE + MLA_ROPE) ** -0.5 * LOG2E)
    k_all = jnp.dot(wk_ref[...], ckv_b, preferred_element_type=F32)
    v_all = jnp.dot(wv_ref[...], ckv_b, preferred_element_type=F32)
    one = jnp.ones((1, tm), F32)
    ones_row = _select_rows([one])
    k_extra = _select_rows([one, one, one])
    zero = jnp.zeros((1, 1), F32)
    stats = []
    for h in range(N_HEADS):
        r0, r1 = h * HEAD_DIM, (h + 1) * HEAD_DIM
        g0, g1 = h * HEAD_PAD, (h + 1) * HEAD_PAD
        qg = q_all[g0:g1]
        q_rope = _rot_half_rows(qg[MLA_NOPE:MLA_NOPE + MLA_ROPE], cos, sin)
        q = jnp.concatenate([qg[0:MLA_NOPE], q_rope], axis=0)
        k = jnp.concatenate([k_all[r0:r1], kr], axis=0)
        qb, kb = q.astype(BF16).astype(F32), k.astype(BF16).astype(F32)
        bound, diag, _, _ = _logit_bound(qb, kb, kmax_ref, N_HEADS + h)
        q_extra = _select_rows([*_split3(-bound)])
        qt_ref[0, g0:g1, :] = _group(q, q_extra).astype(BF16)
        k_ref[:, g0:g1] = _group(k, k_extra).T.astype(BF16)
        vt_ref[0, h, 0] = _group(v_all[r0:r1], ones_row).astype(BF16)
        stats.append((zero,) * STAT_GAP + (jnp.max(bound - diag, axis=1, keepdims=True),))
    _write_stats(st_ref, stats)


def _front_body(x_ref, g_ref, w_ref, wdf_ref,
                cols_ref, ws_ref, bs_ref,
                cosb_ref, sinb_ref, dec_ref, qw_ref, kw_ref,
                tri_ref,
                cosd_ref, sind_ref, wq_ref, wk_ref, wv_ref,
                ya_ref, yb_ref,
                qt_ref, k_ref, vt_ref, stats_ref,
                at_ref, bt_ref, ct_ref, dt_ref, ft_ref, state_ref, carry_ref, kmax_ref, *,
                tiles_per_seq, chunk_decay):
    @pl.when(pl.program_id(0) % tiles_per_seq == 0)
    def _():
        state_ref[...] = jnp.zeros_like(state_ref)
        carry_ref[...] = jnp.zeros_like(carry_ref)
        kmax_ref[...] = jnp.zeros_like(kmax_ref)

    gain_ref = cols_ref.at[pl.ds(COL_GAIN, GROUP_WIDTH)]
    bf_ref = cols_ref.at[pl.ds(COL_BF, ROWS_F)]
    gq_ref = cols_ref.at[pl.ds(COL_GQ, MLA_Q_LORA)]
    gkv_ref = cols_ref.at[pl.ds(COL_GKV, MLA_KV_LORA)]

    x = x_ref[...]
    h = (x * lax.rsqrt(jnp.mean(x * x, axis=-1, keepdims=True) + EPS) * g_ref[...]).astype(BF16)

    def proj(w):
        return lax.dot_general(w, h, NT, preferred_element_type=F32)

    half_b = ROWS_B // 2
    zdf = proj(wdf_ref[...])
    dt_ref[...] = zdf[0:ROWS_D].astype(BF16)
    ft_ref[...] = zdf[ROWS_D:ROWS_D + ROWS_F]
    at_ref[...] = proj(w_ref[0:ROWS_A, :]).astype(BF16)
    _mla_prep(dt_ref, cosd_ref, sind_ref, gq_ref, gkv_ref, wq_ref, wk_ref, wv_ref,
              qt_ref.at[1], k_ref.at[1], vt_ref.at[1], stats_ref.at[1], kmax_ref)
    ct_ref[...] = proj(w_ref[ROWS_A + ROWS_B:ROWS_ABC, :]).astype(BF16)
    _sgu(at_ref, gain_ref, ws_ref, bs_ref, ya_ref)
    bt_ref[0:half_b, :] = proj(w_ref[ROWS_A:ROWS_A + half_b, :]).astype(BF16)
    _fox_prep(ct_ref, ft_ref, bf_ref, tri_ref, qt_ref.at[0], k_ref.at[0], vt_ref.at[0], stats_ref.at[0],
              carry_ref, kmax_ref)
    bt_ref[half_b:ROWS_B, :] = proj(w_ref[ROWS_A + half_b:ROWS_A + ROWS_B, :]).astype(BF16)
    _retention(bt_ref, cosb_ref, sinb_ref, dec_ref, qw_ref, kw_ref, chunk_decay, yb_ref, state_ref)


def _front(x, g, w_t, wdf_t, cols, w_s, b_s, cos_b, sin_b, ret_tables, tri,
           cos_d, sin_d, wq_t, wk_t, wv_t, batch, layer):
    t = x.shape[0]
    tm = TOKEN_TILE
    ns = t // batch // tm
    decay_t, query_w, key_w, chunk_decay = ret_tables
    const = lambda shape: pl.BlockSpec(shape, lambda i: (0,) * len(shape), pipeline_mode=pl.Buffered(1))
    of_layer = lambda shape: pl.BlockSpec((None,) + shape, lambda i: (layer,) + (0,) * len(shape),
                                          pipeline_mode=pl.Buffered(1))
    rows_t = lambda rows: pl.BlockSpec((rows, tm), lambda i: (0, i))
    table = lambda rows: pl.BlockSpec((rows, tm), lambda i: (0, i % ns))
    k_tile = pl.BlockSpec((2, tm, N_HEADS * HEAD_PAD), lambda i: (0, i, 0))
    vt_tile = pl.BlockSpec((2, 1, N_HEADS, 1, HEAD_PAD, tm), lambda i: (0, i // ns, 0, i % ns, 0, 0))
    stats_tile = pl.BlockSpec((2, 1, 1, 8, 128), lambda i: (0, i // ns, i % ns, 0, 0))
    qt_tile = pl.BlockSpec((2, 1, N_HEADS * HEAD_PAD, tm), lambda i: (0, i, 0, 0))
    qkv_shapes = [
        jax.ShapeDtypeStruct((2, t // tm, N_HEADS * HEAD_PAD, tm), BF16),
        jax.ShapeDtypeStruct((2, t, N_HEADS * HEAD_PAD), BF16),
        jax.ShapeDtypeStruct((2, batch, N_HEADS, ns, HEAD_PAD, tm), BF16),
        jax.ShapeDtypeStruct((2, batch, ns, 8, 128), F32),
    ]
    return pl.pallas_call(
        functools.partial(_front_body, tiles_per_seq=ns, chunk_decay=chunk_decay),
        grid=(t // tm,),
        in_specs=[
            pl.BlockSpec((tm, D_MODEL), lambda i: (i, 0)),
            of_layer((1, D_MODEL)),
            of_layer((ROWS_ABC, D_MODEL)),
            of_layer((ROWS_D + ROWS_F, D_MODEL)),
            of_layer((N_COLS, 1)),
            of_layer((N_HEADS, CHUNK, CHUNK)),
            of_layer((N_HEADS, CHUNK)),
            table(HEAD_DIM // 2), table(HEAD_DIM // 2),
            const((N_HEADS, CHUNK, CHUNK)), const((N_HEADS, 1, CHUNK)), const((N_HEADS, 1, CHUNK)),
            const((tm, tm)),
            table(MLA_ROPE // 2), table(MLA_ROPE // 2),
            of_layer((N_HEADS * HEAD_PAD, MLA_Q_LORA)),
            of_layer((GROUP_WIDTH, MLA_KV_LORA)),
            of_layer((GROUP_WIDTH, MLA_KV_LORA)),
        ],
        out_specs=[
            rows_t(GROUP_WIDTH), rows_t(GROUP_WIDTH),
            qt_tile, k_tile, vt_tile, stats_tile,
        ],
        out_shape=[
            jax.ShapeDtypeStruct((GROUP_WIDTH, t), F32),
            jax.ShapeDtypeStruct((GROUP_WIDTH, t), F32),
            *qkv_shapes,
        ],
        scratch_shapes=[
            pltpu.VMEM((ROWS_A, tm), BF16),
            pltpu.VMEM((ROWS_B, tm), BF16),
            pltpu.VMEM((ROWS_C, tm), BF16),
            pltpu.VMEM((ROWS_D, tm), BF16),
            pltpu.VMEM((ROWS_F, tm), F32),
            pltpu.VMEM((N_HEADS, HEAD_DIM, HEAD_DIM), F32),
            pltpu.VMEM((ROWS_F, 1), F32),
            pltpu.VMEM((2 * N_HEADS, 1), F32),
        ],
        compiler_params=_params("arbitrary"),
        name="front",
    )(x, g, w_t, wdf_t, cols, w_s, b_s, cos_b, sin_b, decay_t, query_w, key_w, tri,
      cos_d, sin_d, wq_t, wk_t, wv_t)


def _first_needed_block(stats_ref, b, qi, nb):
    def stat(j, r, h):
        return stats_ref[((b * nb + j) * N_STATS + r) * N_HEADS + h]

    skipped = jnp.int32(0)
    leading = jnp.bool_(True)
    for j in range(nb - 1):
        zero = j < qi
        for h in range(N_HEADS):
            bound = (stat(qi, STAT_QNORM, h) * stat(j, STAT_KNORM, h)
                     + stat(qi, STAT_CMAX, h) - stat(j, STAT_CMIN, h))
            zero = jnp.logical_and(zero, bound - stat(qi, STAT_DIAG, h) < -EXP2_UNDERFLOW)
        leading = jnp.logical_and(leading, zero)
        skipped = skipped + leading.astype(jnp.int32)
    return skipped


def _attn_body(*refs, online):
    stats_ref, qt_ref, k_ref, vt_ref, o_ref, acc_ref, s_ref, first_ref = refs[:8]
    m_ref = refs[8] if online else None
    nb, _, blk = qt_ref.shape
    half = blk // 2
    batch = pl.program_id(0)

    def first_block(qi):
        return _first_needed_block(stats_ref, batch, qi, nb)

    def logits(qi, kj, h):
        start = pl.multiple_of(kj * blk, blk)
        g0, g1 = h * HEAD_PAD, (h + 1) * HEAD_PAD
        return jnp.dot(k_ref[pl.ds(start, blk), g0:g1], qt_ref[qi, g0:g1, :],
                       preferred_element_type=F32)

    def accumulate(s, kj, h):
        acc = acc_ref[h]
        if online:
            m_old = m_ref[h]
            m_new = jnp.maximum(m_old, jnp.max(s, axis=0, keepdims=True))
            m_ref[h] = m_new
            s = s - m_new
            acc = acc * jnp.exp2(m_old - m_new)
        pv = jnp.dot(vt_ref[0, h, kj, 0:PV_ROWS, :], jnp.exp2(s).astype(BF16), preferred_element_type=F32)
        acc_ref[h] = acc + pv[0:HEAD_DIM + AUG_ROWS]

    s_ref[...] = logits(0, 0, 0)

    def query_block(qi, carry):
        first = first_ref[0]
        acc_ref[...] = jnp.zeros_like(acc_ref)
        if online:
            m_ref[...] = jnp.full(m_ref.shape, -jnp.inf, F32)

        def full_blocks(kj, count):
            s = s_ref[...]
            for d in range(count):
                for h in range(N_HEADS):
                    nxt = (kj + d, h + 1) if h + 1 < N_HEADS else (kj + d + 1, 0)
                    s_next = logits(qi, *nxt)
                    accumulate(s, kj + d, h)
                    s = s_next
            s_ref[...] = s

        n_full = qi - first
        n_steps = lax.div(n_full, KEY_UNROLL)
        left = n_full - n_steps * KEY_UNROLL

        def loop_step(i, carry):
            full_blocks(first + KEY_UNROLL * i, KEY_UNROLL)
            return carry

        lax.fori_loop(0, n_steps, loop_step, 0)

        def closing(n_left):
            nxt_qi = jnp.minimum(qi + 1, nb - 1)
            nxt_first = first_block(nxt_qi)
            first_ref[0] = nxt_first
            s = s_ref[...]
            for kj in range(n_left, 0, -1):
                for h in range(N_HEADS):
                    s_next = logits(qi, qi - kj, h + 1) if h + 1 < N_HEADS else logits(qi, qi - kj + 1, 0)
                    accumulate(s, qi - kj, h)
                    s = s_next
            if online:
                key_pos = lax.broadcasted_iota(jnp.int32, (blk, blk), 0)
                qry_pos = lax.broadcasted_iota(jnp.int32, (blk, blk), 1)
                visible = key_pos <= qry_pos
                for h in range(N_HEADS):
                    s_next = logits(qi, qi, h + 1) if h + 1 < N_HEADS else logits(nxt_qi, nxt_first, 0)
                    accumulate(jnp.where(visible, s, -jnp.inf), qi, h)
                    s = s_next
                s_ref[...] = s
            else:
                start = pl.multiple_of(qi * blk, blk)
                key_pos = lax.broadcasted_iota(jnp.int32, (half, blk), 0)
                qry_pos = lax.broadcasted_iota(jnp.int32, (half, blk), 1)
                visible_a = key_pos <= qry_pos
                visible_b = visible_a[:, 0:half]

                def quadrant_logits(h):
                    g0, g1 = h * HEAD_PAD, (h + 1) * HEAD_PAD
                    s_a = jnp.dot(k_ref[pl.ds(start, half), g0:g1], qt_ref[qi, g0:g1, :],
                                  preferred_element_type=F32)
                    s_b = jnp.dot(k_ref[pl.ds(start + half, half), g0:g1], qt_ref[qi, g0:g1, half:blk],
                                  preferred_element_type=F32)
                    return s_a, s_b

                s_a, s_b = s[0:half], s[half:blk, half:blk]
                for h in range(N_HEADS):
                    if h + 1 < N_HEADS:
                        s_next = quadrant_logits(h + 1)
                    else:
                        s_ref[...] = logits(nxt_qi, nxt_first, 0)
                    p_a = jnp.exp2(jnp.where(visible_a, s_a, -jnp.inf)).astype(BF16)
                    p_b = jnp.exp2(jnp.where(visible_b, s_b, -jnp.inf)).astype(BF16)
                    pv_a = jnp.dot(vt_ref[0, h, qi, 0:PV_ROWS, 0:half], p_a, preferred_element_type=F32)
                    pv_b = jnp.dot(vt_ref[0, h, qi, 0:PV_ROWS, half:blk], p_b, preferred_element_type=F32)
                    acc_ref[h] = acc_ref[h] + pv_a[0:HEAD_DIM + AUG_ROWS]
                    acc_ref[h, :, half:blk] = acc_ref[h, :, half:blk] + pv_b[0:HEAD_DIM + AUG_ROWS]
                    if h + 1 < N_HEADS:
                        s_a, s_b = s_next
            for h in range(N_HEADS):
                acc = acc_ref[h]
                o_ref[qi, h * HEAD_DIM:(h + 1) * HEAD_DIM, :] = acc[0:HEAD_DIM] / acc[HEAD_DIM:HEAD_DIM + 1]

        for n_left in range(KEY_UNROLL):
            pl.when(left == n_left)(functools.partial(closing, n_left))
        return carry

    first_ref[0] = jnp.int32(0)
    lax.fori_loop(0, nb, query_block, 0)


def _attention_call(qt, k, vt, skip_stats, online):
    n_mix, batch, _, nb, _, blk = vt.shape
    seq = nb * blk
    scratch = [pltpu.VMEM((N_HEADS, HEAD_DIM + AUG_ROWS, blk), F32), pltpu.VMEM((blk, blk), F32),
               pltpu.SMEM((1,), jnp.int32)]
    if online:
        scratch.append(pltpu.VMEM((N_HEADS, 1, blk), F32))
    return pl.pallas_call(
        functools.partial(_attn_body, online=online),
        grid=(n_mix * batch,),
        in_specs=[
            pl.BlockSpec(memory_space=pltpu.SMEM),
            pl.BlockSpec((None, nb, N_HEADS * HEAD_PAD, blk), lambda i: (i // batch, i % batch, 0, 0)),
            pl.BlockSpec((None, seq, N_HEADS * HEAD_PAD), lambda i: (i // batch, i % batch, 0)),
            pl.BlockSpec((None, 1, N_HEADS, nb, HEAD_PAD, blk), lambda i: (i // batch, i % batch, 0, 0, 0, 0)),
        ],
        out_specs=pl.BlockSpec((nb, GROUP_WIDTH, blk), lambda i: (i, 0, 0)),
        out_shape=jax.ShapeDtypeStruct((n_mix * batch * nb, GROUP_WIDTH, blk), F32),
        scratch_shapes=scratch,
        compiler_params=_params("parallel"),
        name="attention_online" if online else "attention",
    )(skip_stats, qt, k, vt)


def _attention(qt, k, vt, stats):
    stats = stats[:, :, :, :N_STATS, :N_HEADS]
    bound_is_tight = jnp.max(stats[:, :, :, STAT_GAP, :]) <= GAP_LIMIT
    return lax.cond(bound_is_tight,
                    lambda: _attention_call(qt, k, vt, stats.reshape(-1), online=False),
                    lambda: _attention_call(qt, k, vt, stats.reshape(-1), online=True))


def _post_body(x_ref, ya_ref, yb_ref, yc_ref, yd_ref, gg_ref, wo_hbm, gf_ref, wu_hbm, wd_hbm, gl_ref,
               o_ref, wo_ref, wu_ref, wd_ref, stage_tall, stage_wide, sem, *, layer, final):
    n_chunks = D_FF // FF_CHUNK
    tall = [(wo_hbm.at[layer, pl.ds(r, FF_CHUNK), :], wo_ref.at[pl.ds(r, FF_CHUNK), :])
            for r in range(0, D_MODEL, FF_CHUNK)]
    n_out = len(tall)
    tall += [(wd_hbm.at[layer, pl.ds(c * FF_CHUNK, FF_CHUNK), :], wd_ref.at[pl.ds(c * FF_CHUNK, FF_CHUNK), :])
             for c in range(n_chunks)]
    wide = [(wu_hbm.at[layer, :, pl.ds(c * FF_CHUNK, FF_CHUNK)], wu_ref.at[:, pl.ds(c * FF_CHUNK, FF_CHUNK)])
            for c in range(n_chunks)]
    queues = ((tall, stage_tall), (wide, stage_wide))

    def dma(kind, k):
        blocks, stage = queues[kind]
        return pltpu.make_async_copy(blocks[k][0], stage.at[k % 2], sem.at[kind, k % 2])

    def land(kind, k):
        blocks, stage = queues[kind]
        dma(kind, k).wait()
        blocks[k][1][...] = stage[k % 2].astype(BF16)
        if k + 2 < len(blocks):
            dma(kind, k + 2).start()

    def body(load):
        if load:
            for kind in range(2):
                dma(kind, 0).start()
                dma(kind, 1).start()
        ys = []
        for g, y_ref in enumerate((ya_ref, yb_ref, yc_ref, yd_ref)):
            ys.append((_rms_rows(y_ref[...]) * gg_ref[g]).astype(BF16))
        y = jnp.concatenate(ys, axis=0)
        if load:
            for k in range(n_out):
                land(0, k)
        x = x_ref[...] + lax.dot_general(y, wo_ref[...], TN, preferred_element_type=F32)
        h = (x * lax.rsqrt(jnp.mean(x * x, axis=-1, keepdims=True) + EPS) * gf_ref[...]).astype(BF16)
        acc = x
        for c in range(n_chunks):
            c0, c1 = c * FF_CHUNK, (c + 1) * FF_CHUNK
            if load:
                land(1, c)
                land(0, n_out + c)
            a = jnp.maximum(jnp.dot(h, wu_ref[:, c0:c1], preferred_element_type=F32), 0.0)
            acc = acc + jnp.dot((a * a).astype(BF16), wd_ref[c0:c1, :], preferred_element_type=F32)
        if final:
            acc = acc * lax.rsqrt(jnp.mean(acc * acc, axis=-1, keepdims=True) + EPS) * gl_ref[...]
        o_ref[...] = acc

    first_step = pl.program_id(0) == 0
    pl.when(first_step)(functools.partial(body, True))
    pl.when(jnp.logical_not(first_step))(functools.partial(body, False))


def _post(x, ya, yb, ycd, gg_col, w_out, g_ffn, w_up, w_down, g_final, layer, final):
    t = x.shape[0]
    tm = TOKEN_TILE
    const = lambda shape: pl.BlockSpec(shape, lambda i: (0,) * len(shape), pipeline_mode=pl.Buffered(1))
    of_layer = lambda shape: pl.BlockSpec((None,) + shape, lambda i: (layer,) + (0,) * len(shape),
                                          pipeline_mode=pl.Buffered(1))
    in_hbm = pl.BlockSpec(memory_space=pl.ANY)
    ytile = pl.BlockSpec((GROUP_WIDTH, tm), lambda i: (0, i))
    n_tiles = t // tm
    yblock = lambda m: pl.BlockSpec((None, GROUP_WIDTH, tm), lambda i: (m * n_tiles + i, 0, 0))
    return pl.pallas_call(
        functools.partial(_post_body, layer=layer, final=final),
        grid=(t // tm,),
        in_specs=[
            pl.BlockSpec((tm, D_MODEL), lambda i: (i, 0)),
            ytile, ytile, yblock(0), yblock(1),
            of_layer((4, GROUP_WIDTH, 1)),
            in_hbm,
            of_layer((1, D_MODEL)),
            in_hbm,
            in_hbm,
            const((1, D_MODEL)),
        ],
        out_specs=pl.BlockSpec((tm, D_MODEL), lambda i: (i, 0)),
        out_shape=jax.ShapeDtypeStruct((t, D_MODEL), F32),
        scratch_shapes=[
            pltpu.VMEM((D_MODEL, D_MODEL), BF16),
            pltpu.VMEM((D_MODEL, D_FF), BF16),
            pltpu.VMEM((D_FF, D_MODEL), BF16),
            pltpu.VMEM((2, FF_CHUNK, D_MODEL), F32),
            pltpu.VMEM((2, D_MODEL, FF_CHUNK), F32),
            pltpu.SemaphoreType.DMA((2, 2)),
        ],
        compiler_params=_params("arbitrary"),
        name="post",
    )(x, ya, yb, ycd, ycd, gg_col, w_out, g_ffn, w_up, w_down, g_final)


def _rope_tables(seq, half):
    inv_freq = np.power(ROPE_BASE, -np.arange(half, dtype=np.float64) / half)
    ang = inv_freq[:, None] * np.arange(seq, dtype=np.float64)[None, :]
    return jnp.asarray(np.cos(ang), F32), jnp.asarray(np.sin(ang), F32)


def _inproj_weights(w_in):
    wt = jnp.swapaxes(w_in, 1, 2).astype(BF16)
    f = wt[:, ROWS_ABC:ROWS_ABC + N_HEADS]
    d = wt[:, ROWS_ABC + N_HEADS:ROWS_ABC + N_HEADS + ROWS_D]
    pad = jnp.zeros((w_in.shape[0], ROWS_F - N_HEADS, D_MODEL), BF16)
    return wt, jnp.concatenate([d, f, pad], axis=1)


def _mla_weights(w_uq, w_ukv):
    depth = w_uq.shape[0]
    wq = jnp.swapaxes(w_uq, 1, 2).reshape(depth, N_HEADS, MLA_NOPE + MLA_ROPE, MLA_Q_LORA)
    wq = jnp.pad(wq, ((0, 0), (0, 0), (0, HEAD_PAD - MLA_NOPE - MLA_ROPE), (0, 0)))
    wq = wq.reshape(depth, N_HEADS * HEAD_PAD, MLA_Q_LORA).astype(BF16)
    wkv = jnp.swapaxes(w_ukv, 1, 2).reshape(depth, N_HEADS, 2 * HEAD_DIM, MLA_KV_LORA)
    wk = wkv[:, :, :HEAD_DIM].reshape(depth, GROUP_WIDTH, MLA_KV_LORA).astype(BF16)
    wv = wkv[:, :, HEAD_DIM:].reshape(depth, GROUP_WIDTH, MLA_KV_LORA).astype(BF16)
    return wq, wk, wv


def kernel(x, g_mix_norm, w_in, b_forget, g_sgu, w_spatial, b_spatial, g_mla_q, w_uq, g_mla_kv, w_ukv,
           g_group_out, w_out, g_ffn_norm, w_up, w_down, g_final):
    batch, seq, _ = x.shape
    depth = w_in.shape[0]
    assert seq % TOKEN_TILE == 0 and ATTN_BLOCK == TOKEN_TILE
    cos_b, sin_b = _rope_tables(seq, HEAD_DIM // 2)
    cos_d, sin_d = _rope_tables(seq, MLA_ROPE // 2)
    w_abc, w_df = _inproj_weights(w_in)
    ret_tables = _retention_tables()
    tri = jnp.asarray(np.triu(np.ones((TOKEN_TILE, TOKEN_TILE), np.float32)), BF16)
    cols = jnp.concatenate([g_sgu, jnp.pad(b_forget, ((0, 0), (0, ROWS_F - N_HEADS))), g_mla_q, g_mla_kv],
                           axis=1)[:, :, None]
    mla_w = _mla_weights(w_uq, w_ukv)
    g_mix, g_ffn = g_mix_norm[:, None, :], g_ffn_norm[:, None, :]
    g_group = g_group_out.reshape(depth, 4, GROUP_WIDTH, 1)
    xf = x.reshape(batch * seq, D_MODEL)
    for l in range(depth):
        ya, yb, qt, k, vt, stats = _front(
            xf, g_mix, w_abc, w_df, cols, w_spatial, b_spatial, cos_b, sin_b, ret_tables, tri,
            cos_d, sin_d, *mla_w, batch, l)
        ycd = _attention(qt, k, vt, stats)
        xf = _post(xf, ya, yb, ycd, g_group, w_out, g_ffn, w_up, w_down, g_final[None, :],
                   layer=l, final=(l == depth - 1))
    return xf.reshape(batch, seq, D_MODEL)
```
